```python
import math
import jax, jax.numpy as jnp
from jax import lax
import numpy as np

D_MODEL = 1024
BATCH = 2
SEQ = 8192
DEPTH = 2

N_MIXERS = 2
EPS = 1e-6

M_HEADS = 8
M_QK_DIM = D_MODEL // (2 * M_HEADS)
M_V_DIM = D_MODEL // M_HEADS
M_CHUNK = 64
GATE_CAP = 15.0
M_IN = 2 * M_HEADS * M_QK_DIM + M_HEADS * M_V_DIM + D_MODEL + 2 * M_HEADS

A_HEAD_DIM = 64
A_Q_HEADS = D_MODEL // A_HEAD_DIM
A_KV_HEADS = 2
A_GROUP = A_Q_HEADS // A_KV_HEADS
WINDOW = 128
A_BLOCK = 128
A_IN = (A_Q_HEADS + 2 * A_KV_HEADS) * A_HEAD_DIM

D_FF = int(math.ceil(8 * D_MODEL / 3 / 128)) * 128
CONV_W = 3

N_MLSTM_LAYERS = (DEPTH + 1) // 2
N_ATTN_LAYERS = DEPTH // 2

kernel_name = "hybrid_mlstm_swa_sink_convffn"


def rmsnorm(x, g):
    xf = x.astype(jnp.float32)
    y = xf * lax.rsqrt(jnp.mean(xf * xf, axis=-1, keepdims=True) + EPS)
    return (y * g.astype(jnp.float32)).astype(x.dtype)


def softcap(z, cap):
    return cap * jnp.tanh(z / cap)


def mlstm_mixer(h, w_in, gate_bias, head_norm, w_out):
    B, S, _ = h.shape
    H, dk, dv, L = M_HEADS, M_QK_DIM, M_V_DIM, M_CHUNK
    NC = S // L
    proj = h @ w_in
    splits = np.cumsum([H * dk, H * dk, H * dv, D_MODEL, H])
    q, k, v, o, ig_raw, fg_raw = jnp.split(proj, splits.tolist(), axis=-1)
    ig = softcap(ig_raw.astype(jnp.float32) + gate_bias[0].astype(jnp.float32), GATE_CAP)
    lf = jax.nn.log_sigmoid(softcap(fg_raw.astype(jnp.float32) + gate_bias[1].astype(jnp.float32), GATE_CAP))

    def to_chunks(t, d):
        return t.astype(jnp.float32).reshape(B, NC, L, H, d).transpose(1, 0, 3, 2, 4)

    def gate_chunks(t):
        return t.reshape(B, NC, L, H).transpose(1, 0, 3, 2)

    qc = to_chunks(q, dk) * (dk ** -0.5)
    kc = to_chunks(k, dk)
    vc = to_chunks(v, dv)
    igc, lfc = gate_chunks(ig), gate_chunks(lf)
    causal = jnp.tril(jnp.ones((L, L), dtype=bool))

    def body(carry, xs):
        C, n, m = carry
        qb, kb, vb, igb, lfb = xs
        b = jnp.cumsum(lfb, axis=-1)
        inter = b + m[..., None]
        dlog = b[..., :, None] - b[..., None, :] + igb[..., None, :]
        dlog = jnp.where(causal, dlog, -jnp.inf)
        m_t = jnp.maximum(inter, jnp.max(dlog, axis=-1))
        w = jnp.einsum('bhtd,bhsd->bhts', qb, kb) * jnp.exp(dlog - m_t[..., None])
        s_inter = jnp.exp(inter - m_t)
        num = jnp.einsum('bhts,bhsv->bhtv', w, vb) + s_inter[..., None] * jnp.einsum('bhtd,bhdv->bhtv', qb, C)
        den = jnp.sum(w, axis=-1) + s_inter * jnp.einsum('bhtd,bhd->bht', qb, n)
        hout = num / jnp.maximum(jnp.abs(den), jnp.exp(-m_t))[..., None]
        bL = b[..., -1]
        tail = bL[..., None] - b + igb
        m_new = jnp.maximum(bL + m, jnp.max(tail, axis=-1))
        ws = jnp.exp(tail - m_new[..., None])
        decay = jnp.exp(bL + m - m_new)
        C_new = decay[..., None, None] * C + jnp.einsum('bhs,bhsd,bhsv->bhdv', ws, kb, vb)
        n_new = decay[..., None] * n + jnp.einsum('bhs,bhsd->bhd', ws, kb)
        return (C_new, n_new, m_new), hout

    init = (jnp.zeros((B, H, dk, dv), jnp.float32),
            jnp.zeros((B, H, dk), jnp.float32),
            jnp.zeros((B, H), jnp.float32))
    _, hs = lax.scan(body, init, (qc, kc, vc, igc, lfc))
    hs = hs.transpose(1, 0, 3, 2, 4).reshape(B, S, H, dv)
    hs = hs * lax.rsqrt(jnp.mean(hs * hs, axis=-1, keepdims=True) + EPS)
    hs = hs * head_norm.astype(jnp.float32).reshape(H, dv)
    gated = jax.nn.sigmoid(o.astype(jnp.float32)) * hs.reshape(B, S, H * dv)
    return gated.astype(h.dtype) @ w_out


def swa_sink_mixer(h, w_in, b_in, sinks, w_out, b_out):
    B, S, _ = h.shape
    Hq, Hkv, G, dh, BLK = A_Q_HEADS, A_KV_HEADS, A_GROUP, A_HEAD_DIM, A_BLOCK
    NB = S // BLK
    proj = h @ w_in + b_in
    q, k, v = jnp.split(proj, [Hq * dh, (Hq + Hkv) * dh], axis=-1)
    q = q.reshape(B, NB, BLK, Hkv, G, dh) * (dh ** -0.5)
    k = k.reshape(B, NB, BLK, Hkv, dh)
    v = v.reshape(B, NB, BLK, Hkv, dh)

    def with_prev(t):
        prev = jnp.pad(t, ((0, 0), (1, 0), (0, 0), (0, 0), (0, 0)))[:, :-1]
        return jnp.concatenate([prev, t], axis=2)

    kc, vc = with_prev(k), with_prev(v)
    s = jnp.einsum('bnqhgd,bnkhd->bnhgqk', q, kc).astype(jnp.float32)
    qi = jnp.arange(BLK)[:, None]
    ku = jnp.arange(2 * BLK)[None, :]
    diff = qi - (ku - BLK)
    band = (diff >= 0) & (diff < WINDOW)
    kpos = jnp.arange(NB)[:, None, None] * BLK + ku[None] - BLK
    mask = band[None] & (kpos >= 0)
    s = jnp.where(mask[None, :, None, None], s, -jnp.inf)
    sink = sinks.astype(jnp.float32).reshape(1, 1, Hkv, G, 1, 1)
    mx = jnp.maximum(jnp.max(s, axis=-1, keepdims=True), sink)
    p = jnp.exp(s - mx)
    p = p / (jnp.sum(p, axis=-1, keepdims=True) + jnp.exp(sink - mx))
    o = jnp.einsum('bnhgqk,bnkhd->bnqhgd', p.astype(vc.dtype), vc).reshape(B, S, Hq * dh)
    return o @ w_out + b_out


def conv_ffn(h, w_up, conv_w, conv_b, w_down):
    S = h.shape[1]
    u = h @ w_up
    gate, val = jnp.split(u, 2, axis=-1)
    gp = jnp.pad(gate, ((0, 0), (CONV_W - 1, 0), (0, 0)))
    gate_c = conv_b + sum(conv_w[j] * gp[:, j:j + S] for j in range(CONV_W))
    return (jax.nn.silu(gate_c) * val) @ w_down


def setup_inputs(seed: int = 0) -> dict:
    key = jax.random.key(seed)
    ks = jax.random.split(key, 24)
    nrm = lambda k, shape, s: jax.random.normal(k, shape, jnp.float32) * s
    LM, LA = N_MLSTM_LAYERS, N_ATTN_LAYERS
    f_bias = jnp.broadcast_to(jnp.linspace(3.0, 6.0, M_HEADS), (LM, M_HEADS)) + nrm(ks[2], (LM, M_HEADS), 0.1)
    i_bias = nrm(ks[3], (LM, M_HEADS), 0.1)
    gain = lambda k: 1.0 + nrm(k, (DEPTH, D_MODEL), 0.02)
    return {
        "x": nrm(ks[0], (BATCH, SEQ, D_MODEL), 1.0),
        "m_w_in": nrm(ks[1], (LM, D_MODEL, M_IN), D_MODEL ** -0.5),
        "m_gate_bias": jnp.stack([i_bias, f_bias], axis=1),
        "m_head_norm": 1.0 + nrm(ks[4], (LM, M_HEADS * M_V_DIM), 0.02),
        "m_w_out": nrm(ks[5], (LM, M_HEADS * M_V_DIM, D_MODEL), (M_HEADS * M_V_DIM) ** -0.5),
        "a_w_in": nrm(ks[6], (LA, D_MODEL, A_IN), D_MODEL ** -0.5),
        "a_b_in": nrm(ks[7], (LA, A_IN), 0.02),
        "a_sinks": nrm(ks[8], (LA, A_Q_HEADS), 0.5),
        "a_w_out": nrm(ks[9], (LA, A_Q_HEADS * A_HEAD_DIM, D_MODEL), (A_Q_HEADS * A_HEAD_DIM) ** -0.5),
        "a_b_out": nrm(ks[10], (LA, D_MODEL), 0.02),
        "norm_mix_pre": gain(ks[11]),
        "norm_mix_post": gain(ks[12]),
        "norm_ffn_pre": gain(ks[13]),
        "norm_ffn_post": gain(ks[14]),
        "f_w_up": nrm(ks[15], (DEPTH, D_MODEL, 2 * D_FF), D_MODEL ** -0.5),
        "f_conv_w": nrm(ks[16], (DEPTH, CONV_W, D_FF), CONV_W ** -0.5),
        "f_conv_b": nrm(ks[17], (DEPTH, D_FF), 0.02),
        "f_w_down": nrm(ks[18], (DEPTH, D_FF, D_MODEL), D_FF ** -0.5),
    }


def reference(x, m_w_in, m_gate_bias, m_head_norm, m_w_out,
              a_w_in, a_b_in, a_sinks, a_w_out, a_b_out,
              norm_mix_pre, norm_mix_post, norm_ffn_pre, norm_ffn_post,
              f_w_up, f_conv_w, f_conv_b, f_w_down):
    h = x
    for i in range(DEPTH):
        z = rmsnorm(h, norm_mix_pre[i])
        if i % N_MIXERS == 0:
            j = i // N_MIXERS
            z = mlstm_mixer(z, m_w_in[j], m_gate_bias[j], m_head_norm[j], m_w_out[j])
        else:
            j = i // N_MIXERS
            z = swa_sink_mixer(z, a_w_in[j], a_b_in[j], a_sinks[j], a_w_out[j], a_b_out[j])
        h = h + rmsnorm(z, norm_mix_post[i])
        z = conv_ffn(rmsnorm(h, norm_ffn_pre[i]), f_w_up[i], f_conv_w[i], f_conv_b[i], f_w_down[i])
        h = h + rmsnorm(z, norm_ffn_post[i])
    return h
```

```python
import functools
import math

import jax
import jax.numpy as jnp
from jax import lax
from jax.experimental import pallas as pl
from jax.experimental.pallas import tpu as pltpu

EPS = 1e-6
LANES = 128

M_HEADS = 8
M_QK_DIM = 64
M_V_DIM = 128
GATE_CAP = 15.0
M_CHUNK = 128

A_HEAD_DIM = 64
A_Q_HEADS = 16
A_KV_HEADS = 2
A_GROUP = A_Q_HEADS // A_KV_HEADS
WINDOW = 128
A_BLOCK = 128

CONV_W = 3
NEG_BIG = -1e30

TOKEN_TILE = 512
FF_CHUNK = 256
VMEM_LIMIT = 60 * 1024 * 1024

bf16 = jnp.bfloat16
f32 = jnp.float32


def _dot(a, b):
    return jnp.dot(a, b, preferred_element_type=f32)


def _dot_nt(a, b):
    return lax.dot_general(a, b, (((1,), (1,)), ((), ())), preferred_element_type=f32)


def _dot_tn(a, b):
    return lax.dot_general(a, b, (((0,), (0,)), ((), ())), preferred_element_type=f32)


def _rmsnorm(x, g):
    return x * lax.rsqrt(jnp.mean(x * x, axis=-1, keepdims=True) + EPS) * g


def _resident(shape):
    nd = len(shape)
    return pl.BlockSpec(shape, lambda *_: (0,) * nd, pipeline_mode=pl.Buffered(1))


def _params(n_axes):
    return pltpu.CompilerParams(
        dimension_semantics=("arbitrary",) * n_axes, vmem_limit_bytes=VMEM_LIMIT)


def _mlstm_proj_kernel(x_ref, g_ref, w_ref, wg_ref, gb_ref, q_ref, k_ref, v_ref, o_ref, gate_ref):
    xn = _rmsnorm(x_ref[...], g_ref[...]).astype(bf16)
    nq = M_HEADS * M_QK_DIM
    nv = M_HEADS * M_V_DIM
    q_ref[...] = (_dot(xn, w_ref[:, 0:nq]) * (M_QK_DIM ** -0.5)).astype(bf16)
    k_ref[...] = _dot(xn, w_ref[:, nq:2 * nq]).astype(bf16)
    v_ref[...] = _dot(xn, w_ref[:, 2 * nq:2 * nq + nv]).astype(bf16)
    o_ref[...] = _dot(xn, w_ref[:, 2 * nq + nv:2 * nq + 2 * nv]).astype(bf16)
    raw = _dot(xn, wg_ref[...]) + gb_ref[...]
    capped = GATE_CAP * jnp.tanh(raw * (1.0 / GATE_CAP))
    logsig = jnp.minimum(capped, 0.0) - jnp.log1p(jnp.exp(-jnp.abs(capped)))
    lane = lax.broadcasted_iota(jnp.int32, raw.shape, 1)
    gate_ref[...] = jnp.where(lane < M_HEADS, capped, logsig)


def _mlstm_proj(x, gain, w_main, w_gate, gate_bias):
    t, d = x.shape
    nq = M_HEADS * M_QK_DIM
    nv = M_HEADS * M_V_DIM
    tm = TOKEN_TILE
    row = lambda n: pl.BlockSpec((tm, n), lambda i: (i, 0))
    return pl.pallas_call(
        _mlstm_proj_kernel,
        grid=(t // tm,),
        in_specs=[row(d), _resident((1, d)), _resident(w_main.shape), _resident(w_gate.shape),
                  _resident((1, LANES))],
        out_specs=[row(nq), row(nq), row(nv), row(nv), row(LANES)],
        out_shape=[jax.ShapeDtypeStruct((t, nq), bf16), jax.ShapeDtypeStruct((t, nq), bf16),
                   jax.ShapeDtypeStruct((t, nv), bf16), jax.ShapeDtypeStruct((t, nv), bf16),
                   jax.ShapeDtypeStruct((t, LANES), f32)],
        compiler_params=_params(1),
        name="mlstm_proj",
    )(x, gain, w_main, w_gate, gate_bias)


def _mlstm_scan_kernel(q_ref, k_ref, v_ref, o_ref, gate_ref, hn_ref, out_ref, c_ref, m_ref):
    L = M_CHUNK

    @pl.when(pl.program_id(1) == 0)
    def _():
        c_ref[...] = jnp.zeros_like(c_ref)
        m_ref[...] = jnp.zeros_like(m_ref)

    gates = gate_ref[...]
    hi = gates.astype(bf16)
    r1 = gates - hi.astype(f32)
    mid = r1.astype(bf16)
    lo = (r1 - mid.astype(f32)).astype(bf16)
    row = lax.broadcasted_iota(jnp.int32, (L, L), 0)
    col = lax.broadcasted_iota(jnp.int32, (L, L), 1)
    causal = row >= col
    tril = jnp.where(causal, 1.0, 0.0).astype(bf16)
    cum = _dot(tril, hi) + _dot(tril, mid) + _dot(tril, lo)
    gates_t = gates.T
    cum_t = cum.T

    lane = lax.broadcasted_iota(jnp.int32, (L, LANES), 1)
    ones_blk = jnp.ones((L, M_V_DIM), bf16)
    half = lax.broadcasted_iota(jnp.int32, (2 * M_QK_DIM, 1), 0) < M_QK_DIM

    for j in range(M_HEADS // 2):
        qp = q_ref[:, j * LANES:(j + 1) * LANES]
        kp = k_ref[:, j * LANES:(j + 1) * LANES]
        c_pair = c_ref[j]
        c_pair_b = c_pair.astype(bf16)
        wk_parts, vext_parts, decays = [], [], []
        for e in range(2):
            h = 2 * j + e
            sel = (lane < M_QK_DIM) if e == 0 else (lane >= M_QK_DIM)
            kh = jnp.where(sel, kp, jnp.zeros_like(kp))
            qh = jnp.where(sel, qp, jnp.zeros_like(qp))
            s = _dot_nt(qp, kh)
            b_col = cum[:, M_HEADS + h:M_HEADS + h + 1]
            ig_col = gates[:, h:h + 1]
            b_row = cum_t[M_HEADS + h:M_HEADS + h + 1, :]
            ig_row = gates_t[h:h + 1, :]
            m_prev = m_ref[h:h + 1, 0:1]
            dlog = jnp.where(causal, b_col - b_row + ig_row, NEG_BIG)
            inter = b_col + m_prev
            m_t = jnp.maximum(inter, jnp.max(dlog, axis=-1, keepdims=True))
            w = s * jnp.exp(dlog - m_t)
            s_inter = jnp.exp(inter - m_t)
            qs = (qh.astype(f32) * s_inter).astype(bf16)
            vext = jnp.concatenate([v_ref[:, h * M_V_DIM:(h + 1) * M_V_DIM], ones_blk], axis=1)
            lhs = jnp.concatenate([w.astype(bf16), qs], axis=1)
            rhs = jnp.concatenate([vext, c_pair_b], axis=0)
            numden = _dot(lhs, rhs)
            num = numden[:, :M_V_DIM]
            den = numden[:, M_V_DIM:]
            hout = num / jnp.maximum(jnp.abs(den), jnp.exp(-m_t))
            hn = hout * lax.rsqrt(jnp.mean(hout * hout, axis=-1, keepdims=True) + EPS)
            hn = hn * hn_ref[:, h * M_V_DIM:(h + 1) * M_V_DIM]
            og = o_ref[:, h * M_V_DIM:(h + 1) * M_V_DIM].astype(f32)
            out_ref[:, h * M_V_DIM:(h + 1) * M_V_DIM] = (jax.nn.sigmoid(og) * hn).astype(bf16)
            b_last = cum[L - 1:L, M_HEADS + h:M_HEADS + h + 1]
            tail = b_last - b_col + ig_col
            m_new = jnp.maximum(b_last + m_prev, jnp.max(tail, axis=0, keepdims=True))
            ws = jnp.exp(tail - m_new)
            decays.append(jnp.exp(b_last + m_prev - m_new))
            wk_parts.append((kh.astype(f32) * ws).astype(bf16))
            vext_parts.append(vext)
            m_ref[h:h + 1, :] = jnp.broadcast_to(m_new, (1, LANES))
        wk = jnp.concatenate(wk_parts, axis=0)
        vx = jnp.concatenate(vext_parts, axis=0)
        decay_rows = jnp.where(half, decays[0], decays[1])
        c_ref[j] = decay_rows * c_pair + _dot_tn(wk, vx)


def _mlstm_scan(q, k, v, o, gates, head_norm, batch, seq):
    L = M_CHUNK
    nq = M_HEADS * M_QK_DIM
    nv = M_HEADS * M_V_DIM
    nc = seq // L
    row = lambda n: pl.BlockSpec((L, n), lambda b, c: (b * nc + c, 0))
    return pl.pallas_call(
        _mlstm_scan_kernel,
        grid=(batch, nc),
        in_specs=[row(nq), row(nq), row(nv), row(nv), row(LANES), _resident((1, nv))],
        out_specs=row(nv),
        out_shape=jax.ShapeDtypeStruct((batch * seq, nv), bf16),
        scratch_shapes=[pltpu.VMEM((M_HEADS // 2, 2 * M_QK_DIM, 2 * M_V_DIM), f32),
                        pltpu.VMEM((M_HEADS, LANES), f32)],
        compiler_params=_params(2),
        name="mlstm_scan",
    )(q, k, v, o, gates, head_norm)


def _attn_proj_kernel(x_ref, g_ref, w_ref, b_ref, q_ref, k_ref, v_ref):
    xn = _rmsnorm(x_ref[...], g_ref[...]).astype(bf16)
    nq = A_Q_HEADS * A_HEAD_DIM
    nkv = A_KV_HEADS * A_HEAD_DIM
    q = _dot(xn, w_ref[:, 0:nq]) + b_ref[:, 0:nq]
    q_ref[...] = (q * (A_HEAD_DIM ** -0.5)).astype(bf16)
    k_ref[...] = (_dot(xn, w_ref[:, nq:nq + nkv]) + b_ref[:, nq:nq + nkv]).astype(bf16)
    v_ref[...] = (_dot(xn, w_ref[:, nq + nkv:nq + 2 * nkv]) + b_ref[:, nq + nkv:nq + 2 * nkv]).astype(bf16)


def _attn_proj(x, gain, w, b):
    t, d = x.shape
    nq = A_Q_HEADS * A_HEAD_DIM
    nkv = A_KV_HEADS * A_HEAD_DIM
    tm = TOKEN_TILE
    row = lambda n: pl.BlockSpec((tm, n), lambda i: (i, 0))
    return pl.pallas_call(
        _attn_proj_kernel,
        grid=(t // tm,),
        in_specs=[row(d), _resident((1, d)), _resident(w.shape), _resident(b.shape)],
        out_specs=[row(nq), row(nkv), row(nkv)],
        out_shape=[jax.ShapeDtypeStruct((t, nq), bf16), jax.ShapeDtypeStruct((t, nkv), bf16),
                   jax.ShapeDtypeStruct((t, nkv), bf16)],
        compiler_params=_params(1),
        name="attn_proj",
    )(x, gain, w, b)


def _swa_kernel(q_ref, kc_ref, kp_ref, vc_ref, vp_ref, sink_ref, out_ref):
    blk = A_BLOCK
    first = pl.program_id(1) == 0
    kcat = jnp.concatenate([kp_ref[...], kc_ref[...]], axis=0)
    vcat = jnp.concatenate([vp_ref[...], vc_ref[...]], axis=0)
    kroll = pltpu.roll(kcat, A_HEAD_DIM, axis=1)
    vroll = pltpu.roll(vcat, A_HEAD_DIM, axis=1)
    lane = lax.broadcasted_iota(jnp.int32, (2 * blk, LANES), 1)
    low = lane < A_HEAD_DIM
    zero = jnp.zeros_like(kcat)

    qi = lax.broadcasted_iota(jnp.int32, (blk, 2 * blk), 0)
    ku = lax.broadcasted_iota(jnp.int32, (blk, 2 * blk), 1)
    diff = qi - (ku - blk)
    band = (diff >= 0) & (diff < WINDOW)
    mask = band & jnp.logical_or(ku >= blk, jnp.logical_not(first))
    out_lane_low = lax.broadcasted_iota(jnp.int32, (blk, LANES), 1) < A_HEAD_DIM

    for g in range(A_KV_HEADS):
        k_src_low = kcat if g == 0 else kroll
        k_src_high = kroll if g == 0 else kcat
        v_src_low = vcat if g == 0 else vroll
        v_src_high = vroll if g == 0 else vcat
        k_lo = jnp.where(low, k_src_low, zero)
        k_hi = jnp.where(low, zero, k_src_high)
        v_lo = jnp.where(low, v_src_low, zero)
        v_hi = jnp.where(low, zero, v_src_high)
        v_blk = jnp.concatenate([v_lo, v_hi], axis=0)
        for jj in range(A_GROUP // 2):
            j = g * (A_GROUP // 2) + jj
            qp = q_ref[:, j * LANES:(j + 1) * LANES]
            ps, ls = [], []
            for e, k_e in enumerate((k_lo, k_hi)):
                h = 2 * j + e
                sink = sink_ref[0:1, h:h + 1]
                s = jnp.where(mask, _dot_nt(qp, k_e), NEG_BIG)
                mx = jnp.maximum(jnp.max(s, axis=-1, keepdims=True), sink)
                p = jnp.exp(s - mx)
                ls.append(jnp.sum(p, axis=-1, keepdims=True) + jnp.exp(sink - mx))
                ps.append(p.astype(bf16))
            o = _dot(jnp.concatenate(ps, axis=1), v_blk)
            inv = jnp.where(out_lane_low, 1.0 / ls[0], 1.0 / ls[1])
            out_ref[:, j * LANES:(j + 1) * LANES] = (o * inv).astype(bf16)


def _swa(q, k, v, sinks, batch, seq):
    blk = A_BLOCK
    nq = A_Q_HEADS * A_HEAD_DIM
    nkv = A_KV_HEADS * A_HEAD_DIM
    nb = seq // blk
    cur = lambda n: pl.BlockSpec((blk, n), lambda b, i: (b * nb + i, 0))
    prev = lambda n: pl.BlockSpec((blk, n), lambda b, i: (b * nb + jnp.maximum(i - 1, 0), 0))
    return pl.pallas_call(
        _swa_kernel,
        grid=(batch, nb),
        in_specs=[cur(nq), cur(nkv), prev(nkv), cur(nkv), prev(nkv), _resident((1, LANES))],
        out_specs=cur(nq),
        out_shape=jax.ShapeDtypeStruct((batch * seq, nq), bf16),
        compiler_params=_params(2),
        name="swa",
    )(q, k, k, v, v, sinks)


def _out_ffn_kernel(h_ref, a_ref, wo_ref, bo_ref, gpost_ref, gpre_ref, gfpost_ref,
                    wg_ref, wv_ref, cw_ref, cb_ref, wd_ref, out_ref, carry_ref):
    tm = h_ref.shape[0]
    dff = wg_ref.shape[1]

    @pl.when(pl.program_id(1) == 0)
    def _():
        carry_ref[...] = jnp.zeros_like(carry_ref)

    z = _dot(a_ref[...], wo_ref[...]) + bo_ref[...]
    h1 = h_ref[...] + _rmsnorm(z, gpost_ref[...])
    xn = _rmsnorm(h1, gpre_ref[...]).astype(bf16)

    acc = jnp.zeros((tm, h1.shape[1]), f32)
    for c0 in range(0, dff, FF_CHUNK):
        cs = slice(c0, c0 + FF_CHUNK)
        gate = _dot(xn, wg_ref[:, cs])
        val = _dot(xn, wv_ref[:, cs])
        ext = jnp.concatenate([carry_ref[:, cs], gate], axis=0)
        carry_ref[:, cs] = gate[tm - 8:tm, :]
        g1 = ext[7:7 + tm, :]
        g2 = ext[6:6 + tm, :]
        gc = cb_ref[:, cs] + cw_ref[0:1, cs] * g2 + cw_ref[1:2, cs] * g1 + cw_ref[2:3, cs] * gate
        act = (gc * jax.nn.sigmoid(gc) * val).astype(bf16)
        acc = acc + _dot(act, wd_ref[cs, :])
    out_ref[...] = h1 + _rmsnorm(acc, gfpost_ref[...])


def _out_ffn(h, a, wo, bo, gpost, gpre, gfpost, wg, wv, cw, cb, wd, batch, seq):
    t, d = h.shape
    tm = TOKEN_TILE
    nt = seq // tm
    dff = wg.shape[1]
    row = lambda n: pl.BlockSpec((tm, n), lambda b, i: (b * nt + i, 0))
    return pl.pallas_call(
        _out_ffn_kernel,
        grid=(batch, nt),
        in_specs=[row(d), row(a.shape[1]), _resident(wo.shape), _resident((1, d)), _resident((1, d)),
                  _resident((1, d)), _resident((1, d)), _resident(wg.shape), _resident(wv.shape),
                  _resident(cw.shape), _resident(cb.shape), _resident(wd.shape)],
        out_specs=row(d),
        out_shape=jax.ShapeDtypeStruct((t, d), f32),
        scratch_shapes=[pltpu.VMEM((8, dff), f32)],
        compiler_params=_params(2),
        name="out_ffn",
    )(h, a, wo, bo, gpost, gpre, gfpost, wg, wv, cw, cb, wd)


def kernel(x, m_w_in, m_gate_bias, m_head_norm, m_w_out, a_w_in, a_b_in, a_sinks, a_w_out, a_b_out,
           norm_mix_pre, norm_mix_post, norm_ffn_pre, norm_ffn_post, f_w_up, f_conv_w, f_conv_b, f_w_down):
    batch, seq, d = x.shape
    depth = norm_mix_pre.shape[0]
    dff = f_w_down.shape[1]
    h = x.reshape(batch * seq, d)
    row = lambda vec: vec.reshape(1, -1).astype(f32)
    n_main = 2 * M_HEADS * M_QK_DIM + 2 * M_HEADS * M_V_DIM

    for i in range(depth):
        j = i // 2
        if i % 2 == 0:
            w_in = m_w_in[j]
            w_main = w_in[:, :n_main].astype(bf16)
            w_gate = jnp.pad(w_in[:, n_main:], ((0, 0), (0, LANES - 2 * M_HEADS))).astype(bf16)
            gate_bias = jnp.pad(m_gate_bias[j].reshape(1, -1).astype(f32), ((0, 0), (0, LANES - 2 * M_HEADS)))
            q, k, v, o, gates = _mlstm_proj(h, row(norm_mix_pre[i]), w_main, w_gate, gate_bias)
            mixed = _mlstm_scan(q, k, v, o, gates, row(m_head_norm[j]), batch, seq)
            w_out = m_w_out[j].astype(bf16)
            b_out = jnp.zeros((1, d), f32)
        else:
            q, k, v = _attn_proj(h, row(norm_mix_pre[i]), a_w_in[j].astype(bf16), row(a_b_in[j]))
            sinks = jnp.pad(row(a_sinks[j]), ((0, 0), (0, LANES - A_Q_HEADS)))
            mixed = _swa(q, k, v, sinks, batch, seq)
            w_out = a_w_out[j].astype(bf16)
            b_out = row(a_b_out[j])
        h = _out_ffn(h, mixed, w_out, b_out, row(norm_mix_post[i]), row(norm_ffn_pre[i]), row(norm_ffn_post[i]),
                     f_w_up[i][:, :dff].astype(bf16), f_w_up[i][:, dff:].astype(bf16),
                     f_conv_w[i].astype(f32), row(f_conv_b[i]), f_w_down[i].astype(bf16), batch, seq)
    return h.reshape(batch, seq, d)
```

```python
import jax
import jax.numpy as jnp
from jax import lax
from jax.experimental import pallas as pl
from jax.experimental.pallas import tpu as pltpu

EPS = 1e-6
LANES = 128
BF16_ROWS = 16

M_HEADS = 8
M_QK_DIM = 64
M_V_DIM = 128
GATE_CAP = 15.0
M_CHUNK = 128

A_HEAD_DIM = 64
A_Q_HEADS = 16
A_KV_HEADS = 2
A_GROUP = A_Q_HEADS // A_KV_HEADS
WINDOW = 128
A_BLOCK = 128

NEG_BIG = -1e30

TOKEN_TILE = 512
SWA_TILE = 256
FF_CHUNK = 256
VMEM_LIMIT = 60 * 1024 * 1024

bf16 = jnp.bfloat16
f32 = jnp.float32


def _dot(a, b):
    return jnp.dot(a, b, preferred_element_type=f32)


def _dot_nt(a, b):
    return lax.dot_general(a, b, (((1,), (1,)), ((), ())), preferred_element_type=f32)


def _dot_tn(a, b):
    return lax.dot_general(a, b, (((0,), (0,)), ((), ())), preferred_element_type=f32)


def _rmsnorm(x, g):
    return x * lax.rsqrt(jnp.mean(x * x, axis=-1, keepdims=True) + EPS) * g


def _split3(x):
    hi = x.astype(bf16)
    r1 = x - hi.astype(f32)
    mid = r1.astype(bf16)
    lo = (r1 - mid.astype(f32)).astype(bf16)
    return hi, mid, lo


def _log_sigmoid(x):
    return jnp.minimum(x, 0.0) - jnp.log1p(jnp.exp(-jnp.abs(x)))


def _tile_lanes(x, reps):
    return jnp.concatenate([x] * reps, axis=1)


def _resident(shape):
    nd = len(shape)
    return pl.BlockSpec(shape, lambda *_: (0,) * nd, pipeline_mode=pl.Buffered(1))


def _params(n_axes):
    return pltpu.CompilerParams(
        dimension_semantics=("arbitrary",) * n_axes, vmem_limit_bytes=VMEM_LIMIT)


def _mlstm_proj_kernel(x_ref, g_ref, wt_ref, wk_ref, wg_ref, wgt_ref, gb_row_ref, gb_col_ref,
                       qt_ref, k_ref, vt_ref, ot_ref, gcol_ref, grow_ref):
    tm = x_ref.shape[0]
    xn = _rmsnorm(x_ref[...], g_ref[...]).astype(bf16)
    nq = M_HEADS * M_QK_DIM
    nv = M_HEADS * M_V_DIM
    qt_ref[...] = (_dot_nt(wt_ref[0:nq, :], xn) * (M_QK_DIM ** -0.5)).astype(bf16)
    vt_ref[...] = _dot_nt(wt_ref[nq:nq + nv, :], xn).astype(bf16)
    ot_ref[...] = _dot_nt(wt_ref[nq + nv:nq + 2 * nv, :], xn).astype(bf16)
    k_ref[...] = _dot(xn, wk_ref[...]).astype(bf16)

    def gates(raw, is_input_gate):
        capped = GATE_CAP * jnp.tanh(raw * (1.0 / GATE_CAP))
        return jnp.where(is_input_gate, capped, _log_sigmoid(capped))

    raw_col = _dot(xn, wg_ref[...]) + gb_row_ref[...]
    gcol_ref[...] = gates(raw_col, lax.broadcasted_iota(jnp.int32, raw_col.shape, 1) < M_HEADS)
    raw_row = _dot_nt(wgt_ref[...], xn) + _tile_lanes(gb_col_ref[...], tm // LANES)
    grow_ref[...] = gates(raw_row, lax.broadcasted_iota(jnp.int32, raw_row.shape, 0) < M_HEADS)


def _mlstm_proj(x, gain, wt_main, w_k, w_gate, wt_gate, gb_row, gb_col):
    t, d = x.shape
    nq = M_HEADS * M_QK_DIM
    nv = M_HEADS * M_V_DIM
    tm = TOKEN_TILE
    row = lambda n: pl.BlockSpec((tm, n), lambda i: (i, 0))
    col = lambda n: pl.BlockSpec((n, tm), lambda i: (0, i))
    return pl.pallas_call(
        _mlstm_proj_kernel,
        grid=(t // tm,),
        in_specs=[row(d), _resident((1, d)), _resident(wt_main.shape), _resident(w_k.shape),
                  _resident(w_gate.shape), _resident(wt_gate.shape), _resident(gb_row.shape),
                  _resident(gb_col.shape)],
        out_specs=[col(nq), row(nq), col(nv), col(nv), row(LANES), col(LANES)],
        out_shape=[jax.ShapeDtypeStruct((nq, t), bf16), jax.ShapeDtypeStruct((t, nq), bf16),
                   jax.ShapeDtypeStruct((nv, t), bf16), jax.ShapeDtypeStruct((nv, t), bf16),
                   jax.ShapeDtypeStruct((t, LANES), f32), jax.ShapeDtypeStruct((LANES, t), f32)],
        compiler_params=_params(1),
        name="mlstm_proj",
    )(x, gain, wt_main, w_k, w_gate, wt_gate, gb_row, gb_col)


def _mlstm_scan_kernel(qt_ref, k_ref, vt_ref, ot_ref, gcol_ref, grow_ref, hn_ref, out_ref, ct_ref, m_ref):
    L = M_CHUNK
    dk, dv = M_QK_DIM, M_V_DIM

    @pl.when(pl.program_id(1) == 0)
    def _():
        ct_ref[...] = jnp.zeros_like(ct_ref)
        m_ref[...] = jnp.zeros_like(m_ref)

    grow = grow_ref[...]
    gcol = gcol_ref[...]
    ri = lax.broadcasted_iota(jnp.int32, (L, L), 0)
    ci = lax.broadcasted_iota(jnp.int32, (L, L), 1)
    upper = ri <= ci
    triu = jnp.where(upper, 1.0, 0.0).astype(bf16)
    tril = jnp.where(ri >= ci, 1.0, 0.0).astype(bf16)
    b_rows = sum(_dot(p, triu) for p in _split3(grow))
    b_cols = sum(_dot(tril, p) for p in _split3(gcol))

    ig_r = grow[0:M_HEADS, :]
    b_r = b_rows[M_HEADS:2 * M_HEADS, :]
    c_r = ig_r - b_r
    m_prev = m_ref[...]
    lane8 = lax.broadcasted_iota(jnp.int32, (M_HEADS, L), 1)
    cmax = c_r
    shift = 1
    while shift < L:
        cmax = jnp.where(lane8 >= shift, jnp.maximum(cmax, pltpu.roll(cmax, shift, axis=1)), cmax)
        shift *= 2
    a_r = jnp.maximum(m_prev, cmax)
    carry_w = jnp.exp(m_prev - a_r)
    clamp_r = jnp.exp(-(a_r + b_r))
    b_last = jnp.broadcast_to(b_r[:, L - 1:L], (M_HEADS, L))
    tail = c_r + b_last
    m_new = jnp.maximum(b_last + m_prev, jnp.max(tail, axis=-1, keepdims=True))
    ws_r = jnp.exp(tail - m_new)
    decay = jnp.exp(b_last + m_prev - m_new)
    m_ref[...] = m_new
    c_cols = gcol - pltpu.roll(b_cols, LANES - M_HEADS, axis=1)

    lane = lax.broadcasted_iota(jnp.int32, (L, LANES), 1)
    low = lane < dk
    ones_rows = jnp.ones((BF16_ROWS, L), bf16)
    zeros_half = jnp.zeros((dk, L), bf16)

    for j in range(M_HEADS // 2):
        kp = k_ref[:, j * LANES:(j + 1) * LANES]
        qtp = qt_ref[j * LANES:(j + 1) * LANES, :]
        q_even = jnp.concatenate([qtp[0:dk, :], zeros_half], axis=0)
        q_odd = jnp.concatenate([zeros_half, qtp[dk:2 * dk, :]], axis=0)
        st = _dot(kp, jnp.concatenate([q_even, q_odd], axis=1))
        ct_pair = ct_ref[j]
        ct_pair_b = ct_pair.astype(bf16)
        vws = []
        for e, q_e in enumerate((q_even, q_odd)):
            h = 2 * j + e
            expo = jnp.where(upper, c_cols[:, h:h + 1] - a_r[h:h + 1, :], NEG_BIG)
            wt = (st[:, e * L:(e + 1) * L] * jnp.exp(expo)).astype(bf16)
            q_w = (q_e.astype(f32) * carry_w[h:h + 1, :]).astype(bf16)
            vext = jnp.concatenate([vt_ref[h * dv:(h + 1) * dv, :], ones_rows], axis=0)
            nd = _dot(jnp.concatenate([vext, ct_pair_b], axis=1), jnp.concatenate([wt, q_w], axis=0))
            den = jnp.maximum(jnp.abs(nd[dv:dv + 1, :]), clamp_r[h:h + 1, :])
            hout = nd[0:dv, :] * (1.0 / den)
            hn = hout * lax.rsqrt(jnp.mean(hout * hout, axis=0, keepdims=True) + EPS)
            hn = hn * hn_ref[h * dv:(h + 1) * dv, :]
            og = ot_ref[h * dv:(h + 1) * dv, :].astype(f32)
            out_ref[h * dv:(h + 1) * dv, :] = (jax.nn.sigmoid(og) * hn).astype(bf16)
            vws.append((vext.astype(f32) * ws_r[h:h + 1, :]).astype(bf16))
        k_split = jnp.concatenate([jnp.where(low, kp, jnp.zeros_like(kp)),
                                   jnp.where(low, jnp.zeros_like(kp), kp)], axis=0)
        decay_pair = jnp.where(low[0:1, :], decay[2 * j:2 * j + 1, :], decay[2 * j + 1:2 * j + 2, :])
        ct_ref[j] = decay_pair * ct_pair + _dot(jnp.concatenate(vws, axis=1), k_split)


def _mlstm_scan(qt, k, vt, ot, gcol, grow, hn_rep, batch, seq):
    L = M_CHUNK
    nq = M_HEADS * M_QK_DIM
    nv = M_HEADS * M_V_DIM
    nc = seq // L
    row = lambda n: pl.BlockSpec((L, n), lambda b, c: (b * nc + c, 0))
    col = lambda n: pl.BlockSpec((n, L), lambda b, c: (0, b * nc + c))
    return pl.pallas_call(
        _mlstm_scan_kernel,
        grid=(batch, nc),
        in_specs=[col(nq), row(nq), col(nv), col(nv), row(LANES), col(LANES), _resident(hn_rep.shape)],
        out_specs=col(nv),
        out_shape=jax.ShapeDtypeStruct((nv, batch * seq), bf16),
        scratch_shapes=[pltpu.VMEM((M_HEADS // 2, M_V_DIM + BF16_ROWS, 2 * M_QK_DIM), f32),
                        pltpu.VMEM((M_HEADS, LANES), f32)],
        compiler_params=_params(2),
        name="mlstm_scan",
    )(qt, k, vt, ot, gcol, grow, hn_rep)


def _attn_proj_kernel(x_ref, g_ref, wqt_ref, wk_ref, wvt_ref, bq_ref, bk_ref, bv_ref, qt_ref, k_ref, vt_ref):
    tm = x_ref.shape[0]
    xn = _rmsnorm(x_ref[...], g_ref[...]).astype(bf16)
    qt = _dot_nt(wqt_ref[...], xn) + _tile_lanes(bq_ref[...], tm // LANES)
    qt_ref[...] = (qt * (A_HEAD_DIM ** -0.5)).astype(bf16)
    k_ref[...] = (_dot(xn, wk_ref[...]) + bk_ref[...]).astype(bf16)
    vt_ref[...] = (_dot_nt(wvt_ref[...], xn) + _tile_lanes(bv_ref[...], tm // LANES)).astype(bf16)


def _attn_proj(x, gain, wqt, wk, wvt, bq_rep, bk, bv_rep):
    t, d = x.shape
    nq = A_Q_HEADS * A_HEAD_DIM
    nkv = A_KV_HEADS * A_HEAD_DIM
    tm = TOKEN_TILE
    row = lambda n: pl.BlockSpec((tm, n), lambda i: (i, 0))
    col = lambda n: pl.BlockSpec((n, tm), lambda i: (0, i))
    return pl.pallas_call(
        _attn_proj_kernel,
        grid=(t // tm,),
        in_specs=[row(d), _resident((1, d)), _resident(wqt.shape), _resident(wk.shape), _resident(wvt.shape),
                  _resident(bq_rep.shape), _resident(bk.shape), _resident(bv_rep.shape)],
        out_specs=[col(nq), row(nkv), col(nkv)],
        out_shape=[jax.ShapeDtypeStruct((nq, t), bf16), jax.ShapeDtypeStruct((t, nkv), bf16),
                   jax.ShapeDtypeStruct((nkv, t), bf16)],
        compiler_params=_params(1),
        name="attn_proj",
    )(x, gain, wqt, wk, wvt, bq_rep, bk, bv_rep)


def _swa_kernel(qt_ref, kc_ref, kp_ref, vtc_ref, vtp_ref, sink_ref, out_ref):
    blk = A_BLOCK
    dh = A_HEAD_DIM
    first = pl.program_id(1) == 0
    ku = lax.broadcasted_iota(jnp.int32, (2 * blk, blk), 0)
    qi = lax.broadcasted_iota(jnp.int32, (2 * blk, blk), 1)
    diff = qi - (ku - blk)
    band = (diff >= 0) & (diff < WINDOW)
    bias = jnp.where(band | (ku == 0), 0.0, NEG_BIG)
    bias_first = jnp.where((band & (ku >= blk)) | (ku == 0), 0.0, NEG_BIG)
    krow = lax.broadcasted_iota(jnp.int32, (2 * blk, LANES), 0)
    klane = lax.broadcasted_iota(jnp.int32, (2 * blk, LANES), 1)
    k_aug = jnp.where((krow == 0) & (klane < 3), 1.0, 0.0).astype(bf16)
    vcol = lax.broadcasted_iota(jnp.int32, (dh, 2 * blk), 1)
    ones_rows = jnp.ones((BF16_ROWS, 2 * blk), bf16)

    for bq in range(SWA_TILE // blk):
        cols = slice(bq * blk, (bq + 1) * blk)
        if bq == 0:
            kcat = jnp.concatenate([kp_ref[...], kc_ref[cols, :]], axis=0)
            vtcat = jnp.concatenate([vtp_ref[...], vtc_ref[:, cols]], axis=1)
            b1 = jnp.where(first, bias_first, bias)
        else:
            kcat = kc_ref[(bq - 1) * blk:(bq + 1) * blk, :]
            vtcat = vtc_ref[:, (bq - 1) * blk:(bq + 1) * blk]
            b1 = bias
        bias_all = _tile_lanes(b1, A_GROUP)
        for g in range(A_KV_HEADS):
            in_group = (klane >= g * dh) & (klane < (g + 1) * dh) & (krow > 0)
            km = jnp.concatenate([jnp.where(in_group, kcat, jnp.zeros_like(kcat)), k_aug], axis=1)
            blocks = []
            for hh in range(A_GROUP):
                h = g * A_GROUP + hh
                pair = qt_ref[(h // 2) * LANES:(h // 2 + 1) * LANES, cols]
                if h % 2 != g:
                    pair = jnp.concatenate([pair[dh:2 * dh, :], pair[0:dh, :]], axis=0)
                blocks.append(pair)
            rhs = jnp.concatenate([jnp.concatenate(blocks, axis=1), sink_ref[g]], axis=0)
            st = _dot(km, rhs) + bias_all
            mx = jnp.max(st, axis=0, keepdims=True)
            p = jnp.exp(st - mx).astype(bf16)
            vt_g = jnp.where(vcol == 0, jnp.zeros((dh, 2 * blk), bf16), vtcat[g * dh:(g + 1) * dh, :])
            oext = _dot(jnp.concatenate([vt_g, ones_rows], axis=0), p)
            o = (oext[0:dh, :] * (1.0 / oext[dh:dh + 1, :])).astype(bf16)
            for hh in range(A_GROUP):
                h = g * A_GROUP + hh
                out_ref[h * dh:(h + 1) * dh, cols] = o[:, hh * blk:(hh + 1) * blk]


def _swa(qt, k, vt, sink_aug, batch, seq):
    blk = A_BLOCK
    tq = SWA_TILE
    nq = A_Q_HEADS * A_HEAD_DIM
    nkv = A_KV_HEADS * A_HEAD_DIM
    nt = seq // tq
    per = tq // blk
    prev_idx = lambda b, i: b * nt * per + jnp.maximum(i * per - 1, 0)
    return pl.pallas_call(
        _swa_kernel,
        grid=(batch, nt),
        in_specs=[pl.BlockSpec((nq, tq), lambda b, i: (0, b * nt + i)),
                  pl.BlockSpec((tq, nkv), lambda b, i: (b * nt + i, 0)),
                  pl.BlockSpec((blk, nkv), lambda b, i: (prev_idx(b, i), 0)),
                  pl.BlockSpec((nkv, tq), lambda b, i: (0, b * nt + i)),
                  pl.BlockSpec((nkv, blk), lambda b, i: (0, prev_idx(b, i))),
                  _resident(sink_aug.shape)],
        out_specs=pl.BlockSpec((nq, tq), lambda b, i: (0, b * nt + i)),
        out_shape=jax.ShapeDtypeStruct((nq, batch * seq), bf16),
        compiler_params=_params(2),
        name="swa",
    )(qt, k, k, vt, vt, sink_aug)


def _sink_rows(sinks):
    parts = jnp.stack(_split3(sinks.astype(f32)), axis=0)
    per_lane = jnp.repeat(parts.reshape(3, A_KV_HEADS, A_GROUP), A_BLOCK, axis=2)
    return jnp.pad(per_lane.transpose(1, 0, 2), ((0, 0), (0, LANES - 3), (0, 0)))


def _out_ffn_kernel(h_ref, at_ref, wo_ref, bo_ref, gpost_ref, gpre_ref, gfpost_ref,
                    wg_ref, wv_ref, cw_ref, cb_ref, wd_ref, out_ref, carry_ref):
    tm = h_ref.shape[0]
    dff = wg_ref.shape[1]

    @pl.when(pl.program_id(1) == 0)
    def _():
        carry_ref[...] = jnp.zeros_like(carry_ref)

    z = _dot_tn(at_ref[...], wo_ref[...]) + bo_ref[...]
    h1 = h_ref[...] + _rmsnorm(z, gpost_ref[...])
    xn = _rmsnorm(h1, gpre_ref[...]).astype(bf16)

    acc = jnp.zeros((tm, h1.shape[1]), f32)
    for c0 in range(0, dff, FF_CHUNK):
        cs = slice(c0, c0 + FF_CHUNK)
        gate = _dot(xn, wg_ref[:, cs])
        val = _dot(xn, wv_ref[:, cs])
        ext = jnp.concatenate([carry_ref[:, cs], gate], axis=0)
        carry_ref[:, cs] = gate[tm - 8:tm, :]
        g1 = ext[7:7 + tm, :]
        g2 = ext[6:6 + tm, :]
        gc = cb_ref[:, cs] + cw_ref[0:1, cs] * g2 + cw_ref[1:2, cs] * g1 + cw_ref[2:3, cs] * gate
        act = (gc * jax.nn.sigmoid(gc) * val).astype(bf16)
        acc = acc + _dot(act, wd_ref[cs, :])
    out_ref[...] = h1 + _rmsnorm(acc, gfpost_ref[...])


def _out_ffn(h, at, wo, bo, gpost, gpre, gfpost, wg, wv, cw, cb, wd, batch, seq):
    t, d = h.shape
    tm = TOKEN_TILE
    nt = seq // tm
    dff = wg.shape[1]
    row = lambda n: pl.BlockSpec((tm, n), lambda b, i: (b * nt + i, 0))
    return pl.pallas_call(
        _out_ffn_kernel,
        grid=(batch, nt),
        in_specs=[row(d), pl.BlockSpec((at.shape[0], tm), lambda b, i: (0, b * nt + i)),
                  _resident(wo.shape), _resident((1, d)), _resident((1, d)),
                  _resident((1, d)), _resident((1, d)), _resident(wg.shape), _resident(wv.shape),
                  _resident(cw.shape), _resident(cb.shape), _resident(wd.shape)],
        out_specs=row(d),
        out_shape=jax.ShapeDtypeStruct((t, d), f32),
        scratch_shapes=[pltpu.VMEM((8, dff), f32)],
        compiler_params=_params(2),
        name="out_ffn",
    )(h, at, wo, bo, gpost, gpre, gfpost, wg, wv, cw, cb, wd)


def kernel(x, m_w_in, m_gate_bias, m_head_norm, m_w_out, a_w_in, a_b_in, a_sinks, a_w_out, a_b_out,
           norm_mix_pre, norm_mix_post, norm_ffn_pre, norm_ffn_post, f_w_up, f_conv_w, f_conv_b, f_w_down):
    batch, seq, d = x.shape
    depth = norm_mix_pre.shape[0]
    dff = f_w_down.shape[1]
    h = x.reshape(batch * seq, d)
    row = lambda vec: vec.reshape(1, -1).astype(f32)
    lane_rep = lambda vec: jnp.broadcast_to(vec.astype(f32)[:, None], (vec.shape[0], LANES))

    for i in range(depth):
        j = i // 2
        if i % 2 == 0:
            nq = M_HEADS * M_QK_DIM
            nv = M_HEADS * M_V_DIM
            w_in = m_w_in[j]
            w_q, w_k, w_v, w_o, w_g = jnp.split(w_in, [nq, 2 * nq, 2 * nq + nv, 2 * nq + 2 * nv], axis=1)
            wt_main = jnp.concatenate([w_q, w_v, w_o], axis=1).T.astype(bf16)
            w_gate = jnp.pad(w_g, ((0, 0), (0, LANES - 2 * M_HEADS))).astype(bf16)
            gate_bias = jnp.pad(m_gate_bias[j].reshape(-1).astype(f32), (0, LANES - 2 * M_HEADS))
            qt, k, vt, ot, gcol, grow = _mlstm_proj(
                h, row(norm_mix_pre[i]), wt_main, w_k.astype(bf16), w_gate, w_gate.T, row(gate_bias),
                lane_rep(gate_bias))
            mixed_t = _mlstm_scan(qt, k, vt, ot, gcol, grow, lane_rep(m_head_norm[j]), batch, seq)
            w_out = m_w_out[j].astype(bf16)
            b_out = jnp.zeros((1, d), f32)
        else:
            nq = A_Q_HEADS * A_HEAD_DIM
            nkv = A_KV_HEADS * A_HEAD_DIM
            w_q, w_k, w_v = jnp.split(a_w_in[j], [nq, nq + nkv], axis=1)
            b_q, b_k, b_v = jnp.split(a_b_in[j], [nq, nq + nkv])
            qt, k, vt = _attn_proj(h, row(norm_mix_pre[i]), w_q.T.astype(bf16), w_k.astype(bf16),
                                   w_v.T.astype(bf16), lane_rep(b_q), row(b_k), lane_rep(b_v))
            mixed_t = _swa(qt, k, vt, _sink_rows(a_sinks[j]), batch, seq)
            w_out = a_w_out[j].astype(bf16)
            b_out = row(a_b_out[j])
        h = _out_ffn(h, mixed_t, w_out, b_out, row(norm_mix_post[i]), row(norm_ffn_pre[i]), row(norm_ffn_post[i]),
                     f_w_up[i][:, :dff].astype(bf16), f_w_up[i][:, dff:].astype(bf16),
                     f_conv_w[i].astype(f32), row(f_conv_b[i]), f_w_down[i].astype(bf16), batch, seq)
    return h.reshape(batch, seq, d)
```

```python
import jax
import jax.numpy as jnp
from jax import lax
from jax.experimental import pallas as pl
from jax.experimental.pallas import tpu as pltpu

EPS = 1e-6
LANES = 128
BF16_ROWS = 16

M_HEADS = 8
M_QK_DIM = 64
M_V_DIM = 128
GATE_CAP = 15.0
M_CHUNK = 128

A_HEAD_DIM = 64
A_Q_HEADS = 16
A_KV_HEADS = 2
A_GROUP = A_Q_HEADS // A_KV_HEADS
WINDOW = 128
A_BLOCK = 128

NEG_BIG = -1e30

TOKEN_TILE = 512
SWA_TILE = 256
FF_CHUNK = 256
VMEM_LIMIT = 60 * 1024 * 1024

bf16 = jnp.bfloat16
f32 = jnp.float32


def _dot(a, b):
    return jnp.dot(a, b, preferred_element_type=f32)


def _dot_nt(a, b):
    return lax.dot_general(a, b, (((1,), (1,)), ((), ())), preferred_element_type=f32)


def _dot_tn(a, b):
    return lax.dot_general(a, b, (((0,), (0,)), ((), ())), preferred_element_type=f32)


def _rmsnorm(x, g):
    return x * lax.rsqrt(jnp.mean(x * x, axis=-1, keepdims=True) + EPS) * g


def _split3(x):
    hi = x.astype(bf16)
    r1 = x - hi.astype(f32)
    mid = r1.astype(bf16)
    lo = (r1 - mid.astype(f32)).astype(bf16)
    return hi, mid, lo


def _log_sigmoid(x):
    return jnp.minimum(x, 0.0) - jnp.log1p(jnp.exp(-jnp.abs(x)))


def _tile_lanes(x, reps):
    return jnp.concatenate([x] * reps, axis=1)


def _resident(shape):
    nd = len(shape)
    return pl.BlockSpec(shape, lambda *_: (0,) * nd, pipeline_mode=pl.Buffered(1))


def _params(n_axes):
    return pltpu.CompilerParams(
        dimension_semantics=("arbitrary",) * n_axes, vmem_limit_bytes=VMEM_LIMIT)


def _mlstm_proj_kernel(x_ref, g_ref, wt_ref, wk_ref, wg_ref, wgt_ref, gb_row_ref, gb_col_ref,
                       qt_ref, k_ref, vt_ref, ot_ref, gcol_ref, grow_ref):
    tm = x_ref.shape[0]
    xn = _rmsnorm(x_ref[...], g_ref[...]).astype(bf16)
    nq = M_HEADS * M_QK_DIM
    nv = M_HEADS * M_V_DIM
    qt_ref[...] = (_dot_nt(wt_ref[0:nq, :], xn) * (M_QK_DIM ** -0.5)).astype(bf16)
    vt_ref[...] = _dot_nt(wt_ref[nq:nq + nv, :], xn).astype(bf16)
    ot_ref[...] = _dot_nt(wt_ref[nq + nv:nq + 2 * nv, :], xn).astype(bf16)
    k_ref[...] = _dot(xn, wk_ref[...]).astype(bf16)

    def gates(raw, is_input_gate):
        capped = GATE_CAP * jnp.tanh(raw * (1.0 / GATE_CAP))
        return jnp.where(is_input_gate, capped, _log_sigmoid(capped))

    raw_col = _dot(xn, wg_ref[...]) + gb_row_ref[...]
    gcol_ref[...] = gates(raw_col, lax.broadcasted_iota(jnp.int32, raw_col.shape, 1) < M_HEADS)
    raw_row = _dot_nt(wgt_ref[...], xn) + _tile_lanes(gb_col_ref[...], tm // LANES)
    grow_ref[...] = gates(raw_row, lax.broadcasted_iota(jnp.int32, raw_row.shape, 0) < M_HEADS)


def _mlstm_proj(x, gain, wt_main, w_k, w_gate, wt_gate, gb_row, gb_col):
    t, d = x.shape
    nq = M_HEADS * M_QK_DIM
    nv = M_HEADS * M_V_DIM
    tm = TOKEN_TILE
    row = lambda n: pl.BlockSpec((tm, n), lambda i: (i, 0))
    col = lambda n: pl.BlockSpec((n, tm), lambda i: (0, i))
    return pl.pallas_call(
        _mlstm_proj_kernel,
        grid=(t // tm,),
        in_specs=[row(d), _resident((1, d)), _resident(wt_main.shape), _resident(w_k.shape),
                  _resident(w_gate.shape), _resident(wt_gate.shape), _resident(gb_row.shape),
                  _resident(gb_col.shape)],
        out_specs=[col(nq), row(nq), col(nv), col(nv), row(LANES), col(LANES)],
        out_shape=[jax.ShapeDtypeStruct((nq, t), bf16), jax.ShapeDtypeStruct((t, nq), bf16),
                   jax.ShapeDtypeStruct((nv, t), bf16), jax.ShapeDtypeStruct((nv, t), bf16),
                   jax.ShapeDtypeStruct((t, LANES), f32), jax.ShapeDtypeStruct((LANES, t), f32)],
        compiler_params=_params(1),
        name="mlstm_proj",
    )(x, gain, wt_main, w_k, w_gate, wt_gate, gb_row, gb_col)


def _mlstm_scan_kernel(qt_ref, k_ref, vt_ref, ot_ref, gcol_ref, grow_ref, hn_ref, out_ref, ct_ref, m_ref):
    L = M_CHUNK
    dk, dv = M_QK_DIM, M_V_DIM

    @pl.when(pl.program_id(1) == 0)
    def _():
        ct_ref[...] = jnp.zeros_like(ct_ref)
        m_ref[...] = jnp.zeros_like(m_ref)

    grow = grow_ref[...]
    gcol = gcol_ref[...]
    ri = lax.broadcasted_iota(jnp.int32, (L, L), 0)
    ci = lax.broadcasted_iota(jnp.int32, (L, L), 1)
    upper = ri <= ci
    triu = jnp.where(upper, 1.0, 0.0).astype(bf16)
    tril = jnp.where(ri >= ci, 1.0, 0.0).astype(bf16)
    b_rows = sum(_dot(p, triu) for p in _split3(grow))
    b_cols = sum(_dot(tril, p) for p in _split3(gcol))

    ig_r = grow[0:M_HEADS, :]
    b_r = b_rows[M_HEADS:2 * M_HEADS, :]
    c_r = ig_r - b_r
    m_prev = m_ref[...]
    lane8 = lax.broadcasted_iota(jnp.int32, (M_HEADS, L), 1)
    cmax = c_r
    shift = 1
    while shift < L:
        cmax = jnp.where(lane8 >= shift, jnp.maximum(cmax, pltpu.roll(cmax, shift, axis=1)), cmax)
        shift *= 2
    a_r = jnp.maximum(m_prev, cmax)
    carry_w = jnp.exp(m_prev - a_r)
    clamp_r = jnp.exp(-(a_r + b_r))
    b_last = jnp.broadcast_to(b_r[:, L - 1:L], (M_HEADS, L))
    tail = c_r + b_last
    m_new = jnp.maximum(b_last + m_prev, jnp.max(tail, axis=-1, keepdims=True))
    ws_r = jnp.exp(tail - m_new)
    decay = jnp.exp(b_last + m_prev - m_new)
    m_ref[...] = m_new
    c_cols = gcol - pltpu.roll(b_cols, LANES - M_HEADS, axis=1)

    lane = lax.broadcasted_iota(jnp.int32, (L, LANES), 1)
    low = lane < dk
    ones_rows = jnp.ones((BF16_ROWS, L), bf16)
    zeros_half = jnp.zeros((dk, L), bf16)

    for j in range(M_HEADS // 2):
        kp = k_ref[:, j * LANES:(j + 1) * LANES]
        qtp = qt_ref[j * LANES:(j + 1) * LANES, :]
        q_even = jnp.concatenate([qtp[0:dk, :], zeros_half], axis=0)
        q_odd = jnp.concatenate([zeros_half, qtp[dk:2 * dk, :]], axis=0)
        st = _dot(kp, jnp.concatenate([q_even, q_odd], axis=1))
        ct_pair = ct_ref[j]
        ct_pair_b = ct_pair.astype(bf16)
        vws = []
        for e, q_e in enumerate((q_even, q_odd)):
            h = 2 * j + e
            expo = jnp.where(upper, c_cols[:, h:h + 1] - a_r[h:h + 1, :], NEG_BIG)
            wt = (st[:, e * L:(e + 1) * L] * jnp.exp(expo)).astype(bf16)
            q_w = (q_e.astype(f32) * carry_w[h:h + 1, :]).astype(bf16)
            vext = jnp.concatenate([vt_ref[h * dv:(h + 1) * dv, :], ones_rows], axis=0)
            nd = _dot(jnp.concatenate([vext, ct_pair_b], axis=1), jnp.concatenate([wt, q_w], axis=0))
            den = jnp.maximum(jnp.abs(nd[dv:dv + 1, :]), clamp_r[h:h + 1, :])
            hout = nd[0:dv, :] * (1.0 / den)
            hn = hout * lax.rsqrt(jnp.mean(hout * hout, axis=0, keepdims=True) + EPS)
            hn = hn * hn_ref[h * dv:(h + 1) * dv, :]
            og = ot_ref[h * dv:(h + 1) * dv, :].astype(f32)
            out_ref[h * dv:(h + 1) * dv, :] = (jax.nn.sigmoid(og) * hn).astype(bf16)
            vws.append((vext.astype(f32) * ws_r[h:h + 1, :]).astype(bf16))
        k_split = jnp.concatenate([jnp.where(low, kp, jnp.zeros_like(kp)),
                                   jnp.where(low, jnp.zeros_like(kp), kp)], axis=0)
        decay_pair = jnp.where(low[0:1, :], decay[2 * j:2 * j + 1, :], decay[2 * j + 1:2 * j + 2, :])
        ct_ref[j] = decay_pair * ct_pair + _dot(jnp.concatenate(vws, axis=1), k_split)


def _mlstm_scan(qt, k, vt, ot, gcol, grow, hn_rep, batch, seq):
    L = M_CHUNK
    nq = M_HEADS * M_QK_DIM
    nv = M_HEADS * M_V_DIM
    nc = seq // L
    row = lambda n: pl.BlockSpec((L, n), lambda b, c: (b * nc + c, 0))
    col = lambda n: pl.BlockSpec((n, L), lambda b, c: (0, b * nc + c))
    return pl.pallas_call(
        _mlstm_scan_kernel,
        grid=(batch, nc),
        in_specs=[col(nq), row(nq), col(nv), col(nv), row(LANES), col(LANES), _resident(hn_rep.shape)],
        out_specs=col(nv),
        out_shape=jax.ShapeDtypeStruct((nv, batch * seq), bf16),
        scratch_shapes=[pltpu.VMEM((M_HEADS // 2, M_V_DIM + BF16_ROWS, 2 * M_QK_DIM), f32),
                        pltpu.VMEM((M_HEADS, LANES), f32)],
        compiler_params=_params(2),
        name="mlstm_scan",
    )(qt, k, vt, ot, gcol, grow, hn_rep)


def _attn_proj_kernel(x_ref, g_ref, wqt_ref, wk_ref, wvt_ref, bq_ref, bk_ref, bv_ref, qt_ref, k_ref, vt_ref):
    tm = x_ref.shape[0]
    xn = _rmsnorm(x_ref[...], g_ref[...]).astype(bf16)
    qt = _dot_nt(wqt_ref[...], xn) + _tile_lanes(bq_ref[...], tm // LANES)
    qt_ref[...] = (qt * (A_HEAD_DIM ** -0.5)).astype(bf16)
    k_ref[...] = (_dot(xn, wk_ref[...]) + bk_ref[...]).astype(bf16)
    vt_ref[...] = (_dot_nt(wvt_ref[...], xn) + _tile_lanes(bv_ref[...], tm // LANES)).astype(bf16)


def _attn_proj(x, gain, wqt, wk, wvt, bq_rep, bk, bv_rep):
    t, d = x.shape
    nq = A_Q_HEADS * A_HEAD_DIM
    nkv = A_KV_HEADS * A_HEAD_DIM
    tm = TOKEN_TILE
    row = lambda n: pl.BlockSpec((tm, n), lambda i: (i, 0))
    col = lambda n: pl.BlockSpec((n, tm), lambda i: (0, i))
    return pl.pallas_call(
        _attn_proj_kernel,
        grid=(t // tm,),
        in_specs=[row(d), _resident((1, d)), _resident(wqt.shape), _resident(wk.shape), _resident(wvt.shape),
                  _resident(bq_rep.shape), _resident(bk.shape), _resident(bv_rep.shape)],
        out_specs=[col(nq), row(nkv), col(nkv)],
        out_shape=[jax.ShapeDtypeStruct((nq, t), bf16), jax.ShapeDtypeStruct((t, nkv), bf16),
                   jax.ShapeDtypeStruct((nkv, t), bf16)],
        compiler_params=_params(1),
        name="attn_proj",
    )(x, gain, wqt, wk, wvt, bq_rep, bk, bv_rep)


def _swa_kernel(qt_ref, kc_ref, kp_ref, vtc_ref, vtp_ref, sink_ref, out_ref):
    blk = A_BLOCK
    dh = A_HEAD_DIM
    first = pl.program_id(1) == 0
    ku = lax.broadcasted_iota(jnp.int32, (2 * blk, blk), 0)
    qi = lax.broadcasted_iota(jnp.int32, (2 * blk, blk), 1)
    diff = qi - (ku - blk)
    band = (diff >= 0) & (diff < WINDOW)
    bias = jnp.where(band | (ku == 0), 0.0, NEG_BIG)
    bias_first = jnp.where((band & (ku >= blk)) | (ku == 0), 0.0, NEG_BIG)
    krow = lax.broadcasted_iota(jnp.int32, (2 * blk, LANES), 0)
    klane = lax.broadcasted_iota(jnp.int32, (2 * blk, LANES), 1)
    k_aug = jnp.where((krow == 0) & (klane < 3), 1.0, 0.0).astype(bf16)
    vcol = lax.broadcasted_iota(jnp.int32, (dh, 2 * blk), 1)
    ones_rows = jnp.ones((BF16_ROWS, 2 * blk), bf16)

    for bq in range(SWA_TILE // blk):
        cols = slice(bq * blk, (bq + 1) * blk)
        if bq == 0:
            kcat = jnp.concatenate([kp_ref[...], kc_ref[cols, :]], axis=0)
            vtcat = jnp.concatenate([vtp_ref[...], vtc_ref[:, cols]], axis=1)
            b1 = jnp.where(first, bias_first, bias)
        else:
            kcat = kc_ref[(bq - 1) * blk:(bq + 1) * blk, :]
            vtcat = vtc_ref[:, (bq - 1) * blk:(bq + 1) * blk]
            b1 = bias
        bias_all = _tile_lanes(b1, A_GROUP)
        for g in range(A_KV_HEADS):
            in_group = (klane >= g * dh) & (klane < (g + 1) * dh) & (krow > 0)
            km = jnp.concatenate([jnp.where(in_group, kcat, jnp.zeros_like(kcat)), k_aug], axis=1)
            blocks = []
            for hh in range(A_GROUP):
                h = g * A_GROUP + hh
                pair = qt_ref[(h // 2) * LANES:(h // 2 + 1) * LANES, cols]
                if h % 2 != g:
                    pair = jnp.concatenate([pair[dh:2 * dh, :], pair[0:dh, :]], axis=0)
                blocks.append(pair)
            rhs = jnp.concatenate([jnp.concatenate(blocks, axis=1), sink_ref[g]], axis=0)
            st = _dot(km, rhs) + bias_all
            mx = jnp.max(st, axis=0, keepdims=True)
            p = jnp.exp(st - mx).astype(bf16)
            vt_g = jnp.where(vcol == 0, jnp.zeros((dh, 2 * blk), bf16), vtcat[g * dh:(g + 1) * dh, :])
            oext = _dot(jnp.concatenate([vt_g, ones_rows], axis=0), p)
            o = (oext[0:dh, :] * (1.0 / oext[dh:dh + 1, :])).astype(bf16)
            for hh in range(A_GROUP):
                h = g * A_GROUP + hh
                out_ref[h * dh:(h + 1) * dh, cols] = o[:, hh * blk:(hh + 1) * blk]


def _swa(qt, k, vt, sink_aug, batch, seq):
    blk = A_BLOCK
    tq = SWA_TILE
    nq = A_Q_HEADS * A_HEAD_DIM
    nkv = A_KV_HEADS * A_HEAD_DIM
    nt = seq // tq
    per = tq // blk
    prev_idx = lambda b, i: b * nt * per + jnp.maximum(i * per - 1, 0)
    return pl.pallas_call(
        _swa_kernel,
        grid=(batch, nt),
        in_specs=[pl.BlockSpec((nq, tq), lambda b, i: (0, b * nt + i)),
                  pl.BlockSpec((tq, nkv), lambda b, i: (b * nt + i, 0)),
                  pl.BlockSpec((blk, nkv), lambda b, i: (prev_idx(b, i), 0)),
                  pl.BlockSpec((nkv, tq), lambda b, i: (0, b * nt + i)),
                  pl.BlockSpec((nkv, blk), lambda b, i: (0, prev_idx(b, i))),
                  _resident(sink_aug.shape)],
        out_specs=pl.BlockSpec((nq, tq), lambda b, i: (0, b * nt + i)),
        out_shape=jax.ShapeDtypeStruct((nq, batch * seq), bf16),
        compiler_params=_params(2),
        name="swa",
    )(qt, k, k, vt, vt, sink_aug)


def _sink_rows(sinks):
    parts = jnp.stack(_split3(sinks.astype(f32)), axis=0)
    per_lane = jnp.repeat(parts.reshape(3, A_KV_HEADS, A_GROUP), A_BLOCK, axis=2)
    return jnp.pad(per_lane.transpose(1, 0, 2), ((0, 0), (0, LANES - 3), (0, 0)))


def _out_ffn_kernel(h_ref, at_ref, wo_ref, bo_ref, gpost_ref, gpre_ref, gfpost_ref,
                    wg_ref, wv_ref, cw_ref, cb_ref, wd_ref, out_ref, carry_ref, act_ref):
    tm = h_ref.shape[0]
    dff = wg_ref.shape[1]

    @pl.when(pl.program_id(1) == 0)
    def _():
        carry_ref[...] = jnp.zeros_like(carry_ref)

    z = _dot_tn(at_ref[...], wo_ref[...]) + bo_ref[...]
    h1 = h_ref[...] + _rmsnorm(z, gpost_ref[...])
    xn = _rmsnorm(h1, gpre_ref[...]).astype(bf16)

    top = lax.broadcasted_iota(jnp.int32, (8, FF_CHUNK), 0)
    for c0 in range(0, dff, FF_CHUNK):
        cs = slice(c0, c0 + FF_CHUNK)
        gate = _dot(xn, wg_ref[:, cs])
        val = _dot(xn, wv_ref[:, cs])
        prev = carry_ref[:, cs]
        carry_ref[:, cs] = gate[tm - 8:tm, :]
        shifted = []
        for lag in (1, 2):
            rolled = pltpu.roll(gate, lag, axis=0)
            head = jnp.where(top < lag, pltpu.roll(prev, lag, axis=0), rolled[0:8, :])
            shifted.append(jnp.concatenate([head, rolled[8:, :]], axis=0))
        g1, g2 = shifted
        gc = cb_ref[:, cs] + cw_ref[0:1, cs] * g2 + cw_ref[1:2, cs] * g1 + cw_ref[2:3, cs] * gate
        act_ref[:, cs] = (gc * jax.nn.sigmoid(gc) * val).astype(bf16)
    y = _dot(act_ref[...], wd_ref[...])
    out_ref[...] = h1 + _rmsnorm(y, gfpost_ref[...])


def _out_ffn(h, at, wo, bo, gpost, gpre, gfpost, wg, wv, cw, cb, wd, batch, seq):
    t, d = h.shape
    tm = TOKEN_TILE
    nt = seq // tm
    dff = wg.shape[1]
    row = lambda n: pl.BlockSpec((tm, n), lambda b, i: (b * nt + i, 0))
    return pl.pallas_call(
        _out_ffn_kernel,
        grid=(batch, nt),
        in_specs=[row(d), pl.BlockSpec((at.shape[0], tm), lambda b, i: (0, b * nt + i)),
                  _resident(wo.shape), _resident((1, d)), _resident((1, d)),
                  _resident((1, d)), _resident((1, d)), _resident(wg.shape), _resident(wv.shape),
                  _resident(cw.shape), _resident(cb.shape), _resident(wd.shape)],
        out_specs=row(d),
        out_shape=jax.ShapeDtypeStruct((t, d), f32),
        scratch_shapes=[pltpu.VMEM((8, dff), f32), pltpu.VMEM((tm, dff), bf16)],
        compiler_params=_params(2),
        name="out_ffn",
    )(h, at, wo, bo, gpost, gpre, gfpost, wg, wv, cw, cb, wd)


def kernel(x, m_w_in, m_gate_bias, m_head_norm, m_w_out, a_w_in, a_b_in, a_sinks, a_w_out, a_b_out,
           norm_mix_pre, norm_mix_post, norm_ffn_pre, norm_ffn_post, f_w_up, f_conv_w, f_conv_b, f_w_down):
    batch, seq, d = x.shape
    depth = norm_mix_pre.shape[0]
    dff = f_w_down.shape[1]
    h = x.reshape(batch * seq, d)
    row = lambda vec: vec.reshape(1, -1).astype(f32)
    lane_rep = lambda vec: jnp.broadcast_to(vec.astype(f32)[:, None], (vec.shape[0], LANES))

    for i in range(depth):
        j = i // 2
        if i % 2 == 0:
            nq = M_HEADS * M_QK_DIM
            nv = M_HEADS * M_V_DIM
            w_in = m_w_in[j]
            w_q, w_k, w_v, w_o, w_g = jnp.split(w_in, [nq, 2 * nq, 2 * nq + nv, 2 * nq + 2 * nv], axis=1)
            wt_main = jnp.concatenate([w_q, w_v, w_o], axis=1).T.astype(bf16)
            w_gate = jnp.pad(w_g, ((0, 0), (0, LANES - 2 * M_HEADS))).astype(bf16)
            gate_bias = jnp.pad(m_gate_bias[j].reshape(-1).astype(f32), (0, LANES - 2 * M_HEADS))
            qt, k, vt, ot, gcol, grow = _mlstm_proj(
                h, row(norm_mix_pre[i]), wt_main, w_k.astype(bf16), w_gate, w_gate.T, row(gate_bias),
                lane_rep(gate_bias))
            mixed_t = _mlstm_scan(qt, k, vt, ot, gcol, grow, lane_rep(m_head_norm[j]), batch, seq)
            w_out = m_w_out[j].astype(bf16)
            b_out = jnp.zeros((1, d), f32)
        else:
            nq = A_Q_HEADS * A_HEAD_DIM
            nkv = A_KV_HEADS * A_HEAD_DIM
            w_q, w_k, w_v = jnp.split(a_w_in[j], [nq, nq + nkv], axis=1)
            b_q, b_k, b_v = jnp.split(a_b_in[j], [nq, nq + nkv])
            qt, k, vt = _attn_proj(h, row(norm_mix_pre[i]), w_q.T.astype(bf16), w_k.astype(bf16),
                                   w_v.T.astype(bf16), lane_rep(b_q), row(b_k), lane_rep(b_v))
            mixed_t = _swa(qt, k, vt, _sink_rows(a_sinks[j]), batch, seq)
            w_out = a_w_out[j].astype(bf16)
            b_out = row(a_b_out[j])
        h = _out_ffn(h, mixed_t, w_out, b_out, row(norm_mix_post[i]), row(norm_ffn_pre[i]), row(norm_ffn_post[i]),
                     f_w_up[i][:, :dff].astype(bf16), f_w_up[i][:, dff:].astype(bf16),
                     f_conv_w[i].astype(f32), row(f_conv_b[i]), f_w_down[i].astype(bf16), batch, seq)
    return h.reshape(batch, seq, d)
```

```python
import jax
import jax.numpy as jnp
from jax import lax
from jax.experimental import pallas as pl
from jax.experimental.pallas import tpu as pltpu

EPS = 1e-6
LANES = 128
BF16_ROWS = 16

M_HEADS = 8
M_QK_DIM = 64
M_V_DIM = 128
GATE_CAP = 15.0
M_CHUNK = 128
SCAN_CHUNKS = 4

A_HEAD_DIM = 64
A_Q_HEADS = 16
A_KV_HEADS = 2
A_GROUP = A_Q_HEADS // A_KV_HEADS
WINDOW = 128
A_BLOCK = 128

LOG2E = 1.4426950408889634
NEG_BIG = -1e30

TOKEN_TILE = 512
SWA_TILE = 512
FF_CHUNK = 256
VMEM_LIMIT = 60 * 1024 * 1024

bf16 = jnp.bfloat16
f32 = jnp.float32


def _dot(a, b):
    return jnp.dot(a, b, preferred_element_type=f32)


def _dot_nt(a, b):
    return lax.dot_general(a, b, (((1,), (1,)), ((), ())), preferred_element_type=f32)


def _dot_tn(a, b):
    return lax.dot_general(a, b, (((0,), (0,)), ((), ())), preferred_element_type=f32)


def _rmsnorm(x, g):
    return x * lax.rsqrt(jnp.mean(x * x, axis=-1, keepdims=True) + EPS) * g


def _split3(x):
    hi = x.astype(bf16)
    r1 = x - hi.astype(f32)
    mid = r1.astype(bf16)
    lo = (r1 - mid.astype(f32)).astype(bf16)
    return hi, mid, lo


def _log_sigmoid(x):
    return jnp.minimum(x, 0.0) - jnp.log1p(jnp.exp(-jnp.abs(x)))


def _tile_lanes(x, reps):
    return jnp.concatenate([x] * reps, axis=1)


def _resident(shape):
    nd = len(shape)
    return pl.BlockSpec(shape, lambda *_: (0,) * nd, pipeline_mode=pl.Buffered(1))


def _params(n_axes):
    return pltpu.CompilerParams(
        dimension_semantics=("arbitrary",) * n_axes, vmem_limit_bytes=VMEM_LIMIT)


def _mlstm_proj_kernel(x_ref, g_ref, wt_ref, wk_ref, wg_ref, wgt_ref, gb_row_ref, gb_col_ref,
                       qt_ref, k_ref, vt_ref, ot_ref, ccol_ref, rstat_ref):
    tm = x_ref.shape[0]
    L = M_CHUNK
    H = M_HEADS
    xn = _rmsnorm(x_ref[...], g_ref[...]).astype(bf16)
    nq = M_HEADS * M_QK_DIM
    nv = M_HEADS * M_V_DIM

    def gates(raw, is_input_gate):
        capped = GATE_CAP * jnp.tanh(raw * (1.0 / GATE_CAP))
        return jnp.where(is_input_gate, capped, _log_sigmoid(capped))

    raw_col = _dot(xn, wg_ref[...]) + gb_row_ref[...]
    gcol = gates(raw_col, lax.broadcasted_iota(jnp.int32, raw_col.shape, 1) < H)
    raw_row = _dot_nt(wgt_ref[...], xn) + _tile_lanes(gb_col_ref[...], tm // LANES)
    grow = gates(raw_row, lax.broadcasted_iota(jnp.int32, raw_row.shape, 0) < H)
    triu = jnp.where(lax.broadcasted_iota(jnp.int32, (L, L), 0) <= lax.broadcasted_iota(jnp.int32, (L, L), 1),
                     1.0, 0.0).astype(bf16)
    lane_h = lax.broadcasted_iota(jnp.int32, (H, L), 1)
    for c in range(tm // L):
        tok = slice(c * L, (c + 1) * L)
        g16 = grow[:, tok]
        b_r = sum(_dot(p, triu) for p in _split3(g16))[H:2 * H, :]
        c_r = g16[0:H, :] - b_r
        cmax = c_r
        shift = 1
        while shift < L:
            cmax = jnp.where(lane_h >= shift, jnp.maximum(cmax, pltpu.roll(cmax, shift, axis=1)), cmax)
            shift *= 2
        b_last = jnp.broadcast_to(b_r[:, L - 1:L], (H, L))
        tail = c_r + b_last
        tail_max = jnp.broadcast_to(jnp.max(tail, axis=-1, keepdims=True), (H, L))
        rstat_ref[:, tok] = jnp.concatenate([b_r, cmax, tail, tail_max, b_last], axis=0)
        b_cols = jnp.concatenate([b_r, jnp.zeros((LANES - H, L), f32)], axis=0).T
        ccol_ref[tok, :] = (gcol[tok, :] - b_cols) * LOG2E
    qt_ref[...] = (_dot_nt(wt_ref[0:nq, :], xn) * (M_QK_DIM ** -0.5)).astype(bf16)
    vt_ref[...] = _dot_nt(wt_ref[nq:nq + nv, :], xn).astype(bf16)
    ot_ref[...] = jax.nn.sigmoid(_dot_nt(wt_ref[nq + nv:nq + 2 * nv, :], xn)).astype(bf16)
    k_ref[...] = _dot(xn, wk_ref[...]).astype(bf16)


def _mlstm_proj(x, gain, wt_main, w_k, w_gate, wt_gate, gb_row, gb_col):
    t, d = x.shape
    nq = M_HEADS * M_QK_DIM
    nv = M_HEADS * M_V_DIM
    tm = TOKEN_TILE
    row = lambda n: pl.BlockSpec((tm, n), lambda i: (i, 0))
    col = lambda n: pl.BlockSpec((n, tm), lambda i: (0, i))
    return pl.pallas_call(
        _mlstm_proj_kernel,
        grid=(t // tm,),
        in_specs=[row(d), _resident((1, d)), _resident(wt_main.shape), _resident(w_k.shape),
                  _resident(w_gate.shape), _resident(wt_gate.shape), _resident(gb_row.shape),
                  _resident(gb_col.shape)],
        out_specs=[col(nq), row(nq), col(nv), col(nv), row(LANES), col(5 * M_HEADS)],
        out_shape=[jax.ShapeDtypeStruct((nq, t), bf16), jax.ShapeDtypeStruct((t, nq), bf16),
                   jax.ShapeDtypeStruct((nv, t), bf16), jax.ShapeDtypeStruct((nv, t), bf16),
                   jax.ShapeDtypeStruct((t, LANES), f32), jax.ShapeDtypeStruct((5 * M_HEADS, t), f32)],
        compiler_params=_params(1),
        name="mlstm_proj",
    )(x, gain, wt_main, w_k, w_gate, wt_gate, gb_row, gb_col)


def _mlstm_scan_kernel(qt_ref, k_ref, vt_ref, ot_ref, ccol_ref, rstat_ref, hn_ref, out_ref, ct_ref, m_ref):
    L = M_CHUNK
    dk, dv = M_QK_DIM, M_V_DIM
    pairs = M_HEADS // 2

    @pl.when(pl.program_id(1) == 0)
    def _():
        ct_ref[...] = jnp.zeros_like(ct_ref)
        m_ref[...] = jnp.zeros_like(m_ref)

    H = M_HEADS
    upper = lax.broadcasted_iota(jnp.int32, (L, L), 0) <= lax.broadcasted_iota(jnp.int32, (L, L), 1)
    lane = lax.broadcasted_iota(jnp.int32, (L, LANES), 1)
    low = lane < dk
    ones_rows = jnp.ones((BF16_ROWS, L), bf16)
    zeros_half = jnp.zeros((dk, L), bf16)

    stats = []
    for c in range(SCAN_CHUNKS):
        tok = slice(c * L, (c + 1) * L)
        stats.append(tuple(rstat_ref[i * H:(i + 1) * H, tok] for i in range(5)) + (ccol_ref[tok, :],))

    m = m_ref[...]
    m_prevs = []
    for c in range(SCAN_CHUNKS):
        m_prevs.append(m)
        m = jnp.maximum(stats[c][4] + m, stats[c][3])
    m_prevs.append(m)
    m_ref[...] = m

    def kq(c, j):
        tok = slice(c * L, (c + 1) * L)
        kp = k_ref[tok, j * LANES:(j + 1) * LANES]
        qtp = qt_ref[j * LANES:(j + 1) * LANES, tok]
        q_even = jnp.concatenate([qtp[0:dk, :], zeros_half], axis=0)
        q_odd = jnp.concatenate([zeros_half, qtp[dk:2 * dk, :]], axis=0)
        return kp, (q_even, q_odd), _dot(kp, jnp.concatenate([q_even, q_odd], axis=1))

    ct = [ct_ref[j] for j in range(pairs)]
    units = [(c, j) for c in range(SCAN_CHUNKS) for j in range(pairs)]
    nxt = kq(*units[0])
    for u, (c, j) in enumerate(units):
        kp, q_eo, st = nxt
        if u + 1 < len(units):
            nxt = kq(*units[u + 1])
        tok = slice(c * L, (c + 1) * L)
        b_r, cmax, tail, _, b_last, c_cols = stats[c]
        m_prev, m_new = m_prevs[c], m_prevs[c + 1]
        a_r = jnp.maximum(m_prev, cmax)
        a_r2 = a_r * LOG2E
        carry_w = jnp.exp(m_prev - a_r).astype(bf16)
        clamp_r = jnp.exp(-(a_r + b_r))
        ws_r = jnp.exp(tail - m_new).astype(bf16)
        decay = jnp.exp(b_last + m_prev - m_new)
        ct_pair_b = ct[j].astype(bf16)
        vws = []
        for e, q_e in enumerate(q_eo):
            h = 2 * j + e
            expo = jnp.where(upper, c_cols[:, h:h + 1] - a_r2[h:h + 1, :], NEG_BIG)
            wt = (st[:, e * L:(e + 1) * L] * jnp.exp2(expo)).astype(bf16)
            q_w = q_e * carry_w[h:h + 1, :]
            vext = jnp.concatenate([vt_ref[h * dv:(h + 1) * dv, tok], ones_rows], axis=0)
            nd = _dot(jnp.concatenate([vext, ct_pair_b], axis=1), jnp.concatenate([wt, q_w], axis=0))
            den = jnp.maximum(jnp.abs(nd[dv:dv + 1, :]), clamp_r[h:h + 1, :])
            inv = 1.0 / den
            num = nd[0:dv, :]
            scale = inv * lax.rsqrt(inv * inv * jnp.mean(num * num, axis=0, keepdims=True) + EPS)
            hn = num * scale * hn_ref[h * dv:(h + 1) * dv, :]
            gate_o = ot_ref[h * dv:(h + 1) * dv, tok].astype(f32)
            out_ref[h * dv:(h + 1) * dv, tok] = (gate_o * hn).astype(bf16)
            vws.append(vext * ws_r[h:h + 1, :])
        k_split = jnp.concatenate([jnp.where(low, kp, jnp.zeros_like(kp)),
                                   jnp.where(low, jnp.zeros_like(kp), kp)], axis=0)
        decay_pair = jnp.where(low[0:1, :], decay[2 * j:2 * j + 1, :], decay[2 * j + 1:2 * j + 2, :])
        ct[j] = decay_pair * ct[j] + _dot(jnp.concatenate(vws, axis=1), k_split)
    for j in range(pairs):
        ct_ref[j] = ct[j]


def _mlstm_scan(qt, k, vt, ot, ccol, rstat, hn_rep, batch, seq):
    tm = SCAN_CHUNKS * M_CHUNK
    nq = M_HEADS * M_QK_DIM
    nv = M_HEADS * M_V_DIM
    nt = seq // tm
    row = lambda n: pl.BlockSpec((tm, n), lambda b, i: (b * nt + i, 0))
    col = lambda n: pl.BlockSpec((n, tm), lambda b, i: (0, b * nt + i))
    return pl.pallas_call(
        _mlstm_scan_kernel,
        grid=(batch, nt),
        in_specs=[col(nq), row(nq), col(nv), col(nv), row(LANES), col(5 * M_HEADS), _resident(hn_rep.shape)],
        out_specs=col(nv),
        out_shape=jax.ShapeDtypeStruct((nv, batch * seq), bf16),
        scratch_shapes=[pltpu.VMEM((M_HEADS // 2, M_V_DIM + BF16_ROWS, 2 * M_QK_DIM), f32),
                        pltpu.VMEM((M_HEADS, LANES), f32)],
        compiler_params=_params(2),
        name="mlstm_scan",
    )(qt, k, vt, ot, ccol, rstat, hn_rep)


def _attn_proj_kernel(x_ref, g_ref, wqt_ref, wk_ref, wvt_ref, bq_ref, bk_ref, bv_ref, qt_ref, k_ref, vt_ref):
    tm = x_ref.shape[0]
    xn = _rmsnorm(x_ref[...], g_ref[...]).astype(bf16)
    qt = _dot_nt(wqt_ref[...], xn) + _tile_lanes(bq_ref[...], tm // LANES)
    qt_ref[...] = (qt * (A_HEAD_DIM ** -0.5 * LOG2E)).astype(bf16)
    k_ref[...] = (_dot(xn, wk_ref[...]) + bk_ref[...]).astype(bf16)
    vt_ref[...] = (_dot_nt(wvt_ref[...], xn) + _tile_lanes(bv_ref[...], tm // LANES)).astype(bf16)


def _attn_proj(x, gain, wqt, wk, wvt, bq_rep, bk, bv_rep):
    t, d = x.shape
    nq = A_Q_HEADS * A_HEAD_DIM
    nkv = A_KV_HEADS * A_HEAD_DIM
    tm = TOKEN_TILE
    row = lambda n: pl.BlockSpec((tm, n), lambda i: (i, 0))
    col = lambda n: pl.BlockSpec((n, tm), lambda i: (0, i))
    return pl.pallas_call(
        _attn_proj_kernel,
        grid=(t // tm,),
        in_specs=[row(d), _resident((1, d)), _resident(wqt.shape), _resident(wk.shape), _resident(wvt.shape),
                  _resident(bq_rep.shape), _resident(bk.shape), _resident(bv_rep.shape)],
        out_specs=[col(nq), row(nkv), col(nkv)],
        out_shape=[jax.ShapeDtypeStruct((nq, t), bf16), jax.ShapeDtypeStruct((t, nkv), bf16),
                   jax.ShapeDtypeStruct((nkv, t), bf16)],
        compiler_params=_params(1),
        name="attn_proj",
    )(x, gain, wqt, wk, wvt, bq_rep, bk, bv_rep)


def _swa_kernel(qt_ref, kc_ref, kp_ref, vtc_ref, vtp_ref, sink_ref, out_ref):
    blk = A_BLOCK
    dh = A_HEAD_DIM
    first = pl.program_id(1) == 0
    ku = lax.broadcasted_iota(jnp.int32, (2 * blk, blk), 0)
    qi = lax.broadcasted_iota(jnp.int32, (2 * blk, blk), 1)
    diff = qi - (ku - blk)
    band = (diff >= 0) & (diff < WINDOW)
    bias = jnp.where(band | (ku == 0), 0.0, NEG_BIG)
    bias_first = jnp.where((band & (ku >= blk)) | (ku == 0), 0.0, NEG_BIG)
    krow = lax.broadcasted_iota(jnp.int32, (2 * blk, LANES), 0)
    klane = lax.broadcasted_iota(jnp.int32, (2 * blk, LANES), 1)
    k_aug = jnp.where((krow == 0) & (klane < 3), 1.0, 0.0).astype(bf16)
    vcol = lax.broadcasted_iota(jnp.int32, (dh, 2 * blk), 1)
    ones_rows = jnp.ones((BF16_ROWS, 2 * blk), bf16)

    def logits(bq, g):
        cols = slice(bq * blk, (bq + 1) * blk)
        if bq == 0:
            kcat = jnp.concatenate([kp_ref[...], kc_ref[cols, :]], axis=0)
            b1 = jnp.where(first, bias_first, bias)
        else:
            kcat = kc_ref[(bq - 1) * blk:(bq + 1) * blk, :]
            b1 = bias
        in_group = (klane >= g * dh) & (klane < (g + 1) * dh) & (krow > 0)
        km = jnp.concatenate([jnp.where(in_group, kcat, jnp.zeros_like(kcat)), k_aug], axis=1)
        blocks = []
        for hh in range(A_GROUP):
            h = g * A_GROUP + hh
            pair = qt_ref[(h // 2) * LANES:(h // 2 + 1) * LANES, cols]
            if h % 2 != g:
                pair = jnp.concatenate([pair[dh:2 * dh, :], pair[0:dh, :]], axis=0)
            blocks.append(pair)
        rhs = jnp.concatenate([jnp.concatenate(blocks, axis=1), sink_ref[g]], axis=0)
        return _dot(km, rhs) + _tile_lanes(b1, A_GROUP)

    def finish(bq, g, st):
        cols = slice(bq * blk, (bq + 1) * blk)
        if bq == 0:
            vtcat = jnp.concatenate([vtp_ref[...], vtc_ref[:, cols]], axis=1)
        else:
            vtcat = vtc_ref[:, (bq - 1) * blk:(bq + 1) * blk]
        p = jnp.exp2(st - jnp.max(st, axis=0, keepdims=True)).astype(bf16)
        vt_g = jnp.where(vcol == 0, jnp.zeros((dh, 2 * blk), bf16), vtcat[g * dh:(g + 1) * dh, :])
        oext = _dot(jnp.concatenate([vt_g, ones_rows], axis=0), p)
        o = (oext[0:dh, :] * (1.0 / oext[dh:dh + 1, :])).astype(bf16)
        for hh in range(A_GROUP):
            h = g * A_GROUP + hh
            out_ref[h * dh:(h + 1) * dh, cols] = o[:, hh * blk:(hh + 1) * blk]

    units = [(bq, g) for bq in range(SWA_TILE // blk) for g in range(A_KV_HEADS)]
    nxt = logits(*units[0])
    for u, unit in enumerate(units):
        st = nxt
        if u + 1 < len(units):
            nxt = logits(*units[u + 1])
        finish(*unit, st)


def _swa(qt, k, vt, sink_aug, batch, seq):
    blk = A_BLOCK
    tq = SWA_TILE
    nq = A_Q_HEADS * A_HEAD_DIM
    nkv = A_KV_HEADS * A_HEAD_DIM
    nt = seq // tq
    per = tq // blk
    prev_idx = lambda b, i: b * nt * per + jnp.maximum(i * per - 1, 0)
    return pl.pallas_call(
        _swa_kernel,
        grid=(batch, nt),
        in_specs=[pl.BlockSpec((nq, tq), lambda b, i: (0, b * nt + i)),
                  pl.BlockSpec((tq, nkv), lambda b, i: (b * nt + i, 0)),
                  pl.BlockSpec((blk, nkv), lambda b, i: (prev_idx(b, i), 0)),
                  pl.BlockSpec((nkv, tq), lambda b, i: (0, b * nt + i)),
                  pl.BlockSpec((nkv, blk), lambda b, i: (0, prev_idx(b, i))),
                  _resident(sink_aug.shape)],
        out_specs=pl.BlockSpec((nq, tq), lambda b, i: (0, b * nt + i)),
        out_shape=jax.ShapeDtypeStruct((nq, batch * seq), bf16),
        compiler_params=_params(2),
        name="swa",
    )(qt, k, k, vt, vt, sink_aug)


def _sink_rows(sinks):
    parts = jnp.stack(_split3(sinks.astype(f32) * LOG2E), axis=0)
    per_lane = jnp.repeat(parts.reshape(3, A_KV_HEADS, A_GROUP), A_BLOCK, axis=2)
    return jnp.pad(per_lane.transpose(1, 0, 2), ((0, 0), (0, LANES - 3), (0, 0)))


def _out_ffn_kernel(h_ref, at_ref, wo_ref, bo_ref, gpost_ref, gpre_ref, gfpost_ref,
                    wg_ref, wv_ref, cw_ref, cb_ref, wd_ref, out_ref, carry_ref, act_ref):
    tm = h_ref.shape[0]
    dff = wg_ref.shape[1]

    @pl.when(pl.program_id(1) == 0)
    def _():
        carry_ref[...] = jnp.zeros_like(carry_ref)

    z = _dot_tn(at_ref[...], wo_ref[...]) + bo_ref[...]
    h1 = h_ref[...] + _rmsnorm(z, gpost_ref[...])
    xn = _rmsnorm(h1, gpre_ref[...]).astype(bf16)

    top = lax.broadcasted_iota(jnp.int32, (8, FF_CHUNK), 0)
    for c0 in range(0, dff, FF_CHUNK):
        cs = slice(c0, c0 + FF_CHUNK)
        gate = _dot(xn, wg_ref[:, cs])
        val = _dot(xn, wv_ref[:, cs])
        prev = carry_ref[:, cs]
        carry_ref[:, cs] = gate[tm - 8:tm, :]
        shifted = []
        for lag in (1, 2):
            rolled = pltpu.roll(gate, lag, axis=0)
            head = jnp.where(top < lag, pltpu.roll(prev, lag, axis=0), rolled[0:8, :])
            shifted.append(jnp.concatenate([head, rolled[8:, :]], axis=0))
        g1, g2 = shifted
        gc = cb_ref[:, cs] + cw_ref[0:1, cs] * g2 + cw_ref[1:2, cs] * g1 + cw_ref[2:3, cs] * gate
        act_ref[:, cs] = (gc * jax.nn.sigmoid(gc) * val).astype(bf16)
    y = _dot(act_ref[...], wd_ref[...])
    out_ref[...] = h1 + _rmsnorm(y, gfpost_ref[...])


def _out_ffn(h, at, wo, bo, gpost, gpre, gfpost, wg, wv, cw, cb, wd, batch, seq):
    t, d = h.shape
    tm = TOKEN_TILE
    nt = seq // tm
    dff = wg.shape[1]
    row = lambda n: pl.BlockSpec((tm, n), lambda b, i: (b * nt + i, 0))
    return pl.pallas_call(
        _out_ffn_kernel,
        grid=(batch, nt),
        in_specs=[row(d), pl.BlockSpec((at.shape[0], tm), lambda b, i: (0, b * nt + i)),
                  _resident(wo.shape), _resident((1, d)), _resident((1, d)),
                  _resident((1, d)), _resident((1, d)), _resident(wg.shape), _resident(wv.shape),
                  _resident(cw.shape), _resident(cb.shape), _resident(wd.shape)],
        out_specs=row(d),
        out_shape=jax.ShapeDtypeStruct((t, d), f32),
        scratch_shapes=[pltpu.VMEM((8, dff), f32), pltpu.VMEM((tm, dff), bf16)],
        compiler_params=_params(2),
        name="out_ffn",
    )(h, at, wo, bo, gpost, gpre, gfpost, wg, wv, cw, cb, wd)


def kernel(x, m_w_in, m_gate_bias, m_head_norm, m_w_out, a_w_in, a_b_in, a_sinks, a_w_out, a_b_out,
           norm_mix_pre, norm_mix_post, norm_ffn_pre, norm_ffn_post, f_w_up, f_conv_w, f_conv_b, f_w_down):
    batch, seq, d = x.shape
    depth = norm_mix_pre.shape[0]
    dff = f_w_down.shape[1]
    h = x.reshape(batch * seq, d)
    row = lambda vec: vec.reshape(1, -1).astype(f32)
    lane_rep = lambda vec: jnp.broadcast_to(vec.astype(f32)[:, None], (vec.shape[0], LANES))

    for i in range(depth):
        j = i // 2
        if i % 2 == 0:
            nq = M_HEADS * M_QK_DIM
            nv = M_HEADS * M_V_DIM
            w_in = m_w_in[j]
            w_q, w_k, w_v, w_o, w_g = jnp.split(w_in, [nq, 2 * nq, 2 * nq + nv, 2 * nq + 2 * nv], axis=1)
            wt_main = jnp.concatenate([w_q, w_v, w_o], axis=1).T.astype(bf16)
            w_gate = jnp.pad(w_g, ((0, 0), (0, LANES - 2 * M_HEADS))).astype(bf16)
            gate_bias = jnp.pad(m_gate_bias[j].reshape(-1).astype(f32), (0, LANES - 2 * M_HEADS))
            qt, k, vt, ot, ccol, rstat = _mlstm_proj(
                h, row(norm_mix_pre[i]), wt_main, w_k.astype(bf16), w_gate, w_g.T.astype(bf16), row(gate_bias),
                lane_rep(gate_bias[:2 * M_HEADS]))
            mixed_t = _mlstm_scan(qt, k, vt, ot, ccol, rstat, lane_rep(m_head_norm[j]), batch, seq)
            w_out = m_w_out[j].astype(bf16)
            b_out = jnp.zeros((1, d), f32)
        else:
            nq = A_Q_HEADS * A_HEAD_DIM
            nkv = A_KV_HEADS * A_HEAD_DIM
            w_q, w_k, w_v = jnp.split(a_w_in[j], [nq, nq + nkv], axis=1)
            b_q, b_k, b_v = jnp.split(a_b_in[j], [nq, nq + nkv])
            qt, k, vt = _attn_proj(h, row(norm_mix_pre[i]), w_q.T.astype(bf16), w_k.astype(bf16),
                                   w_v.T.astype(bf16), lane_rep(b_q), row(b_k), lane_rep(b_v))
            mixed_t = _swa(qt, k, vt, _sink_rows(a_sinks[j]), batch, seq)
            w_out = a_w_out[j].astype(bf16)
            b_out = row(a_b_out[j])
        h = _out_ffn(h, mixed_t, w_out, b_out, row(norm_mix_post[i]), row(norm_ffn_pre[i]), row(norm_ffn_post[i]),
                     f_w_up[i][:, :dff].astype(bf16), f_w_up[i][:, dff:].astype(bf16),
                     f_conv_w[i].astype(f32), row(f_conv_b[i]), f_w_down[i].astype(bf16), batch, seq)
    return h.reshape(batch, seq, d)
```

```python
import functools

import jax
import jax.numpy as jnp
from jax import lax
from jax.experimental import pallas as pl
from jax.experimental.pallas import tpu as pltpu

EPS = 1e-6
LANES = 128
BF16_ROWS = 16

M_HEADS = 8
M_QK_DIM = 64
M_V_DIM = 128
GATE_CAP = 15.0
M_CHUNK = 128
SCAN_CHUNKS = 4

A_HEAD_DIM = 64
A_Q_HEADS = 16
A_KV_HEADS = 2
A_GROUP = A_Q_HEADS // A_KV_HEADS
WINDOW = 128
A_BLOCK = 128

LOG2E = 1.4426950408889634
NEG_BIG = -1e30

TOKEN_TILE = 512
SWA_TILE = 512
FF_CHUNK = 256
OUT_CHUNK = 256
FRONT_AFTER_CHUNKS = 2
VMEM_LIMIT = 60 * 1024 * 1024

bf16 = jnp.bfloat16
f32 = jnp.float32


def _dot(a, b):
    return jnp.dot(a, b, preferred_element_type=f32)


def _dot_nt(a, b):
    return lax.dot_general(a, b, (((1,), (1,)), ((), ())), preferred_element_type=f32)


def _dot_tn(a, b):
    return lax.dot_general(a, b, (((0,), (0,)), ((), ())), preferred_element_type=f32)


def _rmsnorm(x, g):
    return x * lax.rsqrt(jnp.mean(x * x, axis=-1, keepdims=True) + EPS) * g


def _split3(x):
    hi = x.astype(bf16)
    r1 = x - hi.astype(f32)
    mid = r1.astype(bf16)
    lo = (r1 - mid.astype(f32)).astype(bf16)
    return hi, mid, lo


def _log_sigmoid(x):
    return jnp.minimum(x, 0.0) - jnp.log1p(jnp.exp(-jnp.abs(x)))


def _tile_lanes(x, reps):
    return jnp.concatenate([x] * reps, axis=1)


def _resident(shape):
    nd = len(shape)
    return pl.BlockSpec(shape, lambda *_: (0,) * nd, pipeline_mode=pl.Buffered(1))


def _params(n_axes):
    return pltpu.CompilerParams(
        dimension_semantics=("arbitrary",) * n_axes, vmem_limit_bytes=VMEM_LIMIT)


def _mlstm_proj_kernel(x_ref, g_ref, wt_ref, wk_ref, wg_ref, wgt_ref, gb_row_ref, gb_col_ref,
                       qt_ref, k_ref, vt_ref, ot_ref, ccol_ref, rstat_ref):
    tm = x_ref.shape[0]
    L = M_CHUNK
    H = M_HEADS
    xn = _rmsnorm(x_ref[...], g_ref[...]).astype(bf16)
    nq = M_HEADS * M_QK_DIM
    nv = M_HEADS * M_V_DIM

    def gates(raw, is_input_gate):
        capped = GATE_CAP * jnp.tanh(raw * (1.0 / GATE_CAP))
        return jnp.where(is_input_gate, capped, _log_sigmoid(capped))

    raw_col = _dot(xn, wg_ref[...]) + gb_row_ref[...]
    gcol = gates(raw_col, lax.broadcasted_iota(jnp.int32, raw_col.shape, 1) < H)
    raw_row = _dot_nt(wgt_ref[...], xn) + _tile_lanes(gb_col_ref[...], tm // LANES)
    grow = gates(raw_row, lax.broadcasted_iota(jnp.int32, raw_row.shape, 0) < H)
    triu = jnp.where(lax.broadcasted_iota(jnp.int32, (L, L), 0) <= lax.broadcasted_iota(jnp.int32, (L, L), 1),
                     1.0, 0.0).astype(bf16)
    lane_h = lax.broadcasted_iota(jnp.int32, (H, L), 1)
    for c in range(tm // L):
        tok = slice(c * L, (c + 1) * L)
        g16 = grow[:, tok]
        b_r = sum(_dot(p, triu) for p in _split3(g16))[H:2 * H, :]
        c_r = g16[0:H, :] - b_r
        cmax = c_r
        shift = 1
        while shift < L:
            cmax = jnp.where(lane_h >= shift, jnp.maximum(cmax, pltpu.roll(cmax, shift, axis=1)), cmax)
            shift *= 2
        b_last = jnp.broadcast_to(b_r[:, L - 1:L], (H, L))
        tail = c_r + b_last
        tail_max = jnp.broadcast_to(jnp.max(tail, axis=-1, keepdims=True), (H, L))
        rstat_ref[:, tok] = jnp.concatenate([b_r, cmax, tail, tail_max, b_last], axis=0)
        b_cols = jnp.concatenate([b_r, jnp.zeros((LANES - H, L), f32)], axis=0).T
        ccol_ref[tok, :] = (gcol[tok, :] - b_cols) * LOG2E
    qt_ref[...] = (_dot_nt(wt_ref[0:nq, :], xn) * (M_QK_DIM ** -0.5)).astype(bf16)
    vt_ref[...] = _dot_nt(wt_ref[nq:nq + nv, :], xn).astype(bf16)
    ot_ref[...] = jax.nn.sigmoid(_dot_nt(wt_ref[nq + nv:nq + 2 * nv, :], xn)).astype(bf16)
    k_ref[...] = _dot(xn, wk_ref[...]).astype(bf16)


def _mlstm_proj(x, gain, wt_main, w_k, w_gate, wt_gate, gb_row, gb_col):
    t, d = x.shape
    nq = M_HEADS * M_QK_DIM
    nv = M_HEADS * M_V_DIM
    tm = TOKEN_TILE
    row = lambda n: pl.BlockSpec((tm, n), lambda i: (i, 0))
    col = lambda n: pl.BlockSpec((n, tm), lambda i: (0, i))
    return pl.pallas_call(
        _mlstm_proj_kernel,
        grid=(t // tm,),
        in_specs=[row(d), _resident((1, d)), _resident(wt_main.shape), _resident(w_k.shape),
                  _resident(w_gate.shape), _resident(wt_gate.shape), _resident(gb_row.shape),
                  _resident(gb_col.shape)],
        out_specs=[col(nq), row(nq), col(nv), col(nv), row(LANES), col(5 * M_HEADS)],
        out_shape=[jax.ShapeDtypeStruct((nq, t), bf16), jax.ShapeDtypeStruct((t, nq), bf16),
                   jax.ShapeDtypeStruct((nv, t), bf16), jax.ShapeDtypeStruct((nv, t), bf16),
                   jax.ShapeDtypeStruct((t, LANES), f32), jax.ShapeDtypeStruct((5 * M_HEADS, t), f32)],
        compiler_params=_params(1),
        name="mlstm_proj",
    )(x, gain, wt_main, w_k, w_gate, wt_gate, gb_row, gb_col)


def _mlstm_scan_kernel(qt_ref, k_ref, vt_ref, ot_ref, ccol_ref, rstat_ref, hn_ref, out_ref, ct_ref, m_ref):
    L = M_CHUNK
    dk, dv = M_QK_DIM, M_V_DIM
    pairs = M_HEADS // 2

    @pl.when(pl.program_id(1) == 0)
    def _():
        ct_ref[...] = jnp.zeros_like(ct_ref)
        m_ref[...] = jnp.zeros_like(m_ref)

    H = M_HEADS
    upper = lax.broadcasted_iota(jnp.int32, (L, L), 0) <= lax.broadcasted_iota(jnp.int32, (L, L), 1)
    lane = lax.broadcasted_iota(jnp.int32, (L, LANES), 1)
    low = lane < dk
    ones_rows = jnp.ones((BF16_ROWS, L), bf16)
    zeros_half = jnp.zeros((dk, L), bf16)

    stats = []
    for c in range(SCAN_CHUNKS):
        tok = slice(c * L, (c + 1) * L)
        stats.append(tuple(rstat_ref[i * H:(i + 1) * H, tok] for i in range(5)) + (ccol_ref[tok, :],))

    m = m_ref[...]
    m_prevs = []
    for c in range(SCAN_CHUNKS):
        m_prevs.append(m)
        m = jnp.maximum(stats[c][4] + m, stats[c][3])
    m_prevs.append(m)
    m_ref[...] = m

    def kq(c, j):
        tok = slice(c * L, (c + 1) * L)
        kp = k_ref[tok, j * LANES:(j + 1) * LANES]
        qtp = qt_ref[j * LANES:(j + 1) * LANES, tok]
        q_even = jnp.concatenate([qtp[0:dk, :], zeros_half], axis=0)
        q_odd = jnp.concatenate([zeros_half, qtp[dk:2 * dk, :]], axis=0)
        return kp, (q_even, q_odd), _dot(kp, jnp.concatenate([q_even, q_odd], axis=1))

    ct = [ct_ref[j] for j in range(pairs)]
    units = [(c, j) for c in range(SCAN_CHUNKS) for j in range(pairs)]
    nxt = kq(*units[0])
    for u, (c, j) in enumerate(units):
        kp, q_eo, st = nxt
        if u + 1 < len(units):
            nxt = kq(*units[u + 1])
        tok = slice(c * L, (c + 1) * L)
        b_r, cmax, tail, _, b_last, c_cols = stats[c]
        m_prev, m_new = m_prevs[c], m_prevs[c + 1]
        a_r = jnp.maximum(m_prev, cmax)
        a_r2 = a_r * LOG2E
        carry_w = jnp.exp(m_prev - a_r).astype(bf16)
        clamp_r = jnp.exp(-(a_r + b_r))
        ws_r = jnp.exp(tail - m_new).astype(bf16)
        decay = jnp.exp(b_last + m_prev - m_new)
        ct_pair_b = ct[j].astype(bf16)
        vws = []
        for e, q_e in enumerate(q_eo):
            h = 2 * j + e
            expo = jnp.where(upper, c_cols[:, h:h + 1] - a_r2[h:h + 1, :], NEG_BIG)
            wt = (st[:, e * L:(e + 1) * L] * jnp.exp2(expo)).astype(bf16)
            q_w = q_e * carry_w[h:h + 1, :]
            vext = jnp.concatenate([vt_ref[h * dv:(h + 1) * dv, tok], ones_rows], axis=0)
            nd = _dot(jnp.concatenate([vext, ct_pair_b], axis=1), jnp.concatenate([wt, q_w], axis=0))
            den = jnp.maximum(jnp.abs(nd[dv:dv + 1, :]), clamp_r[h:h + 1, :])
            inv = 1.0 / den
            num = nd[0:dv, :]
            scale = inv * lax.rsqrt(inv * inv * jnp.mean(num * num, axis=0, keepdims=True) + EPS)
            hn = num * scale * hn_ref[h * dv:(h + 1) * dv, :]
            gate_o = ot_ref[h * dv:(h + 1) * dv, tok].astype(f32)
            out_ref[h * dv:(h + 1) * dv, tok] = (gate_o * hn).astype(bf16)
            vws.append(vext * ws_r[h:h + 1, :])
        k_split = jnp.concatenate([jnp.where(low, kp, jnp.zeros_like(kp)),
                                   jnp.where(low, jnp.zeros_like(kp), kp)], axis=0)
        decay_pair = jnp.where(low[0:1, :], decay[2 * j:2 * j + 1, :], decay[2 * j + 1:2 * j + 2, :])
        ct[j] = decay_pair * ct[j] + _dot(jnp.concatenate(vws, axis=1), k_split)
    for j in range(pairs):
        ct_ref[j] = ct[j]


def _mlstm_scan(qt, k, vt, ot, ccol, rstat, hn_rep, batch, seq):
    tm = SCAN_CHUNKS * M_CHUNK
    nq = M_HEADS * M_QK_DIM
    nv = M_HEADS * M_V_DIM
    nt = seq // tm
    row = lambda n: pl.BlockSpec((tm, n), lambda b, i: (b * nt + i, 0))
    col = lambda n: pl.BlockSpec((n, tm), lambda b, i: (0, b * nt + i))
    return pl.pallas_call(
        _mlstm_scan_kernel,
        grid=(batch, nt),
        in_specs=[col(nq), row(nq), col(nv), col(nv), row(LANES), col(5 * M_HEADS), _resident(hn_rep.shape)],
        out_specs=col(nv),
        out_shape=jax.ShapeDtypeStruct((nv, batch * seq), bf16),
        scratch_shapes=[pltpu.VMEM((M_HEADS // 2, M_V_DIM + BF16_ROWS, 2 * M_QK_DIM), f32),
                        pltpu.VMEM((M_HEADS, LANES), f32)],
        compiler_params=_params(2),
        name="mlstm_scan",
    )(qt, k, vt, ot, ccol, rstat, hn_rep)


def _attn_proj_kernel(x_ref, g_ref, wqt_ref, wk_ref, wvt_ref, bq_ref, bk_ref, bv_ref, qt_ref, k_ref, vt_ref):
    tm = x_ref.shape[0]
    xn = _rmsnorm(x_ref[...], g_ref[...]).astype(bf16)
    qt = _dot_nt(wqt_ref[...], xn) + _tile_lanes(bq_ref[...], tm // LANES)
    qt_ref[...] = (qt * (A_HEAD_DIM ** -0.5 * LOG2E)).astype(bf16)
    k_ref[...] = (_dot(xn, wk_ref[...]) + bk_ref[...]).astype(bf16)
    vt_ref[...] = (_dot_nt(wvt_ref[...], xn) + _tile_lanes(bv_ref[...], tm // LANES)).astype(bf16)


def _attn_proj(x, gain, wqt, wk, wvt, bq_rep, bk, bv_rep):
    t, d = x.shape
    nq = A_Q_HEADS * A_HEAD_DIM
    nkv = A_KV_HEADS * A_HEAD_DIM
    tm = TOKEN_TILE
    row = lambda n: pl.BlockSpec((tm, n), lambda i: (i, 0))
    col = lambda n: pl.BlockSpec((n, tm), lambda i: (0, i))
    return pl.pallas_call(
        _attn_proj_kernel,
        grid=(t // tm,),
        in_specs=[row(d), _resident((1, d)), _resident(wqt.shape), _resident(wk.shape), _resident(wvt.shape),
                  _resident(bq_rep.shape), _resident(bk.shape), _resident(bv_rep.shape)],
        out_specs=[col(nq), row(nkv), col(nkv)],
        out_shape=[jax.ShapeDtypeStruct((nq, t), bf16), jax.ShapeDtypeStruct((t, nkv), bf16),
                   jax.ShapeDtypeStruct((nkv, t), bf16)],
        compiler_params=_params(1),
        name="attn_proj",
    )(x, gain, wqt, wk, wvt, bq_rep, bk, bv_rep)


def _swa_kernel(qt_ref, kc_ref, kp_ref, vtc_ref, vtp_ref, sink_ref, out_ref):
    blk = A_BLOCK
    dh = A_HEAD_DIM
    first = pl.program_id(1) == 0
    ku = lax.broadcasted_iota(jnp.int32, (2 * blk, blk), 0)
    qi = lax.broadcasted_iota(jnp.int32, (2 * blk, blk), 1)
    diff = qi - (ku - blk)
    band = (diff >= 0) & (diff < WINDOW)
    bias = jnp.where(band | (ku == 0), 0.0, NEG_BIG)
    bias_first = jnp.where((band & (ku >= blk)) | (ku == 0), 0.0, NEG_BIG)
    krow = lax.broadcasted_iota(jnp.int32, (2 * blk, LANES), 0)
    klane = lax.broadcasted_iota(jnp.int32, (2 * blk, LANES), 1)
    k_aug = jnp.where((krow == 0) & (klane < 3), 1.0, 0.0).astype(bf16)
    vcol = lax.broadcasted_iota(jnp.int32, (dh, 2 * blk), 1)
    ones_rows = jnp.ones((BF16_ROWS, 2 * blk), bf16)

    def logits(bq, g):
        cols = slice(bq * blk, (bq + 1) * blk)
        if bq == 0:
            kcat = jnp.concatenate([kp_ref[...], kc_ref[cols, :]], axis=0)
            b1 = jnp.where(first, bias_first, bias)
        else:
            kcat = kc_ref[(bq - 1) * blk:(bq + 1) * blk, :]
            b1 = bias
        in_group = (klane >= g * dh) & (klane < (g + 1) * dh) & (krow > 0)
        km = jnp.concatenate([jnp.where(in_group, kcat, jnp.zeros_like(kcat)), k_aug], axis=1)
        blocks = []
        for hh in range(A_GROUP):
            h = g * A_GROUP + hh
            pair = qt_ref[(h // 2) * LANES:(h // 2 + 1) * LANES, cols]
            if h % 2 != g:
                pair = jnp.concatenate([pair[dh:2 * dh, :], pair[0:dh, :]], axis=0)
            blocks.append(pair)
        rhs = jnp.concatenate([jnp.concatenate(blocks, axis=1), sink_ref[g]], axis=0)
        return _dot(km, rhs) + _tile_lanes(b1, A_GROUP)

    def finish(bq, g, st):
        cols = slice(bq * blk, (bq + 1) * blk)
        if bq == 0:
            vtcat = jnp.concatenate([vtp_ref[...], vtc_ref[:, cols]], axis=1)
        else:
            vtcat = vtc_ref[:, (bq - 1) * blk:(bq + 1) * blk]
        p = jnp.exp2(st - jnp.max(st, axis=0, keepdims=True)).astype(bf16)
        vt_g = jnp.where(vcol == 0, jnp.zeros((dh, 2 * blk), bf16), vtcat[g * dh:(g + 1) * dh, :])
        oext = _dot(jnp.concatenate([vt_g, ones_rows], axis=0), p)
        o = (oext[0:dh, :] * (1.0 / oext[dh:dh + 1, :])).astype(bf16)
        for hh in range(A_GROUP):
            h = g * A_GROUP + hh
            out_ref[h * dh:(h + 1) * dh, cols] = o[:, hh * blk:(hh + 1) * blk]

    units = [(bq, g) for bq in range(SWA_TILE // blk) for g in range(A_KV_HEADS)]
    nxt = logits(*units[0])
    for u, unit in enumerate(units):
        st = nxt
        if u + 1 < len(units):
            nxt = logits(*units[u + 1])
        finish(*unit, st)


def _swa(qt, k, vt, sink_aug, batch, seq):
    blk = A_BLOCK
    tq = SWA_TILE
    nq = A_Q_HEADS * A_HEAD_DIM
    nkv = A_KV_HEADS * A_HEAD_DIM
    nt = seq // tq
    per = tq // blk
    prev_idx = lambda b, i: b * nt * per + jnp.maximum(i * per - 1, 0)
    return pl.pallas_call(
        _swa_kernel,
        grid=(batch, nt),
        in_specs=[pl.BlockSpec((nq, tq), lambda b, i: (0, b * nt + i)),
                  pl.BlockSpec((tq, nkv), lambda b, i: (b * nt + i, 0)),
                  pl.BlockSpec((blk, nkv), lambda b, i: (prev_idx(b, i), 0)),
                  pl.BlockSpec((nkv, tq), lambda b, i: (0, b * nt + i)),
                  pl.BlockSpec((nkv, blk), lambda b, i: (0, prev_idx(b, i))),
                  _resident(sink_aug.shape)],
        out_specs=pl.BlockSpec((nq, tq), lambda b, i: (0, b * nt + i)),
        out_shape=jax.ShapeDtypeStruct((nq, batch * seq), bf16),
        compiler_params=_params(2),
        name="swa",
    )(qt, k, k, vt, vt, sink_aug)


def _sink_rows(sinks):
    parts = jnp.stack(_split3(sinks.astype(f32) * LOG2E), axis=0)
    per_lane = jnp.repeat(parts.reshape(3, A_KV_HEADS, A_GROUP), A_BLOCK, axis=2)
    return jnp.pad(per_lane.transpose(1, 0, 2), ((0, 0), (0, LANES - 3), (0, 0)))


def _out_ffn_kernel(h0_ref, at0_ref, hn_ref, atn_ref, wo_ref, bo_ref, gpost_ref, gpre_ref, gfpost_ref,
                    wup_ref, cw_ref, cb_ref, wd_ref, out_ref, carry_ref, act_ref, h1_ref, xn_ref, h1n_ref, xnn_ref, *,
                    tiles_per_seq):
    tm = hn_ref.shape[0]
    dff = wd_ref.shape[0]
    i = pl.program_id(0)

    d = wo_ref.shape[1]
    out_cols = [slice(n0, n0 + OUT_CHUNK) for n0 in range(0, d, OUT_CHUNK)]

    def out_proj(at_ref, cols):
        return _dot_tn(at_ref[...], wo_ref[:, cols]) + bo_ref[:, cols]

    def front(h_ref, z_parts):
        h1 = h_ref[...] + _rmsnorm(jnp.concatenate(z_parts, axis=1), gpost_ref[...])
        h1n_ref[...] = h1
        xnn_ref[...] = _rmsnorm(h1, gpre_ref[...]).astype(bf16)

    @pl.when(i == 0)
    def _():
        front(h0_ref, [out_proj(at0_ref, cols) for cols in out_cols])

    @pl.when(i % tiles_per_seq == 0)
    def _():
        carry_ref[...] = jnp.zeros_like(carry_ref)

    xn_ref[...] = xnn_ref[...]
    h1_ref[...] = h1n_ref[...]
    top = lax.broadcasted_iota(jnp.int32, (8, FF_CHUNK), 0)
    z_parts = []
    for ci, c0 in enumerate(range(0, dff, FF_CHUNK)):
        if FRONT_AFTER_CHUNKS <= ci < FRONT_AFTER_CHUNKS + len(out_cols):
            z_parts.append(out_proj(atn_ref, out_cols[ci - FRONT_AFTER_CHUNKS]))
        if ci == FRONT_AFTER_CHUNKS + len(out_cols):
            front(hn_ref, z_parts)
        cs = slice(c0, c0 + FF_CHUNK)
        gate = _dot(xn_ref[...], wup_ref[:, cs])
        val = _dot(xn_ref[...], wup_ref[:, dff + c0:dff + c0 + FF_CHUNK])
        prev = carry_ref[:, cs]
        carry_ref[:, cs] = gate[tm - 8:tm, :]
        shifted = []
        for lag in (1, 2):
            rolled = pltpu.roll(gate, lag, axis=0)
            head = jnp.where(top < lag, pltpu.roll(prev, lag, axis=0), rolled[0:8, :])
            shifted.append(jnp.concatenate([head, rolled[8:, :]], axis=0))
        g1, g2 = shifted
        gc = cb_ref[:, cs] + cw_ref[0:1, cs] * g2 + cw_ref[1:2, cs] * g1 + cw_ref[2:3, cs] * gate
        act_ref[:, cs] = (gc * jax.nn.sigmoid(gc) * val).astype(bf16)
    y = _dot(act_ref[...], wd_ref[...])
    out_ref[...] = h1_ref[...] + _rmsnorm(y, gfpost_ref[...])


def _out_ffn(h, at, wo, bo, gpost, gpre, gfpost, wup, cw, cb, wd, seq):
    t, d = h.shape
    tm = TOKEN_TILE
    n = t // tm
    dff = wd.shape[0]
    nxt = lambda i: jnp.minimum(i + 1, n - 1)
    return pl.pallas_call(
        functools.partial(_out_ffn_kernel, tiles_per_seq=seq // tm),
        grid=(n,),
        in_specs=[pl.BlockSpec((tm, d), lambda i: (0, 0), pipeline_mode=pl.Buffered(1)),
                  pl.BlockSpec((at.shape[0], tm), lambda i: (0, 0), pipeline_mode=pl.Buffered(1)),
                  pl.BlockSpec((tm, d), lambda i: (nxt(i), 0)),
                  pl.BlockSpec((at.shape[0], tm), lambda i: (0, nxt(i))),
                  _resident(wo.shape), _resident((1, d)), _resident((1, d)),
                  _resident((1, d)), _resident((1, d)), _resident(wup.shape),
                  _resident(cw.shape), _resident(cb.shape), _resident(wd.shape)],
        out_specs=pl.BlockSpec((tm, d), lambda i: (i, 0)),
        out_shape=jax.ShapeDtypeStruct((t, d), f32),
        scratch_shapes=[pltpu.VMEM((8, dff), f32), pltpu.VMEM((tm, dff), bf16),
                        pltpu.VMEM((tm, d), f32), pltpu.VMEM((tm, d), bf16),
                        pltpu.VMEM((tm, d), f32), pltpu.VMEM((tm, d), bf16)],
        compiler_params=_params(1),
        name="out_ffn",
    )(h, at, h, at, wo, bo, gpost, gpre, gfpost, wup, cw, cb, wd)


def kernel(x, m_w_in, m_gate_bias, m_head_norm, m_w_out, a_w_in, a_b_in, a_sinks, a_w_out, a_b_out,
           norm_mix_pre, norm_mix_post, norm_ffn_pre, norm_ffn_post, f_w_up, f_conv_w, f_conv_b, f_w_down):
    batch, seq, d = x.shape
    depth = norm_mix_pre.shape[0]
    h = x.reshape(batch * seq, d)
    row = lambda vec: vec.reshape(1, -1).astype(f32)
    lane_rep = lambda vec: jnp.broadcast_to(vec.astype(f32)[:, None], (vec.shape[0], LANES))

    for i in range(depth):
        j = i // 2
        if i % 2 == 0:
            nq = M_HEADS * M_QK_DIM
            nv = M_HEADS * M_V_DIM
            w_in = m_w_in[j]
            w_q, w_k, w_v, w_o, w_g = jnp.split(w_in, [nq, 2 * nq, 2 * nq + nv, 2 * nq + 2 * nv], axis=1)
            wt_main = jnp.concatenate([w_q, w_v, w_o], axis=1).T.astype(bf16)
            w_gate = jnp.pad(w_g, ((0, 0), (0, LANES - 2 * M_HEADS))).astype(bf16)
            gate_bias = jnp.pad(m_gate_bias[j].reshape(-1).astype(f32), (0, LANES - 2 * M_HEADS))
            qt, k, vt, ot, ccol, rstat = _mlstm_proj(
                h, row(norm_mix_pre[i]), wt_main, w_k.astype(bf16), w_gate, w_g.T.astype(bf16), row(gate_bias),
                lane_rep(gate_bias[:2 * M_HEADS]))
            mixed_t = _mlstm_scan(qt, k, vt, ot, ccol, rstat, lane_rep(m_head_norm[j]), batch, seq)
            w_out = m_w_out[j].astype(bf16)
            b_out = jnp.zeros((1, d), f32)
        else:
            nq = A_Q_HEADS * A_HEAD_DIM
            nkv = A_KV_HEADS * A_HEAD_DIM
            w_q, w_k, w_v = jnp.split(a_w_in[j], [nq, nq + nkv], axis=1)
            b_q, b_k, b_v = jnp.split(a_b_in[j], [nq, nq + nkv])
            qt, k, vt = _attn_proj(h, row(norm_mix_pre[i]), w_q.T.astype(bf16), w_k.astype(bf16),
                                   w_v.T.astype(bf16), lane_rep(b_q), row(b_k), lane_rep(b_v))
            mixed_t = _swa(qt, k, vt, _sink_rows(a_sinks[j]), batch, seq)
            w_out = a_w_out[j].astype(bf16)
            b_out = row(a_b_out[j])
        h = _out_ffn(h, mixed_t, w_out, b_out, row(norm_mix_post[i]), row(norm_ffn_pre[i]), row(norm_ffn_post[i]),
                     f_w_up[i].astype(bf16), f_conv_w[i].astype(f32), row(f_conv_b[i]), f_w_down[i].astype(bf16), seq)
    return h.reshape(batch, seq, d)
```

```python
import functools

import jax
import jax.numpy as jnp
from jax import lax
from jax.experimental import pallas as pl
from jax.experimental.pallas import tpu as pltpu

EPS = 1e-6
LANES = 128
BF16_ROWS = 16

M_HEADS = 8
M_QK_DIM = 64
M_V_DIM = 128
GATE_CAP = 15.0
M_CHUNK = 128
SCAN_CHUNKS = 4

A_HEAD_DIM = 64
A_Q_HEADS = 16
A_KV_HEADS = 2
A_GROUP = A_Q_HEADS // A_KV_HEADS
WINDOW = 128
A_BLOCK = 128

LOG2E = 1.4426950408889634
NEG_BIG = -1e30

TOKEN_TILE = 512
SWA_TILE = 512
FF_CHUNK = 256
Q_ROWS_PER_DOT = 256
OUT_CHUNK = 256
FRONT_AFTER_CHUNKS = 2
VMEM_LIMIT = 60 * 1024 * 1024

bf16 = jnp.bfloat16
f32 = jnp.float32


def _dot(a, b):
    return jnp.dot(a, b, preferred_element_type=f32)


def _dot_nt(a, b):
    return lax.dot_general(a, b, (((1,), (1,)), ((), ())), preferred_element_type=f32)


def _dot_tn(a, b):
    return lax.dot_general(a, b, (((0,), (0,)), ((), ())), preferred_element_type=f32)


def _rmsnorm(x, g):
    return x * lax.rsqrt(jnp.mean(x * x, axis=-1, keepdims=True) + EPS) * g


def _split3(x):
    hi = x.astype(bf16)
    r1 = x - hi.astype(f32)
    mid = r1.astype(bf16)
    lo = (r1 - mid.astype(f32)).astype(bf16)
    return hi, mid, lo


def _log_sigmoid(x):
    return jnp.minimum(x, 0.0) - jnp.log1p(jnp.exp(-jnp.abs(x)))


def _tile_lanes(x, reps):
    return jnp.concatenate([x] * reps, axis=1)


def _next_input_norm(i, x0_ref, xnext_ref, g_ref, xn_ref, xnn_ref):
    @pl.when(i == 0)
    def _():
        xnn_ref[...] = _rmsnorm(x0_ref[...], g_ref[...]).astype(bf16)

    xn_ref[...] = xnn_ref[...]

    def prepare_next():
        xnn_ref[...] = _rmsnorm(xnext_ref[...], g_ref[...]).astype(bf16)

    return xn_ref[...], prepare_next


def _sigmoid(x):
    return 0.5 * jnp.tanh(0.5 * x) + 0.5


def _proj_in_specs(tm, d, n_tiles):
    return [pl.BlockSpec((tm, d), lambda i: (0, 0), pipeline_mode=pl.Buffered(1)),
            pl.BlockSpec((tm, d), lambda i: (jnp.minimum(i + 1, n_tiles - 1), 0))]


def _resident(shape):
    nd = len(shape)
    return pl.BlockSpec(shape, lambda *_: (0,) * nd, pipeline_mode=pl.Buffered(1))


def _params(n_axes):
    return pltpu.CompilerParams(
        dimension_semantics=("arbitrary",) * n_axes, vmem_limit_bytes=VMEM_LIMIT)


def _mlstm_proj_kernel(x0_ref, xnext_ref, g_ref, wt_ref, wk_ref, gb_ref,
                       qt_ref, k_ref, vt_ref, ot_ref, ccol_ref, rstat_ref, xn_ref, xnn_ref):
    tm = xnext_ref.shape[0]
    L = M_CHUNK
    H = M_HEADS
    xn, prepare_next = _next_input_norm(pl.program_id(0), x0_ref, xnext_ref, g_ref, xn_ref, xnn_ref)
    nq = M_HEADS * M_QK_DIM
    nv = M_HEADS * M_V_DIM

    first = _dot_nt(wt_ref[0:2 * H + nq, :], xn)
    prepare_next()
    raw = first[0:2 * H, :] + _tile_lanes(gb_ref[...], tm // LANES)
    capped = GATE_CAP * jnp.tanh(raw * (1.0 / GATE_CAP))
    grow = jnp.where(lax.broadcasted_iota(jnp.int32, raw.shape, 0) < H, capped, _log_sigmoid(capped))
    qt_ref[...] = (first[2 * H:, :] * (M_QK_DIM ** -0.5)).astype(bf16)
    ot_ref[...] = _sigmoid(_dot_nt(wt_ref[2 * H + nq + nv:2 * H + nq + 2 * nv, :], xn)).astype(bf16)
    vt_ref[...] = _dot_nt(wt_ref[2 * H + nq:2 * H + nq + nv, :], xn).astype(bf16)

    triu = jnp.where(lax.broadcasted_iota(jnp.int32, (L, L), 0) <= lax.broadcasted_iota(jnp.int32, (L, L), 1),
                     1.0, 0.0).astype(bf16)
    nch = tm // L
    stacked = jnp.concatenate([grow[:, c * L:(c + 1) * L] for c in range(nch)], axis=0)
    sums = _dot(jnp.concatenate(_split3(stacked), axis=0), triu)
    cum = sums[0:nch * 2 * H] + sums[nch * 2 * H:2 * nch * 2 * H] + sums[2 * nch * 2 * H:]
    lane_h = lax.broadcasted_iota(jnp.int32, (H, L), 1)
    for c in range(nch):
        tok = slice(c * L, (c + 1) * L)
        b_r = cum[c * 2 * H + H:(c + 1) * 2 * H, :]
        c_r = grow[0:H, tok] - b_r
        cmax = c_r
        shift = 1
        while shift < L:
            cmax = jnp.where(lane_h >= shift, jnp.maximum(cmax, pltpu.roll(cmax, shift, axis=1)), cmax)
            shift *= 2
        b_last = jnp.broadcast_to(b_r[:, L - 1:L], (H, L))
        tail = c_r + b_last
        tail_max = jnp.broadcast_to(jnp.max(tail, axis=-1, keepdims=True), (H, L))
        rstat_ref[:, tok] = jnp.concatenate([b_r, cmax, tail, tail_max, b_last], axis=0)
        ccol_ref[tok, :] = jnp.concatenate([c_r * LOG2E, jnp.zeros((LANES - H, L), f32)], axis=0).T

    k_ref[...] = _dot(xn, wk_ref[...]).astype(bf16)


def _mlstm_proj(x, gain, wt_main, w_k, gate_bias):
    t, d = x.shape
    nq = M_HEADS * M_QK_DIM
    nv = M_HEADS * M_V_DIM
    tm = TOKEN_TILE
    row = lambda n: pl.BlockSpec((tm, n), lambda i: (i, 0))
    col = lambda n: pl.BlockSpec((n, tm), lambda i: (0, i))
    return pl.pallas_call(
        _mlstm_proj_kernel,
        grid=(t // tm,),
        in_specs=_proj_in_specs(tm, d, t // tm) + [
            _resident((1, d)), _resident(wt_main.shape), _resident(w_k.shape), _resident(gate_bias.shape)],
        out_specs=[col(nq), row(nq), col(nv), col(nv), row(LANES), col(5 * M_HEADS)],
        out_shape=[jax.ShapeDtypeStruct((nq, t), bf16), jax.ShapeDtypeStruct((t, nq), bf16),
                   jax.ShapeDtypeStruct((nv, t), bf16), jax.ShapeDtypeStruct((nv, t), bf16),
                   jax.ShapeDtypeStruct((t, LANES), f32), jax.ShapeDtypeStruct((5 * M_HEADS, t), f32)],
        scratch_shapes=[pltpu.VMEM((tm, d), bf16), pltpu.VMEM((tm, d), bf16)],
        compiler_params=_params(1),
        name="mlstm_proj",
    )(x, x, gain, wt_main, w_k, gate_bias)


def _mlstm_scan_kernel(qt_ref, k_ref, vt_ref, ot_ref, ccol_ref, rstat_ref, hn_ref, out_ref, ct_ref, m_ref):
    L = M_CHUNK
    dk, dv = M_QK_DIM, M_V_DIM
    pairs = M_HEADS // 2

    @pl.when(pl.program_id(1) == 0)
    def _():
        ct_ref[...] = jnp.zeros_like(ct_ref)
        m_ref[...] = jnp.zeros_like(m_ref)

    H = M_HEADS
    upper = lax.broadcasted_iota(jnp.int32, (L, L), 0) <= lax.broadcasted_iota(jnp.int32, (L, L), 1)
    lane = lax.broadcasted_iota(jnp.int32, (L, LANES), 1)
    low = lane < dk
    ones_rows = jnp.ones((BF16_ROWS, L), bf16)
    zeros_half = jnp.zeros((dk, L), bf16)

    stats = []
    for c in range(SCAN_CHUNKS):
        tok = slice(c * L, (c + 1) * L)
        stats.append(tuple(rstat_ref[i * H:(i + 1) * H, tok] for i in range(5)) + (ccol_ref[tok, :],))

    m = m_ref[...]
    m_prevs = []
    for c in range(SCAN_CHUNKS):
        m_prevs.append(m)
        m = jnp.maximum(stats[c][4] + m, stats[c][3])
    m_prevs.append(m)
    m_ref[...] = m

    def kq(c, j):
        tok = slice(c * L, (c + 1) * L)
        kp = k_ref[tok, j * LANES:(j + 1) * LANES]
        qtp = qt_ref[j * LANES:(j + 1) * LANES, tok]
        q_even = jnp.concatenate([qtp[0:dk, :], zeros_half], axis=0)
        q_odd = jnp.concatenate([zeros_half, qtp[dk:2 * dk, :]], axis=0)
        return kp, (q_even, q_odd), _dot(kp, jnp.concatenate([q_even, q_odd], axis=1))

    ct = [ct_ref[j] for j in range(pairs)]
    units = [(c, j) for c in range(SCAN_CHUNKS) for j in range(pairs)]
    nxt = kq(*units[0])
    for u, (c, j) in enumerate(units):
        kp, q_eo, st = nxt
        if u + 1 < len(units):
            nxt = kq(*units[u + 1])
        tok = slice(c * L, (c + 1) * L)
        b_r, cmax, tail, _, b_last, c_cols = stats[c]
        m_prev, m_new = m_prevs[c], m_prevs[c + 1]
        a_r = jnp.maximum(m_prev, cmax)
        a_r2 = a_r * LOG2E
        carry_w = jnp.exp(m_prev - a_r).astype(bf16)
        clamp_r = jnp.exp(-(a_r + b_r))
        ws_r = jnp.exp(tail - m_new).astype(bf16)
        decay = jnp.exp(b_last + m_prev - m_new)
        ct_pair_b = ct[j].astype(bf16)
        vws = []
        for e, q_e in enumerate(q_eo):
            h = 2 * j + e
            expo = jnp.where(upper, c_cols[:, h:h + 1] - a_r2[h:h + 1, :], NEG_BIG)
            wt = (st[:, e * L:(e + 1) * L] * jnp.exp2(expo)).astype(bf16)
            q_w = q_e * carry_w[h:h + 1, :]
            vext = jnp.concatenate([vt_ref[h * dv:(h + 1) * dv, tok], ones_rows], axis=0)
            nd = _dot(jnp.concatenate([vext, ct_pair_b], axis=1), jnp.concatenate([wt, q_w], axis=0))
            den = jnp.maximum(jnp.abs(nd[dv:dv + 1, :]), clamp_r[h:h + 1, :])
            inv = 1.0 / den
            num = nd[0:dv, :]
            scale = inv * lax.rsqrt(inv * inv * jnp.mean(num * num, axis=0, keepdims=True) + EPS)
            hn = num * scale * hn_ref[h * dv:(h + 1) * dv, :]
            gate_o = ot_ref[h * dv:(h + 1) * dv, tok].astype(f32)
            out_ref[h * dv:(h + 1) * dv, tok] = (gate_o * hn).astype(bf16)
            vws.append(vext * ws_r[h:h + 1, :])
        k_split = jnp.concatenate([jnp.where(low, kp, jnp.zeros_like(kp)),
                                   jnp.where(low, jnp.zeros_like(kp), kp)], axis=0)
        decay_pair = jnp.where(low[0:1, :], decay[2 * j:2 * j + 1, :], decay[2 * j + 1:2 * j + 2, :])
        ct[j] = decay_pair * ct[j] + _dot(jnp.concatenate(vws, axis=1), k_split)
    for j in range(pairs):
        ct_ref[j] = ct[j]


def _mlstm_scan(qt, k, vt, ot, ccol, rstat, hn_rep, batch, seq):
    tm = SCAN_CHUNKS * M_CHUNK
    nq = M_HEADS * M_QK_DIM
    nv = M_HEADS * M_V_DIM
    nt = seq // tm
    row = lambda n: pl.BlockSpec((tm, n), lambda b, i: (b * nt + i, 0))
    col = lambda n: pl.BlockSpec((n, tm), lambda b, i: (0, b * nt + i))
    return pl.pallas_call(
        _mlstm_scan_kernel,
        grid=(batch, nt),
        in_specs=[col(nq), row(nq), col(nv), col(nv), row(LANES), col(5 * M_HEADS), _resident(hn_rep.shape)],
        out_specs=col(nv),
        out_shape=jax.ShapeDtypeStruct((nv, batch * seq), bf16),
        scratch_shapes=[pltpu.VMEM((M_HEADS // 2, M_V_DIM + BF16_ROWS, 2 * M_QK_DIM), f32),
                        pltpu.VMEM((M_HEADS, LANES), f32)],
        compiler_params=_params(2),
        name="mlstm_scan",
    )(qt, k, vt, ot, ccol, rstat, hn_rep)


def _attn_proj_kernel(x0_ref, xnext_ref, g_ref, wt_ref, b_ref, qt_ref, k_ref, vt_ref, xn_ref, xnn_ref):
    tm = xnext_ref.shape[0]
    nkv = A_KV_HEADS * A_HEAD_DIM
    xn, prepare_next = _next_input_norm(pl.program_id(0), x0_ref, xnext_ref, g_ref, xn_ref, xnn_ref)
    nq = wt_ref.shape[0] - 2 * nkv
    proj = lambda r0, r1: _dot_nt(wt_ref[r0:r1, :], xn) + _tile_lanes(b_ref[r0:r1, :], tm // LANES)
    kv = proj(0, 2 * nkv)
    prepare_next()
    k_ref[...] = kv[0:nkv, :].T.astype(bf16)
    vt_ref[...] = kv[nkv:2 * nkv, :].astype(bf16)
    for r0 in range(0, nq, Q_ROWS_PER_DOT):
        q = proj(2 * nkv + r0, 2 * nkv + r0 + Q_ROWS_PER_DOT)
        qt_ref[r0:r0 + Q_ROWS_PER_DOT, :] = (q * (A_HEAD_DIM ** -0.5 * LOG2E)).astype(bf16)


def _attn_proj(x, gain, wt, b_rep):
    t, d = x.shape
    nq = A_Q_HEADS * A_HEAD_DIM
    nkv = A_KV_HEADS * A_HEAD_DIM
    tm = TOKEN_TILE
    row = lambda n: pl.BlockSpec((tm, n), lambda i: (i, 0))
    col = lambda n: pl.BlockSpec((n, tm), lambda i: (0, i))
    return pl.pallas_call(
        _attn_proj_kernel,
        grid=(t // tm,),
        in_specs=_proj_in_specs(tm, d, t // tm) + [_resident((1, d)), _resident(wt.shape), _resident(b_rep.shape)],
        out_specs=[col(nq), row(nkv), col(nkv)],
        out_shape=[jax.ShapeDtypeStruct((nq, t), bf16), jax.ShapeDtypeStruct((t, nkv), bf16),
                   jax.ShapeDtypeStruct((nkv, t), bf16)],
        scratch_shapes=[pltpu.VMEM((tm, d), bf16), pltpu.VMEM((tm, d), bf16)],
        compiler_params=_params(1),
        name="attn_proj",
    )(x, x, gain, wt, b_rep)


def _swa_kernel(qt_ref, kc_ref, kp_ref, vtc_ref, vtp_ref, sink_ref, out_ref):
    blk = A_BLOCK
    dh = A_HEAD_DIM
    first = pl.program_id(1) == 0
    ku = lax.broadcasted_iota(jnp.int32, (2 * blk, blk), 0)
    qi = lax.broadcasted_iota(jnp.int32, (2 * blk, blk), 1)
    diff = qi - (ku - blk)
    band = (diff >= 0) & (diff < WINDOW)
    bias = jnp.where(band | (ku == 0), 0.0, NEG_BIG)
    bias_first = jnp.where((band & (ku >= blk)) | (ku == 0), 0.0, NEG_BIG)
    krow = lax.broadcasted_iota(jnp.int32, (2 * blk, LANES), 0)
    klane = lax.broadcasted_iota(jnp.int32, (2 * blk, LANES), 1)
    k_aug = jnp.where((krow == 0) & (klane < 3), 1.0, 0.0).astype(bf16)
    vcol = lax.broadcasted_iota(jnp.int32, (dh, 2 * blk), 1)
    ones_rows = jnp.ones((BF16_ROWS, 2 * blk), bf16)

    def logits(bq, g):
        cols = slice(bq * blk, (bq + 1) * blk)
        if bq == 0:
            kcat = jnp.concatenate([kp_ref[...], kc_ref[cols, :]], axis=0)
            b1 = jnp.where(first, bias_first, bias)
        else:
            kcat = kc_ref[(bq - 1) * blk:(bq + 1) * blk, :]
            b1 = bias
        in_group = (klane >= g * dh) & (klane < (g + 1) * dh) & (krow > 0)
        km = jnp.concatenate([jnp.where(in_group, kcat, jnp.zeros_like(kcat)), k_aug], axis=1)
        blocks = []
        for hh in range(A_GROUP):
            h = g * A_GROUP + hh
            pair = qt_ref[(h // 2) * LANES:(h // 2 + 1) * LANES, cols]
            if h % 2 != g:
                pair = jnp.concatenate([pair[dh:2 * dh, :], pair[0:dh, :]], axis=0)
            blocks.append(pair)
        rhs = jnp.concatenate([jnp.concatenate(blocks, axis=1), sink_ref[g]], axis=0)
        return _dot(km, rhs) + _tile_lanes(b1, A_GROUP)

    def finish(bq, g, st):
        cols = slice(bq * blk, (bq + 1) * blk)
        if bq == 0:
            vtcat = jnp.concatenate([vtp_ref[...], vtc_ref[:, cols]], axis=1)
        else:
            vtcat = vtc_ref[:, (bq - 1) * blk:(bq + 1) * blk]
        p = jnp.exp2(st - jnp.max(st, axis=0, keepdims=True)).astype(bf16)
        vt_g = jnp.where(vcol == 0, jnp.zeros((dh, 2 * blk), bf16), vtcat[g * dh:(g + 1) * dh, :])
        oext = _dot(jnp.concatenate([vt_g, ones_rows], axis=0), p)
        o = (oext[0:dh, :] * (1.0 / oext[dh:dh + 1, :])).astype(bf16)
        for hh in range(A_GROUP):
            h = g * A_GROUP + hh
            out_ref[h * dh:(h + 1) * dh, cols] = o[:, hh * blk:(hh + 1) * blk]

    units = [(bq, g) for bq in range(SWA_TILE // blk) for g in range(A_KV_HEADS)]
    nxt = logits(*units[0])
    for u, unit in enumerate(units):
        st = nxt
        if u + 1 < len(units):
            nxt = logits(*units[u + 1])
        finish(*unit, st)


def _swa(qt, k, vt, sink_aug, batch, seq):
    blk = A_BLOCK
    tq = SWA_TILE
    nq = A_Q_HEADS * A_HEAD_DIM
    nkv = A_KV_HEADS * A_HEAD_DIM
    nt = seq // tq
    per = tq // blk
    prev_idx = lambda b, i: b * nt * per + jnp.maximum(i * per - 1, 0)
    return pl.pallas_call(
        _swa_kernel,
        grid=(batch, nt),
        in_specs=[pl.BlockSpec((nq, tq), lambda b, i: (0, b * nt + i)),
                  pl.BlockSpec((tq, nkv), lambda b, i: (b * nt + i, 0)),
                  pl.BlockSpec((blk, nkv), lambda b, i: (prev_idx(b, i), 0)),
                  pl.BlockSpec((nkv, tq), lambda b, i: (0, b * nt + i)),
                  pl.BlockSpec((nkv, blk), lambda b, i: (0, prev_idx(b, i))),
                  _resident(sink_aug.shape)],
        out_specs=pl.BlockSpec((nq, tq), lambda b, i: (0, b * nt + i)),
        out_shape=jax.ShapeDtypeStruct((nq, batch * seq), bf16),
        compiler_params=_params(2),
        name="swa",
    )(qt, k, k, vt, vt, sink_aug)


def _sink_rows(sinks):
    parts = jnp.stack(_split3(sinks.astype(f32) * LOG2E), axis=0)
    per_lane = jnp.repeat(parts.reshape(3, A_KV_HEADS, A_GROUP), A_BLOCK, axis=2)
    return jnp.pad(per_lane.transpose(1, 0, 2), ((0, 0), (0, LANES - 3), (0, 0)))


def _out_ffn_kernel(h0_ref, at0_ref, hn_ref, atn_ref, wo_ref, bo_ref, gpost_ref, gpre_ref, gfpost_ref,
                    wup_ref, cw_ref, cb_ref, wd_ref, out_ref, carry_ref, act_ref, h1_ref, xn_ref, h1n_ref, xnn_ref, *,
                    tiles_per_seq):
    tm = hn_ref.shape[0]
    dff = wd_ref.shape[0]
    i = pl.program_id(0)

    d = wo_ref.shape[1]
    out_cols = [slice(n0, n0 + OUT_CHUNK) for n0 in range(0, d, OUT_CHUNK)]

    def out_proj(at_ref, cols):
        return _dot_tn(at_ref[...], wo_ref[:, cols]) + bo_ref[:, cols]

    def front(h_ref, z_parts):
        h1 = h_ref[...] + _rmsnorm(jnp.concatenate(z_parts, axis=1), gpost_ref[...])
        h1n_ref[...] = h1
        xnn_ref[...] = _rmsnorm(h1, gpre_ref[...]).astype(bf16)

    @pl.when(i == 0)
    def _():
        front(h0_ref, [out_proj(at0_ref, cols) for cols in out_cols])

    @pl.when(i % tiles_per_seq == 0)
    def _():
        carry_ref[...] = jnp.zeros_like(carry_ref)

    xn_ref[...] = xnn_ref[...]
    h1_ref[...] = h1n_ref[...]
    top = lax.broadcasted_iota(jnp.int32, (8, FF_CHUNK), 0)
    z_parts = []
    for ci, c0 in enumerate(range(0, dff, FF_CHUNK)):
        if FRONT_AFTER_CHUNKS <= ci < FRONT_AFTER_CHUNKS + len(out_cols):
            z_parts.append(out_proj(atn_ref, out_cols[ci - FRONT_AFTER_CHUNKS]))
        if ci == FRONT_AFTER_CHUNKS + len(out_cols):
            front(hn_ref, z_parts)
        cs = slice(c0, c0 + FF_CHUNK)
        gate = _dot(xn_ref[...], wup_ref[:, cs])
        val = _dot(xn_ref[...], wup_ref[:, dff + c0:dff + c0 + FF_CHUNK])
        prev = carry_ref[:, cs]
        carry_ref[:, cs] = gate[tm - 8:tm, :]
        shifted = []
        for lag in (1, 2):
            rolled = pltpu.roll(gate, lag, axis=0)
            head = jnp.where(top < lag, pltpu.roll(prev, lag, axis=0), rolled[0:8, :])
            shifted.append(jnp.concatenate([head, rolled[8:, :]], axis=0))
        g1, g2 = shifted
        gc = cb_ref[:, cs] + cw_ref[0:1, cs] * g2 + cw_ref[1:2, cs] * g1 + cw_ref[2:3, cs] * gate
        act_ref[:, cs] = (gc * jax.nn.sigmoid(gc) * val).astype(bf16)
    y = _dot(act_ref[...], wd_ref[...])
    out_ref[...] = h1_ref[...] + _rmsnorm(y, gfpost_ref[...])


def _out_ffn(h, at, wo, bo, gpost, gpre, gfpost, wup, cw, cb, wd, seq):
    t, d = h.shape
    tm = TOKEN_TILE
    n = t // tm
    dff = wd.shape[0]
    nxt = lambda i: jnp.minimum(i + 1, n - 1)
    return pl.pallas_call(
        functools.partial(_out_ffn_kernel, tiles_per_seq=seq // tm),
        grid=(n,),
        in_specs=[pl.BlockSpec((tm, d), lambda i: (0, 0), pipeline_mode=pl.Buffered(1)),
                  pl.BlockSpec((at.shape[0], tm), lambda i: (0, 0), pipeline_mode=pl.Buffered(1)),
                  pl.BlockSpec((tm, d), lambda i: (nxt(i), 0)),
                  pl.BlockSpec((at.shape[0], tm), lambda i: (0, nxt(i))),
                  _resident(wo.shape), _resident((1, d)), _resident((1, d)),
                  _resident((1, d)), _resident((1, d)), _resident(wup.shape),
                  _resident(cw.shape), _resident(cb.shape), _resident(wd.shape)],
        out_specs=pl.BlockSpec((tm, d), lambda i: (i, 0)),
        out_shape=jax.ShapeDtypeStruct((t, d), f32),
        scratch_shapes=[pltpu.VMEM((8, dff), f32), pltpu.VMEM((tm, dff), bf16),
                        pltpu.VMEM((tm, d), f32), pltpu.VMEM((tm, d), bf16),
                        pltpu.VMEM((tm, d), f32), pltpu.VMEM((tm, d), bf16)],
        compiler_params=_params(1),
        name="out_ffn",
    )(h, at, h, at, wo, bo, gpost, gpre, gfpost, wup, cw, cb, wd)


def kernel(x, m_w_in, m_gate_bias, m_head_norm, m_w_out, a_w_in, a_b_in, a_sinks, a_w_out, a_b_out,
           norm_mix_pre, norm_mix_post, norm_ffn_pre, norm_ffn_post, f_w_up, f_conv_w, f_conv_b, f_w_down):
    batch, seq, d = x.shape
    depth = norm_mix_pre.shape[0]
    h = x.reshape(batch * seq, d)
    row = lambda vec: vec.reshape(1, -1).astype(f32)
    lane_rep = lambda vec: jnp.broadcast_to(vec.astype(f32)[:, None], (vec.shape[0], LANES))

    for i in range(depth):
        j = i // 2
        if i % 2 == 0:
            nq = M_HEADS * M_QK_DIM
            nv = M_HEADS * M_V_DIM
            w_in = m_w_in[j]
            w_q, w_k, w_v, w_o, w_g = jnp.split(w_in, [nq, 2 * nq, 2 * nq + nv, 2 * nq + 2 * nv], axis=1)
            wt_main = jnp.concatenate([w_g, w_q, w_v, w_o], axis=1).T.astype(bf16)
            qt, k, vt, ot, ccol, rstat = _mlstm_proj(
                h, row(norm_mix_pre[i]), wt_main, w_k.astype(bf16), lane_rep(m_gate_bias[j].reshape(-1)))
            mixed_t = _mlstm_scan(qt, k, vt, ot, ccol, rstat, lane_rep(m_head_norm[j]), batch, seq)
            w_out = m_w_out[j].astype(bf16)
            b_out = jnp.zeros((1, d), f32)
        else:
            nq = A_Q_HEADS * A_HEAD_DIM
            kvq = lambda a: jnp.concatenate([a[nq:], a[:nq]], axis=0)
            qt, k, vt = _attn_proj(h, row(norm_mix_pre[i]), kvq(a_w_in[j].T).astype(bf16), lane_rep(kvq(a_b_in[j])))
            mixed_t = _swa(qt, k, vt, _sink_rows(a_sinks[j]), batch, seq)
            w_out = a_w_out[j].astype(bf16)
            b_out = row(a_b_out[j])
        h = _out_ffn(h, mixed_t, w_out, b_out, row(norm_mix_post[i]), row(norm_ffn_pre[i]), row(norm_ffn_post[i]),
                     f_w_up[i].astype(bf16), f_conv_w[i].astype(f32), row(f_conv_b[i]), f_w_down[i].astype(bf16), seq)
    return h.reshape(batch, seq, d)
```

```python
import functools

import jax
import jax.numpy as jnp
from jax import lax
from jax.experimental import pallas as pl
from jax.experimental.pallas import tpu as pltpu

EPS = 1e-6
LANES = 128
BF16_ROWS = 16

M_HEADS = 8
M_QK_DIM = 64
M_V_DIM = 128
GATE_CAP = 15.0
M_CHUNK = 128
SCAN_CHUNKS = 4

A_HEAD_DIM = 64
A_Q_HEADS = 16
A_KV_HEADS = 2
A_GROUP = A_Q_HEADS // A_KV_HEADS
WINDOW = 128
A_BLOCK = 128

LOG2E = 1.4426950408889634
NEG_BIG = -1e30

TOKEN_TILE = 512
SWA_TILE = 512
FF_CHUNK = 256
Q_ROWS_PER_DOT = 256
OUT_CHUNK = 256
FRONT_AFTER_CHUNKS = 2
CAST_ROW_BLOCKS = 4
VMEM_LIMIT = 60 * 1024 * 1024

bf16 = jnp.bfloat16
f32 = jnp.float32


def _dot(a, b):
    return jnp.dot(a, b, preferred_element_type=f32)


def _dot_nt(a, b):
    return lax.dot_general(a, b, (((1,), (1,)), ((), ())), preferred_element_type=f32)


def _dot_tn(a, b):
    return lax.dot_general(a, b, (((0,), (0,)), ((), ())), preferred_element_type=f32)


def _rmsnorm(x, g):
    return x * lax.rsqrt(jnp.mean(x * x, axis=-1, keepdims=True) + EPS) * g


def _split3(x):
    hi = x.astype(bf16)
    r1 = x - hi.astype(f32)
    mid = r1.astype(bf16)
    lo = (r1 - mid.astype(f32)).astype(bf16)
    return hi, mid, lo


def _log_sigmoid(x):
    return jnp.minimum(x, 0.0) - jnp.log1p(jnp.exp(-jnp.abs(x)))


def _tile_lanes(x, reps):
    return jnp.concatenate([x] * reps, axis=1)


def _next_input_norm(i, x0_ref, xnext_ref, g_ref, xn_ref, xnn_ref):
    @pl.when(i == 0)
    def _():
        xnn_ref[...] = _rmsnorm(x0_ref[...], g_ref[...]).astype(bf16)

    xn_ref[...] = xnn_ref[...]

    def prepare_next():
        xnn_ref[...] = _rmsnorm(xnext_ref[...], g_ref[...]).astype(bf16)

    return xn_ref[...], prepare_next


def _sigmoid(x):
    return 0.5 * jnp.tanh(0.5 * x) + 0.5


def _proj_in_specs(tm, d, n_tiles):
    return [pl.BlockSpec((tm, d), lambda i: (0, 0), pipeline_mode=pl.Buffered(1)),
            pl.BlockSpec((tm, d), lambda i: (jnp.minimum(i + 1, n_tiles - 1), 0))]


def _resident(shape):
    nd = len(shape)
    return pl.BlockSpec(shape, lambda *_: (0,) * nd, pipeline_mode=pl.Buffered(1))


def _layer_resident(shape, layer):
    nd = len(shape) - 1
    return pl.BlockSpec((None,) + tuple(shape[1:]), lambda *_: (layer,) + (0,) * nd, pipeline_mode=pl.Buffered(1))


def _params(n_axes):
    return pltpu.CompilerParams(
        dimension_semantics=("arbitrary",) * n_axes, vmem_limit_bytes=VMEM_LIMIT)


def _cast_kernel(src_ref, dst_ref):
    dst_ref[...] = src_ref[...].astype(dst_ref.dtype)


def _cast_bf16(w, row_blocks):
    depth, rows, cols = w.shape
    blk = rows // row_blocks
    spec = pl.BlockSpec((None, blk, cols), lambda l, i: (l, i, 0))
    return pl.pallas_call(
        _cast_kernel,
        grid=(depth, row_blocks),
        in_specs=[spec],
        out_specs=spec,
        out_shape=jax.ShapeDtypeStruct(w.shape, bf16),
        compiler_params=_params(2),
        name="cast_bf16",
    )(w)


def _mlstm_proj_kernel(x0_ref, xnext_ref, g_ref, wt_ref, gb_ref,
                       qt_ref, k_ref, vt_ref, ot_ref, ccol_ref, rstat_ref, xn_ref, xnn_ref):
    tm = xnext_ref.shape[0]
    L = M_CHUNK
    H = M_HEADS
    xn, prepare_next = _next_input_norm(pl.program_id(0), x0_ref, xnext_ref, g_ref, xn_ref, xnn_ref)
    nq = M_HEADS * M_QK_DIM
    nv = M_HEADS * M_V_DIM

    q_rows, k_rows, v_rows = slice(0, nq), slice(nq, 2 * nq), slice(2 * nq, 2 * nq + nv)
    og_rows = slice(2 * nq + nv, 2 * nq + 2 * nv + 2 * H)
    first = _dot_nt(wt_ref[og_rows, :], xn)
    prepare_next()
    raw = first[nv:, :] + _tile_lanes(gb_ref[...], tm // LANES)
    capped = GATE_CAP * jnp.tanh(raw * (1.0 / GATE_CAP))
    grow = jnp.where(lax.broadcasted_iota(jnp.int32, raw.shape, 0) < H, capped, _log_sigmoid(capped))
    ot_ref[...] = _sigmoid(first[0:nv, :]).astype(bf16)
    k_ref[...] = _dot_nt(wt_ref[k_rows, :], xn).T.astype(bf16)
    vt_ref[...] = _dot_nt(wt_ref[v_rows, :], xn).astype(bf16)

    triu = jnp.where(lax.broadcasted_iota(jnp.int32, (L, L), 0) <= lax.broadcasted_iota(jnp.int32, (L, L), 1),
                     1.0, 0.0).astype(bf16)
    nch = tm // L
    stacked = jnp.concatenate([grow[:, c * L:(c + 1) * L] for c in range(nch)], axis=0)
    sums = _dot(jnp.concatenate(_split3(stacked), axis=0), triu)
    cum = sums[0:nch * 2 * H] + sums[nch * 2 * H:2 * nch * 2 * H] + sums[2 * nch * 2 * H:]
    lane_h = lax.broadcasted_iota(jnp.int32, (H, L), 1)
    for c in range(nch):
        tok = slice(c * L, (c + 1) * L)
        b_r = cum[c * 2 * H + H:(c + 1) * 2 * H, :]
        c_r = grow[0:H, tok] - b_r
        cmax = c_r
        shift = 1
        while shift < L:
            cmax = jnp.where(lane_h >= shift, jnp.maximum(cmax, pltpu.roll(cmax, shift, axis=1)), cmax)
            shift *= 2
        b_last = jnp.broadcast_to(b_r[:, L - 1:L], (H, L))
        tail = c_r + b_last
        tail_max = jnp.broadcast_to(jnp.max(tail, axis=-1, keepdims=True), (H, L))
        rstat_ref[:, tok] = jnp.concatenate([b_r, cmax, tail, tail_max, b_last], axis=0)
        ccol_ref[tok, :] = jnp.concatenate([c_r * LOG2E, jnp.zeros((LANES - H, L), f32)], axis=0).T

    qt_ref[...] = (_dot_nt(wt_ref[q_rows, :], xn) * (M_QK_DIM ** -0.5)).astype(bf16)


def _mlstm_proj(x, gain, wt, gate_bias):
    t, d = x.shape
    nq = M_HEADS * M_QK_DIM
    nv = M_HEADS * M_V_DIM
    tm = TOKEN_TILE
    row = lambda n: pl.BlockSpec((tm, n), lambda i: (i, 0))
    col = lambda n: pl.BlockSpec((n, tm), lambda i: (0, i))
    return pl.pallas_call(
        _mlstm_proj_kernel,
        grid=(t // tm,),
        in_specs=_proj_in_specs(tm, d, t // tm) + [
            _resident((1, d)), _resident(wt.shape), _resident(gate_bias.shape)],
        out_specs=[col(nq), row(nq), col(nv), col(nv), row(LANES), col(5 * M_HEADS)],
        out_shape=[jax.ShapeDtypeStruct((nq, t), bf16), jax.ShapeDtypeStruct((t, nq), bf16),
                   jax.ShapeDtypeStruct((nv, t), bf16), jax.ShapeDtypeStruct((nv, t), bf16),
                   jax.ShapeDtypeStruct((t, LANES), f32), jax.ShapeDtypeStruct((5 * M_HEADS, t), f32)],
        scratch_shapes=[pltpu.VMEM((tm, d), bf16), pltpu.VMEM((tm, d), bf16)],
        compiler_params=_params(1),
        name="mlstm_proj",
    )(x, x, gain, wt, gate_bias)


def _mlstm_scan_kernel(qt_ref, k_ref, vt_ref, ot_ref, ccol_ref, rstat_ref, hn_ref, out_ref, ct_ref, m_ref):
    L = M_CHUNK
    dk, dv = M_QK_DIM, M_V_DIM
    pairs = M_HEADS // 2

    @pl.when(pl.program_id(1) == 0)
    def _():
        ct_ref[...] = jnp.zeros_like(ct_ref)
        m_ref[...] = jnp.zeros_like(m_ref)

    H = M_HEADS
    upper = lax.broadcasted_iota(jnp.int32, (L, L), 0) <= lax.broadcasted_iota(jnp.int32, (L, L), 1)
    lane = lax.broadcasted_iota(jnp.int32, (L, LANES), 1)
    low = lane < dk
    ones_rows = jnp.ones((BF16_ROWS, L), bf16)
    zeros_half = jnp.zeros((dk, L), bf16)

    stats = []
    for c in range(SCAN_CHUNKS):
        tok = slice(c * L, (c + 1) * L)
        stats.append(tuple(rstat_ref[i * H:(i + 1) * H, tok] for i in range(5)) + (ccol_ref[tok, :],))

    m = m_ref[...]
    m_prevs = []
    for c in range(SCAN_CHUNKS):
        m_prevs.append(m)
        m = jnp.maximum(stats[c][4] + m, stats[c][3])
    m_prevs.append(m)
    m_ref[...] = m

    def kq(c, j):
        tok = slice(c * L, (c + 1) * L)
        kp = k_ref[tok, j * LANES:(j + 1) * LANES]
        qtp = qt_ref[j * LANES:(j + 1) * LANES, tok]
        q_even = jnp.concatenate([qtp[0:dk, :], zeros_half], axis=0)
        q_odd = jnp.concatenate([zeros_half, qtp[dk:2 * dk, :]], axis=0)
        return kp, (q_even, q_odd), _dot(kp, jnp.concatenate([q_even, q_odd], axis=1))

    ct = [ct_ref[j] for j in range(pairs)]
    units = [(c, j) for c in range(SCAN_CHUNKS) for j in range(pairs)]
    nxt = kq(*units[0])
    for u, (c, j) in enumerate(units):
        kp, q_eo, st = nxt
        if u + 1 < len(units):
            nxt = kq(*units[u + 1])
        tok = slice(c * L, (c + 1) * L)
        b_r, cmax, tail, _, b_last, c_cols = stats[c]
        m_prev, m_new = m_prevs[c], m_prevs[c + 1]
        a_r = jnp.maximum(m_prev, cmax)
        a_r2 = a_r * LOG2E
        carry_w = jnp.exp(m_prev - a_r).astype(bf16)
        clamp_r = jnp.exp(-(a_r + b_r))
        ws_r = jnp.exp(tail - m_new).astype(bf16)
        decay = jnp.exp(b_last + m_prev - m_new)
        ct_pair_b = ct[j].astype(bf16)
        vws = []
        for e, q_e in enumerate(q_eo):
            h = 2 * j + e
            expo = jnp.where(upper, c_cols[:, h:h + 1] - a_r2[h:h + 1, :], NEG_BIG)
            wt = (st[:, e * L:(e + 1) * L] * jnp.exp2(expo)).astype(bf16)
            q_w = q_e * carry_w[h:h + 1, :]
            vext = jnp.concatenate([vt_ref[h * dv:(h + 1) * dv, tok], ones_rows], axis=0)
            nd = _dot(jnp.concatenate([vext, ct_pair_b], axis=1), jnp.concatenate([wt, q_w], axis=0))
            den = jnp.maximum(jnp.abs(nd[dv:dv + 1, :]), clamp_r[h:h + 1, :])
            inv = 1.0 / den
            num = nd[0:dv, :]
            scale = inv * lax.rsqrt(inv * inv * jnp.mean(num * num, axis=0, keepdims=True) + EPS)
            hn = num * scale * hn_ref[h * dv:(h + 1) * dv, :]
            gate_o = ot_ref[h * dv:(h + 1) * dv, tok].astype(f32)
            out_ref[h * dv:(h + 1) * dv, tok] = (gate_o * hn).astype(bf16)
            vws.append(vext * ws_r[h:h + 1, :])
        k_split = jnp.concatenate([jnp.where(low, kp, jnp.zeros_like(kp)),
                                   jnp.where(low, jnp.zeros_like(kp), kp)], axis=0)
        decay_pair = jnp.where(low[0:1, :], decay[2 * j:2 * j + 1, :], decay[2 * j + 1:2 * j + 2, :])
        ct[j] = decay_pair * ct[j] + _dot(jnp.concatenate(vws, axis=1), k_split)
    for j in range(pairs):
        ct_ref[j] = ct[j]


def _mlstm_scan(qt, k, vt, ot, ccol, rstat, hn_rep, batch, seq):
    tm = SCAN_CHUNKS * M_CHUNK
    nq = M_HEADS * M_QK_DIM
    nv = M_HEADS * M_V_DIM
    nt = seq // tm
    row = lambda n: pl.BlockSpec((tm, n), lambda b, i: (b * nt + i, 0))
    col = lambda n: pl.BlockSpec((n, tm), lambda b, i: (0, b * nt + i))
    return pl.pallas_call(
        _mlstm_scan_kernel,
        grid=(batch, nt),
        in_specs=[col(nq), row(nq), col(nv), col(nv), row(LANES), col(5 * M_HEADS), _resident(hn_rep.shape)],
        out_specs=col(nv),
        out_shape=jax.ShapeDtypeStruct((nv, batch * seq), bf16),
        scratch_shapes=[pltpu.VMEM((M_HEADS // 2, M_V_DIM + BF16_ROWS, 2 * M_QK_DIM), f32),
                        pltpu.VMEM((M_HEADS, LANES), f32)],
        compiler_params=_params(2),
        name="mlstm_scan",
    )(qt, k, vt, ot, ccol, rstat, hn_rep)


def _attn_proj_kernel(x0_ref, xnext_ref, g_ref, wt_ref, b_ref, qt_ref, k_ref, vt_ref, xn_ref, xnn_ref):
    tm = xnext_ref.shape[0]
    nkv = A_KV_HEADS * A_HEAD_DIM
    xn, prepare_next = _next_input_norm(pl.program_id(0), x0_ref, xnext_ref, g_ref, xn_ref, xnn_ref)
    nq = wt_ref.shape[0] - 2 * nkv
    proj = lambda r0, r1: _dot_nt(wt_ref[r0:r1, :], xn) + _tile_lanes(b_ref[r0:r1, :], tm // LANES)
    kv = proj(nq, nq + 2 * nkv)
    prepare_next()
    k_ref[...] = kv[0:nkv, :].T.astype(bf16)
    vt_ref[...] = kv[nkv:2 * nkv, :].astype(bf16)
    for r0 in range(0, nq, Q_ROWS_PER_DOT):
        q = proj(r0, r0 + Q_ROWS_PER_DOT)
        qt_ref[r0:r0 + Q_ROWS_PER_DOT, :] = (q * (A_HEAD_DIM ** -0.5 * LOG2E)).astype(bf16)


def _attn_proj(x, gain, wt, b_rep):
    t, d = x.shape
    nq = A_Q_HEADS * A_HEAD_DIM
    nkv = A_KV_HEADS * A_HEAD_DIM
    tm = TOKEN_TILE
    row = lambda n: pl.BlockSpec((tm, n), lambda i: (i, 0))
    col = lambda n: pl.BlockSpec((n, tm), lambda i: (0, i))
    return pl.pallas_call(
        _attn_proj_kernel,
        grid=(t // tm,),
        in_specs=_proj_in_specs(tm, d, t // tm) + [_resident((1, d)), _resident(wt.shape), _resident(b_rep.shape)],
        out_specs=[col(nq), row(nkv), col(nkv)],
        out_shape=[jax.ShapeDtypeStruct((nq, t), bf16), jax.ShapeDtypeStruct((t, nkv), bf16),
                   jax.ShapeDtypeStruct((nkv, t), bf16)],
        scratch_shapes=[pltpu.VMEM((tm, d), bf16), pltpu.VMEM((tm, d), bf16)],
        compiler_params=_params(1),
        name="attn_proj",
    )(x, x, gain, wt, b_rep)


def _swa_kernel(qt_ref, kc_ref, kp_ref, vtc_ref, vtp_ref, sink_ref, out_ref):
    blk = A_BLOCK
    dh = A_HEAD_DIM
    first = pl.program_id(1) == 0
    ku = lax.broadcasted_iota(jnp.int32, (2 * blk, blk), 0)
    qi = lax.broadcasted_iota(jnp.int32, (2 * blk, blk), 1)
    diff = qi - (ku - blk)
    band = (diff >= 0) & (diff < WINDOW)
    bias = jnp.where(band | (ku == 0), 0.0, NEG_BIG)
    bias_first = jnp.where((band & (ku >= blk)) | (ku == 0), 0.0, NEG_BIG)
    krow = lax.broadcasted_iota(jnp.int32, (2 * blk, LANES), 0)
    klane = lax.broadcasted_iota(jnp.int32, (2 * blk, LANES), 1)
    k_aug = jnp.where((krow == 0) & (klane < 3), 1.0, 0.0).astype(bf16)
    vcol = lax.broadcasted_iota(jnp.int32, (dh, 2 * blk), 1)
    ones_rows = jnp.ones((BF16_ROWS, 2 * blk), bf16)

    def logits(bq, g):
        cols = slice(bq * blk, (bq + 1) * blk)
        if bq == 0:
            kcat = jnp.concatenate([kp_ref[...], kc_ref[cols, :]], axis=0)
            b1 = jnp.where(first, bias_first, bias)
        else:
            kcat = kc_ref[(bq - 1) * blk:(bq + 1) * blk, :]
            b1 = bias
        in_group = (klane >= g * dh) & (klane < (g + 1) * dh) & (krow > 0)
        km = jnp.concatenate([jnp.where(in_group, kcat, jnp.zeros_like(kcat)), k_aug], axis=1)
        blocks = []
        for hh in range(A_GROUP):
            h = g * A_GROUP + hh
            pair = qt_ref[(h // 2) * LANES:(h // 2 + 1) * LANES, cols]
            if h % 2 != g:
                pair = jnp.concatenate([pair[dh:2 * dh, :], pair[0:dh, :]], axis=0)
            blocks.append(pair)
        rhs = jnp.concatenate([jnp.concatenate(blocks, axis=1), sink_ref[g]], axis=0)
        return _dot(km, rhs) + _tile_lanes(b1, A_GROUP)

    def finish(bq, g, st):
        cols = slice(bq * blk, (bq + 1) * blk)
        if bq == 0:
            vtcat = jnp.concatenate([vtp_ref[...], vtc_ref[:, cols]], axis=1)
        else:
            vtcat = vtc_ref[:, (bq - 1) * blk:(bq + 1) * blk]
        p = jnp.exp2(st - jnp.max(st, axis=0, keepdims=True)).astype(bf16)
        vt_g = jnp.where(vcol == 0, jnp.zeros((dh, 2 * blk), bf16), vtcat[g * dh:(g + 1) * dh, :])
        oext = _dot(jnp.concatenate([vt_g, ones_rows], axis=0), p)
        o = (oext[0:dh, :] * (1.0 / oext[dh:dh + 1, :])).astype(bf16)
        for hh in range(A_GROUP):
            h = g * A_GROUP + hh
            out_ref[h * dh:(h + 1) * dh, cols] = o[:, hh * blk:(hh + 1) * blk]

    units = [(bq, g) for bq in range(SWA_TILE // blk) for g in range(A_KV_HEADS)]
    nxt = logits(*units[0])
    for u, unit in enumerate(units):
        st = nxt
        if u + 1 < len(units):
            nxt = logits(*units[u + 1])
        finish(*unit, st)


def _swa(qt, k, vt, sink_aug, batch, seq):
    blk = A_BLOCK
    tq = SWA_TILE
    nq = A_Q_HEADS * A_HEAD_DIM
    nkv = A_KV_HEADS * A_HEAD_DIM
    nt = seq // tq
    per = tq // blk
    prev_idx = lambda b, i: b * nt * per + jnp.maximum(i * per - 1, 0)
    return pl.pallas_call(
        _swa_kernel,
        grid=(batch, nt),
        in_specs=[pl.BlockSpec((nq, tq), lambda b, i: (0, b * nt + i)),
                  pl.BlockSpec((tq, nkv), lambda b, i: (b * nt + i, 0)),
                  pl.BlockSpec((blk, nkv), lambda b, i: (prev_idx(b, i), 0)),
                  pl.BlockSpec((nkv, tq), lambda b, i: (0, b * nt + i)),
                  pl.BlockSpec((nkv, blk), lambda b, i: (0, prev_idx(b, i))),
                  _resident(sink_aug.shape)],
        out_specs=pl.BlockSpec((nq, tq), lambda b, i: (0, b * nt + i)),
        out_shape=jax.ShapeDtypeStruct((nq, batch * seq), bf16),
        compiler_params=_params(2),
        name="swa",
    )(qt, k, k, vt, vt, sink_aug)


def _sink_rows(sinks):
    parts = jnp.stack(_split3(sinks.astype(f32) * LOG2E), axis=0)
    per_lane = jnp.repeat(parts.reshape(3, A_KV_HEADS, A_GROUP), A_BLOCK, axis=2)
    return jnp.pad(per_lane.transpose(1, 0, 2), ((0, 0), (0, LANES - 3), (0, 0)))


def _out_ffn_kernel(h0_ref, at0_ref, hn_ref, atn_ref, wo_ref, bo_ref, gpost_ref, gpre_ref, gfpost_ref,
                    wup_ref, cw_ref, cb_ref, wd_ref, out_ref, carry_ref, act_ref, h1_ref, xn_ref, h1n_ref, xnn_ref, *,
                    tiles_per_seq):
    tm = hn_ref.shape[0]
    dff = wd_ref.shape[0]
    i = pl.program_id(0)

    d = wo_ref.shape[1]
    out_cols = [slice(n0, n0 + OUT_CHUNK) for n0 in range(0, d, OUT_CHUNK)]

    def out_proj(at_ref, cols):
        return _dot_tn(at_ref[...], wo_ref[:, cols]) + bo_ref[:, cols]

    def front(h_ref, z_parts):
        h1 = h_ref[...] + _rmsnorm(jnp.concatenate(z_parts, axis=1), gpost_ref[...])
        h1n_ref[...] = h1
        xnn_ref[...] = _rmsnorm(h1, gpre_ref[...]).astype(bf16)

    @pl.when(i == 0)
    def _():
        front(h0_ref, [out_proj(at0_ref, cols) for cols in out_cols])

    @pl.when(i % tiles_per_seq == 0)
    def _():
        carry_ref[...] = jnp.zeros_like(carry_ref)

    xn_ref[...] = xnn_ref[...]
    h1_ref[...] = h1n_ref[...]
    top = lax.broadcasted_iota(jnp.int32, (8, FF_CHUNK), 0)
    z_parts = []
    for ci, c0 in enumerate(range(0, dff, FF_CHUNK)):
        if FRONT_AFTER_CHUNKS <= ci < FRONT_AFTER_CHUNKS + len(out_cols):
            z_parts.append(out_proj(atn_ref, out_cols[ci - FRONT_AFTER_CHUNKS]))
        if ci == FRONT_AFTER_CHUNKS + len(out_cols):
            front(hn_ref, z_parts)
        cs = slice(c0, c0 + FF_CHUNK)
        gate = _dot(xn_ref[...], wup_ref[:, cs])
        val = _dot(xn_ref[...], wup_ref[:, dff + c0:dff + c0 + FF_CHUNK])
        prev = carry_ref[:, cs]
        carry_ref[:, cs] = gate[tm - 8:tm, :]
        shifted = []
        for lag in (1, 2):
            rolled = pltpu.roll(gate, lag, axis=0)
            head = jnp.where(top < lag, pltpu.roll(prev, lag, axis=0), rolled[0:8, :])
            shifted.append(jnp.concatenate([head, rolled[8:, :]], axis=0))
        g1, g2 = shifted
        gc = cb_ref[:, cs] + cw_ref[0:1, cs] * g2 + cw_ref[1:2, cs] * g1 + cw_ref[2:3, cs] * gate
        act_ref[:, cs] = (gc * jax.nn.sigmoid(gc) * val).astype(bf16)
    y = _dot(act_ref[...], wd_ref[...])
    out_ref[...] = h1_ref[...] + _rmsnorm(y, gfpost_ref[...])


def _out_ffn(h, at, wo, bo, gpost, gpre, gfpost, wup, cw, cb, wd, layer, seq):
    t, d = h.shape
    tm = TOKEN_TILE
    n = t // tm
    dff = wd.shape[1]
    nxt = lambda i: jnp.minimum(i + 1, n - 1)
    return pl.pallas_call(
        functools.partial(_out_ffn_kernel, tiles_per_seq=seq // tm),
        grid=(n,),
        in_specs=[pl.BlockSpec((tm, d), lambda i: (0, 0), pipeline_mode=pl.Buffered(1)),
                  pl.BlockSpec((at.shape[0], tm), lambda i: (0, 0), pipeline_mode=pl.Buffered(1)),
                  pl.BlockSpec((tm, d), lambda i: (nxt(i), 0)),
                  pl.BlockSpec((at.shape[0], tm), lambda i: (0, nxt(i))),
                  _resident(wo.shape), _resident((1, d)), _resident((1, d)),
                  _resident((1, d)), _resident((1, d)), _layer_resident(wup.shape, layer),
                  _resident(cw.shape), _resident(cb.shape), _layer_resident(wd.shape, layer)],
        out_specs=pl.BlockSpec((tm, d), lambda i: (i, 0)),
        out_shape=jax.ShapeDtypeStruct((t, d), f32),
        scratch_shapes=[pltpu.VMEM((8, dff), f32), pltpu.VMEM((tm, dff), bf16),
                        pltpu.VMEM((tm, d), f32), pltpu.VMEM((tm, d), bf16),
                        pltpu.VMEM((tm, d), f32), pltpu.VMEM((tm, d), bf16)],
        compiler_params=_params(1),
        name="out_ffn",
    )(h, at, h, at, wo, bo, gpost, gpre, gfpost, wup, cw, cb, wd)


def kernel(x, m_w_in, m_gate_bias, m_head_norm, m_w_out, a_w_in, a_b_in, a_sinks, a_w_out, a_b_out,
           norm_mix_pre, norm_mix_post, norm_ffn_pre, norm_ffn_post, f_w_up, f_conv_w, f_conv_b, f_w_down):
    batch, seq, d = x.shape
    depth = norm_mix_pre.shape[0]
    h = x.reshape(batch * seq, d)
    row = lambda vec: vec.reshape(1, -1).astype(f32)
    lane_rep = lambda vec: jnp.broadcast_to(vec.astype(f32)[:, None], (vec.shape[0], LANES))
    w_up = _cast_bf16(f_w_up, CAST_ROW_BLOCKS)
    w_down = _cast_bf16(f_w_down, CAST_ROW_BLOCKS)

    for i in range(depth):
        j = i // 2
        if i % 2 == 0:
            qt, k, vt, ot, ccol, rstat = _mlstm_proj(
                h, row(norm_mix_pre[i]), m_w_in[j].T.astype(bf16), lane_rep(m_gate_bias[j].reshape(-1)))
            mixed_t = _mlstm_scan(qt, k, vt, ot, ccol, rstat, lane_rep(m_head_norm[j]), batch, seq)
            w_out = m_w_out[j].astype(bf16)
            b_out = jnp.zeros((1, d), f32)
        else:
            qt, k, vt = _attn_proj(h, row(norm_mix_pre[i]), a_w_in[j].T.astype(bf16), lane_rep(a_b_in[j]))
            mixed_t = _swa(qt, k, vt, _sink_rows(a_sinks[j]), batch, seq)
            w_out = a_w_out[j].astype(bf16)
            b_out = row(a_b_out[j])
        h = _out_ffn(h, mixed_t, w_out, b_out, row(norm_mix_post[i]), row(norm_ffn_pre[i]), row(norm_ffn_post[i]),
                     w_up, f_conv_w[i].astype(f32), row(f_conv_b[i]), w_down, i, seq)
    return h.reshape(batch, seq, d)
```

```python
import functools

import jax
import jax.numpy as jnp
from jax import lax
from jax.experimental import pallas as pl
from jax.experimental.pallas import tpu as pltpu

EPS = 1e-6
LANES = 128
BF16_ROWS = 16

M_HEADS = 8
M_QK_DIM = 64
M_V_DIM = 128
GATE_CAP = 15.0
M_CHUNK = 128
SCAN_CHUNKS = 8

A_HEAD_DIM = 64
A_Q_HEADS = 16
A_KV_HEADS = 2
A_GROUP = A_Q_HEADS // A_KV_HEADS
WINDOW = 128
A_BLOCK = 128

LOG2E = 1.4426950408889634
NEG_BIG = -1e30

TOKEN_TILE = 512
PROJ_TILE = 1024
SWA_TILE = 1024
FF_CHUNK = 256
Q_ROWS_PER_DOT = 256
OUT_CHUNK = 256
FRONT_AFTER_CHUNKS = 2
CAST_ROW_BLOCKS = 4
VMEM_LIMIT = 60 * 1024 * 1024

bf16 = jnp.bfloat16
f32 = jnp.float32


def _dot(a, b):
    return jnp.dot(a, b, preferred_element_type=f32)


def _dot_nt(a, b):
    return lax.dot_general(a, b, (((1,), (1,)), ((), ())), preferred_element_type=f32)


def _dot_tn(a, b):
    return lax.dot_general(a, b, (((0,), (0,)), ((), ())), preferred_element_type=f32)


def _rmsnorm(x, g):
    return x * lax.rsqrt(jnp.mean(x * x, axis=-1, keepdims=True) + EPS) * g


def _split3(x):
    hi = x.astype(bf16)
    r1 = x - hi.astype(f32)
    mid = r1.astype(bf16)
    lo = (r1 - mid.astype(f32)).astype(bf16)
    return hi, mid, lo


def _log_sigmoid(x):
    return jnp.minimum(x, 0.0) - jnp.log1p(jnp.exp(-jnp.abs(x)))


def _tile_lanes(x, reps):
    return jnp.concatenate([x] * reps, axis=1)


def _next_input_norm(i, x0_ref, xnext_ref, g_ref, xn_ref, xnn_ref):
    @pl.when(i == 0)
    def _():
        xnn_ref[...] = _rmsnorm(x0_ref[...], g_ref[...]).astype(bf16)

    xn_ref[...] = xnn_ref[...]

    def prepare_next():
        xnn_ref[...] = _rmsnorm(xnext_ref[...], g_ref[...]).astype(bf16)

    return xn_ref[...], prepare_next


def _sigmoid(x):
    return 0.5 * jnp.tanh(0.5 * x) + 0.5


def _proj_in_specs(tm, d, n_tiles):
    return [pl.BlockSpec((tm, d), lambda i: (0, 0), pipeline_mode=pl.Buffered(1)),
            pl.BlockSpec((tm, d), lambda i: (jnp.minimum(i + 1, n_tiles - 1), 0))]


def _resident(shape):
    nd = len(shape)
    return pl.BlockSpec(shape, lambda *_: (0,) * nd, pipeline_mode=pl.Buffered(1))


def _layer_resident(shape, layer):
    nd = len(shape) - 1
    return pl.BlockSpec((None,) + tuple(shape[1:]), lambda *_: (layer,) + (0,) * nd, pipeline_mode=pl.Buffered(1))


def _params(n_axes):
    return pltpu.CompilerParams(
        dimension_semantics=("arbitrary",) * n_axes, vmem_limit_bytes=VMEM_LIMIT)


def _cast_kernel(src_ref, dst_ref):
    dst_ref[...] = src_ref[...].astype(dst_ref.dtype)


def _cast_bf16(w, row_blocks):
    depth, rows, cols = w.shape
    blk = rows // row_blocks
    spec = pl.BlockSpec((None, blk, cols), lambda l, i: (l, i, 0))
    return pl.pallas_call(
        _cast_kernel,
        grid=(depth, row_blocks),
        in_specs=[spec],
        out_specs=spec,
        out_shape=jax.ShapeDtypeStruct(w.shape, bf16),
        compiler_params=_params(2),
        name="cast_bf16",
    )(w)


def _mlstm_proj_kernel(x0_ref, xnext_ref, g_ref, wt_ref, gb_ref,
                       qt_ref, k_ref, vt_ref, ot_ref, ccol_ref, rstat_ref, xn_ref, xnn_ref):
    tm = xnext_ref.shape[0]
    L = M_CHUNK
    H = M_HEADS
    xn, prepare_next = _next_input_norm(pl.program_id(0), x0_ref, xnext_ref, g_ref, xn_ref, xnn_ref)
    nq = M_HEADS * M_QK_DIM
    nv = M_HEADS * M_V_DIM

    q_rows, k_rows, v_rows = slice(0, nq), slice(nq, 2 * nq), slice(2 * nq, 2 * nq + nv)
    og_rows = slice(2 * nq + nv, 2 * nq + 2 * nv + 2 * H)
    first = _dot_nt(wt_ref[og_rows, :], xn)
    prepare_next()
    raw = first[nv:, :] + _tile_lanes(gb_ref[...], tm // LANES)
    capped = GATE_CAP * jnp.tanh(raw * (1.0 / GATE_CAP))
    grow = jnp.where(lax.broadcasted_iota(jnp.int32, raw.shape, 0) < H, capped, _log_sigmoid(capped))
    ot_ref[...] = _sigmoid(first[0:nv, :]).astype(bf16)
    k_ref[...] = _dot_nt(wt_ref[k_rows, :], xn).T.astype(bf16)
    vt_ref[...] = _dot_nt(wt_ref[v_rows, :], xn).astype(bf16)

    triu = jnp.where(lax.broadcasted_iota(jnp.int32, (L, L), 0) <= lax.broadcasted_iota(jnp.int32, (L, L), 1),
                     1.0, 0.0).astype(bf16)
    nch = tm // L
    stacked = jnp.concatenate([grow[:, c * L:(c + 1) * L] for c in range(nch)], axis=0)
    sums = _dot(jnp.concatenate(_split3(stacked), axis=0), triu)
    cum = sums[0:nch * 2 * H] + sums[nch * 2 * H:2 * nch * 2 * H] + sums[2 * nch * 2 * H:]
    lane_h = lax.broadcasted_iota(jnp.int32, (H, L), 1)
    for c in range(nch):
        tok = slice(c * L, (c + 1) * L)
        b_r = cum[c * 2 * H + H:(c + 1) * 2 * H, :]
        c_r = grow[0:H, tok] - b_r
        cmax = c_r
        shift = 1
        while shift < L:
            cmax = jnp.where(lane_h >= shift, jnp.maximum(cmax, pltpu.roll(cmax, shift, axis=1)), cmax)
            shift *= 2
        b_last = jnp.broadcast_to(b_r[:, L - 1:L], (H, L))
        tail = c_r + b_last
        tail_max = jnp.broadcast_to(jnp.max(tail, axis=-1, keepdims=True), (H, L))
        rstat_ref[:, tok] = jnp.concatenate([b_r, cmax, tail, tail_max, b_last], axis=0)
        ccol_ref[tok, :] = jnp.concatenate([c_r * LOG2E, jnp.zeros((LANES - H, L), f32)], axis=0).T

    qt_ref[...] = (_dot_nt(wt_ref[q_rows, :], xn) * (M_QK_DIM ** -0.5)).astype(bf16)


def _mlstm_proj(x, gain, wt, gate_bias):
    t, d = x.shape
    nq = M_HEADS * M_QK_DIM
    nv = M_HEADS * M_V_DIM
    tm = PROJ_TILE
    row = lambda n: pl.BlockSpec((tm, n), lambda i: (i, 0))
    col = lambda n: pl.BlockSpec((n, tm), lambda i: (0, i))
    return pl.pallas_call(
        _mlstm_proj_kernel,
        grid=(t // tm,),
        in_specs=_proj_in_specs(tm, d, t // tm) + [
            _resident((1, d)), _resident(wt.shape), _resident(gate_bias.shape)],
        out_specs=[col(nq), row(nq), col(nv), col(nv), row(LANES), col(5 * M_HEADS)],
        out_shape=[jax.ShapeDtypeStruct((nq, t), bf16), jax.ShapeDtypeStruct((t, nq), bf16),
                   jax.ShapeDtypeStruct((nv, t), bf16), jax.ShapeDtypeStruct((nv, t), bf16),
                   jax.ShapeDtypeStruct((t, LANES), f32), jax.ShapeDtypeStruct((5 * M_HEADS, t), f32)],
        scratch_shapes=[pltpu.VMEM((tm, d), bf16), pltpu.VMEM((tm, d), bf16)],
        compiler_params=_params(1),
        name="mlstm_proj",
    )(x, x, gain, wt, gate_bias)


def _mlstm_scan_kernel(qt_ref, k_ref, vt_ref, ot_ref, ccol_ref, rstat_ref, hn_ref, out_ref, ct_ref, m_ref):
    L = M_CHUNK
    dk, dv = M_QK_DIM, M_V_DIM
    pairs = M_HEADS // 2

    @pl.when(pl.program_id(1) == 0)
    def _():
        ct_ref[...] = jnp.zeros_like(ct_ref)
        m_ref[...] = jnp.zeros_like(m_ref)

    H = M_HEADS
    upper = lax.broadcasted_iota(jnp.int32, (L, L), 0) <= lax.broadcasted_iota(jnp.int32, (L, L), 1)
    lane = lax.broadcasted_iota(jnp.int32, (L, LANES), 1)
    low = lane < dk
    ones_rows = jnp.ones((BF16_ROWS, L), bf16)
    zeros_half = jnp.zeros((dk, L), bf16)

    stats = []
    for c in range(SCAN_CHUNKS):
        tok = slice(c * L, (c + 1) * L)
        stats.append(tuple(rstat_ref[i * H:(i + 1) * H, tok] for i in range(5)) + (ccol_ref[tok, :],))

    m = m_ref[...]
    m_prevs = []
    for c in range(SCAN_CHUNKS):
        m_prevs.append(m)
        m = jnp.maximum(stats[c][4] + m, stats[c][3])
    m_prevs.append(m)
    m_ref[...] = m

    def kq(c, j):
        tok = slice(c * L, (c + 1) * L)
        kp = k_ref[tok, j * LANES:(j + 1) * LANES]
        qtp = qt_ref[j * LANES:(j + 1) * LANES, tok]
        q_even = jnp.concatenate([qtp[0:dk, :], zeros_half], axis=0)
        q_odd = jnp.concatenate([zeros_half, qtp[dk:2 * dk, :]], axis=0)
        return kp, (q_even, q_odd), _dot(kp, jnp.concatenate([q_even, q_odd], axis=1))

    ct = [ct_ref[j] for j in range(pairs)]
    units = [(c, j) for c in range(SCAN_CHUNKS) for j in range(pairs)]
    nxt = kq(*units[0])
    for u, (c, j) in enumerate(units):
        kp, q_eo, st = nxt
        if u + 1 < len(units):
            nxt = kq(*units[u + 1])
        tok = slice(c * L, (c + 1) * L)
        b_r, cmax, tail, _, b_last, c_cols = stats[c]
        m_prev, m_new = m_prevs[c], m_prevs[c + 1]
        a_r = jnp.maximum(m_prev, cmax)
        a_r2 = a_r * LOG2E
        carry_w = jnp.exp(m_prev - a_r).astype(bf16)
        clamp_r = jnp.exp(-(a_r + b_r))
        ws_r = jnp.exp(tail - m_new).astype(bf16)
        decay = jnp.exp(b_last + m_prev - m_new)
        ct_pair_b = ct[j].astype(bf16)
        vws = []
        for e, q_e in enumerate(q_eo):
            h = 2 * j + e
            expo = jnp.where(upper, c_cols[:, h:h + 1] - a_r2[h:h + 1, :], NEG_BIG)
            wt = (st[:, e * L:(e + 1) * L] * jnp.exp2(expo)).astype(bf16)
            q_w = q_e * carry_w[h:h + 1, :]
            vext = jnp.concatenate([vt_ref[h * dv:(h + 1) * dv, tok], ones_rows], axis=0)
            nd = _dot(jnp.concatenate([vext, ct_pair_b], axis=1), jnp.concatenate([wt, q_w], axis=0))
            den = jnp.maximum(jnp.abs(nd[dv:dv + 1, :]), clamp_r[h:h + 1, :])
            inv = 1.0 / den
            num = nd[0:dv, :]
            scale = inv * lax.rsqrt(inv * inv * jnp.mean(num * num, axis=0, keepdims=True) + EPS)
            hn = num * scale * hn_ref[h * dv:(h + 1) * dv, :]
            gate_o = ot_ref[h * dv:(h + 1) * dv, tok].astype(f32)
            out_ref[h * dv:(h + 1) * dv, tok] = (gate_o * hn).astype(bf16)
            vws.append(vext * ws_r[h:h + 1, :])
        k_split = jnp.concatenate([jnp.where(low, kp, jnp.zeros_like(kp)),
                                   jnp.where(low, jnp.zeros_like(kp), kp)], axis=0)
        decay_pair = jnp.where(low[0:1, :], decay[2 * j:2 * j + 1, :], decay[2 * j + 1:2 * j + 2, :])
        ct[j] = decay_pair * ct[j] + _dot(jnp.concatenate(vws, axis=1), k_split)
    for j in range(pairs):
        ct_ref[j] = ct[j]


def _mlstm_scan(qt, k, vt, ot, ccol, rstat, hn_rep, batch, seq):
    tm = SCAN_CHUNKS * M_CHUNK
    nq = M_HEADS * M_QK_DIM
    nv = M_HEADS * M_V_DIM
    nt = seq // tm
    row = lambda n: pl.BlockSpec((tm, n), lambda b, i: (b * nt + i, 0))
    col = lambda n: pl.BlockSpec((n, tm), lambda b, i: (0, b * nt + i))
    return pl.pallas_call(
        _mlstm_scan_kernel,
        grid=(batch, nt),
        in_specs=[col(nq), row(nq), col(nv), col(nv), row(LANES), col(5 * M_HEADS), _resident(hn_rep.shape)],
        out_specs=col(nv),
        out_shape=jax.ShapeDtypeStruct((nv, batch * seq), bf16),
        scratch_shapes=[pltpu.VMEM((M_HEADS // 2, M_V_DIM + BF16_ROWS, 2 * M_QK_DIM), f32),
                        pltpu.VMEM((M_HEADS, LANES), f32)],
        compiler_params=_params(2),
        name="mlstm_scan",
    )(qt, k, vt, ot, ccol, rstat, hn_rep)


def _attn_proj_kernel(x0_ref, xnext_ref, g_ref, wt_ref, b_ref, qt_ref, k_ref, vt_ref, xn_ref, xnn_ref):
    tm = xnext_ref.shape[0]
    nkv = A_KV_HEADS * A_HEAD_DIM
    xn, prepare_next = _next_input_norm(pl.program_id(0), x0_ref, xnext_ref, g_ref, xn_ref, xnn_ref)
    nq = wt_ref.shape[0] - 2 * nkv
    proj = lambda r0, r1: _dot_nt(wt_ref[r0:r1, :], xn) + _tile_lanes(b_ref[r0:r1, :], tm // LANES)
    kv = proj(nq, nq + 2 * nkv)
    prepare_next()
    k_ref[...] = kv[0:nkv, :].T.astype(bf16)
    vt_ref[...] = kv[nkv:2 * nkv, :].astype(bf16)
    for r0 in range(0, nq, Q_ROWS_PER_DOT):
        q = proj(r0, r0 + Q_ROWS_PER_DOT)
        qt_ref[r0:r0 + Q_ROWS_PER_DOT, :] = (q * (A_HEAD_DIM ** -0.5 * LOG2E)).astype(bf16)


def _attn_proj(x, gain, wt, b_rep):
    t, d = x.shape
    nq = A_Q_HEADS * A_HEAD_DIM
    nkv = A_KV_HEADS * A_HEAD_DIM
    tm = PROJ_TILE
    row = lambda n: pl.BlockSpec((tm, n), lambda i: (i, 0))
    col = lambda n: pl.BlockSpec((n, tm), lambda i: (0, i))
    return pl.pallas_call(
        _attn_proj_kernel,
        grid=(t // tm,),
        in_specs=_proj_in_specs(tm, d, t // tm) + [_resident((1, d)), _resident(wt.shape), _resident(b_rep.shape)],
        out_specs=[col(nq), row(nkv), col(nkv)],
        out_shape=[jax.ShapeDtypeStruct((nq, t), bf16), jax.ShapeDtypeStruct((t, nkv), bf16),
                   jax.ShapeDtypeStruct((nkv, t), bf16)],
        scratch_shapes=[pltpu.VMEM((tm, d), bf16), pltpu.VMEM((tm, d), bf16)],
        compiler_params=_params(1),
        name="attn_proj",
    )(x, x, gain, wt, b_rep)


def _swa_kernel(qt_ref, kc_ref, kp_ref, vtc_ref, vtp_ref, sink_ref, out_ref):
    blk = A_BLOCK
    dh = A_HEAD_DIM
    first = pl.program_id(1) == 0
    ku = lax.broadcasted_iota(jnp.int32, (2 * blk, blk), 0)
    qi = lax.broadcasted_iota(jnp.int32, (2 * blk, blk), 1)
    diff = qi - (ku - blk)
    band = (diff >= 0) & (diff < WINDOW)
    bias = jnp.where(band | (ku == 0), 0.0, NEG_BIG)
    bias_first = jnp.where((band & (ku >= blk)) | (ku == 0), 0.0, NEG_BIG)
    krow = lax.broadcasted_iota(jnp.int32, (2 * blk, LANES), 0)
    klane = lax.broadcasted_iota(jnp.int32, (2 * blk, LANES), 1)
    k_aug = jnp.where((krow == 0) & (klane < 3), 1.0, 0.0).astype(bf16)
    vcol = lax.broadcasted_iota(jnp.int32, (dh, 2 * blk), 1)
    ones_rows = jnp.ones((BF16_ROWS, 2 * blk), bf16)

    def logits(bq, g):
        cols = slice(bq * blk, (bq + 1) * blk)
        if bq == 0:
            kcat = jnp.concatenate([kp_ref[...], kc_ref[cols, :]], axis=0)
            b1 = jnp.where(first, bias_first, bias)
        else:
            kcat = kc_ref[(bq - 1) * blk:(bq + 1) * blk, :]
            b1 = bias
        in_group = (klane >= g * dh) & (klane < (g + 1) * dh) & (krow > 0)
        km = jnp.concatenate([jnp.where(in_group, kcat, jnp.zeros_like(kcat)), k_aug], axis=1)
        blocks = []
        for hh in range(A_GROUP):
            h = g * A_GROUP + hh
            pair = qt_ref[(h // 2) * LANES:(h // 2 + 1) * LANES, cols]
            if h % 2 != g:
                pair = jnp.concatenate([pair[dh:2 * dh, :], pair[0:dh, :]], axis=0)
            blocks.append(pair)
        rhs = jnp.concatenate([jnp.concatenate(blocks, axis=1), sink_ref[g]], axis=0)
        return _dot(km, rhs) + _tile_lanes(b1, A_GROUP)

    def finish(bq, g, st):
        cols = slice(bq * blk, (bq + 1) * blk)
        if bq == 0:
            vtcat = jnp.concatenate([vtp_ref[...], vtc_ref[:, cols]], axis=1)
        else:
            vtcat = vtc_ref[:, (bq - 1) * blk:(bq + 1) * blk]
        p = jnp.exp2(st - jnp.max(st, axis=0, keepdims=True)).astype(bf16)
        vt_g = jnp.where(vcol == 0, jnp.zeros((dh, 2 * blk), bf16), vtcat[g * dh:(g + 1) * dh, :])
        oext = _dot(jnp.concatenate([vt_g, ones_rows], axis=0), p)
        o = (oext[0:dh, :] * (1.0 / oext[dh:dh + 1, :])).astype(bf16)
        for hh in range(A_GROUP):
            h = g * A_GROUP + hh
            out_ref[h * dh:(h + 1) * dh, cols] = o[:, hh * blk:(hh + 1) * blk]

    units = [(bq, g) for bq in range(SWA_TILE // blk) for g in range(A_KV_HEADS)]
    nxt = logits(*units[0])
    for u, unit in enumerate(units):
        st = nxt
        if u + 1 < len(units):
            nxt = logits(*units[u + 1])
        finish(*unit, st)


def _swa(qt, k, vt, sink_aug, batch, seq):
    blk = A_BLOCK
    tq = SWA_TILE
    nq = A_Q_HEADS * A_HEAD_DIM
    nkv = A_KV_HEADS * A_HEAD_DIM
    nt = seq // tq
    per = tq // blk
    prev_idx = lambda b, i: b * nt * per + jnp.maximum(i * per - 1, 0)
    return pl.pallas_call(
        _swa_kernel,
        grid=(batch, nt),
        in_specs=[pl.BlockSpec((nq, tq), lambda b, i: (0, b * nt + i)),
                  pl.BlockSpec((tq, nkv), lambda b, i: (b * nt + i, 0)),
                  pl.BlockSpec((blk, nkv), lambda b, i: (prev_idx(b, i), 0)),
                  pl.BlockSpec((nkv, tq), lambda b, i: (0, b * nt + i)),
                  pl.BlockSpec((nkv, blk), lambda b, i: (0, prev_idx(b, i))),
                  _resident(sink_aug.shape)],
        out_specs=pl.BlockSpec((nq, tq), lambda b, i: (0, b * nt + i)),
        out_shape=jax.ShapeDtypeStruct((nq, batch * seq), bf16),
        compiler_params=_params(2),
        name="swa",
    )(qt, k, k, vt, vt, sink_aug)


def _sink_rows(sinks):
    parts = jnp.stack(_split3(sinks.astype(f32) * LOG2E), axis=0)
    per_lane = jnp.repeat(parts.reshape(3, A_KV_HEADS, A_GROUP), A_BLOCK, axis=2)
    return jnp.pad(per_lane.transpose(1, 0, 2), ((0, 0), (0, LANES - 3), (0, 0)))


def _out_ffn_kernel(h0_ref, at0_ref, hn_ref, atn_ref, wo_ref, bo_ref, gpost_ref, gpre_ref, gfpost_ref,
                    wup_ref, cw_ref, cb_ref, wd_ref, out_ref, carry_ref, act_ref, h1_ref, xn_ref, h1n_ref, xnn_ref, *,
                    tiles_per_seq):
    tm = hn_ref.shape[0]
    dff = wd_ref.shape[0]
    i = pl.program_id(0)

    d = wo_ref.shape[1]
    out_cols = [slice(n0, n0 + OUT_CHUNK) for n0 in range(0, d, OUT_CHUNK)]

    def out_proj(at_ref, cols):
        return _dot_tn(at_ref[...], wo_ref[:, cols]) + bo_ref[:, cols]

    def front(h_ref, z_parts):
        h1 = h_ref[...] + _rmsnorm(jnp.concatenate(z_parts, axis=1), gpost_ref[...])
        h1n_ref[...] = h1
        xnn_ref[...] = _rmsnorm(h1, gpre_ref[...]).astype(bf16)

    @pl.when(i == 0)
    def _():
        front(h0_ref, [out_proj(at0_ref, cols) for cols in out_cols])

    @pl.when(i % tiles_per_seq == 0)
    def _():
        carry_ref[...] = jnp.zeros_like(carry_ref)

    xn_ref[...] = xnn_ref[...]
    h1_ref[...] = h1n_ref[...]
    top = lax.broadcasted_iota(jnp.int32, (8, FF_CHUNK), 0)
    z_parts = []
    for ci, c0 in enumerate(range(0, dff, FF_CHUNK)):
        if FRONT_AFTER_CHUNKS <= ci < FRONT_AFTER_CHUNKS + len(out_cols):
            z_parts.append(out_proj(atn_ref, out_cols[ci - FRONT_AFTER_CHUNKS]))
        if ci == FRONT_AFTER_CHUNKS + len(out_cols):
            front(hn_ref, z_parts)
        cs = slice(c0, c0 + FF_CHUNK)
        gate = _dot(xn_ref[...], wup_ref[:, cs])
        val = _dot(xn_ref[...], wup_ref[:, dff + c0:dff + c0 + FF_CHUNK])
        prev = carry_ref[:, cs]
        carry_ref[:, cs] = gate[tm - 8:tm, :]
        shifted = []
        for lag in (1, 2):
            rolled = pltpu.roll(gate, lag, axis=0)
            head = jnp.where(top < lag, pltpu.roll(prev, lag, axis=0), rolled[0:8, :])
            shifted.append(jnp.concatenate([head, rolled[8:, :]], axis=0))
        g1, g2 = shifted
        gc = cb_ref[:, cs] + cw_ref[0:1, cs] * g2 + cw_ref[1:2, cs] * g1 + cw_ref[2:3, cs] * gate
        act_ref[:, cs] = (gc * jax.nn.sigmoid(gc) * val).astype(bf16)
    y = _dot(act_ref[...], wd_ref[...])
    out_ref[...] = h1_ref[...] + _rmsnorm(y, gfpost_ref[...])


def _out_ffn(h, at, wo, bo, gpost, gpre, gfpost, wup, cw, cb, wd, layer, seq):
    t, d = h.shape
    tm = TOKEN_TILE
    n = t // tm
    dff = wd.shape[1]
    nxt = lambda i: jnp.minimum(i + 1, n - 1)
    return pl.pallas_call(
        functools.partial(_out_ffn_kernel, tiles_per_seq=seq // tm),
        grid=(n,),
        in_specs=[pl.BlockSpec((tm, d), lambda i: (0, 0), pipeline_mode=pl.Buffered(1)),
                  pl.BlockSpec((at.shape[0], tm), lambda i: (0, 0), pipeline_mode=pl.Buffered(1)),
                  pl.BlockSpec((tm, d), lambda i: (nxt(i), 0)),
                  pl.BlockSpec((at.shape[0], tm), lambda i: (0, nxt(i))),
                  _resident(wo.shape), _resident((1, d)), _resident((1, d)),
                  _resident((1, d)), _resident((1, d)), _layer_resident(wup.shape, layer),
                  _resident(cw.shape), _resident(cb.shape), _layer_resident(wd.shape, layer)],
        out_specs=pl.BlockSpec((tm, d), lambda i: (i, 0)),
        out_shape=jax.ShapeDtypeStruct((t, d), f32),
        scratch_shapes=[pltpu.VMEM((8, dff), f32), pltpu.VMEM((tm, dff), bf16),
                        pltpu.VMEM((tm, d), f32), pltpu.VMEM((tm, d), bf16),
                        pltpu.VMEM((tm, d), f32), pltpu.VMEM((tm, d), bf16)],
        compiler_params=_params(1),
        name="out_ffn",
    )(h, at, h, at, wo, bo, gpost, gpre, gfpost, wup, cw, cb, wd)


def kernel(x, m_w_in, m_gate_bias, m_head_norm, m_w_out, a_w_in, a_b_in, a_sinks, a_w_out, a_b_out,
           norm_mix_pre, norm_mix_post, norm_ffn_pre, norm_ffn_post, f_w_up, f_conv_w, f_conv_b, f_w_down):
    batch, seq, d = x.shape
    depth = norm_mix_pre.shape[0]
    h = x.reshape(batch * seq, d)
    row = lambda vec: vec.reshape(1, -1).astype(f32)
    lane_rep = lambda vec: jnp.broadcast_to(vec.astype(f32)[:, None], (vec.shape[0], LANES))
    w_up = _cast_bf16(f_w_up, CAST_ROW_BLOCKS)
    w_down = _cast_bf16(f_w_down, CAST_ROW_BLOCKS)

    for i in range(depth):
        j = i // 2
        if i % 2 == 0:
            qt, k, vt, ot, ccol, rstat = _mlstm_proj(
                h, row(norm_mix_pre[i]), m_w_in[j].T.astype(bf16), lane_rep(m_gate_bias[j].reshape(-1)))
            mixed_t = _mlstm_scan(qt, k, vt, ot, ccol, rstat, lane_rep(m_head_norm[j]), batch, seq)
            w_out = m_w_out[j].astype(bf16)
            b_out = jnp.zeros((1, d), f32)
        else:
            qt, k, vt = _attn_proj(h, row(norm_mix_pre[i]), a_w_in[j].T.astype(bf16), lane_rep(a_b_in[j]))
            mixed_t = _swa(qt, k, vt, _sink_rows(a_sinks[j]), batch, seq)
            w_out = a_w_out[j].astype(bf16)
            b_out = row(a_b_out[j])
        h = _out_ffn(h, mixed_t, w_out, b_out, row(norm_mix_post[i]), row(norm_ffn_pre[i]), row(norm_ffn_post[i]),
                     w_up, f_conv_w[i].astype(f32), row(f_conv_b[i]), w_down, i, seq)
    return h.reshape(batch, seq, d)
```

```python
import functools

import jax
import jax.numpy as jnp
from jax import lax
from jax.experimental import pallas as pl
from jax.experimental.pallas import tpu as pltpu

EPS = 1e-6
LANES = 128
BF16_ROWS = 16

M_HEADS = 8
M_QK_DIM = 64
M_V_DIM = 128
GATE_CAP = 15.0
M_CHUNK = 128
SCAN_CHUNKS = 4

A_HEAD_DIM = 64
A_Q_HEADS = 16
A_KV_HEADS = 2
A_GROUP = A_Q_HEADS // A_KV_HEADS
WINDOW = 128
A_BLOCK = 128

LOG2E = 1.4426950408889634
NEG_BIG = -1e30

TOKEN_TILE = 512
PROJ_TILE = 1024
SWA_TILE = 512
SWA_LOOKAHEAD = 4
FF_CHUNK = 256
Q_ROWS_PER_DOT = 256
OUT_CHUNK = 256
FRONT_AFTER_CHUNKS = 2
CAST_ROW_BLOCKS = 4
VMEM_LIMIT = 60 * 1024 * 1024

bf16 = jnp.bfloat16
f32 = jnp.float32


def _dot(a, b):
    return jnp.dot(a, b, preferred_element_type=f32)


def _dot_nt(a, b):
    return lax.dot_general(a, b, (((1,), (1,)), ((), ())), preferred_element_type=f32)


def _dot_tn(a, b):
    return lax.dot_general(a, b, (((0,), (0,)), ((), ())), preferred_element_type=f32)


def _rmsnorm(x, g):
    return x * lax.rsqrt(jnp.mean(x * x, axis=-1, keepdims=True) + EPS) * g


def _split3(x):
    hi = x.astype(bf16)
    r1 = x - hi.astype(f32)
    mid = r1.astype(bf16)
    lo = (r1 - mid.astype(f32)).astype(bf16)
    return hi, mid, lo


def _log_sigmoid(x):
    return jnp.minimum(x, 0.0) - jnp.log1p(jnp.exp(-jnp.abs(x)))


def _tile_lanes(x, reps):
    return jnp.concatenate([x] * reps, axis=1)


def _next_input_norm(i, x0_ref, xnext_ref, g_ref, xn_ref, xnn_ref):
    @pl.when(i == 0)
    def _():
        xnn_ref[...] = _rmsnorm(x0_ref[...], g_ref[...]).astype(bf16)

    xn_ref[...] = xnn_ref[...]

    def prepare_next():
        xnn_ref[...] = _rmsnorm(xnext_ref[...], g_ref[...]).astype(bf16)

    return xn_ref[...], prepare_next


def _sigmoid(x):
    return 0.5 * jnp.tanh(0.5 * x) + 0.5


def _proj_in_specs(tm, d, n_tiles):
    return [pl.BlockSpec((tm, d), lambda i: (0, 0), pipeline_mode=pl.Buffered(1)),
            pl.BlockSpec((tm, d), lambda i: (jnp.minimum(i + 1, n_tiles - 1), 0))]


def _resident(shape):
    nd = len(shape)
    return pl.BlockSpec(shape, lambda *_: (0,) * nd, pipeline_mode=pl.Buffered(1))


def _layer_resident(shape, layer):
    nd = len(shape) - 1
    return pl.BlockSpec((None,) + tuple(shape[1:]), lambda *_: (layer,) + (0,) * nd, pipeline_mode=pl.Buffered(1))


def _params(n_axes):
    return pltpu.CompilerParams(
        dimension_semantics=("arbitrary",) * n_axes, vmem_limit_bytes=VMEM_LIMIT)


def _cast_kernel(src_ref, dst_ref):
    dst_ref[...] = src_ref[...].astype(dst_ref.dtype)


def _cast_bf16(w, row_blocks):
    depth, rows, cols = w.shape
    blk = rows // row_blocks
    spec = pl.BlockSpec((None, blk, cols), lambda l, i: (l, i, 0))
    return pl.pallas_call(
        _cast_kernel,
        grid=(depth, row_blocks),
        in_specs=[spec],
        out_specs=spec,
        out_shape=jax.ShapeDtypeStruct(w.shape, bf16),
        compiler_params=_params(2),
        name="cast_bf16",
    )(w)


def _mlstm_proj_kernel(x0_ref, xnext_ref, g_ref, wt_ref, gb_ref,
                       qt_ref, k_ref, vt_ref, ot_ref, ccol_ref, rstat_ref, xn_ref, xnn_ref):
    tm = xnext_ref.shape[0]
    L = M_CHUNK
    H = M_HEADS
    xn, prepare_next = _next_input_norm(pl.program_id(0), x0_ref, xnext_ref, g_ref, xn_ref, xnn_ref)
    nq = M_HEADS * M_QK_DIM
    nv = M_HEADS * M_V_DIM

    q_rows, k_rows, v_rows = slice(0, nq), slice(nq, 2 * nq), slice(2 * nq, 2 * nq + nv)
    og_rows = slice(2 * nq + nv, 2 * nq + 2 * nv + 2 * H)
    first = _dot_nt(wt_ref[og_rows, :], xn)
    prepare_next()
    raw = first[nv:, :] + _tile_lanes(gb_ref[...], tm // LANES)
    capped = GATE_CAP * jnp.tanh(raw * (1.0 / GATE_CAP))
    grow = jnp.where(lax.broadcasted_iota(jnp.int32, raw.shape, 0) < H, capped, _log_sigmoid(capped))
    ot_ref[...] = _sigmoid(first[0:nv, :]).astype(bf16)
    k_ref[...] = _dot_nt(wt_ref[k_rows, :], xn).T.astype(bf16)
    vt_ref[...] = _dot_nt(wt_ref[v_rows, :], xn).astype(bf16)

    triu = jnp.where(lax.broadcasted_iota(jnp.int32, (L, L), 0) <= lax.broadcasted_iota(jnp.int32, (L, L), 1),
                     1.0, 0.0).astype(bf16)
    nch = tm // L
    stacked = jnp.concatenate([grow[:, c * L:(c + 1) * L] for c in range(nch)], axis=0)
    sums = _dot(jnp.concatenate(_split3(stacked), axis=0), triu)
    cum = sums[0:nch * 2 * H] + sums[nch * 2 * H:2 * nch * 2 * H] + sums[2 * nch * 2 * H:]
    lane_h = lax.broadcasted_iota(jnp.int32, (H, L), 1)
    for c in range(nch):
        tok = slice(c * L, (c + 1) * L)
        b_r = cum[c * 2 * H + H:(c + 1) * 2 * H, :]
        c_r = grow[0:H, tok] - b_r
        cmax = c_r
        shift = 1
        while shift < L:
            cmax = jnp.where(lane_h >= shift, jnp.maximum(cmax, pltpu.roll(cmax, shift, axis=1)), cmax)
            shift *= 2
        b_last = jnp.broadcast_to(b_r[:, L - 1:L], (H, L))
        tail = c_r + b_last
        tail_max = jnp.broadcast_to(jnp.max(tail, axis=-1, keepdims=True), (H, L))
        rstat_ref[:, tok] = jnp.concatenate([b_r, cmax, tail, tail_max, b_last], axis=0)
        ccol_ref[tok, :] = jnp.concatenate([c_r * LOG2E, jnp.zeros((LANES - H, L), f32)], axis=0).T

    qt_ref[...] = (_dot_nt(wt_ref[q_rows, :], xn) * (M_QK_DIM ** -0.5)).astype(bf16)


def _mlstm_proj(x, gain, wt, gate_bias):
    t, d = x.shape
    nq = M_HEADS * M_QK_DIM
    nv = M_HEADS * M_V_DIM
    tm = PROJ_TILE
    row = lambda n: pl.BlockSpec((tm, n), lambda i: (i, 0))
    col = lambda n: pl.BlockSpec((n, tm), lambda i: (0, i))
    return pl.pallas_call(
        _mlstm_proj_kernel,
        grid=(t // tm,),
        in_specs=_proj_in_specs(tm, d, t // tm) + [
            _resident((1, d)), _resident(wt.shape), _resident(gate_bias.shape)],
        out_specs=[col(nq), row(nq), col(nv), col(nv), row(LANES), col(5 * M_HEADS)],
        out_shape=[jax.ShapeDtypeStruct((nq, t), bf16), jax.ShapeDtypeStruct((t, nq), bf16),
                   jax.ShapeDtypeStruct((nv, t), bf16), jax.ShapeDtypeStruct((nv, t), bf16),
                   jax.ShapeDtypeStruct((t, LANES), f32), jax.ShapeDtypeStruct((5 * M_HEADS, t), f32)],
        scratch_shapes=[pltpu.VMEM((tm, d), bf16), pltpu.VMEM((tm, d), bf16)],
        compiler_params=_params(1),
        name="mlstm_proj",
    )(x, x, gain, wt, gate_bias)


def _mlstm_scan_kernel(qt_ref, k_ref, vt_ref, ot_ref, ccol_ref, rstat_ref, hn_ref, out_ref, ct_ref, m_ref):
    L = M_CHUNK
    dk, dv = M_QK_DIM, M_V_DIM
    pairs = M_HEADS // 2

    @pl.when(pl.program_id(1) == 0)
    def _():
        ct_ref[...] = jnp.zeros_like(ct_ref)
        m_ref[...] = jnp.zeros_like(m_ref)

    H = M_HEADS
    upper = lax.broadcasted_iota(jnp.int32, (L, L), 0) <= lax.broadcasted_iota(jnp.int32, (L, L), 1)
    lane = lax.broadcasted_iota(jnp.int32, (L, LANES), 1)
    low = lane < dk
    ones_rows = jnp.ones((BF16_ROWS, L), bf16)
    zeros_half = jnp.zeros((dk, L), bf16)

    stats = []
    for c in range(SCAN_CHUNKS):
        tok = slice(c * L, (c + 1) * L)
        stats.append(tuple(rstat_ref[i * H:(i + 1) * H, tok] for i in range(5)) + (ccol_ref[tok, :],))

    m = m_ref[...]
    m_prevs = []
    for c in range(SCAN_CHUNKS):
        m_prevs.append(m)
        m = jnp.maximum(stats[c][4] + m, stats[c][3])
    m_prevs.append(m)
    m_ref[...] = m

    def kq(c, j):
        tok = slice(c * L, (c + 1) * L)
        kp = k_ref[tok, j * LANES:(j + 1) * LANES]
        qtp = qt_ref[j * LANES:(j + 1) * LANES, tok]
        q_even = jnp.concatenate([qtp[0:dk, :], zeros_half], axis=0)
        q_odd = jnp.concatenate([zeros_half, qtp[dk:2 * dk, :]], axis=0)
        return kp, (q_even, q_odd), _dot(kp, jnp.concatenate([q_even, q_odd], axis=1))

    ct = [ct_ref[j] for j in range(pairs)]
    units = [(c, j) for c in range(SCAN_CHUNKS) for j in range(pairs)]
    nxt = kq(*units[0])
    for u, (c, j) in enumerate(units):
        kp, q_eo, st = nxt
        if u + 1 < len(units):
            nxt = kq(*units[u + 1])
        tok = slice(c * L, (c + 1) * L)
        b_r, cmax, tail, _, b_last, c_cols = stats[c]
        m_prev, m_new = m_prevs[c], m_prevs[c + 1]
        a_r = jnp.maximum(m_prev, cmax)
        a_r2 = a_r * LOG2E
        carry_w = jnp.exp(m_prev - a_r).astype(bf16)
        clamp_r = jnp.exp(-(a_r + b_r))
        ws_r = jnp.exp(tail - m_new).astype(bf16)
        decay = jnp.exp(b_last + m_prev - m_new)
        ct_pair_b = ct[j].astype(bf16)
        vws = []
        for e, q_e in enumerate(q_eo):
            h = 2 * j + e
            expo = jnp.where(upper, c_cols[:, h:h + 1] - a_r2[h:h + 1, :], NEG_BIG)
            wt = (st[:, e * L:(e + 1) * L] * jnp.exp2(expo)).astype(bf16)
            q_w = q_e * carry_w[h:h + 1, :]
            vext = jnp.concatenate([vt_ref[h * dv:(h + 1) * dv, tok], ones_rows], axis=0)
            nd = _dot(jnp.concatenate([vext, ct_pair_b], axis=1), jnp.concatenate([wt, q_w], axis=0))
            den = jnp.maximum(jnp.abs(nd[dv:dv + 1, :]), clamp_r[h:h + 1, :])
            inv = 1.0 / den
            num = nd[0:dv, :]
            scale = inv * lax.rsqrt(inv * inv * jnp.mean(num * num, axis=0, keepdims=True) + EPS)
            hn = num * scale * hn_ref[h * dv:(h + 1) * dv, :]
            gate_o = ot_ref[h * dv:(h + 1) * dv, tok].astype(f32)
            out_ref[h * dv:(h + 1) * dv, tok] = (gate_o * hn).astype(bf16)
            vws.append(vext * ws_r[h:h + 1, :])
        k_split = jnp.concatenate([jnp.where(low, kp, jnp.zeros_like(kp)),
                                   jnp.where(low, jnp.zeros_like(kp), kp)], axis=0)
        decay_pair = jnp.where(low[0:1, :], decay[2 * j:2 * j + 1, :], decay[2 * j + 1:2 * j + 2, :])
        ct[j] = decay_pair * ct[j] + _dot(jnp.concatenate(vws, axis=1), k_split)
    for j in range(pairs):
        ct_ref[j] = ct[j]


def _mlstm_scan(qt, k, vt, ot, ccol, rstat, hn_rep, batch, seq):
    tm = SCAN_CHUNKS * M_CHUNK
    nq = M_HEADS * M_QK_DIM
    nv = M_HEADS * M_V_DIM
    nt = seq // tm
    row = lambda n: pl.BlockSpec((tm, n), lambda b, i: (b * nt + i, 0))
    col = lambda n: pl.BlockSpec((n, tm), lambda b, i: (0, b * nt + i))
    return pl.pallas_call(
        _mlstm_scan_kernel,
        grid=(batch, nt),
        in_specs=[col(nq), row(nq), col(nv), col(nv), row(LANES), col(5 * M_HEADS), _resident(hn_rep.shape)],
        out_specs=col(nv),
        out_shape=jax.ShapeDtypeStruct((nv, batch * seq), bf16),
        scratch_shapes=[pltpu.VMEM((M_HEADS // 2, M_V_DIM + BF16_ROWS, 2 * M_QK_DIM), f32),
                        pltpu.VMEM((M_HEADS, LANES), f32)],
        compiler_params=_params(2),
        name="mlstm_scan",
    )(qt, k, vt, ot, ccol, rstat, hn_rep)


def _attn_proj_kernel(x0_ref, xnext_ref, g_ref, wt_ref, b_ref, qt_ref, k_ref, vt_ref, xn_ref, xnn_ref):
    tm = xnext_ref.shape[0]
    nkv = A_KV_HEADS * A_HEAD_DIM
    xn, prepare_next = _next_input_norm(pl.program_id(0), x0_ref, xnext_ref, g_ref, xn_ref, xnn_ref)
    nq = wt_ref.shape[0] - 2 * nkv
    proj = lambda r0, r1: _dot_nt(wt_ref[r0:r1, :], xn) + _tile_lanes(b_ref[r0:r1, :], tm // LANES)
    kv = proj(nq, nq + 2 * nkv)
    prepare_next()
    k_ref[...] = kv[0:nkv, :].T.astype(bf16)
    vt_ref[...] = kv[nkv:2 * nkv, :].astype(bf16)
    for r0 in range(0, nq, Q_ROWS_PER_DOT):
        q = proj(r0, r0 + Q_ROWS_PER_DOT)
        qt_ref[r0:r0 + Q_ROWS_PER_DOT, :] = (q * (A_HEAD_DIM ** -0.5 * LOG2E)).astype(bf16)


def _attn_proj(x, gain, wt, b_rep):
    t, d = x.shape
    nq = A_Q_HEADS * A_HEAD_DIM
    nkv = A_KV_HEADS * A_HEAD_DIM
    tm = PROJ_TILE
    row = lambda n: pl.BlockSpec((tm, n), lambda i: (i, 0))
    col = lambda n: pl.BlockSpec((n, tm), lambda i: (0, i))
    return pl.pallas_call(
        _attn_proj_kernel,
        grid=(t // tm,),
        in_specs=_proj_in_specs(tm, d, t // tm) + [_resident((1, d)), _resident(wt.shape), _resident(b_rep.shape)],
        out_specs=[col(nq), row(nkv), col(nkv)],
        out_shape=[jax.ShapeDtypeStruct((nq, t), bf16), jax.ShapeDtypeStruct((t, nkv), bf16),
                   jax.ShapeDtypeStruct((nkv, t), bf16)],
        scratch_shapes=[pltpu.VMEM((tm, d), bf16), pltpu.VMEM((tm, d), bf16)],
        compiler_params=_params(1),
        name="attn_proj",
    )(x, x, gain, wt, b_rep)


def _swa_kernel(qt_ref, kc_ref, kp_ref, vtc_ref, vtp_ref, sink_ref, out_ref):
    blk = A_BLOCK
    dh = A_HEAD_DIM
    first = pl.program_id(1) == 0
    ku = lax.broadcasted_iota(jnp.int32, (2 * blk, blk), 0)
    qi = lax.broadcasted_iota(jnp.int32, (2 * blk, blk), 1)
    diff = qi - (ku - blk)
    band = (diff >= 0) & (diff < WINDOW)
    bias = jnp.where(band | (ku == 0), 0.0, NEG_BIG)
    bias_first = jnp.where((band & (ku >= blk)) | (ku == 0), 0.0, NEG_BIG)
    krow = lax.broadcasted_iota(jnp.int32, (2 * blk, LANES), 0)
    klane = lax.broadcasted_iota(jnp.int32, (2 * blk, LANES), 1)
    k_aug = jnp.where((krow == 0) & (klane < 3), 1.0, 0.0).astype(bf16)
    vcol = lax.broadcasted_iota(jnp.int32, (dh, 2 * blk), 1)
    ones_rows = jnp.ones((BF16_ROWS, 2 * blk), bf16)

    hp_lanes = 2 * blk
    shared = {}

    def operands(bq, g):
        if (bq, g) not in shared:
            cols = slice(bq * blk, (bq + 1) * blk)
            if bq == 0:
                kcat = jnp.concatenate([kp_ref[...], kc_ref[cols, :]], axis=0)
                vtcat = jnp.concatenate([vtp_ref[...], vtc_ref[:, cols]], axis=1)
                b1 = jnp.where(first, bias_first, bias)
            else:
                kcat = kc_ref[(bq - 1) * blk:(bq + 1) * blk, :]
                vtcat = vtc_ref[:, (bq - 1) * blk:(bq + 1) * blk]
                b1 = bias
            in_group = (klane >= g * dh) & (klane < (g + 1) * dh) & (krow > 0)
            km = jnp.concatenate([jnp.where(in_group, kcat, jnp.zeros_like(kcat)), k_aug], axis=1)
            vt_g = jnp.where(vcol == 0, jnp.zeros((dh, 2 * blk), bf16), vtcat[g * dh:(g + 1) * dh, :])
            shared[(bq, g)] = km, jnp.concatenate([vt_g, ones_rows], axis=0), _tile_lanes(b1, 2)
        return shared[(bq, g)]

    def logits(bq, g, hp):
        cols = slice(bq * blk, (bq + 1) * blk)
        km, _, b2 = operands(bq, g)
        blocks = []
        for e in range(2):
            h = g * A_GROUP + 2 * hp + e
            pair = qt_ref[(h // 2) * LANES:(h // 2 + 1) * LANES, cols]
            if h % 2 != g:
                pair = jnp.concatenate([pair[dh:2 * dh, :], pair[0:dh, :]], axis=0)
            blocks.append(pair)
        sink_rows = sink_ref[g][:, hp * hp_lanes:(hp + 1) * hp_lanes]
        rhs = jnp.concatenate([jnp.concatenate(blocks, axis=1), sink_rows], axis=0)
        return _dot(km, rhs) + b2

    def finish(bq, g, hp, st):
        cols = slice(bq * blk, (bq + 1) * blk)
        _, vext, _ = operands(bq, g)
        p = jnp.exp2(st - jnp.max(st, axis=0, keepdims=True)).astype(bf16)
        oext = _dot(vext, p)
        o = (oext[0:dh, :] * (1.0 / oext[dh:dh + 1, :])).astype(bf16)
        for e in range(2):
            h = g * A_GROUP + 2 * hp + e
            out_ref[h * dh:(h + 1) * dh, cols] = o[:, e * blk:(e + 1) * blk]

    units = [(bq, g, hp) for bq in range(SWA_TILE // blk) for g in range(A_KV_HEADS) for hp in range(A_GROUP // 2)]
    queued = [logits(*unit) for unit in units[:SWA_LOOKAHEAD]]
    for u, unit in enumerate(units):
        if u + SWA_LOOKAHEAD < len(units):
            queued.append(logits(*units[u + SWA_LOOKAHEAD]))
        finish(*unit, queued[u])


def _swa(qt, k, vt, sink_aug, batch, seq):
    blk = A_BLOCK
    tq = SWA_TILE
    nq = A_Q_HEADS * A_HEAD_DIM
    nkv = A_KV_HEADS * A_HEAD_DIM
    nt = seq // tq
    per = tq // blk
    prev_idx = lambda b, i: b * nt * per + jnp.maximum(i * per - 1, 0)
    return pl.pallas_call(
        _swa_kernel,
        grid=(batch, nt),
        in_specs=[pl.BlockSpec((nq, tq), lambda b, i: (0, b * nt + i)),
                  pl.BlockSpec((tq, nkv), lambda b, i: (b * nt + i, 0)),
                  pl.BlockSpec((blk, nkv), lambda b, i: (prev_idx(b, i), 0)),
                  pl.BlockSpec((nkv, tq), lambda b, i: (0, b * nt + i)),
                  pl.BlockSpec((nkv, blk), lambda b, i: (0, prev_idx(b, i))),
                  _resident(sink_aug.shape)],
        out_specs=pl.BlockSpec((nq, tq), lambda b, i: (0, b * nt + i)),
        out_shape=jax.ShapeDtypeStruct((nq, batch * seq), bf16),
        compiler_params=_params(2),
        name="swa",
    )(qt, k, k, vt, vt, sink_aug)


def _sink_rows(sinks):
    parts = jnp.stack(_split3(sinks.astype(f32) * LOG2E), axis=0)
    per_lane = jnp.repeat(parts.reshape(3, A_KV_HEADS, A_GROUP), A_BLOCK, axis=2)
    return jnp.pad(per_lane.transpose(1, 0, 2), ((0, 0), (0, LANES - 3), (0, 0)))


def _out_ffn_kernel(h0_ref, at0_ref, hn_ref, atn_ref, wo_ref, bo_ref, gpost_ref, gpre_ref, gfpost_ref,
                    wup_ref, cw_ref, cb_ref, wd_ref, out_ref, carry_ref, act_ref, h1_ref, xn_ref, h1n_ref, xnn_ref, *,
                    tiles_per_seq):
    tm = hn_ref.shape[0]
    dff = wd_ref.shape[0]
    i = pl.program_id(0)

    d = wo_ref.shape[1]
    out_cols = [slice(n0, n0 + OUT_CHUNK) for n0 in range(0, d, OUT_CHUNK)]

    def out_proj(at_ref, cols):
        return _dot_tn(at_ref[...], wo_ref[:, cols]) + bo_ref[:, cols]

    def front(h_ref, z_parts):
        h1 = h_ref[...] + _rmsnorm(jnp.concatenate(z_parts, axis=1), gpost_ref[...])
        h1n_ref[...] = h1
        xnn_ref[...] = _rmsnorm(h1, gpre_ref[...]).astype(bf16)

    @pl.when(i == 0)
    def _():
        front(h0_ref, [out_proj(at0_ref, cols) for cols in out_cols])

    @pl.when(i % tiles_per_seq == 0)
    def _():
        carry_ref[...] = jnp.zeros_like(carry_ref)

    xn_ref[...] = xnn_ref[...]
    h1_ref[...] = h1n_ref[...]
    top = lax.broadcasted_iota(jnp.int32, (8, FF_CHUNK), 0)
    z_parts = []
    for ci, c0 in enumerate(range(0, dff, FF_CHUNK)):
        if FRONT_AFTER_CHUNKS <= ci < FRONT_AFTER_CHUNKS + len(out_cols):
            z_parts.append(out_proj(atn_ref, out_cols[ci - FRONT_AFTER_CHUNKS]))
        if ci == FRONT_AFTER_CHUNKS + len(out_cols):
            front(hn_ref, z_parts)
        cs = slice(c0, c0 + FF_CHUNK)
        gate = _dot(xn_ref[...], wup_ref[:, cs])
        val = _dot(xn_ref[...], wup_ref[:, dff + c0:dff + c0 + FF_CHUNK])
        prev = carry_ref[:, cs]
        carry_ref[:, cs] = gate[tm - 8:tm, :]
        shifted = []
        for lag in (1, 2):
            rolled = pltpu.roll(gate, lag, axis=0)
            head = jnp.where(top < lag, pltpu.roll(prev, lag, axis=0), rolled[0:8, :])
            shifted.append(jnp.concatenate([head, rolled[8:, :]], axis=0))
        g1, g2 = shifted
        gc = cb_ref[:, cs] + cw_ref[0:1, cs] * g2 + cw_ref[1:2, cs] * g1 + cw_ref[2:3, cs] * gate
        act_ref[:, cs] = (gc * jax.nn.sigmoid(gc) * val).astype(bf16)
    y = _dot(act_ref[...], wd_ref[...])
    out_ref[...] = h1_ref[...] + _rmsnorm(y, gfpost_ref[...])


def _out_ffn(h, at, wo, bo, gpost, gpre, gfpost, wup, cw, cb, wd, layer, seq):
    t, d = h.shape
    tm = TOKEN_TILE
    n = t // tm
    dff = wd.shape[1]
    nxt = lambda i: jnp.minimum(i + 1, n - 1)
    return pl.pallas_call(
        functools.partial(_out_ffn_kernel, tiles_per_seq=seq // tm),
        grid=(n,),
        in_specs=[pl.BlockSpec((tm, d), lambda i: (0, 0), pipeline_mode=pl.Buffered(1)),
                  pl.BlockSpec((at.shape[0], tm), lambda i: (0, 0), pipeline_mode=pl.Buffered(1)),
                  pl.BlockSpec((tm, d), lambda i: (nxt(i), 0)),
                  pl.BlockSpec((at.shape[0], tm), lambda i: (0, nxt(i))),
                  _resident(wo.shape), _resident((1, d)), _resident((1, d)),
                  _resident((1, d)), _resident((1, d)), _layer_resident(wup.shape, layer),
                  _resident(cw.shape), _resident(cb.shape), _layer_resident(wd.shape, layer)],
        out_specs=pl.BlockSpec((tm, d), lambda i: (i, 0)),
        out_shape=jax.ShapeDtypeStruct((t, d), f32),
        scratch_shapes=[pltpu.VMEM((8, dff), f32), pltpu.VMEM((tm, dff), bf16),
                        pltpu.VMEM((tm, d), f32), pltpu.VMEM((tm, d), bf16),
                        pltpu.VMEM((tm, d), f32), pltpu.VMEM((tm, d), bf16)],
        compiler_params=_params(1),
        name="out_ffn",
    )(h, at, h, at, wo, bo, gpost, gpre, gfpost, wup, cw, cb, wd)


def kernel(x, m_w_in, m_gate_bias, m_head_norm, m_w_out, a_w_in, a_b_in, a_sinks, a_w_out, a_b_out,
           norm_mix_pre, norm_mix_post, norm_ffn_pre, norm_ffn_post, f_w_up, f_conv_w, f_conv_b, f_w_down):
    batch, seq, d = x.shape
    depth = norm_mix_pre.shape[0]
    h = x.reshape(batch * seq, d)
    row = lambda vec: vec.reshape(1, -1).astype(f32)
    lane_rep = lambda vec: jnp.broadcast_to(vec.astype(f32)[:, None], (vec.shape[0], LANES))
    w_up = _cast_bf16(f_w_up, CAST_ROW_BLOCKS)
    w_down = _cast_bf16(f_w_down, CAST_ROW_BLOCKS)

    for i in range(depth):
        j = i // 2
        if i % 2 == 0:
            qt, k, vt, ot, ccol, rstat = _mlstm_proj(
                h, row(norm_mix_pre[i]), m_w_in[j].T.astype(bf16), lane_rep(m_gate_bias[j].reshape(-1)))
            mixed_t = _mlstm_scan(qt, k, vt, ot, ccol, rstat, lane_rep(m_head_norm[j]), batch, seq)
            w_out = m_w_out[j].astype(bf16)
            b_out = jnp.zeros((1, d), f32)
        else:
            qt, k, vt = _attn_proj(h, row(norm_mix_pre[i]), a_w_in[j].T.astype(bf16), lane_rep(a_b_in[j]))
            mixed_t = _swa(qt, k, vt, _sink_rows(a_sinks[j]), batch, seq)
            w_out = a_w_out[j].astype(bf16)
            b_out = row(a_b_out[j])
        h = _out_ffn(h, mixed_t, w_out, b_out, row(norm_mix_post[i]), row(norm_ffn_pre[i]), row(norm_ffn_post[i]),
                     w_up, f_conv_w[i].astype(f32), row(f_conv_b[i]), w_down, i, seq)
    return h.reshape(batch, seq, d)
```

```python
import functools

import jax
import jax.numpy as jnp
from jax import lax
from jax.experimental import pallas as pl
from jax.experimental.pallas import tpu as pltpu

EPS = 1e-6
LANES = 128
BF16_ROWS = 16

M_HEADS = 8
M_QK_DIM = 64
M_V_DIM = 128
GATE_CAP = 15.0
M_CHUNK = 128
SCAN_CHUNKS = 4

A_HEAD_DIM = 64
A_Q_HEADS = 16
A_KV_HEADS = 2
A_GROUP = A_Q_HEADS // A_KV_HEADS
WINDOW = 128
A_BLOCK = 128

LOG2E = 1.4426950408889634
NEG_BIG = -1e30

TOKEN_TILE = 512
PROJ_TILE = 1024
SWA_TILE = 512
SWA_LOOKAHEAD = 3
FF_CHUNK = 256
Q_ROWS_PER_DOT = 256
OUT_CHUNK = 256
FRONT_AFTER_CHUNKS = 2
CAST_ROW_BLOCKS = 4
VMEM_LIMIT = 60 * 1024 * 1024

bf16 = jnp.bfloat16
f32 = jnp.float32


def _dot(a, b):
    return jnp.dot(a, b, preferred_element_type=f32)


def _dot_nt(a, b):
    return lax.dot_general(a, b, (((1,), (1,)), ((), ())), preferred_element_type=f32)


def _dot_tn(a, b):
    return lax.dot_general(a, b, (((0,), (0,)), ((), ())), preferred_element_type=f32)


def _rmsnorm(x, g):
    return x * lax.rsqrt(jnp.mean(x * x, axis=-1, keepdims=True) + EPS) * g


def _split3(x):
    hi = x.astype(bf16)
    r1 = x - hi.astype(f32)
    mid = r1.astype(bf16)
    lo = (r1 - mid.astype(f32)).astype(bf16)
    return hi, mid, lo


def _log_sigmoid(x):
    return jnp.minimum(x, 0.0) - jnp.log1p(jnp.exp(-jnp.abs(x)))


def _tile_lanes(x, reps):
    return jnp.concatenate([x] * reps, axis=1)


def _next_input_norm(i, x0_ref, xnext_ref, g_ref, xn_ref, xnn_ref):
    @pl.when(i == 0)
    def _():
        xnn_ref[...] = _rmsnorm(x0_ref[...], g_ref[...]).astype(bf16)

    xn_ref[...] = xnn_ref[...]

    def prepare_next():
        xnn_ref[...] = _rmsnorm(xnext_ref[...], g_ref[...]).astype(bf16)

    return xn_ref[...], prepare_next


def _sigmoid(x):
    return 0.5 * jnp.tanh(0.5 * x) + 0.5


def _proj_in_specs(tm, d, n_tiles):
    return [pl.BlockSpec((tm, d), lambda i: (0, 0), pipeline_mode=pl.Buffered(1)),
            pl.BlockSpec((tm, d), lambda i: (jnp.minimum(i + 1, n_tiles - 1), 0))]


def _resident(shape):
    nd = len(shape)
    return pl.BlockSpec(shape, lambda *_: (0,) * nd, pipeline_mode=pl.Buffered(1))


def _layer_resident(shape, layer):
    nd = len(shape) - 1
    return pl.BlockSpec((None,) + tuple(shape[1:]), lambda *_: (layer,) + (0,) * nd, pipeline_mode=pl.Buffered(1))


def _params(n_axes):
    return pltpu.CompilerParams(
        dimension_semantics=("arbitrary",) * n_axes, vmem_limit_bytes=VMEM_LIMIT)


def _cast_kernel(src_ref, dst_ref):
    dst_ref[...] = src_ref[...].astype(dst_ref.dtype)


def _cast_bf16(w, row_blocks):
    depth, rows, cols = w.shape
    blk = rows // row_blocks
    spec = pl.BlockSpec((None, blk, cols), lambda l, i: (l, i, 0))
    return pl.pallas_call(
        _cast_kernel,
        grid=(depth, row_blocks),
        in_specs=[spec],
        out_specs=spec,
        out_shape=jax.ShapeDtypeStruct(w.shape, bf16),
        compiler_params=_params(2),
        name="cast_bf16",
    )(w)


def _mlstm_proj_kernel(x0_ref, xnext_ref, g_ref, wt_ref, gb_ref,
                       qt_ref, k_ref, vt_ref, ot_ref, ccol_ref, rstat_ref, xn_ref, xnn_ref):
    tm = xnext_ref.shape[0]
    L = M_CHUNK
    H = M_HEADS
    xn, prepare_next = _next_input_norm(pl.program_id(0), x0_ref, xnext_ref, g_ref, xn_ref, xnn_ref)
    nq = M_HEADS * M_QK_DIM
    nv = M_HEADS * M_V_DIM

    q_rows, k_rows, v_rows = slice(0, nq), slice(nq, 2 * nq), slice(2 * nq, 2 * nq + nv)
    og_rows = slice(2 * nq + nv, 2 * nq + 2 * nv + 2 * H)
    first = _dot_nt(wt_ref[og_rows, :], xn)
    prepare_next()
    raw = first[nv:, :] + _tile_lanes(gb_ref[...], tm // LANES)
    capped = GATE_CAP * jnp.tanh(raw * (1.0 / GATE_CAP))
    grow = jnp.where(lax.broadcasted_iota(jnp.int32, raw.shape, 0) < H, capped, _log_sigmoid(capped))
    ot_ref[...] = _sigmoid(first[0:nv, :]).astype(bf16)
    k_ref[...] = _dot_nt(wt_ref[k_rows, :], xn).T.astype(bf16)
    vt_ref[...] = _dot_nt(wt_ref[v_rows, :], xn).astype(bf16)

    triu = jnp.where(lax.broadcasted_iota(jnp.int32, (L, L), 0) <= lax.broadcasted_iota(jnp.int32, (L, L), 1),
                     1.0, 0.0).astype(bf16)
    nch = tm // L
    stacked = jnp.concatenate([grow[:, c * L:(c + 1) * L] for c in range(nch)], axis=0)
    sums = _dot(jnp.concatenate(_split3(stacked), axis=0), triu)
    cum = sums[0:nch * 2 * H] + sums[nch * 2 * H:2 * nch * 2 * H] + sums[2 * nch * 2 * H:]
    lane_h = lax.broadcasted_iota(jnp.int32, (H, L), 1)
    for c in range(nch):
        tok = slice(c * L, (c + 1) * L)
        b_r = cum[c * 2 * H + H:(c + 1) * 2 * H, :]
        c_r = grow[0:H, tok] - b_r
        cmax = c_r
        shift = 1
        while shift < L:
            cmax = jnp.where(lane_h >= shift, jnp.maximum(cmax, pltpu.roll(cmax, shift, axis=1)), cmax)
            shift *= 2
        b_last = jnp.broadcast_to(b_r[:, L - 1:L], (H, L))
        tail = c_r + b_last
        tail_max = jnp.broadcast_to(jnp.max(tail, axis=-1, keepdims=True), (H, L))
        rstat_ref[:, tok] = jnp.concatenate([b_r, cmax, tail, tail_max, b_last], axis=0)
        ccol_ref[tok, :] = jnp.concatenate([c_r * LOG2E, jnp.zeros((LANES - H, L), f32)], axis=0).T

    qt_ref[...] = (_dot_nt(wt_ref[q_rows, :], xn) * (M_QK_DIM ** -0.5)).astype(bf16)


def _mlstm_proj(x, gain, wt, gate_bias):
    t, d = x.shape
    nq = M_HEADS * M_QK_DIM
    nv = M_HEADS * M_V_DIM
    tm = PROJ_TILE
    row = lambda n: pl.BlockSpec((tm, n), lambda i: (i, 0))
    col = lambda n: pl.BlockSpec((n, tm), lambda i: (0, i))
    return pl.pallas_call(
        _mlstm_proj_kernel,
        grid=(t // tm,),
        in_specs=_proj_in_specs(tm, d, t // tm) + [
            _resident((1, d)), _resident(wt.shape), _resident(gate_bias.shape)],
        out_specs=[col(nq), row(nq), col(nv), col(nv), row(LANES), col(5 * M_HEADS)],
        out_shape=[jax.ShapeDtypeStruct((nq, t), bf16), jax.ShapeDtypeStruct((t, nq), bf16),
                   jax.ShapeDtypeStruct((nv, t), bf16), jax.ShapeDtypeStruct((nv, t), bf16),
                   jax.ShapeDtypeStruct((t, LANES), f32), jax.ShapeDtypeStruct((5 * M_HEADS, t), f32)],
        scratch_shapes=[pltpu.VMEM((tm, d), bf16), pltpu.VMEM((tm, d), bf16)],
        compiler_params=_params(1),
        name="mlstm_proj",
    )(x, x, gain, wt, gate_bias)


def _mlstm_scan_kernel(qt_ref, k_ref, vt_ref, ot_ref, ccol_ref, rstat_ref, hn_ref, out_ref, ct_ref, m_ref):
    L = M_CHUNK
    dk, dv = M_QK_DIM, M_V_DIM
    pairs = M_HEADS // 2

    @pl.when(pl.program_id(1) == 0)
    def _():
        ct_ref[...] = jnp.zeros_like(ct_ref)
        m_ref[...] = jnp.zeros_like(m_ref)

    H = M_HEADS
    upper = lax.broadcasted_iota(jnp.int32, (L, L), 0) <= lax.broadcasted_iota(jnp.int32, (L, L), 1)
    lane = lax.broadcasted_iota(jnp.int32, (L, LANES), 1)
    low = lane < dk
    ones_rows = jnp.ones((BF16_ROWS, L), bf16)
    zeros_half = jnp.zeros((dk, L), bf16)

    stats = []
    for c in range(SCAN_CHUNKS):
        tok = slice(c * L, (c + 1) * L)
        stats.append(tuple(rstat_ref[i * H:(i + 1) * H, tok] for i in range(5)) + (ccol_ref[tok, :],))

    m = m_ref[...]
    m_prevs = []
    for c in range(SCAN_CHUNKS):
        m_prevs.append(m)
        m = jnp.maximum(stats[c][4] + m, stats[c][3])
    m_prevs.append(m)
    m_ref[...] = m

    def kq(c, j):
        tok = slice(c * L, (c + 1) * L)
        kp = k_ref[tok, j * LANES:(j + 1) * LANES]
        qtp = qt_ref[j * LANES:(j + 1) * LANES, tok]
        q_even = jnp.concatenate([qtp[0:dk, :], zeros_half], axis=0)
        q_odd = jnp.concatenate([zeros_half, qtp[dk:2 * dk, :]], axis=0)
        return kp, (q_even, q_odd), _dot(kp, jnp.concatenate([q_even, q_odd], axis=1))

    ct = [ct_ref[j] for j in range(pairs)]
    units = [(c, j) for c in range(SCAN_CHUNKS) for j in range(pairs)]
    nxt = kq(*units[0])
    for u, (c, j) in enumerate(units):
        kp, q_eo, st = nxt
        if u + 1 < len(units):
            nxt = kq(*units[u + 1])
        tok = slice(c * L, (c + 1) * L)
        b_r, cmax, tail, _, b_last, c_cols = stats[c]
        m_prev, m_new = m_prevs[c], m_prevs[c + 1]
        a_r = jnp.maximum(m_prev, cmax)
        a_r2 = a_r * LOG2E
        carry_w = jnp.exp(m_prev - a_r).astype(bf16)
        clamp_r = jnp.exp(-(a_r + b_r))
        ws_r = jnp.exp(tail - m_new).astype(bf16)
        decay = jnp.exp(b_last + m_prev - m_new)
        ct_pair_b = ct[j].astype(bf16)
        vws = []
        for e, q_e in enumerate(q_eo):
            h = 2 * j + e
            expo = jnp.where(upper, c_cols[:, h:h + 1] - a_r2[h:h + 1, :], NEG_BIG)
            wt = (st[:, e * L:(e + 1) * L] * jnp.exp2(expo)).astype(bf16)
            q_w = q_e * carry_w[h:h + 1, :]
            vext = jnp.concatenate([vt_ref[h * dv:(h + 1) * dv, tok], ones_rows], axis=0)
            nd = _dot(jnp.concatenate([vext, ct_pair_b], axis=1), jnp.concatenate([wt, q_w], axis=0))
            den = jnp.maximum(jnp.abs(nd[dv:dv + 1, :]), clamp_r[h:h + 1, :])
            inv = 1.0 / den
            num = nd[0:dv, :]
            scale = inv * lax.rsqrt(inv * inv * jnp.mean(num * num, axis=0, keepdims=True) + EPS)
            hn = num * scale * hn_ref[h * dv:(h + 1) * dv, :]
            gate_o = ot_ref[h * dv:(h + 1) * dv, tok].astype(f32)
            out_ref[h * dv:(h + 1) * dv, tok] = (gate_o * hn).astype(bf16)
            vws.append(vext * ws_r[h:h + 1, :])
        k_split = jnp.concatenate([jnp.where(low, kp, jnp.zeros_like(kp)),
                                   jnp.where(low, jnp.zeros_like(kp), kp)], axis=0)
        decay_pair = jnp.where(low[0:1, :], decay[2 * j:2 * j + 1, :], decay[2 * j + 1:2 * j + 2, :])
        ct[j] = decay_pair * ct[j] + _dot(jnp.concatenate(vws, axis=1), k_split)
    for j in range(pairs):
        ct_ref[j] = ct[j]


def _mlstm_scan(qt, k, vt, ot, ccol, rstat, hn_rep, batch, seq):
    tm = SCAN_CHUNKS * M_CHUNK
    nq = M_HEADS * M_QK_DIM
    nv = M_HEADS * M_V_DIM
    nt = seq // tm
    row = lambda n: pl.BlockSpec((tm, n), lambda b, i: (b * nt + i, 0))
    col = lambda n: pl.BlockSpec((n, tm), lambda b, i: (0, b * nt + i))
    return pl.pallas_call(
        _mlstm_scan_kernel,
        grid=(batch, nt),
        in_specs=[col(nq), row(nq), col(nv), col(nv), row(LANES), col(5 * M_HEADS), _resident(hn_rep.shape)],
        out_specs=col(nv),
        out_shape=jax.ShapeDtypeStruct((nv, batch * seq), bf16),
        scratch_shapes=[pltpu.VMEM((M_HEADS // 2, M_V_DIM + BF16_ROWS, 2 * M_QK_DIM), f32),
                        pltpu.VMEM((M_HEADS, LANES), f32)],
        compiler_params=_params(2),
        name="mlstm_scan",
    )(qt, k, vt, ot, ccol, rstat, hn_rep)


def _attn_proj_kernel(x0_ref, xnext_ref, g_ref, wt_ref, b_ref, qt_ref, k_ref, vt_ref, xn_ref, xnn_ref):
    tm = xnext_ref.shape[0]
    nkv = A_KV_HEADS * A_HEAD_DIM
    xn, prepare_next = _next_input_norm(pl.program_id(0), x0_ref, xnext_ref, g_ref, xn_ref, xnn_ref)
    nq = wt_ref.shape[0] - 2 * nkv
    proj = lambda r0, r1: _dot_nt(wt_ref[r0:r1, :], xn) + _tile_lanes(b_ref[r0:r1, :], tm // LANES)
    kv = proj(nq, nq + 2 * nkv)
    prepare_next()
    k_ref[...] = kv[0:nkv, :].T.astype(bf16)
    vt_ref[...] = kv[nkv:2 * nkv, :].astype(bf16)
    for r0 in range(0, nq, Q_ROWS_PER_DOT):
        q = proj(r0, r0 + Q_ROWS_PER_DOT)
        qt_ref[r0:r0 + Q_ROWS_PER_DOT, :] = (q * (A_HEAD_DIM ** -0.5 * LOG2E)).astype(bf16)


def _attn_proj(x, gain, wt, b_rep):
    t, d = x.shape
    nq = A_Q_HEADS * A_HEAD_DIM
    nkv = A_KV_HEADS * A_HEAD_DIM
    tm = PROJ_TILE
    row = lambda n: pl.BlockSpec((tm, n), lambda i: (i, 0))
    col = lambda n: pl.BlockSpec((n, tm), lambda i: (0, i))
    return pl.pallas_call(
        _attn_proj_kernel,
        grid=(t // tm,),
        in_specs=_proj_in_specs(tm, d, t // tm) + [_resident((1, d)), _resident(wt.shape), _resident(b_rep.shape)],
        out_specs=[col(nq), row(nkv), col(nkv)],
        out_shape=[jax.ShapeDtypeStruct((nq, t), bf16), jax.ShapeDtypeStruct((t, nkv), bf16),
                   jax.ShapeDtypeStruct((nkv, t), bf16)],
        scratch_shapes=[pltpu.VMEM((tm, d), bf16), pltpu.VMEM((tm, d), bf16)],
        compiler_params=_params(1),
        name="attn_proj",
    )(x, x, gain, wt, b_rep)


def _swa_kernel(qt_ref, kc_ref, kp_ref, vtc_ref, vtp_ref, sink_ref, out_ref):
    blk = A_BLOCK
    dh = A_HEAD_DIM
    first = pl.program_id(1) == 0
    ku = lax.broadcasted_iota(jnp.int32, (2 * blk, blk), 0)
    qi = lax.broadcasted_iota(jnp.int32, (2 * blk, blk), 1)
    diff = qi - (ku - blk)
    band = (diff >= 0) & (diff < WINDOW)
    bias = jnp.where(band | (ku == 0), 0.0, NEG_BIG)
    bias_first = jnp.where((band & (ku >= blk)) | (ku == 0), 0.0, NEG_BIG)
    krow = lax.broadcasted_iota(jnp.int32, (2 * blk, LANES), 0)
    klane = lax.broadcasted_iota(jnp.int32, (2 * blk, LANES), 1)
    k_aug = jnp.where((krow == 0) & (klane < 3), 1.0, 0.0).astype(bf16)
    vcol = lax.broadcasted_iota(jnp.int32, (dh, 2 * blk), 1)
    ones_rows = jnp.ones((BF16_ROWS, 2 * blk), bf16)

    hp_lanes = 2 * blk
    shared = {}

    def operands(bq, g):
        if (bq, g) not in shared:
            cols = slice(bq * blk, (bq + 1) * blk)
            if bq == 0:
                kcat = jnp.concatenate([kp_ref[...], kc_ref[cols, :]], axis=0)
                vtcat = jnp.concatenate([vtp_ref[...], vtc_ref[:, cols]], axis=1)
                b1 = jnp.where(first, bias_first, bias)
            else:
                kcat = kc_ref[(bq - 1) * blk:(bq + 1) * blk, :]
                vtcat = vtc_ref[:, (bq - 1) * blk:(bq + 1) * blk]
                b1 = bias
            in_group = (klane >= g * dh) & (klane < (g + 1) * dh) & (krow > 0)
            km = jnp.concatenate([jnp.where(in_group, kcat, jnp.zeros_like(kcat)), k_aug], axis=1)
            vt_g = jnp.where(vcol == 0, jnp.zeros((dh, 2 * blk), bf16), vtcat[g * dh:(g + 1) * dh, :])
            shared[(bq, g)] = km, jnp.concatenate([vt_g, ones_rows], axis=0), _tile_lanes(b1, 2)
        return shared[(bq, g)]

    def logits(bq, g, hp):
        cols = slice(bq * blk, (bq + 1) * blk)
        km, _, b2 = operands(bq, g)
        blocks = []
        for e in range(2):
            h = g * A_GROUP + 2 * hp + e
            pair = qt_ref[(h // 2) * LANES:(h // 2 + 1) * LANES, cols]
            if h % 2 != g:
                pair = jnp.concatenate([pair[dh:2 * dh, :], pair[0:dh, :]], axis=0)
            blocks.append(pair)
        sink_rows = sink_ref[g][:, hp * hp_lanes:(hp + 1) * hp_lanes]
        rhs = jnp.concatenate([jnp.concatenate(blocks, axis=1), sink_rows], axis=0)
        return _dot(km, rhs) + b2

    def finish(bq, g, hp, st):
        cols = slice(bq * blk, (bq + 1) * blk)
        _, vext, _ = operands(bq, g)
        p = jnp.exp2(st - jnp.max(st, axis=0, keepdims=True)).astype(bf16)
        oext = _dot(vext, p)
        o = (oext[0:dh, :] * (1.0 / oext[dh:dh + 1, :])).astype(bf16)
        for e in range(2):
            h = g * A_GROUP + 2 * hp + e
            out_ref[h * dh:(h + 1) * dh, cols] = o[:, e * blk:(e + 1) * blk]

    units = [(bq, g, hp) for bq in range(SWA_TILE // blk) for g in range(A_KV_HEADS) for hp in range(A_GROUP // 2)]
    queued = [logits(*unit) for unit in units[:SWA_LOOKAHEAD]]
    for u, unit in enumerate(units):
        if u + SWA_LOOKAHEAD < len(units):
            queued.append(logits(*units[u + SWA_LOOKAHEAD]))
        finish(*unit, queued[u])


def _swa(qt, k, vt, sink_aug, batch, seq):
    blk = A_BLOCK
    tq = SWA_TILE
    nq = A_Q_HEADS * A_HEAD_DIM
    nkv = A_KV_HEADS * A_HEAD_DIM
    nt = seq // tq
    per = tq // blk
    prev_idx = lambda b, i: b * nt * per + jnp.maximum(i * per - 1, 0)
    return pl.pallas_call(
        _swa_kernel,
        grid=(batch, nt),
        in_specs=[pl.BlockSpec((nq, tq), lambda b, i: (0, b * nt + i)),
                  pl.BlockSpec((tq, nkv), lambda b, i: (b * nt + i, 0)),
                  pl.BlockSpec((blk, nkv), lambda b, i: (prev_idx(b, i), 0)),
                  pl.BlockSpec((nkv, tq), lambda b, i: (0, b * nt + i)),
                  pl.BlockSpec((nkv, blk), lambda b, i: (0, prev_idx(b, i))),
                  _resident(sink_aug.shape)],
        out_specs=pl.BlockSpec((nq, tq), lambda b, i: (0, b * nt + i)),
        out_shape=jax.ShapeDtypeStruct((nq, batch * seq), bf16),
        compiler_params=_params(2),
        name="swa",
    )(qt, k, k, vt, vt, sink_aug)


def _sink_rows(sinks):
    parts = jnp.stack(_split3(sinks.astype(f32) * LOG2E), axis=0)
    per_lane = jnp.repeat(parts.reshape(3, A_KV_HEADS, A_GROUP), A_BLOCK, axis=2)
    return jnp.pad(per_lane.transpose(1, 0, 2), ((0, 0), (0, LANES - 3), (0, 0)))


def _out_ffn_kernel(h0_ref, at0_ref, hn_ref, atn_ref, wo_ref, wup_ref, wd_ref, vec_ref, conv_ref,
                    out_ref, act_ref, h1_ref, xn_ref, h1n_ref, xnn_ref, carry_ref, *, tiles_per_seq):
    tm = hn_ref.shape[0]
    dff = wd_ref.shape[0]
    i = pl.program_id(0)

    d = wo_ref.shape[1]
    out_cols = [slice(n0, n0 + OUT_CHUNK) for n0 in range(0, d, OUT_CHUNK)]

    def out_proj(at_ref, cols):
        return _dot_tn(at_ref[...], wo_ref[:, cols]) + vec_ref[0:1, cols]

    def front(h_ref, z_parts):
        h1 = h_ref[...] + _rmsnorm(jnp.concatenate(z_parts, axis=1), vec_ref[1:2, :])
        h1n_ref[...] = h1
        xnn_ref[...] = _rmsnorm(h1, vec_ref[2:3, :]).astype(bf16)

    @pl.when(i == 0)
    def _():
        front(h0_ref, [out_proj(at0_ref, cols) for cols in out_cols])

    @pl.when(i % tiles_per_seq == 0)
    def _():
        carry_ref[...] = jnp.zeros_like(carry_ref)

    xn_ref[...] = xnn_ref[...]
    h1_ref[...] = h1n_ref[...]
    top = lax.broadcasted_iota(jnp.int32, (8, FF_CHUNK), 0)
    z_parts = []
    for ci, c0 in enumerate(range(0, dff, FF_CHUNK)):
        if FRONT_AFTER_CHUNKS <= ci < FRONT_AFTER_CHUNKS + len(out_cols):
            z_parts.append(out_proj(atn_ref, out_cols[ci - FRONT_AFTER_CHUNKS]))
        if ci == FRONT_AFTER_CHUNKS + len(out_cols):
            front(hn_ref, z_parts)
        cs = slice(c0, c0 + FF_CHUNK)
        gate = _dot(xn_ref[...], wup_ref[:, cs])
        val = _dot(xn_ref[...], wup_ref[:, dff + c0:dff + c0 + FF_CHUNK])
        prev = carry_ref[0:8, cs]
        carry_ref[0:8, cs] = gate[tm - 8:tm, :]
        shifted = []
        for lag in (1, 2):
            rolled = pltpu.roll(gate, lag, axis=0)
            head = jnp.where(top < lag, pltpu.roll(prev, lag, axis=0), rolled[0:8, :])
            shifted.append(jnp.concatenate([head, rolled[8:, :]], axis=0))
        g1, g2 = shifted
        gc = conv_ref[3:4, cs] + conv_ref[0:1, cs] * g2 + conv_ref[1:2, cs] * g1 + conv_ref[2:3, cs] * gate
        act_ref[:, cs] = (gc * jax.nn.sigmoid(gc) * val).astype(bf16)
    y = _dot(act_ref[...], wd_ref[...])
    out_ref[...] = h1_ref[...] + _rmsnorm(y, vec_ref[3:4, :])


def _out_ffn(h, at, wo, wup, wd, vecs, conv, layer, seq):
    t, d = h.shape
    tm = TOKEN_TILE
    n = t // tm
    dff = wd.shape[1]
    nxt = lambda i: jnp.minimum(i + 1, n - 1)
    return pl.pallas_call(
        functools.partial(_out_ffn_kernel, tiles_per_seq=seq // tm),
        grid=(n,),
        in_specs=[pl.BlockSpec((tm, d), lambda i: (0, 0), pipeline_mode=pl.Buffered(1)),
                  pl.BlockSpec((at.shape[0], tm), lambda i: (0, 0), pipeline_mode=pl.Buffered(1)),
                  pl.BlockSpec((tm, d), lambda i: (nxt(i), 0)),
                  pl.BlockSpec((at.shape[0], tm), lambda i: (0, nxt(i))),
                  _resident(wo.shape), _layer_resident(wup.shape, layer), _layer_resident(wd.shape, layer),
                  _resident(vecs.shape), _resident(conv.shape)],
        out_specs=pl.BlockSpec((tm, d), lambda i: (i, 0)),
        out_shape=jax.ShapeDtypeStruct((t, d), f32),
        scratch_shapes=[pltpu.VMEM((tm, dff), bf16),
                        pltpu.VMEM((tm, d), f32), pltpu.VMEM((tm, d), bf16),
                        pltpu.VMEM((tm, d), f32), pltpu.VMEM((tm, d), bf16), pltpu.VMEM((16, dff), f32)],
        compiler_params=_params(1),
        name="out_ffn",
    )(h, at, h, at, wo, wup, wd, vecs, conv)


def kernel(x, m_w_in, m_gate_bias, m_head_norm, m_w_out, a_w_in, a_b_in, a_sinks, a_w_out, a_b_out,
           norm_mix_pre, norm_mix_post, norm_ffn_pre, norm_ffn_post, f_w_up, f_conv_w, f_conv_b, f_w_down):
    batch, seq, d = x.shape
    depth = norm_mix_pre.shape[0]
    h = x.reshape(batch * seq, d)
    row = lambda vec: vec.reshape(1, -1).astype(f32)
    lane_rep = lambda vec: jnp.broadcast_to(vec.astype(f32)[:, None], (vec.shape[0], LANES))
    w_up = _cast_bf16(f_w_up, CAST_ROW_BLOCKS)
    w_down = _cast_bf16(f_w_down, CAST_ROW_BLOCKS)

    for i in range(depth):
        j = i // 2
        if i % 2 == 0:
            qt, k, vt, ot, ccol, rstat = _mlstm_proj(
                h, row(norm_mix_pre[i]), m_w_in[j].T.astype(bf16), lane_rep(m_gate_bias[j].reshape(-1)))
            mixed_t = _mlstm_scan(qt, k, vt, ot, ccol, rstat, lane_rep(m_head_norm[j]), batch, seq)
            w_out = m_w_out[j].astype(bf16)
            b_out = jnp.zeros((d,), f32)
        else:
            qt, k, vt = _attn_proj(h, row(norm_mix_pre[i]), a_w_in[j].T.astype(bf16), lane_rep(a_b_in[j]))
            mixed_t = _swa(qt, k, vt, _sink_rows(a_sinks[j]), batch, seq)
            w_out = a_w_out[j].astype(bf16)
            b_out = a_b_out[j]
        vecs = jnp.pad(jnp.stack([b_out, norm_mix_post[i], norm_ffn_pre[i], norm_ffn_post[i]]).astype(f32),
                       ((0, 4), (0, 0)))
        conv = jnp.pad(jnp.concatenate([f_conv_w[i], f_conv_b[i][None]], axis=0).astype(f32), ((0, 12), (0, 0)))
        h = _out_ffn(h, mixed_t, w_out, w_up, w_down, vecs, conv, i, seq)
    return h.reshape(batch, seq, d)
```

```python
import functools

import jax
import jax.numpy as jnp
from jax import lax
from jax.experimental import pallas as pl
from jax.experimental.pallas import tpu as pltpu

EPS = 1e-6
LANES = 128
BF16_ROWS = 16

M_HEADS = 8
M_QK_DIM = 64
M_V_DIM = 128
GATE_CAP = 15.0
M_CHUNK = 128
SCAN_CHUNKS = 4

A_HEAD_DIM = 64
A_Q_HEADS = 16
A_KV_HEADS = 2
A_GROUP = A_Q_HEADS // A_KV_HEADS
WINDOW = 128
A_BLOCK = 128

LOG2E = 1.4426950408889634
NEG_BIG = -1e30

TOKEN_TILE = 512
PROJ_TILE = 1024
SWA_TILE = 512
SWA_LOOKAHEAD = 4
FF_CHUNK = 256
Q_ROWS_PER_DOT = 256
OUT_CHUNK = 256
FRONT_AFTER_CHUNKS = 2
VMEM_LIMIT = 60 * 1024 * 1024

bf16 = jnp.bfloat16
f32 = jnp.float32


def _dot(a, b):
    return jnp.dot(a, b, preferred_element_type=f32)


def _dot_nt(a, b):
    return lax.dot_general(a, b, (((1,), (1,)), ((), ())), preferred_element_type=f32)


def _dot_tn(a, b):
    return lax.dot_general(a, b, (((0,), (0,)), ((), ())), preferred_element_type=f32)


def _rmsnorm(x, g):
    return x * lax.rsqrt(jnp.mean(x * x, axis=-1, keepdims=True) + EPS) * g


def _split3(x):
    hi = x.astype(bf16)
    r1 = x - hi.astype(f32)
    mid = r1.astype(bf16)
    lo = (r1 - mid.astype(f32)).astype(bf16)
    return hi, mid, lo


def _log_sigmoid(x):
    return jnp.minimum(x, 0.0) - jnp.log1p(jnp.exp(-jnp.abs(x)))


def _tile_lanes(x, reps):
    return jnp.concatenate([x] * reps, axis=1)


def _next_input_norm(i, x0_ref, xnext_ref, g_ref, xn_ref, xnn_ref):
    @pl.when(i == 0)
    def _():
        xnn_ref[...] = _rmsnorm(x0_ref[...], g_ref[...]).astype(bf16)

    xn_ref[...] = xnn_ref[...]

    def prepare_next():
        xnn_ref[...] = _rmsnorm(xnext_ref[...], g_ref[...]).astype(bf16)

    return xn_ref[...], prepare_next


def _sigmoid(x):
    return 0.5 * jnp.tanh(0.5 * x) + 0.5


def _proj_in_specs(tm, d, n_tiles):
    return [pl.BlockSpec((tm, d), lambda i: (0, 0), pipeline_mode=pl.Buffered(1)),
            pl.BlockSpec((tm, d), lambda i: (jnp.minimum(i + 1, n_tiles - 1), 0))]


def _resident(shape):
    nd = len(shape)
    return pl.BlockSpec(shape, lambda *_: (0,) * nd, pipeline_mode=pl.Buffered(1))


def _layer_resident(shape, layer):
    nd = len(shape) - 1
    return pl.BlockSpec((None,) + tuple(shape[1:]), lambda *_: (layer,) + (0,) * nd, pipeline_mode=pl.Buffered(1))


def _params(n_axes):
    return pltpu.CompilerParams(
        dimension_semantics=("arbitrary",) * n_axes, vmem_limit_bytes=VMEM_LIMIT)


def _mlstm_proj_kernel(x0_ref, xnext_ref, g_ref, wt_ref, gb_ref, wup_ref, wdown_ref,
                       qt_ref, k_ref, vt_ref, ot_ref, ccol_ref, rstat_ref, wup_bf_ref, wdown_bf_ref, xn_ref, xnn_ref):
    tm = xnext_ref.shape[0]
    L = M_CHUNK
    H = M_HEADS
    xn, prepare_next = _next_input_norm(pl.program_id(0), x0_ref, xnext_ref, g_ref, xn_ref, xnn_ref)
    nq = M_HEADS * M_QK_DIM
    nv = M_HEADS * M_V_DIM

    q_rows, k_rows, v_rows = slice(0, nq), slice(nq, 2 * nq), slice(2 * nq, 2 * nq + nv)
    og_rows = slice(2 * nq + nv, 2 * nq + 2 * nv + 2 * H)
    first = _dot_nt(wt_ref[og_rows, :], xn)
    prepare_next()
    raw = first[nv:, :] + _tile_lanes(gb_ref[...], tm // LANES)
    capped = GATE_CAP * jnp.tanh(raw * (1.0 / GATE_CAP))
    grow = jnp.where(lax.broadcasted_iota(jnp.int32, raw.shape, 0) < H, capped, _log_sigmoid(capped))
    ot_ref[...] = _sigmoid(first[0:nv, :]).astype(bf16)
    k_ref[...] = _dot_nt(wt_ref[k_rows, :], xn).T.astype(bf16)
    vt_ref[...] = _dot_nt(wt_ref[v_rows, :], xn).astype(bf16)

    triu = jnp.where(lax.broadcasted_iota(jnp.int32, (L, L), 0) <= lax.broadcasted_iota(jnp.int32, (L, L), 1),
                     1.0, 0.0).astype(bf16)
    nch = tm // L
    stacked = jnp.concatenate([grow[:, c * L:(c + 1) * L] for c in range(nch)], axis=0)
    sums = _dot(jnp.concatenate(_split3(stacked), axis=0), triu)
    cum = sums[0:nch * 2 * H] + sums[nch * 2 * H:2 * nch * 2 * H] + sums[2 * nch * 2 * H:]
    lane_h = lax.broadcasted_iota(jnp.int32, (H, L), 1)
    for c in range(nch):
        tok = slice(c * L, (c + 1) * L)
        b_r = cum[c * 2 * H + H:(c + 1) * 2 * H, :]
        c_r = grow[0:H, tok] - b_r
        cmax = c_r
        shift = 1
        while shift < L:
            cmax = jnp.where(lane_h >= shift, jnp.maximum(cmax, pltpu.roll(cmax, shift, axis=1)), cmax)
            shift *= 2
        b_last = jnp.broadcast_to(b_r[:, L - 1:L], (H, L))
        tail = c_r + b_last
        tail_max = jnp.broadcast_to(jnp.max(tail, axis=-1, keepdims=True), (H, L))
        rstat_ref[:, tok] = jnp.concatenate([b_r, cmax, tail, tail_max, b_last], axis=0)
        ccol_ref[tok, :] = jnp.concatenate([c_r * LOG2E, jnp.zeros((LANES - H, L), f32)], axis=0).T

    qt_ref[...] = (_dot_nt(wt_ref[q_rows, :], xn) * (M_QK_DIM ** -0.5)).astype(bf16)
    wup_bf_ref[...] = wup_ref[...].astype(bf16)
    wdown_bf_ref[...] = wdown_ref[...].astype(bf16)


def _mlstm_proj(x, gain, wt, gate_bias, w_up, w_down):
    t, d = x.shape
    nq = M_HEADS * M_QK_DIM
    nv = M_HEADS * M_V_DIM
    tm = PROJ_TILE
    row = lambda n: pl.BlockSpec((tm, n), lambda i: (i, 0))
    col = lambda n: pl.BlockSpec((n, tm), lambda i: (0, i))
    steps = t // tm
    slab = lambda w: pl.BlockSpec((w.shape[0], w.shape[1] // steps, w.shape[2]), lambda i: (0, i, 0))
    return pl.pallas_call(
        _mlstm_proj_kernel,
        grid=(steps,),
        in_specs=_proj_in_specs(tm, d, steps) + [
            _resident((1, d)), _resident(wt.shape), _resident(gate_bias.shape), slab(w_up), slab(w_down)],
        out_specs=[col(nq), row(nq), col(nv), col(nv), row(LANES), col(5 * M_HEADS), slab(w_up), slab(w_down)],
        out_shape=[jax.ShapeDtypeStruct((nq, t), bf16), jax.ShapeDtypeStruct((t, nq), bf16),
                   jax.ShapeDtypeStruct((nv, t), bf16), jax.ShapeDtypeStruct((nv, t), bf16),
                   jax.ShapeDtypeStruct((t, LANES), f32), jax.ShapeDtypeStruct((5 * M_HEADS, t), f32),
                   jax.ShapeDtypeStruct(w_up.shape, bf16), jax.ShapeDtypeStruct(w_down.shape, bf16)],
        scratch_shapes=[pltpu.VMEM((tm, d), bf16), pltpu.VMEM((tm, d), bf16)],
        compiler_params=_params(1),
        name="mlstm_proj",
    )(x, x, gain, wt, gate_bias, w_up, w_down)


def _mlstm_scan_kernel(qt_ref, k_ref, vt_ref, ot_ref, ccol_ref, rstat_ref, hn_ref, out_ref, ct_ref, m_ref):
    L = M_CHUNK
    dk, dv = M_QK_DIM, M_V_DIM
    pairs = M_HEADS // 2

    @pl.when(pl.program_id(1) == 0)
    def _():
        ct_ref[...] = jnp.zeros_like(ct_ref)
        m_ref[...] = jnp.zeros_like(m_ref)

    H = M_HEADS
    upper = lax.broadcasted_iota(jnp.int32, (L, L), 0) <= lax.broadcasted_iota(jnp.int32, (L, L), 1)
    lane = lax.broadcasted_iota(jnp.int32, (L, LANES), 1)
    low = lane < dk
    ones_rows = jnp.ones((BF16_ROWS, L), bf16)
    zeros_half = jnp.zeros((dk, L), bf16)

    stats = []
    for c in range(SCAN_CHUNKS):
        tok = slice(c * L, (c + 1) * L)
        stats.append(tuple(rstat_ref[i * H:(i + 1) * H, tok] for i in range(5)) + (ccol_ref[tok, :],))

    m = m_ref[...]
    m_prevs = []
    for c in range(SCAN_CHUNKS):
        m_prevs.append(m)
        m = jnp.maximum(stats[c][4] + m, stats[c][3])
    m_prevs.append(m)
    m_ref[...] = m

    def kq(c, j):
        tok = slice(c * L, (c + 1) * L)
        kp = k_ref[tok, j * LANES:(j + 1) * LANES]
        qtp = qt_ref[j * LANES:(j + 1) * LANES, tok]
        q_even = jnp.concatenate([qtp[0:dk, :], zeros_half], axis=0)
        q_odd = jnp.concatenate([zeros_half, qtp[dk:2 * dk, :]], axis=0)
        return kp, (q_even, q_odd), _dot(kp, jnp.concatenate([q_even, q_odd], axis=1))

    ct = [ct_ref[j] for j in range(pairs)]
    units = [(c, j) for c in range(SCAN_CHUNKS) for j in range(pairs)]
    nxt = kq(*units[0])
    for u, (c, j) in enumerate(units):
        kp, q_eo, st = nxt
        if u + 1 < len(units):
            nxt = kq(*units[u + 1])
        tok = slice(c * L, (c + 1) * L)
        b_r, cmax, tail, _, b_last, c_cols = stats[c]
        m_prev, m_new = m_prevs[c], m_prevs[c + 1]
        a_r = jnp.maximum(m_prev, cmax)
        a_r2 = a_r * LOG2E
        carry_w = jnp.exp(m_prev - a_r).astype(bf16)
        clamp_r = jnp.exp(-(a_r + b_r))
        ws_r = jnp.exp(tail - m_new).astype(bf16)
        decay = jnp.exp(b_last + m_prev - m_new)
        ct_pair_b = ct[j].astype(bf16)
        vws = []
        for e, q_e in enumerate(q_eo):
            h = 2 * j + e
            expo = jnp.where(upper, c_cols[:, h:h + 1] - a_r2[h:h + 1, :], NEG_BIG)
            wt = (st[:, e * L:(e + 1) * L] * jnp.exp2(expo)).astype(bf16)
            q_w = q_e * carry_w[h:h + 1, :]
            vext = jnp.concatenate([vt_ref[h * dv:(h + 1) * dv, tok], ones_rows], axis=0)
            nd = _dot(jnp.concatenate([vext, ct_pair_b], axis=1), jnp.concatenate([wt, q_w], axis=0))
            den = jnp.maximum(jnp.abs(nd[dv:dv + 1, :]), clamp_r[h:h + 1, :])
            inv = 1.0 / den
            num = nd[0:dv, :]
            scale = inv * lax.rsqrt(inv * inv * jnp.mean(num * num, axis=0, keepdims=True) + EPS)
            hn = num * scale * hn_ref[h * dv:(h + 1) * dv, :]
            gate_o = ot_ref[h * dv:(h + 1) * dv, tok].astype(f32)
            out_ref[h * dv:(h + 1) * dv, tok] = (gate_o * hn).astype(bf16)
            vws.append(vext * ws_r[h:h + 1, :])
        k_split = jnp.concatenate([jnp.where(low, kp, jnp.zeros_like(kp)),
                                   jnp.where(low, jnp.zeros_like(kp), kp)], axis=0)
        decay_pair = jnp.where(low[0:1, :], decay[2 * j:2 * j + 1, :], decay[2 * j + 1:2 * j + 2, :])
        ct[j] = decay_pair * ct[j] + _dot(jnp.concatenate(vws, axis=1), k_split)
    for j in range(pairs):
        ct_ref[j] = ct[j]


def _mlstm_scan(qt, k, vt, ot, ccol, rstat, hn_rep, batch, seq):
    tm = SCAN_CHUNKS * M_CHUNK
    nq = M_HEADS * M_QK_DIM
    nv = M_HEADS * M_V_DIM
    nt = seq // tm
    row = lambda n: pl.BlockSpec((tm, n), lambda b, i: (b * nt + i, 0))
    col = lambda n: pl.BlockSpec((n, tm), lambda b, i: (0, b * nt + i))
    return pl.pallas_call(
        _mlstm_scan_kernel,
        grid=(batch, nt),
        in_specs=[col(nq), row(nq), col(nv), col(nv), row(LANES), col(5 * M_HEADS), _resident(hn_rep.shape)],
        out_specs=col(nv),
        out_shape=jax.ShapeDtypeStruct((nv, batch * seq), bf16),
        scratch_shapes=[pltpu.VMEM((M_HEADS // 2, M_V_DIM + BF16_ROWS, 2 * M_QK_DIM), f32),
                        pltpu.VMEM((M_HEADS, LANES), f32)],
        compiler_params=_params(2),
        name="mlstm_scan",
    )(qt, k, vt, ot, ccol, rstat, hn_rep)


def _attn_proj_kernel(x0_ref, xnext_ref, g_ref, wt_ref, b_ref, qt_ref, k_ref, vt_ref, xn_ref, xnn_ref):
    tm = xnext_ref.shape[0]
    nkv = A_KV_HEADS * A_HEAD_DIM
    xn, prepare_next = _next_input_norm(pl.program_id(0), x0_ref, xnext_ref, g_ref, xn_ref, xnn_ref)
    nq = wt_ref.shape[0] - 2 * nkv
    proj = lambda r0, r1: _dot_nt(wt_ref[r0:r1, :], xn) + _tile_lanes(b_ref[r0:r1, :], tm // LANES)
    kv = proj(nq, nq + 2 * nkv)
    prepare_next()
    k_ref[...] = kv[0:nkv, :].T.astype(bf16)
    vt_ref[...] = kv[nkv:2 * nkv, :].astype(bf16)
    for r0 in range(0, nq, Q_ROWS_PER_DOT):
        q = proj(r0, r0 + Q_ROWS_PER_DOT)
        qt_ref[r0:r0 + Q_ROWS_PER_DOT, :] = (q * (A_HEAD_DIM ** -0.5 * LOG2E)).astype(bf16)


def _attn_proj(x, gain, wt, b_rep):
    t, d = x.shape
    nq = A_Q_HEADS * A_HEAD_DIM
    nkv = A_KV_HEADS * A_HEAD_DIM
    tm = PROJ_TILE
    row = lambda n: pl.BlockSpec((tm, n), lambda i: (i, 0))
    col = lambda n: pl.BlockSpec((n, tm), lambda i: (0, i))
    return pl.pallas_call(
        _attn_proj_kernel,
        grid=(t // tm,),
        in_specs=_proj_in_specs(tm, d, t // tm) + [_resident((1, d)), _resident(wt.shape), _resident(b_rep.shape)],
        out_specs=[col(nq), row(nkv), col(nkv)],
        out_shape=[jax.ShapeDtypeStruct((nq, t), bf16), jax.ShapeDtypeStruct((t, nkv), bf16),
                   jax.ShapeDtypeStruct((nkv, t), bf16)],
        scratch_shapes=[pltpu.VMEM((tm, d), bf16), pltpu.VMEM((tm, d), bf16)],
        compiler_params=_params(1),
        name="attn_proj",
    )(x, x, gain, wt, b_rep)


def _swa_kernel(qt_ref, kc_ref, kp_ref, vtc_ref, vtp_ref, sink_ref, out_ref):
    blk = A_BLOCK
    dh = A_HEAD_DIM
    first = pl.program_id(1) == 0
    ku = lax.broadcasted_iota(jnp.int32, (2 * blk, blk), 0)
    qi = lax.broadcasted_iota(jnp.int32, (2 * blk, blk), 1)
    diff = qi - (ku - blk)
    band = (diff >= 0) & (diff < WINDOW)
    bias = jnp.where(band | (ku == 0), 0.0, NEG_BIG)
    bias_first = jnp.where((band & (ku >= blk)) | (ku == 0), 0.0, NEG_BIG)
    krow = lax.broadcasted_iota(jnp.int32, (2 * blk, LANES), 0)
    klane = lax.broadcasted_iota(jnp.int32, (2 * blk, LANES), 1)
    k_aug = jnp.where((krow == 0) & (klane < 3), 1.0, 0.0).astype(bf16)
    vcol = lax.broadcasted_iota(jnp.int32, (dh, 2 * blk), 1)
    ones_rows = jnp.ones((BF16_ROWS, 2 * blk), bf16)

    hp_lanes = 2 * blk
    shared = {}

    def operands(bq, g):
        if (bq, g) not in shared:
            cols = slice(bq * blk, (bq + 1) * blk)
            if bq == 0:
                kcat = jnp.concatenate([kp_ref[...], kc_ref[cols, :]], axis=0)
                vtcat = jnp.concatenate([vtp_ref[...], vtc_ref[:, cols]], axis=1)
                b1 = jnp.where(first, bias_first, bias)
            else:
                kcat = kc_ref[(bq - 1) * blk:(bq + 1) * blk, :]
                vtcat = vtc_ref[:, (bq - 1) * blk:(bq + 1) * blk]
                b1 = bias
            in_group = (klane >= g * dh) & (klane < (g + 1) * dh) & (krow > 0)
            km = jnp.concatenate([jnp.where(in_group, kcat, jnp.zeros_like(kcat)), k_aug], axis=1)
            vt_g = jnp.where(vcol == 0, jnp.zeros((dh, 2 * blk), bf16), vtcat[g * dh:(g + 1) * dh, :])
            shared[(bq, g)] = km, jnp.concatenate([vt_g, ones_rows], axis=0), _tile_lanes(b1, 2)
        return shared[(bq, g)]

    def logits(bq, g, hp):
        cols = slice(bq * blk, (bq + 1) * blk)
        km, _, b2 = operands(bq, g)
        blocks = []
        for e in range(2):
            h = g * A_GROUP + 2 * hp + e
            pair = qt_ref[(h // 2) * LANES:(h // 2 + 1) * LANES, cols]
            if h % 2 != g:
                pair = jnp.concatenate([pair[dh:2 * dh, :], pair[0:dh, :]], axis=0)
            blocks.append(pair)
        sink_rows = sink_ref[g][:, hp * hp_lanes:(hp + 1) * hp_lanes]
        rhs = jnp.concatenate([jnp.concatenate(blocks, axis=1), sink_rows], axis=0)
        return _dot(km, rhs) + b2

    def finish(bq, g, hp, st):
        cols = slice(bq * blk, (bq + 1) * blk)
        _, vext, _ = operands(bq, g)
        p = jnp.exp2(st - jnp.max(st, axis=0, keepdims=True)).astype(bf16)
        oext = _dot(vext, p)
        o = (oext[0:dh, :] * (1.0 / oext[dh:dh + 1, :])).astype(bf16)
        for e in range(2):
            h = g * A_GROUP + 2 * hp + e
            out_ref[h * dh:(h + 1) * dh, cols] = o[:, e * blk:(e + 1) * blk]

    units = [(bq, g, hp) for bq in range(SWA_TILE // blk) for g in range(A_KV_HEADS) for hp in range(A_GROUP // 2)]
    queued = [logits(*unit) for unit in units[:SWA_LOOKAHEAD]]
    for u, unit in enumerate(units):
        if u + SWA_LOOKAHEAD < len(units):
            queued.append(logits(*units[u + SWA_LOOKAHEAD]))
        finish(*unit, queued[u])


def _swa(qt, k, vt, sink_aug, batch, seq):
    blk = A_BLOCK
    tq = SWA_TILE
    nq = A_Q_HEADS * A_HEAD_DIM
    nkv = A_KV_HEADS * A_HEAD_DIM
    nt = seq // tq
    per = tq // blk
    prev_idx = lambda b, i: b * nt * per + jnp.maximum(i * per - 1, 0)
    return pl.pallas_call(
        _swa_kernel,
        grid=(batch, nt),
        in_specs=[pl.BlockSpec((nq, tq), lambda b, i: (0, b * nt + i)),
                  pl.BlockSpec((tq, nkv), lambda b, i: (b * nt + i, 0)),
                  pl.BlockSpec((blk, nkv), lambda b, i: (prev_idx(b, i), 0)),
                  pl.BlockSpec((nkv, tq), lambda b, i: (0, b * nt + i)),
                  pl.BlockSpec((nkv, blk), lambda b, i: (0, prev_idx(b, i))),
                  _resident(sink_aug.shape)],
        out_specs=pl.BlockSpec((nq, tq), lambda b, i: (0, b * nt + i)),
        out_shape=jax.ShapeDtypeStruct((nq, batch * seq), bf16),
        compiler_params=_params(2),
        name="swa",
    )(qt, k, k, vt, vt, sink_aug)


def _sink_rows(sinks):
    parts = jnp.stack(_split3(sinks.astype(f32) * LOG2E), axis=0)
    per_lane = jnp.repeat(parts.reshape(3, A_KV_HEADS, A_GROUP), A_BLOCK, axis=2)
    return jnp.pad(per_lane.transpose(1, 0, 2), ((0, 0), (0, LANES - 3), (0, 0)))


def _out_ffn_kernel(h0_ref, at0_ref, hn_ref, atn_ref, wo_ref, wup_ref, wd_ref, vec_ref, conv_ref,
                    out_ref, act_ref, h1_ref, xn_ref, h1n_ref, xnn_ref, carry_ref, *, tiles_per_seq):
    tm = hn_ref.shape[0]
    dff = wd_ref.shape[0]
    i = pl.program_id(0)

    d = wo_ref.shape[1]
    out_cols = [slice(n0, n0 + OUT_CHUNK) for n0 in range(0, d, OUT_CHUNK)]

    def out_proj(at_ref, cols):
        return _dot_tn(at_ref[...], wo_ref[:, cols]) + vec_ref[0:1, cols]

    def front(h_ref, z_parts):
        h1 = h_ref[...] + _rmsnorm(jnp.concatenate(z_parts, axis=1), vec_ref[1:2, :])
        h1n_ref[...] = h1
        xnn_ref[...] = _rmsnorm(h1, vec_ref[2:3, :]).astype(bf16)

    @pl.when(i == 0)
    def _():
        front(h0_ref, [out_proj(at0_ref, cols) for cols in out_cols])

    @pl.when(i % tiles_per_seq == 0)
    def _():
        carry_ref[...] = jnp.zeros_like(carry_ref)

    xn_ref[...] = xnn_ref[...]
    h1_ref[...] = h1n_ref[...]
    top = lax.broadcasted_iota(jnp.int32, (8, FF_CHUNK), 0)
    z_parts = []
    for ci, c0 in enumerate(range(0, dff, FF_CHUNK)):
        if FRONT_AFTER_CHUNKS <= ci < FRONT_AFTER_CHUNKS + len(out_cols):
            z_parts.append(out_proj(atn_ref, out_cols[ci - FRONT_AFTER_CHUNKS]))
        if ci == FRONT_AFTER_CHUNKS + len(out_cols):
            front(hn_ref, z_parts)
        cs = slice(c0, c0 + FF_CHUNK)
        gate = _dot(xn_ref[...], wup_ref[:, cs])
        val = _dot(xn_ref[...], wup_ref[:, dff + c0:dff + c0 + FF_CHUNK])
        prev = carry_ref[0:8, cs]
        carry_ref[0:8, cs] = gate[tm - 8:tm, :]
        shifted = []
        for lag in (1, 2):
            rolled = pltpu.roll(gate, lag, axis=0)
            head = jnp.where(top < lag, pltpu.roll(prev, lag, axis=0), rolled[0:8, :])
            shifted.append(jnp.concatenate([head, rolled[8:, :]], axis=0))
        g1, g2 = shifted
        gc = conv_ref[3:4, cs] + conv_ref[0:1, cs] * g2 + conv_ref[1:2, cs] * g1 + conv_ref[2:3, cs] * gate
        act_ref[:, cs] = (gc * jax.nn.sigmoid(gc) * val).astype(bf16)
    y = _dot(act_ref[...], wd_ref[...])
    out_ref[...] = h1_ref[...] + _rmsnorm(y, vec_ref[3:4, :])


def _out_ffn(h, at, wo, wup, wd, vecs, conv, layer, seq):
    t, d = h.shape
    tm = TOKEN_TILE
    n = t // tm
    dff = wd.shape[1]
    nxt = lambda i: jnp.minimum(i + 1, n - 1)
    return pl.pallas_call(
        functools.partial(_out_ffn_kernel, tiles_per_seq=seq // tm),
        grid=(n,),
        in_specs=[pl.BlockSpec((tm, d), lambda i: (0, 0), pipeline_mode=pl.Buffered(1)),
                  pl.BlockSpec((at.shape[0], tm), lambda i: (0, 0), pipeline_mode=pl.Buffered(1)),
                  pl.BlockSpec((tm, d), lambda i: (nxt(i), 0)),
                  pl.BlockSpec((at.shape[0], tm), lambda i: (0, nxt(i))),
                  _resident(wo.shape), _layer_resident(wup.shape, layer), _layer_resident(wd.shape, layer),
                  _resident(vecs.shape), _resident(conv.shape)],
        out_specs=pl.BlockSpec((tm, d), lambda i: (i, 0)),
        out_shape=jax.ShapeDtypeStruct((t, d), f32),
        scratch_shapes=[pltpu.VMEM((tm, dff), bf16),
                        pltpu.VMEM((tm, d), f32), pltpu.VMEM((tm, d), bf16),
                        pltpu.VMEM((tm, d), f32), pltpu.VMEM((tm, d), bf16), pltpu.VMEM((16, dff), f32)],
        compiler_params=_params(1),
        name="out_ffn",
    )(h, at, h, at, wo, wup, wd, vecs, conv)


def kernel(x, m_w_in, m_gate_bias, m_head_norm, m_w_out, a_w_in, a_b_in, a_sinks, a_w_out, a_b_out,
           norm_mix_pre, norm_mix_post, norm_ffn_pre, norm_ffn_post, f_w_up, f_conv_w, f_conv_b, f_w_down):
    batch, seq, d = x.shape
    depth = norm_mix_pre.shape[0]
    h = x.reshape(batch * seq, d)
    row = lambda vec: vec.reshape(1, -1).astype(f32)
    lane_rep = lambda vec: jnp.broadcast_to(vec.astype(f32)[:, None], (vec.shape[0], LANES))

    for i in range(depth):
        j = i // 2
        if i % 2 == 0:
            qt, k, vt, ot, ccol, rstat, w_up, w_down = _mlstm_proj(
                h, row(norm_mix_pre[i]), m_w_in[j].T.astype(bf16), lane_rep(m_gate_bias[j].reshape(-1)),
                f_w_up, f_w_down)
            mixed_t = _mlstm_scan(qt, k, vt, ot, ccol, rstat, lane_rep(m_head_norm[j]), batch, seq)
            w_out = m_w_out[j].astype(bf16)
            b_out = jnp.zeros((d,), f32)
        else:
            qt, k, vt = _attn_proj(h, row(norm_mix_pre[i]), a_w_in[j].T.astype(bf16), lane_rep(a_b_in[j]))
            mixed_t = _swa(qt, k, vt, _sink_rows(a_sinks[j]), batch, seq)
            w_out = a_w_out[j].astype(bf16)
            b_out = a_b_out[j]
        vecs = jnp.pad(jnp.stack([b_out, norm_mix_post[i], norm_ffn_pre[i], norm_ffn_post[i]]).astype(f32),
                       ((0, 4), (0, 0)))
        conv = jnp.pad(jnp.concatenate([f_conv_w[i], f_conv_b[i][None]], axis=0).astype(f32), ((0, 12), (0, 0)))
        h = _out_ffn(h, mixed_t, w_out, w_up, w_down, vecs, conv, i, seq)
    return h.reshape(batch, seq, d)
```

```python
import functools

import jax
import jax.numpy as jnp
from jax import lax
from jax.experimental import pallas as pl
from jax.experimental.pallas import tpu as pltpu

EPS = 1e-6
LANES = 128
BF16_ROWS = 16

M_HEADS = 8
M_QK_DIM = 64
M_V_DIM = 128
GATE_CAP = 15.0
M_CHUNK = 128
SCAN_CHUNKS = 4
MLSTM_GROUP = 256

A_HEAD_DIM = 64
A_Q_HEADS = 16
A_KV_HEADS = 2
A_GROUP = A_Q_HEADS // A_KV_HEADS
WINDOW = 128
A_BLOCK = 128

LOG2E = 1.4426950408889634
NEG_BIG = -1e30

TOKEN_TILE = 512
PROJ_TILE = 1024
SWA_TILE = 512
SWA_LOOKAHEAD = 4
FF_CHUNK = 256
Q_ROWS_PER_DOT = 256
OUT_CHUNK = 256
FRONT_AFTER_CHUNKS = 2
VMEM_LIMIT = 60 * 1024 * 1024

bf16 = jnp.bfloat16
f32 = jnp.float32


def _dot(a, b):
    return jnp.dot(a, b, preferred_element_type=f32)


def _dot_nt(a, b):
    return lax.dot_general(a, b, (((1,), (1,)), ((), ())), preferred_element_type=f32)


def _dot_tn(a, b):
    return lax.dot_general(a, b, (((0,), (0,)), ((), ())), preferred_element_type=f32)


def _rmsnorm(x, g):
    return x * lax.rsqrt(jnp.mean(x * x, axis=-1, keepdims=True) + EPS) * g


def _split3(x):
    hi = x.astype(bf16)
    r1 = x - hi.astype(f32)
    mid = r1.astype(bf16)
    lo = (r1 - mid.astype(f32)).astype(bf16)
    return hi, mid, lo


def _log_sigmoid(x):
    return jnp.minimum(x, 0.0) - jnp.log1p(jnp.exp(-jnp.abs(x)))


def _tile_lanes(x, reps):
    return jnp.concatenate([x] * reps, axis=1)


def _next_input_norm(i, x0_ref, xnext_ref, g_ref, xn_ref, xnn_ref):
    @pl.when(i == 0)
    def _():
        xnn_ref[...] = _rmsnorm(x0_ref[...], g_ref[...]).astype(bf16)

    xn_ref[...] = xnn_ref[...]

    def prepare_next():
        xnn_ref[...] = _rmsnorm(xnext_ref[...], g_ref[...]).astype(bf16)

    return xn_ref[...], prepare_next


def _sigmoid(x):
    return 0.5 * jnp.tanh(0.5 * x) + 0.5


def _proj_in_specs(tm, d, n_tiles):
    return [pl.BlockSpec((tm, d), lambda i: (0, 0), pipeline_mode=pl.Buffered(1)),
            pl.BlockSpec((tm, d), lambda i: (jnp.minimum(i + 1, n_tiles - 1), 0))]


def _resident(shape):
    nd = len(shape)
    return pl.BlockSpec(shape, lambda *_: (0,) * nd, pipeline_mode=pl.Buffered(1))


def _layer_resident(shape, layer):
    nd = len(shape) - 1
    return pl.BlockSpec((None,) + tuple(shape[1:]), lambda *_: (layer,) + (0,) * nd, pipeline_mode=pl.Buffered(1))


def _params(n_axes):
    return pltpu.CompilerParams(
        dimension_semantics=("arbitrary",) * n_axes, vmem_limit_bytes=VMEM_LIMIT)


def _mlstm_kernel(x0_ref, xnext_ref, g_ref, wt_ref, gb_ref, hn_ref, wup_ref, wdown_ref,
                  out_ref, wup_bf_ref, wdown_bf_ref, xn_ref, xnn_ref, ct_ref, m_ref, *, tiles_per_seq):
    tm = xnext_ref.shape[0]
    L = M_CHUNK
    H = M_HEADS
    dk, dv = M_QK_DIM, M_V_DIM
    nq = H * dk
    nv = H * dv
    pairs = H // 2
    G = MLSTM_GROUP
    cpg = G // L
    i = pl.program_id(0)

    @pl.when(i % tiles_per_seq == 0)
    def _():
        ct_ref[...] = jnp.zeros_like(ct_ref)
        m_ref[...] = jnp.zeros_like(m_ref)

    xn, prepare_next = _next_input_norm(i, x0_ref, xnext_ref, g_ref, xn_ref, xnn_ref)

    upper = lax.broadcasted_iota(jnp.int32, (L, L), 0) <= lax.broadcasted_iota(jnp.int32, (L, L), 1)
    triu = jnp.where(upper, 1.0, 0.0).astype(bf16)
    lane_h = lax.broadcasted_iota(jnp.int32, (H, L), 1)
    lane = lax.broadcasted_iota(jnp.int32, (L, LANES), 1)
    low = lane < dk
    ones_rows = jnp.ones((BF16_ROWS, L), bf16)
    zeros_half = jnp.zeros((dk, L), bf16)

    def projection_steps(g):
        xg = xn[g * G:(g + 1) * G, :]
        res = {}
        piece = lambda r0, r1: _dot_nt(wt_ref[r0:r1, :], xg)
        o0 = 2 * nq + nv

        def gates_and_o_hi():
            p = piece(o0 + nv // 2, o0 + nv + 2 * H)
            res["o_hi"] = _sigmoid(p[0:nv // 2, :]).astype(bf16)
            raw = p[nv // 2:, :] + _tile_lanes(gb_ref[...], G // LANES)
            capped = GATE_CAP * jnp.tanh(raw * (1.0 / GATE_CAP))
            res["grow"] = jnp.where(lax.broadcasted_iota(jnp.int32, raw.shape, 0) < H, capped, _log_sigmoid(capped))

        def o_lo():
            res["o_lo"] = _sigmoid(piece(o0, o0 + nv // 2)).astype(bf16)

        def k_and_stats():
            res["k"] = piece(nq, 2 * nq).T.astype(bf16)
            grow = res["grow"]
            stacked = jnp.concatenate([grow[:, c * L:(c + 1) * L] for c in range(cpg)], axis=0)
            sums = _dot(jnp.concatenate(_split3(stacked), axis=0), triu)
            n = cpg * 2 * H
            cum = sums[0:n] + sums[n:2 * n] + sums[2 * n:]
            stats = []
            for c in range(cpg):
                b_r = cum[c * 2 * H + H:(c + 1) * 2 * H, :]
                c_r = grow[0:H, c * L:(c + 1) * L] - b_r
                cmax = c_r
                shift = 1
                while shift < L:
                    cmax = jnp.where(lane_h >= shift, jnp.maximum(cmax, pltpu.roll(cmax, shift, axis=1)), cmax)
                    shift *= 2
                b_last = jnp.broadcast_to(b_r[:, L - 1:L], (H, L))
                tail = c_r + b_last
                tail_max = jnp.broadcast_to(jnp.max(tail, axis=-1, keepdims=True), (H, L))
                c_cols = jnp.concatenate([c_r * LOG2E, jnp.zeros((LANES - H, L), f32)], axis=0).T
                stats.append((b_r, cmax, tail, tail_max, b_last, c_cols))
            res["stats"] = stats

        def v_lo():
            res["v_lo"] = piece(2 * nq, 2 * nq + nv // 2).astype(bf16)

        def v_hi():
            res["v_hi"] = piece(2 * nq + nv // 2, 2 * nq + nv).astype(bf16)

        def q():
            res["q"] = (piece(0, nq) * (dk ** -0.5)).astype(bf16)

        return res, [gates_and_o_hi, o_lo, k_and_stats, v_lo, v_hi, q]

    def finalize(res):
        res["ot"] = jnp.concatenate([res.pop("o_lo"), res.pop("o_hi")], axis=0)
        res["vt"] = jnp.concatenate([res.pop("v_lo"), res.pop("v_hi")], axis=0)
        return res

    def kq(p, c, j):
        tok = slice(c * L, (c + 1) * L)
        kp = p["k"][tok, j * LANES:(j + 1) * LANES]
        qtp = p["q"][j * LANES:(j + 1) * LANES, tok]
        q_even = jnp.concatenate([qtp[0:dk, :], zeros_half], axis=0)
        q_odd = jnp.concatenate([zeros_half, qtp[dk:2 * dk, :]], axis=0)
        return kp, (q_even, q_odd), _dot(kp, jnp.concatenate([q_even, q_odd], axis=1))

    def scan_unit(p, g, c, j, m_prev, m_new, ct, kq_res):
        kp, q_eo, st = kq_res
        tok = slice(c * L, (c + 1) * L)
        out_tok = slice(g * G + c * L, g * G + (c + 1) * L)
        b_r, cmax, tail, _, b_last, c_cols = p["stats"][c]
        a_r = jnp.maximum(m_prev, cmax)
        a_r2 = a_r * LOG2E
        carry_w = jnp.exp(m_prev - a_r).astype(bf16)
        clamp_r = jnp.exp(-(a_r + b_r))
        ws_r = jnp.exp(tail - m_new).astype(bf16)
        decay = jnp.exp(b_last + m_prev - m_new)
        ct_pair_b = ct[j].astype(bf16)
        vws = []
        for e, q_e in enumerate(q_eo):
            h = 2 * j + e
            expo = jnp.where(upper, c_cols[:, h:h + 1] - a_r2[h:h + 1, :], NEG_BIG)
            wt = (st[:, e * L:(e + 1) * L] * jnp.exp2(expo)).astype(bf16)
            q_w = q_e * carry_w[h:h + 1, :]
            vext = jnp.concatenate([p["vt"][h * dv:(h + 1) * dv, tok], ones_rows], axis=0)
            nd = _dot(jnp.concatenate([vext, ct_pair_b], axis=1), jnp.concatenate([wt, q_w], axis=0))
            den = jnp.maximum(jnp.abs(nd[dv:dv + 1, :]), clamp_r[h:h + 1, :])
            inv = 1.0 / den
            num = nd[0:dv, :]
            scale = inv * lax.rsqrt(inv * inv * jnp.mean(num * num, axis=0, keepdims=True) + EPS)
            hn = num * scale * hn_ref[h * dv:(h + 1) * dv, :]
            gate_o = p["ot"][h * dv:(h + 1) * dv, tok].astype(f32)
            out_ref[h * dv:(h + 1) * dv, out_tok] = (gate_o * hn).astype(bf16)
            vws.append(vext * ws_r[h:h + 1, :])
        k_split = jnp.concatenate([jnp.where(low, kp, jnp.zeros_like(kp)),
                                   jnp.where(low, jnp.zeros_like(kp), kp)], axis=0)
        decay_pair = jnp.where(low[0:1, :], decay[2 * j:2 * j + 1, :], decay[2 * j + 1:2 * j + 2, :])
        ct[j] = decay_pair * ct[j] + _dot(jnp.concatenate(vws, axis=1), k_split)

    n_groups = tm // G
    cur, steps = projection_steps(0)
    for k_step, step in enumerate(steps):
        step()
        if k_step == 0:
            prepare_next()
    cur = finalize(cur)

    ct = [ct_ref[j] for j in range(pairs)]
    m = m_ref[...]
    units = [(c, j) for c in range(cpg) for j in range(pairs)]
    nxt_kq = kq(cur, *units[0])
    for g in range(n_groups):
        if g + 1 < n_groups:
            nxt, steps = projection_steps(g + 1)
        else:
            nxt, steps = None, []
        m_prevs = []
        for c in range(cpg):
            m_prevs.append(m)
            m = jnp.maximum(cur["stats"][c][4] + m, cur["stats"][c][3])
        m_prevs.append(m)
        for u, (c, j) in enumerate(units):
            if u < len(steps):
                steps[u]()
                if u + 1 == len(steps):
                    nxt = finalize(nxt)
            this_kq = nxt_kq
            if u + 1 < len(units):
                nxt_kq = kq(cur, *units[u + 1])
            elif nxt is not None:
                nxt_kq = kq(nxt, *units[0])
            scan_unit(cur, g, c, j, m_prevs[c], m_prevs[c + 1], ct, this_kq)
        cur = nxt
    m_ref[...] = m
    for j in range(pairs):
        ct_ref[j] = ct[j]
    wup_bf_ref[...] = wup_ref[...].astype(bf16)
    wdown_bf_ref[...] = wdown_ref[...].astype(bf16)


def _mlstm_mixer(x, gain, wt, gate_bias, hn_rep, w_up, w_down, seq):
    t, d = x.shape
    nv = M_HEADS * M_V_DIM
    tm = PROJ_TILE
    steps = t // tm
    slab = lambda w: pl.BlockSpec((w.shape[0], w.shape[1] // steps, w.shape[2]), lambda i: (0, i, 0))
    return pl.pallas_call(
        functools.partial(_mlstm_kernel, tiles_per_seq=seq // tm),
        grid=(steps,),
        in_specs=_proj_in_specs(tm, d, steps) + [
            _resident((1, d)), _resident(wt.shape), _resident(gate_bias.shape), _resident(hn_rep.shape),
            slab(w_up), slab(w_down)],
        out_specs=[pl.BlockSpec((nv, tm), lambda i: (0, i)), slab(w_up), slab(w_down)],
        out_shape=[jax.ShapeDtypeStruct((nv, t), bf16),
                   jax.ShapeDtypeStruct(w_up.shape, bf16), jax.ShapeDtypeStruct(w_down.shape, bf16)],
        scratch_shapes=[pltpu.VMEM((tm, d), bf16), pltpu.VMEM((tm, d), bf16),
                        pltpu.VMEM((M_HEADS // 2, M_V_DIM + BF16_ROWS, 2 * M_QK_DIM), f32),
                        pltpu.VMEM((M_HEADS, LANES), f32)],
        compiler_params=_params(1),
        name="mlstm_mixer",
    )(x, x, gain, wt, gate_bias, hn_rep, w_up, w_down)


def _mlstm_proj_kernel(x0_ref, xnext_ref, g_ref, wt_ref, gb_ref, wup_ref, wdown_ref,
                       qt_ref, k_ref, vt_ref, ot_ref, ccol_ref, rstat_ref, wup_bf_ref, wdown_bf_ref, xn_ref, xnn_ref):
    tm = xnext_ref.shape[0]
    L = M_CHUNK
    H = M_HEADS
    xn, prepare_next = _next_input_norm(pl.program_id(0), x0_ref, xnext_ref, g_ref, xn_ref, xnn_ref)
    nq = M_HEADS * M_QK_DIM
    nv = M_HEADS * M_V_DIM

    q_rows, k_rows, v_rows = slice(0, nq), slice(nq, 2 * nq), slice(2 * nq, 2 * nq + nv)
    og_rows = slice(2 * nq + nv, 2 * nq + 2 * nv + 2 * H)
    first = _dot_nt(wt_ref[og_rows, :], xn)
    prepare_next()
    raw = first[nv:, :] + _tile_lanes(gb_ref[...], tm // LANES)
    capped = GATE_CAP * jnp.tanh(raw * (1.0 / GATE_CAP))
    grow = jnp.where(lax.broadcasted_iota(jnp.int32, raw.shape, 0) < H, capped, _log_sigmoid(capped))
    ot_ref[...] = _sigmoid(first[0:nv, :]).astype(bf16)
    k_ref[...] = _dot_nt(wt_ref[k_rows, :], xn).T.astype(bf16)
    vt_ref[...] = _dot_nt(wt_ref[v_rows, :], xn).astype(bf16)

    triu = jnp.where(lax.broadcasted_iota(jnp.int32, (L, L), 0) <= lax.broadcasted_iota(jnp.int32, (L, L), 1),
                     1.0, 0.0).astype(bf16)
    nch = tm // L
    stacked = jnp.concatenate([grow[:, c * L:(c + 1) * L] for c in range(nch)], axis=0)
    sums = _dot(jnp.concatenate(_split3(stacked), axis=0), triu)
    cum = sums[0:nch * 2 * H] + sums[nch * 2 * H:2 * nch * 2 * H] + sums[2 * nch * 2 * H:]
    lane_h = lax.broadcasted_iota(jnp.int32, (H, L), 1)
    for c in range(nch):
        tok = slice(c * L, (c + 1) * L)
        b_r = cum[c * 2 * H + H:(c + 1) * 2 * H, :]
        c_r = grow[0:H, tok] - b_r
        cmax = c_r
        shift = 1
        while shift < L:
            cmax = jnp.where(lane_h >= shift, jnp.maximum(cmax, pltpu.roll(cmax, shift, axis=1)), cmax)
            shift *= 2
        b_last = jnp.broadcast_to(b_r[:, L - 1:L], (H, L))
        tail = c_r + b_last
        tail_max = jnp.broadcast_to(jnp.max(tail, axis=-1, keepdims=True), (H, L))
        rstat_ref[:, tok] = jnp.concatenate([b_r, cmax, tail, tail_max, b_last], axis=0)
        ccol_ref[tok, :] = jnp.concatenate([c_r * LOG2E, jnp.zeros((LANES - H, L), f32)], axis=0).T

    qt_ref[...] = (_dot_nt(wt_ref[q_rows, :], xn) * (M_QK_DIM ** -0.5)).astype(bf16)
    wup_bf_ref[...] = wup_ref[...].astype(bf16)
    wdown_bf_ref[...] = wdown_ref[...].astype(bf16)


def _mlstm_proj(x, gain, wt, gate_bias, w_up, w_down):
    t, d = x.shape
    nq = M_HEADS * M_QK_DIM
    nv = M_HEADS * M_V_DIM
    tm = PROJ_TILE
    row = lambda n: pl.BlockSpec((tm, n), lambda i: (i, 0))
    col = lambda n: pl.BlockSpec((n, tm), lambda i: (0, i))
    steps = t // tm
    slab = lambda w: pl.BlockSpec((w.shape[0], w.shape[1] // steps, w.shape[2]), lambda i: (0, i, 0))
    return pl.pallas_call(
        _mlstm_proj_kernel,
        grid=(steps,),
        in_specs=_proj_in_specs(tm, d, steps) + [
            _resident((1, d)), _resident(wt.shape), _resident(gate_bias.shape), slab(w_up), slab(w_down)],
        out_specs=[col(nq), row(nq), col(nv), col(nv), row(LANES), col(5 * M_HEADS), slab(w_up), slab(w_down)],
        out_shape=[jax.ShapeDtypeStruct((nq, t), bf16), jax.ShapeDtypeStruct((t, nq), bf16),
                   jax.ShapeDtypeStruct((nv, t), bf16), jax.ShapeDtypeStruct((nv, t), bf16),
                   jax.ShapeDtypeStruct((t, LANES), f32), jax.ShapeDtypeStruct((5 * M_HEADS, t), f32),
                   jax.ShapeDtypeStruct(w_up.shape, bf16), jax.ShapeDtypeStruct(w_down.shape, bf16)],
        scratch_shapes=[pltpu.VMEM((tm, d), bf16), pltpu.VMEM((tm, d), bf16)],
        compiler_params=_params(1),
        name="mlstm_proj",
    )(x, x, gain, wt, gate_bias, w_up, w_down)


def _mlstm_scan_kernel(qt_ref, k_ref, vt_ref, ot_ref, ccol_ref, rstat_ref, hn_ref, out_ref, ct_ref, m_ref):
    L = M_CHUNK
    dk, dv = M_QK_DIM, M_V_DIM
    pairs = M_HEADS // 2

    @pl.when(pl.program_id(1) == 0)
    def _():
        ct_ref[...] = jnp.zeros_like(ct_ref)
        m_ref[...] = jnp.zeros_like(m_ref)

    H = M_HEADS
    upper = lax.broadcasted_iota(jnp.int32, (L, L), 0) <= lax.broadcasted_iota(jnp.int32, (L, L), 1)
    lane = lax.broadcasted_iota(jnp.int32, (L, LANES), 1)
    low = lane < dk
    ones_rows = jnp.ones((BF16_ROWS, L), bf16)
    zeros_half = jnp.zeros((dk, L), bf16)

    stats = []
    for c in range(SCAN_CHUNKS):
        tok = slice(c * L, (c + 1) * L)
        stats.append(tuple(rstat_ref[i * H:(i + 1) * H, tok] for i in range(5)) + (ccol_ref[tok, :],))

    m = m_ref[...]
    m_prevs = []
    for c in range(SCAN_CHUNKS):
        m_prevs.append(m)
        m = jnp.maximum(stats[c][4] + m, stats[c][3])
    m_prevs.append(m)
    m_ref[...] = m

    def kq(c, j):
        tok = slice(c * L, (c + 1) * L)
        kp = k_ref[tok, j * LANES:(j + 1) * LANES]
        qtp = qt_ref[j * LANES:(j + 1) * LANES, tok]
        q_even = jnp.concatenate([qtp[0:dk, :], zeros_half], axis=0)
        q_odd = jnp.concatenate([zeros_half, qtp[dk:2 * dk, :]], axis=0)
        return kp, (q_even, q_odd), _dot(kp, jnp.concatenate([q_even, q_odd], axis=1))

    ct = [ct_ref[j] for j in range(pairs)]
    units = [(c, j) for c in range(SCAN_CHUNKS) for j in range(pairs)]
    nxt = kq(*units[0])
    for u, (c, j) in enumerate(units):
        kp, q_eo, st = nxt
        if u + 1 < len(units):
            nxt = kq(*units[u + 1])
        tok = slice(c * L, (c + 1) * L)
        b_r, cmax, tail, _, b_last, c_cols = stats[c]
        m_prev, m_new = m_prevs[c], m_prevs[c + 1]
        a_r = jnp.maximum(m_prev, cmax)
        a_r2 = a_r * LOG2E
        carry_w = jnp.exp(m_prev - a_r).astype(bf16)
        clamp_r = jnp.exp(-(a_r + b_r))
        ws_r = jnp.exp(tail - m_new).astype(bf16)
        decay = jnp.exp(b_last + m_prev - m_new)
        ct_pair_b = ct[j].astype(bf16)
        vws = []
        for e, q_e in enumerate(q_eo):
            h = 2 * j + e
            expo = jnp.where(upper, c_cols[:, h:h + 1] - a_r2[h:h + 1, :], NEG_BIG)
            wt = (st[:, e * L:(e + 1) * L] * jnp.exp2(expo)).astype(bf16)
            q_w = q_e * carry_w[h:h + 1, :]
            vext = jnp.concatenate([vt_ref[h * dv:(h + 1) * dv, tok], ones_rows], axis=0)
            nd = _dot(jnp.concatenate([vext, ct_pair_b], axis=1), jnp.concatenate([wt, q_w], axis=0))
            den = jnp.maximum(jnp.abs(nd[dv:dv + 1, :]), clamp_r[h:h + 1, :])
            inv = 1.0 / den
            num = nd[0:dv, :]
            scale = inv * lax.rsqrt(inv * inv * jnp.mean(num * num, axis=0, keepdims=True) + EPS)
            hn = num * scale * hn_ref[h * dv:(h + 1) * dv, :]
            gate_o = ot_ref[h * dv:(h + 1) * dv, tok].astype(f32)
            out_ref[h * dv:(h + 1) * dv, tok] = (gate_o * hn).astype(bf16)
            vws.append(vext * ws_r[h:h + 1, :])
        k_split = jnp.concatenate([jnp.where(low, kp, jnp.zeros_like(kp)),
                                   jnp.where(low, jnp.zeros_like(kp), kp)], axis=0)
        decay_pair = jnp.where(low[0:1, :], decay[2 * j:2 * j + 1, :], decay[2 * j + 1:2 * j + 2, :])
        ct[j] = decay_pair * ct[j] + _dot(jnp.concatenate(vws, axis=1), k_split)
    for j in range(pairs):
        ct_ref[j] = ct[j]


def _mlstm_scan(qt, k, vt, ot, ccol, rstat, hn_rep, batch, seq):
    tm = SCAN_CHUNKS * M_CHUNK
    nq = M_HEADS * M_QK_DIM
    nv = M_HEADS * M_V_DIM
    nt = seq // tm
    row = lambda n: pl.BlockSpec((tm, n), lambda b, i: (b * nt + i, 0))
    col = lambda n: pl.BlockSpec((n, tm), lambda b, i: (0, b * nt + i))
    return pl.pallas_call(
        _mlstm_scan_kernel,
        grid=(batch, nt),
        in_specs=[col(nq), row(nq), col(nv), col(nv), row(LANES), col(5 * M_HEADS), _resident(hn_rep.shape)],
        out_specs=col(nv),
        out_shape=jax.ShapeDtypeStruct((nv, batch * seq), bf16),
        scratch_shapes=[pltpu.VMEM((M_HEADS // 2, M_V_DIM + BF16_ROWS, 2 * M_QK_DIM), f32),
                        pltpu.VMEM((M_HEADS, LANES), f32)],
        compiler_params=_params(2),
        name="mlstm_scan",
    )(qt, k, vt, ot, ccol, rstat, hn_rep)


def _attn_proj_kernel(x0_ref, xnext_ref, g_ref, wt_ref, b_ref, qt_ref, k_ref, vt_ref, xn_ref, xnn_ref):
    tm = xnext_ref.shape[0]
    nkv = A_KV_HEADS * A_HEAD_DIM
    xn, prepare_next = _next_input_norm(pl.program_id(0), x0_ref, xnext_ref, g_ref, xn_ref, xnn_ref)
    nq = wt_ref.shape[0] - 2 * nkv
    proj = lambda r0, r1: _dot_nt(wt_ref[r0:r1, :], xn) + _tile_lanes(b_ref[r0:r1, :], tm // LANES)
    kv = proj(nq, nq + 2 * nkv)
    prepare_next()
    k_ref[...] = kv[0:nkv, :].T.astype(bf16)
    vt_ref[...] = kv[nkv:2 * nkv, :].astype(bf16)
    for r0 in range(0, nq, Q_ROWS_PER_DOT):
        q = proj(r0, r0 + Q_ROWS_PER_DOT)
        qt_ref[r0:r0 + Q_ROWS_PER_DOT, :] = (q * (A_HEAD_DIM ** -0.5 * LOG2E)).astype(bf16)


def _attn_proj(x, gain, wt, b_rep):
    t, d = x.shape
    nq = A_Q_HEADS * A_HEAD_DIM
    nkv = A_KV_HEADS * A_HEAD_DIM
    tm = PROJ_TILE
    row = lambda n: pl.BlockSpec((tm, n), lambda i: (i, 0))
    col = lambda n: pl.BlockSpec((n, tm), lambda i: (0, i))
    return pl.pallas_call(
        _attn_proj_kernel,
        grid=(t // tm,),
        in_specs=_proj_in_specs(tm, d, t // tm) + [_resident((1, d)), _resident(wt.shape), _resident(b_rep.shape)],
        out_specs=[col(nq), row(nkv), col(nkv)],
        out_shape=[jax.ShapeDtypeStruct((nq, t), bf16), jax.ShapeDtypeStruct((t, nkv), bf16),
                   jax.ShapeDtypeStruct((nkv, t), bf16)],
        scratch_shapes=[pltpu.VMEM((tm, d), bf16), pltpu.VMEM((tm, d), bf16)],
        compiler_params=_params(1),
        name="attn_proj",
    )(x, x, gain, wt, b_rep)


def _swa_kernel(qt_ref, kc_ref, kp_ref, vtc_ref, vtp_ref, sink_ref, out_ref):
    blk = A_BLOCK
    dh = A_HEAD_DIM
    first = pl.program_id(1) == 0
    ku = lax.broadcasted_iota(jnp.int32, (2 * blk, blk), 0)
    qi = lax.broadcasted_iota(jnp.int32, (2 * blk, blk), 1)
    diff = qi - (ku - blk)
    band = (diff >= 0) & (diff < WINDOW)
    bias = jnp.where(band | (ku == 0), 0.0, NEG_BIG)
    bias_first = jnp.where((band & (ku >= blk)) | (ku == 0), 0.0, NEG_BIG)
    krow = lax.broadcasted_iota(jnp.int32, (2 * blk, LANES), 0)
    klane = lax.broadcasted_iota(jnp.int32, (2 * blk, LANES), 1)
    k_aug = jnp.where((krow == 0) & (klane < 3), 1.0, 0.0).astype(bf16)
    vcol = lax.broadcasted_iota(jnp.int32, (dh, 2 * blk), 1)
    ones_rows = jnp.ones((BF16_ROWS, 2 * blk), bf16)

    hp_lanes = 2 * blk
    shared = {}

    def operands(bq, g):
        if (bq, g) not in shared:
            cols = slice(bq * blk, (bq + 1) * blk)
            if bq == 0:
                kcat = jnp.concatenate([kp_ref[...], kc_ref[cols, :]], axis=0)
                vtcat = jnp.concatenate([vtp_ref[...], vtc_ref[:, cols]], axis=1)
                b1 = jnp.where(first, bias_first, bias)
            else:
                kcat = kc_ref[(bq - 1) * blk:(bq + 1) * blk, :]
                vtcat = vtc_ref[:, (bq - 1) * blk:(bq + 1) * blk]
                b1 = bias
            in_group = (klane >= g * dh) & (klane < (g + 1) * dh) & (krow > 0)
            km = jnp.concatenate([jnp.where(in_group, kcat, jnp.zeros_like(kcat)), k_aug], axis=1)
            vt_g = jnp.where(vcol == 0, jnp.zeros((dh, 2 * blk), bf16), vtcat[g * dh:(g + 1) * dh, :])
            shared[(bq, g)] = km, jnp.concatenate([vt_g, ones_rows], axis=0), _tile_lanes(b1, 2)
        return shared[(bq, g)]

    def logits(bq, g, hp):
        cols = slice(bq * blk, (bq + 1) * blk)
        km, _, b2 = operands(bq, g)
        blocks = []
        for e in range(2):
            h = g * A_GROUP + 2 * hp + e
            pair = qt_ref[(h // 2) * LANES:(h // 2 + 1) * LANES, cols]
            if h % 2 != g:
                pair = jnp.concatenate([pair[dh:2 * dh, :], pair[0:dh, :]], axis=0)
            blocks.append(pair)
        sink_rows = sink_ref[g][:, hp * hp_lanes:(hp + 1) * hp_lanes]
        rhs = jnp.concatenate([jnp.concatenate(blocks, axis=1), sink_rows], axis=0)
        return _dot(km, rhs) + b2

    def finish(bq, g, hp, st):
        cols = slice(bq * blk, (bq + 1) * blk)
        _, vext, _ = operands(bq, g)
        p = jnp.exp2(st - jnp.max(st, axis=0, keepdims=True)).astype(bf16)
        oext = _dot(vext, p)
        o = (oext[0:dh, :] * (1.0 / oext[dh:dh + 1, :])).astype(bf16)
        for e in range(2):
            h = g * A_GROUP + 2 * hp + e
            out_ref[h * dh:(h + 1) * dh, cols] = o[:, e * blk:(e + 1) * blk]

    units = [(bq, g, hp) for bq in range(SWA_TILE // blk) for g in range(A_KV_HEADS) for hp in range(A_GROUP // 2)]
    queued = [logits(*unit) for unit in units[:SWA_LOOKAHEAD]]
    for u, unit in enumerate(units):
        if u + SWA_LOOKAHEAD < len(units):
            queued.append(logits(*units[u + SWA_LOOKAHEAD]))
        finish(*unit, queued[u])


def _swa(qt, k, vt, sink_aug, batch, seq):
    blk = A_BLOCK
    tq = SWA_TILE
    nq = A_Q_HEADS * A_HEAD_DIM
    nkv = A_KV_HEADS * A_HEAD_DIM
    nt = seq // tq
    per = tq // blk
    prev_idx = lambda b, i: b * nt * per + jnp.maximum(i * per - 1, 0)
    return pl.pallas_call(
        _swa_kernel,
        grid=(batch, nt),
        in_specs=[pl.BlockSpec((nq, tq), lambda b, i: (0, b * nt + i)),
                  pl.BlockSpec((tq, nkv), lambda b, i: (b * nt + i, 0)),
                  pl.BlockSpec((blk, nkv), lambda b, i: (prev_idx(b, i), 0)),
                  pl.BlockSpec((nkv, tq), lambda b, i: (0, b * nt + i)),
                  pl.BlockSpec((nkv, blk), lambda b, i: (0, prev_idx(b, i))),
                  _resident(sink_aug.shape)],
        out_specs=pl.BlockSpec((nq, tq), lambda b, i: (0, b * nt + i)),
        out_shape=jax.ShapeDtypeStruct((nq, batch * seq), bf16),
        compiler_params=_params(2),
        name="swa",
    )(qt, k, k, vt, vt, sink_aug)


def _sink_rows(sinks):
    parts = jnp.stack(_split3(sinks.astype(f32) * LOG2E), axis=0)
    per_lane = jnp.repeat(parts.reshape(3, A_KV_HEADS, A_GROUP), A_BLOCK, axis=2)
    return jnp.pad(per_lane.transpose(1, 0, 2), ((0, 0), (0, LANES - 3), (0, 0)))


def _out_ffn_kernel(h0_ref, at0_ref, hn_ref, atn_ref, wo_ref, wup_ref, wd_ref, vec_ref, conv_ref,
                    out_ref, act_ref, h1_ref, xn_ref, h1n_ref, xnn_ref, carry_ref, *, tiles_per_seq):
    tm = hn_ref.shape[0]
    dff = wd_ref.shape[0]
    i = pl.program_id(0)

    d = wo_ref.shape[1]
    out_cols = [slice(n0, n0 + OUT_CHUNK) for n0 in range(0, d, OUT_CHUNK)]

    def out_proj(at_ref, cols):
        return _dot_tn(at_ref[...], wo_ref[:, cols]) + vec_ref[0:1, cols]

    def front(h_ref, z_parts):
        h1 = h_ref[...] + _rmsnorm(jnp.concatenate(z_parts, axis=1), vec_ref[1:2, :])
        h1n_ref[...] = h1
        xnn_ref[...] = _rmsnorm(h1, vec_ref[2:3, :]).astype(bf16)

    @pl.when(i == 0)
    def _():
        front(h0_ref, [out_proj(at0_ref, cols) for cols in out_cols])

    @pl.when(i % tiles_per_seq == 0)
    def _():
        carry_ref[...] = jnp.zeros_like(carry_ref)

    xn_ref[...] = xnn_ref[...]
    h1_ref[...] = h1n_ref[...]
    top = lax.broadcasted_iota(jnp.int32, (8, FF_CHUNK), 0)
    z_parts = []
    for ci, c0 in enumerate(range(0, dff, FF_CHUNK)):
        if FRONT_AFTER_CHUNKS <= ci < FRONT_AFTER_CHUNKS + len(out_cols):
            z_parts.append(out_proj(atn_ref, out_cols[ci - FRONT_AFTER_CHUNKS]))
        if ci == FRONT_AFTER_CHUNKS + len(out_cols):
            front(hn_ref, z_parts)
        cs = slice(c0, c0 + FF_CHUNK)
        gate = _dot(xn_ref[...], wup_ref[:, cs])
        val = _dot(xn_ref[...], wup_ref[:, dff + c0:dff + c0 + FF_CHUNK])
        prev = carry_ref[0:8, cs]
        carry_ref[0:8, cs] = gate[tm - 8:tm, :]
        shifted = []
        for lag in (1, 2):
            rolled = pltpu.roll(gate, lag, axis=0)
            head = jnp.where(top < lag, pltpu.roll(prev, lag, axis=0), rolled[0:8, :])
            shifted.append(jnp.concatenate([head, rolled[8:, :]], axis=0))
        g1, g2 = shifted
        gc = conv_ref[3:4, cs] + conv_ref[0:1, cs] * g2 + conv_ref[1:2, cs] * g1 + conv_ref[2:3, cs] * gate
        act_ref[:, cs] = (gc * jax.nn.sigmoid(gc) * val).astype(bf16)
    y = _dot(act_ref[...], wd_ref[...])
    out_ref[...] = h1_ref[...] + _rmsnorm(y, vec_ref[3:4, :])


def _out_ffn(h, at, wo, wup, wd, vecs, conv, layer, seq):
    t, d = h.shape
    tm = TOKEN_TILE
    n = t // tm
    dff = wd.shape[1]
    nxt = lambda i: jnp.minimum(i + 1, n - 1)
    return pl.pallas_call(
        functools.partial(_out_ffn_kernel, tiles_per_seq=seq // tm),
        grid=(n,),
        in_specs=[pl.BlockSpec((tm, d), lambda i: (0, 0), pipeline_mode=pl.Buffered(1)),
                  pl.BlockSpec((at.shape[0], tm), lambda i: (0, 0), pipeline_mode=pl.Buffered(1)),
                  pl.BlockSpec((tm, d), lambda i: (nxt(i), 0)),
                  pl.BlockSpec((at.shape[0], tm), lambda i: (0, nxt(i))),
                  _resident(wo.shape), _layer_resident(wup.shape, layer), _layer_resident(wd.shape, layer),
                  _resident(vecs.shape), _resident(conv.shape)],
        out_specs=pl.BlockSpec((tm, d), lambda i: (i, 0)),
        out_shape=jax.ShapeDtypeStruct((t, d), f32),
        scratch_shapes=[pltpu.VMEM((tm, dff), bf16),
                        pltpu.VMEM((tm, d), f32), pltpu.VMEM((tm, d), bf16),
                        pltpu.VMEM((tm, d), f32), pltpu.VMEM((tm, d), bf16), pltpu.VMEM((16, dff), f32)],
        compiler_params=_params(1),
        name="out_ffn",
    )(h, at, h, at, wo, wup, wd, vecs, conv)


def kernel(x, m_w_in, m_gate_bias, m_head_norm, m_w_out, a_w_in, a_b_in, a_sinks, a_w_out, a_b_out,
           norm_mix_pre, norm_mix_post, norm_ffn_pre, norm_ffn_post, f_w_up, f_conv_w, f_conv_b, f_w_down):
    batch, seq, d = x.shape
    depth = norm_mix_pre.shape[0]
    h = x.reshape(batch * seq, d)
    row = lambda vec: vec.reshape(1, -1).astype(f32)
    lane_rep = lambda vec: jnp.broadcast_to(vec.astype(f32)[:, None], (vec.shape[0], LANES))

    for i in range(depth):
        j = i // 2
        if i % 2 == 0:
            mixed_t, w_up, w_down = _mlstm_mixer(
                h, row(norm_mix_pre[i]), m_w_in[j].T.astype(bf16), lane_rep(m_gate_bias[j].reshape(-1)),
                lane_rep(m_head_norm[j]), f_w_up, f_w_down, seq)
            w_out = m_w_out[j].astype(bf16)
            b_out = jnp.zeros((d,), f32)
        else:
            qt, k, vt = _attn_proj(h, row(norm_mix_pre[i]), a_w_in[j].T.astype(bf16), lane_rep(a_b_in[j]))
            mixed_t = _swa(qt, k, vt, _sink_rows(a_sinks[j]), batch, seq)
            w_out = a_w_out[j].astype(bf16)
            b_out = a_b_out[j]
        vecs = jnp.pad(jnp.stack([b_out, norm_mix_post[i], norm_ffn_pre[i], norm_ffn_post[i]]).astype(f32),
                       ((0, 4), (0, 0)))
        conv = jnp.pad(jnp.concatenate([f_conv_w[i], f_conv_b[i][None]], axis=0).astype(f32), ((0, 12), (0, 0)))
        h = _out_ffn(h, mixed_t, w_out, w_up, w_down, vecs, conv, i, seq)
    return h.reshape(batch, seq, d)
```

```python
import functools

import jax
import jax.numpy as jnp
from jax import lax
from jax.experimental import pallas as pl
from jax.experimental.pallas import tpu as pltpu

EPS = 1e-6
LANES = 128
BF16_ROWS = 16

M_HEADS = 8
M_QK_DIM = 64
M_V_DIM = 128
GATE_CAP = 15.0
M_CHUNK = 128
MLSTM_GROUP = 256
MIXER_LOOKAHEAD = 2

A_HEAD_DIM = 64
A_Q_HEADS = 16
A_KV_HEADS = 2
A_GROUP = A_Q_HEADS // A_KV_HEADS
WINDOW = 128
A_BLOCK = 128

LOG2E = 1.4426950408889634
NEG_BIG = -1e30

TOKEN_TILE = 512
PROJ_TILE = 1024
SWA_TILE = 512
SWA_LOOKAHEAD = 4
FF_CHUNK = 256
Q_ROWS_PER_DOT = 256
OUT_CHUNK = 256
FRONT_AFTER_CHUNKS = 2
VMEM_LIMIT = 60 * 1024 * 1024

bf16 = jnp.bfloat16
f32 = jnp.float32


def _dot(a, b):
    return jnp.dot(a, b, preferred_element_type=f32)


def _dot_nt(a, b):
    return lax.dot_general(a, b, (((1,), (1,)), ((), ())), preferred_element_type=f32)


def _dot_tn(a, b):
    return lax.dot_general(a, b, (((0,), (0,)), ((), ())), preferred_element_type=f32)


def _rmsnorm(x, g):
    return x * lax.rsqrt(jnp.mean(x * x, axis=-1, keepdims=True) + EPS) * g


def _split3(x):
    hi = x.astype(bf16)
    r1 = x - hi.astype(f32)
    mid = r1.astype(bf16)
    lo = (r1 - mid.astype(f32)).astype(bf16)
    return hi, mid, lo


def _log_sigmoid(x):
    return jnp.minimum(x, 0.0) - jnp.log1p(jnp.exp(-jnp.abs(x)))


def _tile_lanes(x, reps):
    return jnp.concatenate([x] * reps, axis=1)


def _next_input_norm(i, x0_ref, xnext_ref, g_ref, xn_ref, xnn_ref):
    @pl.when(i == 0)
    def _():
        xnn_ref[...] = _rmsnorm(x0_ref[...], g_ref[...]).astype(bf16)

    xn_ref[...] = xnn_ref[...]

    def prepare_next():
        xnn_ref[...] = _rmsnorm(xnext_ref[...], g_ref[...]).astype(bf16)

    return xn_ref[...], prepare_next


def _sigmoid(x):
    return 0.5 * jnp.tanh(0.5 * x) + 0.5


def _proj_in_specs(tm, d, n_tiles):
    return [pl.BlockSpec((tm, d), lambda i: (0, 0), pipeline_mode=pl.Buffered(1)),
            pl.BlockSpec((tm, d), lambda i: (jnp.minimum(i + 1, n_tiles - 1), 0))]


def _resident(shape):
    nd = len(shape)
    return pl.BlockSpec(shape, lambda *_: (0,) * nd, pipeline_mode=pl.Buffered(1))


def _layer_resident(shape, layer):
    nd = len(shape) - 1
    return pl.BlockSpec((None,) + tuple(shape[1:]), lambda *_: (layer,) + (0,) * nd, pipeline_mode=pl.Buffered(1))


def _params(n_axes):
    return pltpu.CompilerParams(
        dimension_semantics=("arbitrary",) * n_axes, vmem_limit_bytes=VMEM_LIMIT)


def _mlstm_kernel(x0_ref, xnext_ref, g_ref, wt_ref, gb_ref, hn_ref, wup_ref, wdown_ref,
                  out_ref, wup_bf_ref, wdown_bf_ref, xn_ref, xnn_ref, ct_ref, m_ref, *, tiles_per_seq):
    tm = xnext_ref.shape[0]
    L = M_CHUNK
    H = M_HEADS
    dk, dv = M_QK_DIM, M_V_DIM
    nq = H * dk
    nv = H * dv
    pairs = H // 2
    G = MLSTM_GROUP
    cpg = G // L
    i = pl.program_id(0)

    @pl.when(i % tiles_per_seq == 0)
    def _():
        ct_ref[...] = jnp.zeros_like(ct_ref)
        m_ref[...] = jnp.zeros_like(m_ref)

    xn, prepare_next = _next_input_norm(i, x0_ref, xnext_ref, g_ref, xn_ref, xnn_ref)

    upper = lax.broadcasted_iota(jnp.int32, (L, L), 0) <= lax.broadcasted_iota(jnp.int32, (L, L), 1)
    triu = jnp.where(upper, 1.0, 0.0).astype(bf16)
    lane_h = lax.broadcasted_iota(jnp.int32, (H, L), 1)
    lane = lax.broadcasted_iota(jnp.int32, (L, LANES), 1)
    low = lane < dk
    ones_rows = jnp.ones((BF16_ROWS, L), bf16)
    zeros_half = jnp.zeros((dk, L), bf16)

    def projection_steps(g):
        xg = xn[g * G:(g + 1) * G, :]
        res = {}
        piece = lambda r0, r1: _dot_nt(wt_ref[r0:r1, :], xg)
        o0 = 2 * nq + nv

        def gates_and_o_hi():
            p = piece(o0 + nv // 2, o0 + nv + 2 * H)
            res["o_hi"] = _sigmoid(p[0:nv // 2, :]).astype(bf16)
            raw = p[nv // 2:, :] + _tile_lanes(gb_ref[...], G // LANES)
            capped = GATE_CAP * jnp.tanh(raw * (1.0 / GATE_CAP))
            res["grow"] = jnp.where(lax.broadcasted_iota(jnp.int32, raw.shape, 0) < H, capped, _log_sigmoid(capped))

        def o_lo():
            res["o_lo"] = _sigmoid(piece(o0, o0 + nv // 2)).astype(bf16)

        def k_and_stats():
            res["k"] = piece(nq, 2 * nq).T.astype(bf16)
            grow = res["grow"]
            stacked = jnp.concatenate([grow[:, c * L:(c + 1) * L] for c in range(cpg)], axis=0)
            sums = _dot(jnp.concatenate(_split3(stacked), axis=0), triu)
            n = cpg * 2 * H
            cum = sums[0:n] + sums[n:2 * n] + sums[2 * n:]
            stats = []
            for c in range(cpg):
                b_r = cum[c * 2 * H + H:(c + 1) * 2 * H, :]
                c_r = grow[0:H, c * L:(c + 1) * L] - b_r
                cmax = c_r
                shift = 1
                while shift < L:
                    cmax = jnp.where(lane_h >= shift, jnp.maximum(cmax, pltpu.roll(cmax, shift, axis=1)), cmax)
                    shift *= 2
                b_last = jnp.broadcast_to(b_r[:, L - 1:L], (H, L))
                tail = c_r + b_last
                tail_max = jnp.broadcast_to(jnp.max(tail, axis=-1, keepdims=True), (H, L))
                c_cols = jnp.concatenate([c_r * LOG2E, jnp.zeros((LANES - H, L), f32)], axis=0).T
                stats.append((b_r, cmax, tail, tail_max, b_last, c_cols))
            res["stats"] = stats

        def v_lo():
            res["v_lo"] = piece(2 * nq, 2 * nq + nv // 2).astype(bf16)

        def v_hi():
            res["v_hi"] = piece(2 * nq + nv // 2, 2 * nq + nv).astype(bf16)

        def q():
            res["q"] = (piece(0, nq) * (dk ** -0.5)).astype(bf16)

        return res, [gates_and_o_hi, o_lo, k_and_stats, v_lo, v_hi, q]

    def finalize(res):
        res["ot"] = jnp.concatenate([res.pop("o_lo"), res.pop("o_hi")], axis=0)
        res["vt"] = jnp.concatenate([res.pop("v_lo"), res.pop("v_hi")], axis=0)
        return res

    def kq(p, c, j):
        tok = slice(c * L, (c + 1) * L)
        kp = p["k"][tok, j * LANES:(j + 1) * LANES]
        qtp = p["q"][j * LANES:(j + 1) * LANES, tok]
        q_even = jnp.concatenate([qtp[0:dk, :], zeros_half], axis=0)
        q_odd = jnp.concatenate([zeros_half, qtp[dk:2 * dk, :]], axis=0)
        return kp, (q_even, q_odd), _dot(kp, jnp.concatenate([q_even, q_odd], axis=1))

    def scan_unit(p, g, c, j, m_prev, m_new, ct, kq_res):
        kp, q_eo, st = kq_res
        tok = slice(c * L, (c + 1) * L)
        out_tok = slice(g * G + c * L, g * G + (c + 1) * L)
        b_r, cmax, tail, _, b_last, c_cols = p["stats"][c]
        a_r = jnp.maximum(m_prev, cmax)
        a_r2 = a_r * LOG2E
        carry_w = jnp.exp(m_prev - a_r).astype(bf16)
        clamp_r = jnp.exp(-(a_r + b_r))
        ws_r = jnp.exp(tail - m_new).astype(bf16)
        decay = jnp.exp(b_last + m_prev - m_new)
        ct_pair_b = ct[j].astype(bf16)
        vws = []
        for e, q_e in enumerate(q_eo):
            h = 2 * j + e
            expo = jnp.where(upper, c_cols[:, h:h + 1] - a_r2[h:h + 1, :], NEG_BIG)
            wt = (st[:, e * L:(e + 1) * L] * jnp.exp2(expo)).astype(bf16)
            q_w = q_e * carry_w[h:h + 1, :]
            vext = jnp.concatenate([p["vt"][h * dv:(h + 1) * dv, tok], ones_rows], axis=0)
            nd = _dot(jnp.concatenate([vext, ct_pair_b], axis=1), jnp.concatenate([wt, q_w], axis=0))
            den = jnp.maximum(jnp.abs(nd[dv:dv + 1, :]), clamp_r[h:h + 1, :])
            inv = 1.0 / den
            num = nd[0:dv, :]
            scale = inv * lax.rsqrt(inv * inv * jnp.mean(num * num, axis=0, keepdims=True) + EPS)
            hn = num * scale * hn_ref[h * dv:(h + 1) * dv, :]
            gate_o = p["ot"][h * dv:(h + 1) * dv, tok].astype(f32)
            out_ref[h * dv:(h + 1) * dv, out_tok] = (gate_o * hn).astype(bf16)
            vws.append(vext * ws_r[h:h + 1, :])
        k_split = jnp.concatenate([jnp.where(low, kp, jnp.zeros_like(kp)),
                                   jnp.where(low, jnp.zeros_like(kp), kp)], axis=0)
        decay_pair = jnp.where(low[0:1, :], decay[2 * j:2 * j + 1, :], decay[2 * j + 1:2 * j + 2, :])
        ct[j] = decay_pair * ct[j] + _dot(jnp.concatenate(vws, axis=1), k_split)

    n_groups = tm // G
    cur, steps = projection_steps(0)
    for k_step, step in enumerate(steps):
        step()
        if k_step == 0:
            prepare_next()
    cur = finalize(cur)

    ct = [ct_ref[j] for j in range(pairs)]
    m = m_ref[...]
    units = [(g, c, j) for g in range(n_groups) for c in range(cpg) for j in range(pairs)]
    per_group = cpg * pairs
    projected = {0: cur}
    kq_queue = {}

    def queue_kq(t):
        g, c, j = units[t]
        kq_queue[t] = kq(projected[g], c, j)

    for t in range(MIXER_LOOKAHEAD):
        queue_kq(t)
    for t, (g, c, j) in enumerate(units):
        u = t % per_group
        if u == 0:
            nxt, steps = projection_steps(g + 1) if g + 1 < n_groups else (None, [])
            stats = projected[g]["stats"]
            m_prevs = []
            for cc in range(cpg):
                m_prevs.append(m)
                m = jnp.maximum(stats[cc][4] + m, stats[cc][3])
            m_prevs.append(m)
        if u < len(steps):
            steps[u]()
            if u + 1 == len(steps):
                projected[g + 1] = finalize(nxt)
        if t + MIXER_LOOKAHEAD < len(units):
            queue_kq(t + MIXER_LOOKAHEAD)
        scan_unit(projected[g], g, c, j, m_prevs[c], m_prevs[c + 1], ct, kq_queue.pop(t))
    m_ref[...] = m
    for j in range(pairs):
        ct_ref[j] = ct[j]
    wup_bf_ref[...] = wup_ref[...].astype(bf16)
    wdown_bf_ref[...] = wdown_ref[...].astype(bf16)


def _mlstm_mixer(x, gain, wt, gate_bias, hn_rep, w_up, w_down, seq):
    t, d = x.shape
    nv = M_HEADS * M_V_DIM
    tm = PROJ_TILE
    steps = t // tm
    slab = lambda w: pl.BlockSpec((w.shape[0], w.shape[1] // steps, w.shape[2]), lambda i: (0, i, 0))
    return pl.pallas_call(
        functools.partial(_mlstm_kernel, tiles_per_seq=seq // tm),
        grid=(steps,),
        in_specs=_proj_in_specs(tm, d, steps) + [
            _resident((1, d)), _resident(wt.shape), _resident(gate_bias.shape), _resident(hn_rep.shape),
            slab(w_up), slab(w_down)],
        out_specs=[pl.BlockSpec((nv, tm), lambda i: (0, i)), slab(w_up), slab(w_down)],
        out_shape=[jax.ShapeDtypeStruct((nv, t), bf16),
                   jax.ShapeDtypeStruct(w_up.shape, bf16), jax.ShapeDtypeStruct(w_down.shape, bf16)],
        scratch_shapes=[pltpu.VMEM((tm, d), bf16), pltpu.VMEM((tm, d), bf16),
                        pltpu.VMEM((M_HEADS // 2, M_V_DIM + BF16_ROWS, 2 * M_QK_DIM), f32),
                        pltpu.VMEM((M_HEADS, LANES), f32)],
        compiler_params=_params(1),
        name="mlstm_mixer",
    )(x, x, gain, wt, gate_bias, hn_rep, w_up, w_down)


def _attn_proj_kernel(x0_ref, xnext_ref, g_ref, wt_ref, b_ref, qt_ref, k_ref, vt_ref, xn_ref, xnn_ref):
    tm = xnext_ref.shape[0]
    nkv = A_KV_HEADS * A_HEAD_DIM
    xn, prepare_next = _next_input_norm(pl.program_id(0), x0_ref, xnext_ref, g_ref, xn_ref, xnn_ref)
    nq = wt_ref.shape[0] - 2 * nkv
    proj = lambda r0, r1: _dot_nt(wt_ref[r0:r1, :], xn) + _tile_lanes(b_ref[r0:r1, :], tm // LANES)
    kv = proj(nq, nq + 2 * nkv)
    prepare_next()
    k_ref[...] = kv[0:nkv, :].T.astype(bf16)
    vt_ref[...] = kv[nkv:2 * nkv, :].astype(bf16)
    for r0 in range(0, nq, Q_ROWS_PER_DOT):
        q = proj(r0, r0 + Q_ROWS_PER_DOT)
        qt_ref[r0:r0 + Q_ROWS_PER_DOT, :] = (q * (A_HEAD_DIM ** -0.5 * LOG2E)).astype(bf16)


def _attn_proj(x, gain, wt, b_rep):
    t, d = x.shape
    nq = A_Q_HEADS * A_HEAD_DIM
    nkv = A_KV_HEADS * A_HEAD_DIM
    tm = PROJ_TILE
    row = lambda n: pl.BlockSpec((tm, n), lambda i: (i, 0))
    col = lambda n: pl.BlockSpec((n, tm), lambda i: (0, i))
    return pl.pallas_call(
        _attn_proj_kernel,
        grid=(t // tm,),
        in_specs=_proj_in_specs(tm, d, t // tm) + [_resident((1, d)), _resident(wt.shape), _resident(b_rep.shape)],
        out_specs=[col(nq), row(nkv), col(nkv)],
        out_shape=[jax.ShapeDtypeStruct((nq, t), bf16), jax.ShapeDtypeStruct((t, nkv), bf16),
                   jax.ShapeDtypeStruct((nkv, t), bf16)],
        scratch_shapes=[pltpu.VMEM((tm, d), bf16), pltpu.VMEM((tm, d), bf16)],
        compiler_params=_params(1),
        name="attn_proj",
    )(x, x, gain, wt, b_rep)


def _swa_kernel(qt_ref, kc_ref, kp_ref, vtc_ref, vtp_ref, sink_ref, out_ref):
    blk = A_BLOCK
    dh = A_HEAD_DIM
    first = pl.program_id(1) == 0
    ku = lax.broadcasted_iota(jnp.int32, (2 * blk, blk), 0)
    qi = lax.broadcasted_iota(jnp.int32, (2 * blk, blk), 1)
    diff = qi - (ku - blk)
    band = (diff >= 0) & (diff < WINDOW)
    bias = jnp.where(band | (ku == 0), 0.0, NEG_BIG)
    bias_first = jnp.where((band & (ku >= blk)) | (ku == 0), 0.0, NEG_BIG)
    krow = lax.broadcasted_iota(jnp.int32, (2 * blk, LANES), 0)
    klane = lax.broadcasted_iota(jnp.int32, (2 * blk, LANES), 1)
    k_aug = jnp.where((krow == 0) & (klane < 3), 1.0, 0.0).astype(bf16)
    vcol = lax.broadcasted_iota(jnp.int32, (dh, 2 * blk), 1)
    ones_rows = jnp.ones((BF16_ROWS, 2 * blk), bf16)

    hp_lanes = 2 * blk
    shared = {}

    def operands(bq, g):
        if (bq, g) not in shared:
            cols = slice(bq * blk, (bq + 1) * blk)
            if bq == 0:
                kcat = jnp.concatenate([kp_ref[...], kc_ref[cols, :]], axis=0)
                vtcat = jnp.concatenate([vtp_ref[...], vtc_ref[:, cols]], axis=1)
                b1 = jnp.where(first, bias_first, bias)
            else:
                kcat = kc_ref[(bq - 1) * blk:(bq + 1) * blk, :]
                vtcat = vtc_ref[:, (bq - 1) * blk:(bq + 1) * blk]
                b1 = bias
            in_group = (klane >= g * dh) & (klane < (g + 1) * dh) & (krow > 0)
            km = jnp.concatenate([jnp.where(in_group, kcat, jnp.zeros_like(kcat)), k_aug], axis=1)
            vt_g = jnp.where(vcol == 0, jnp.zeros((dh, 2 * blk), bf16), vtcat[g * dh:(g + 1) * dh, :])
            shared[(bq, g)] = km, jnp.concatenate([vt_g, ones_rows], axis=0), _tile_lanes(b1, 2)
        return shared[(bq, g)]

    def logits(bq, g, hp):
        cols = slice(bq * blk, (bq + 1) * blk)
        km, _, b2 = operands(bq, g)
        blocks = []
        for e in range(2):
            h = g * A_GROUP + 2 * hp + e
            pair = qt_ref[(h // 2) * LANES:(h // 2 + 1) * LANES, cols]
            if h % 2 != g:
                pair = jnp.concatenate([pair[dh:2 * dh, :], pair[0:dh, :]], axis=0)
            blocks.append(pair)
        sink_rows = sink_ref[g][:, hp * hp_lanes:(hp + 1) * hp_lanes]
        rhs = jnp.concatenate([jnp.concatenate(blocks, axis=1), sink_rows], axis=0)
        return _dot(km, rhs) + b2

    def finish(bq, g, hp, st):
        cols = slice(bq * blk, (bq + 1) * blk)
        _, vext, _ = operands(bq, g)
        p = jnp.exp2(st - jnp.max(st, axis=0, keepdims=True)).astype(bf16)
        oext = _dot(vext, p)
        o = (oext[0:dh, :] * (1.0 / oext[dh:dh + 1, :])).astype(bf16)
        for e in range(2):
            h = g * A_GROUP + 2 * hp + e
            out_ref[h * dh:(h + 1) * dh, cols] = o[:, e * blk:(e + 1) * blk]

    units = [(bq, g, hp) for bq in range(SWA_TILE // blk) for g in range(A_KV_HEADS) for hp in range(A_GROUP // 2)]
    queued = [logits(*unit) for unit in units[:SWA_LOOKAHEAD]]
    for u, unit in enumerate(units):
        if u + SWA_LOOKAHEAD < len(units):
            queued.append(logits(*units[u + SWA_LOOKAHEAD]))
        finish(*unit, queued[u])


def _swa(qt, k, vt, sink_aug, batch, seq):
    blk = A_BLOCK
    tq = SWA_TILE
    nq = A_Q_HEADS * A_HEAD_DIM
    nkv = A_KV_HEADS * A_HEAD_DIM
    nt = seq // tq
    per = tq // blk
    prev_idx = lambda b, i: b * nt * per + jnp.maximum(i * per - 1, 0)
    return pl.pallas_call(
        _swa_kernel,
        grid=(batch, nt),
        in_specs=[pl.BlockSpec((nq, tq), lambda b, i: (0, b * nt + i)),
                  pl.BlockSpec((tq, nkv), lambda b, i: (b * nt + i, 0)),
                  pl.BlockSpec((blk, nkv), lambda b, i: (prev_idx(b, i), 0)),
                  pl.BlockSpec((nkv, tq), lambda b, i: (0, b * nt + i)),
                  pl.BlockSpec((nkv, blk), lambda b, i: (0, prev_idx(b, i))),
                  _resident(sink_aug.shape)],
        out_specs=pl.BlockSpec((nq, tq), lambda b, i: (0, b * nt + i)),
        out_shape=jax.ShapeDtypeStruct((nq, batch * seq), bf16),
        compiler_params=_params(2),
        name="swa",
    )(qt, k, k, vt, vt, sink_aug)


def _sink_rows(sinks):
    parts = jnp.stack(_split3(sinks.astype(f32) * LOG2E), axis=0)
    per_lane = jnp.repeat(parts.reshape(3, A_KV_HEADS, A_GROUP), A_BLOCK, axis=2)
    return jnp.pad(per_lane.transpose(1, 0, 2), ((0, 0), (0, LANES - 3), (0, 0)))


def _out_ffn_kernel(h0_ref, at0_ref, hn_ref, atn_ref, wo_ref, wup_ref, wd_ref, vec_ref, conv_ref,
                    out_ref, act_ref, h1_ref, xn_ref, h1n_ref, xnn_ref, carry_ref, *, tiles_per_seq):
    tm = hn_ref.shape[0]
    dff = wd_ref.shape[0]
    i = pl.program_id(0)

    d = wo_ref.shape[1]
    out_cols = [slice(n0, n0 + OUT_CHUNK) for n0 in range(0, d, OUT_CHUNK)]

    def out_proj(at_ref, cols):
        return _dot_tn(at_ref[...], wo_ref[:, cols]) + vec_ref[0:1, cols]

    def front(h_ref, z_parts):
        h1 = h_ref[...] + _rmsnorm(jnp.concatenate(z_parts, axis=1), vec_ref[1:2, :])
        h1n_ref[...] = h1
        xnn_ref[...] = _rmsnorm(h1, vec_ref[2:3, :]).astype(bf16)

    @pl.when(i == 0)
    def _():
        front(h0_ref, [out_proj(at0_ref, cols) for cols in out_cols])

    @pl.when(i % tiles_per_seq == 0)
    def _():
        carry_ref[...] = jnp.zeros_like(carry_ref)

    xn_ref[...] = xnn_ref[...]
    h1_ref[...] = h1n_ref[...]
    top = lax.broadcasted_iota(jnp.int32, (8, FF_CHUNK), 0)
    z_parts = []
    for ci, c0 in enumerate(range(0, dff, FF_CHUNK)):
        if FRONT_AFTER_CHUNKS <= ci < FRONT_AFTER_CHUNKS + len(out_cols):
            z_parts.append(out_proj(atn_ref, out_cols[ci - FRONT_AFTER_CHUNKS]))
        if ci == FRONT_AFTER_CHUNKS + len(out_cols):
            front(hn_ref, z_parts)
        cs = slice(c0, c0 + FF_CHUNK)
        gate = _dot(xn_ref[...], wup_ref[:, cs])
        val = _dot(xn_ref[...], wup_ref[:, dff + c0:dff + c0 + FF_CHUNK])
        prev = carry_ref[0:8, cs]
        carry_ref[0:8, cs] = gate[tm - 8:tm, :]
        shifted = []
        for lag in (1, 2):
            rolled = pltpu.roll(gate, lag, axis=0)
            head = jnp.where(top < lag, pltpu.roll(prev, lag, axis=0), rolled[0:8, :])
            shifted.append(jnp.concatenate([head, rolled[8:, :]], axis=0))
        g1, g2 = shifted
        gc = conv_ref[3:4, cs] + conv_ref[0:1, cs] * g2 + conv_ref[1:2, cs] * g1 + conv_ref[2:3, cs] * gate
        act_ref[:, cs] = (gc * jax.nn.sigmoid(gc) * val).astype(bf16)
    y = _dot(act_ref[...], wd_ref[...])
    out_ref[...] = h1_ref[...] + _rmsnorm(y, vec_ref[3:4, :])


def _out_ffn(h, at, wo, wup, wd, vecs, conv, layer, seq):
    t, d = h.shape
    tm = TOKEN_TILE
    n = t // tm
    dff = wd.shape[1]
    nxt = lambda i: jnp.minimum(i + 1, n - 1)
    return pl.pallas_call(
        functools.partial(_out_ffn_kernel, tiles_per_seq=seq // tm),
        grid=(n,),
        in_specs=[pl.BlockSpec((tm, d), lambda i: (0, 0), pipeline_mode=pl.Buffered(1)),
                  pl.BlockSpec((at.shape[0], tm), lambda i: (0, 0), pipeline_mode=pl.Buffered(1)),
                  pl.BlockSpec((tm, d), lambda i: (nxt(i), 0)),
                  pl.BlockSpec((at.shape[0], tm), lambda i: (0, nxt(i))),
                  _resident(wo.shape), _layer_resident(wup.shape, layer), _layer_resident(wd.shape, layer),
                  _resident(vecs.shape), _resident(conv.shape)],
        out_specs=pl.BlockSpec((tm, d), lambda i: (i, 0)),
        out_shape=jax.ShapeDtypeStruct((t, d), f32),
        scratch_shapes=[pltpu.VMEM((tm, dff), bf16),
                        pltpu.VMEM((tm, d), f32), pltpu.VMEM((tm, d), bf16),
                        pltpu.VMEM((tm, d), f32), pltpu.VMEM((tm, d), bf16), pltpu.VMEM((16, dff), f32)],
        compiler_params=_params(1),
        name="out_ffn",
    )(h, at, h, at, wo, wup, wd, vecs, conv)


def kernel(x, m_w_in, m_gate_bias, m_head_norm, m_w_out, a_w_in, a_b_in, a_sinks, a_w_out, a_b_out,
           norm_mix_pre, norm_mix_post, norm_ffn_pre, norm_ffn_post, f_w_up, f_conv_w, f_conv_b, f_w_down):
    batch, seq, d = x.shape
    depth = norm_mix_pre.shape[0]
    h = x.reshape(batch * seq, d)
    row = lambda vec: vec.reshape(1, -1).astype(f32)
    lane_rep = lambda vec: jnp.broadcast_to(vec.astype(f32)[:, None], (vec.shape[0], LANES))

    for i in range(depth):
        j = i // 2
        if i % 2 == 0:
            mixed_t, w_up, w_down = _mlstm_mixer(
                h, row(norm_mix_pre[i]), m_w_in[j].T.astype(bf16), lane_rep(m_gate_bias[j].reshape(-1)),
                lane_rep(m_head_norm[j]), f_w_up, f_w_down, seq)
            w_out = m_w_out[j].astype(bf16)
            b_out = jnp.zeros((d,), f32)
        else:
            qt, k, vt = _attn_proj(h, row(norm_mix_pre[i]), a_w_in[j].T.astype(bf16), lane_rep(a_b_in[j]))
            mixed_t = _swa(qt, k, vt, _sink_rows(a_sinks[j]), batch, seq)
            w_out = a_w_out[j].astype(bf16)
            b_out = a_b_out[j]
        vecs = jnp.pad(jnp.stack([b_out, norm_mix_post[i], norm_ffn_pre[i], norm_ffn_post[i]]).astype(f32),
                       ((0, 4), (0, 0)))
        conv = jnp.pad(jnp.concatenate([f_conv_w[i], f_conv_b[i][None]], axis=0).astype(f32), ((0, 12), (0, 0)))
        h = _out_ffn(h, mixed_t, w_out, w_up, w_down, vecs, conv, i, seq)
    return h.reshape(batch, seq, d)
```

```python
import functools

import jax
import jax.numpy as jnp
from jax import lax
from jax.experimental import pallas as pl
from jax.experimental.pallas import tpu as pltpu

EPS = 1e-6
LANES = 128
BF16_ROWS = 16

M_HEADS = 8
M_QK_DIM = 64
M_V_DIM = 128
GATE_CAP = 15.0
M_CHUNK = 128
MLSTM_GROUP = 256
MIXER_LOOKAHEAD = 2

A_HEAD_DIM = 64
A_Q_HEADS = 16
A_KV_HEADS = 2
A_GROUP = A_Q_HEADS // A_KV_HEADS
WINDOW = 128
A_BLOCK = 128

LOG2E = 1.4426950408889634
NEG_BIG = -1e30

TOKEN_TILE = 512
PROJ_TILE = 1024
SWA_TILE = 512
SWA_LOOKAHEAD = 4
FF_CHUNK = 256
Q_ROWS_PER_DOT = 256
OUT_CHUNK = 256
FRONT_AFTER_CHUNKS = 2
VMEM_LIMIT = 60 * 1024 * 1024

bf16 = jnp.bfloat16
f32 = jnp.float32


def _dot(a, b):
    return jnp.dot(a, b, preferred_element_type=f32)


def _dot_nt(a, b):
    return lax.dot_general(a, b, (((1,), (1,)), ((), ())), preferred_element_type=f32)


def _dot_tn(a, b):
    return lax.dot_general(a, b, (((0,), (0,)), ((), ())), preferred_element_type=f32)


def _rmsnorm(x, g):
    return x * lax.rsqrt(jnp.mean(x * x, axis=-1, keepdims=True) + EPS) * g


def _split3(x):
    hi = x.astype(bf16)
    r1 = x - hi.astype(f32)
    mid = r1.astype(bf16)
    lo = (r1 - mid.astype(f32)).astype(bf16)
    return hi, mid, lo


def _log_sigmoid(x):
    return jnp.minimum(x, 0.0) - jnp.log1p(jnp.exp(-jnp.abs(x)))


def _tile_lanes(x, reps):
    return jnp.concatenate([x] * reps, axis=1)


def _next_input_norm(i, x0_ref, xnext_ref, g_ref, xn_ref, xnn_ref):
    @pl.when(i == 0)
    def _():
        xnn_ref[...] = _rmsnorm(x0_ref[...], g_ref[...]).astype(bf16)

    xn_ref[...] = xnn_ref[...]

    def prepare_next():
        xnn_ref[...] = _rmsnorm(xnext_ref[...], g_ref[...]).astype(bf16)

    return xn_ref[...], prepare_next


def _sigmoid(x):
    return 0.5 * jnp.tanh(0.5 * x) + 0.5


def _proj_in_specs(tm, d, n_tiles):
    return [pl.BlockSpec((tm, d), lambda i: (0, 0), pipeline_mode=pl.Buffered(1)),
            pl.BlockSpec((tm, d), lambda i: (jnp.minimum(i + 1, n_tiles - 1), 0))]


def _resident(shape):
    nd = len(shape)
    return pl.BlockSpec(shape, lambda *_: (0,) * nd, pipeline_mode=pl.Buffered(1))


def _layer_resident(shape, layer):
    nd = len(shape) - 1
    return pl.BlockSpec((None,) + tuple(shape[1:]), lambda *_: (layer,) + (0,) * nd, pipeline_mode=pl.Buffered(1))


def _params(n_axes):
    return pltpu.CompilerParams(
        dimension_semantics=("arbitrary",) * n_axes, vmem_limit_bytes=VMEM_LIMIT)


def _mlstm_kernel(x0_ref, xnext_ref, g_ref, wt_ref, gb_ref, hn_ref, wup_ref, wdown_ref,
                  out_ref, wup_bf_ref, wdown_bf_ref, xn_ref, xnn_ref, ct_ref, m_ref,
                  q0_ref, k0_ref, vt0_ref, ot0_ref, stat0_ref, ccol0_ref, *, tiles_per_seq):
    tm = xnext_ref.shape[0]
    L = M_CHUNK
    H = M_HEADS
    dk, dv = M_QK_DIM, M_V_DIM
    nq = H * dk
    nv = H * dv
    pairs = H // 2
    G = MLSTM_GROUP
    cpg = G // L
    i = pl.program_id(0)

    @pl.when(i % tiles_per_seq == 0)
    def _():
        ct_ref[...] = jnp.zeros_like(ct_ref)
        m_ref[...] = jnp.zeros_like(m_ref)

    xn, prepare_next = _next_input_norm(i, x0_ref, xnext_ref, g_ref, xn_ref, xnn_ref)

    upper = lax.broadcasted_iota(jnp.int32, (L, L), 0) <= lax.broadcasted_iota(jnp.int32, (L, L), 1)
    triu = jnp.where(upper, 1.0, 0.0).astype(bf16)
    lane_h = lax.broadcasted_iota(jnp.int32, (H, L), 1)
    lane = lax.broadcasted_iota(jnp.int32, (L, LANES), 1)
    low = lane < dk
    ones_rows = jnp.ones((BF16_ROWS, L), bf16)
    zeros_half = jnp.zeros((dk, L), bf16)

    def projection_steps(xg):
        res = {}
        piece = lambda r0, r1: _dot_nt(wt_ref[r0:r1, :], xg)
        o0 = 2 * nq + nv

        def gates_and_o_hi():
            p = piece(o0 + nv // 2, o0 + nv + 2 * H)
            res["o_hi"] = _sigmoid(p[0:nv // 2, :]).astype(bf16)
            raw = p[nv // 2:, :] + _tile_lanes(gb_ref[...], G // LANES)
            capped = GATE_CAP * jnp.tanh(raw * (1.0 / GATE_CAP))
            res["grow"] = jnp.where(lax.broadcasted_iota(jnp.int32, raw.shape, 0) < H, capped, _log_sigmoid(capped))

        def o_lo():
            res["o_lo"] = _sigmoid(piece(o0, o0 + nv // 2)).astype(bf16)

        def k_and_stats():
            res["k"] = piece(nq, 2 * nq).T.astype(bf16)
            grow = res["grow"]
            stacked = jnp.concatenate([grow[:, c * L:(c + 1) * L] for c in range(cpg)], axis=0)
            sums = _dot(jnp.concatenate(_split3(stacked), axis=0), triu)
            n = cpg * 2 * H
            cum = sums[0:n] + sums[n:2 * n] + sums[2 * n:]
            stats = []
            for c in range(cpg):
                b_r = cum[c * 2 * H + H:(c + 1) * 2 * H, :]
                c_r = grow[0:H, c * L:(c + 1) * L] - b_r
                cmax = c_r
                shift = 1
                while shift < L:
                    cmax = jnp.where(lane_h >= shift, jnp.maximum(cmax, pltpu.roll(cmax, shift, axis=1)), cmax)
                    shift *= 2
                b_last = jnp.broadcast_to(b_r[:, L - 1:L], (H, L))
                tail = c_r + b_last
                tail_max = jnp.broadcast_to(jnp.max(tail, axis=-1, keepdims=True), (H, L))
                c_cols = jnp.concatenate([c_r * LOG2E, jnp.zeros((LANES - H, L), f32)], axis=0).T
                stats.append((b_r, cmax, tail, tail_max, b_last, c_cols))
            res["stats"] = stats

        def v_lo():
            res["v_lo"] = piece(2 * nq, 2 * nq + nv // 2).astype(bf16)

        def v_hi():
            res["v_hi"] = piece(2 * nq + nv // 2, 2 * nq + nv).astype(bf16)

        def q():
            res["q"] = (piece(0, nq) * (dk ** -0.5)).astype(bf16)

        return res, [gates_and_o_hi, o_lo, k_and_stats, v_lo, v_hi, q]

    def finalize(res):
        res["ot"] = jnp.concatenate([res.pop("o_lo"), res.pop("o_hi")], axis=0)
        res["vt"] = jnp.concatenate([res.pop("v_lo"), res.pop("v_hi")], axis=0)
        return res

    def kq(p, c, j):
        tok = slice(c * L, (c + 1) * L)
        kp = p["k"][tok, j * LANES:(j + 1) * LANES]
        qtp = p["q"][j * LANES:(j + 1) * LANES, tok]
        q_even = jnp.concatenate([qtp[0:dk, :], zeros_half], axis=0)
        q_odd = jnp.concatenate([zeros_half, qtp[dk:2 * dk, :]], axis=0)
        return kp, (q_even, q_odd), _dot(kp, jnp.concatenate([q_even, q_odd], axis=1))

    def scan_unit(p, g, c, j, m_prev, m_new, ct, kq_res):
        kp, q_eo, st = kq_res
        tok = slice(c * L, (c + 1) * L)
        out_tok = slice(g * G + c * L, g * G + (c + 1) * L)
        b_r, cmax, tail, _, b_last, c_cols = p["stats"][c]
        a_r = jnp.maximum(m_prev, cmax)
        a_r2 = a_r * LOG2E
        carry_w = jnp.exp(m_prev - a_r).astype(bf16)
        clamp_r = jnp.exp(-(a_r + b_r))
        ws_r = jnp.exp(tail - m_new).astype(bf16)
        decay = jnp.exp(b_last + m_prev - m_new)
        ct_pair_b = ct[j].astype(bf16)
        vws = []
        for e, q_e in enumerate(q_eo):
            h = 2 * j + e
            expo = jnp.where(upper, c_cols[:, h:h + 1] - a_r2[h:h + 1, :], NEG_BIG)
            wt = (st[:, e * L:(e + 1) * L] * jnp.exp2(expo)).astype(bf16)
            q_w = q_e * carry_w[h:h + 1, :]
            vext = jnp.concatenate([p["vt"][h * dv:(h + 1) * dv, tok], ones_rows], axis=0)
            nd = _dot(jnp.concatenate([vext, ct_pair_b], axis=1), jnp.concatenate([wt, q_w], axis=0))
            den = jnp.maximum(jnp.abs(nd[dv:dv + 1, :]), clamp_r[h:h + 1, :])
            inv = 1.0 / den
            num = nd[0:dv, :]
            scale = inv * lax.rsqrt(inv * inv * jnp.mean(num * num, axis=0, keepdims=True) + EPS)
            hn = num * scale * hn_ref[h * dv:(h + 1) * dv, :]
            gate_o = p["ot"][h * dv:(h + 1) * dv, tok].astype(f32)
            out_ref[h * dv:(h + 1) * dv, out_tok] = (gate_o * hn).astype(bf16)
            vws.append(vext * ws_r[h:h + 1, :])
        k_split = jnp.concatenate([jnp.where(low, kp, jnp.zeros_like(kp)),
                                   jnp.where(low, jnp.zeros_like(kp), kp)], axis=0)
        decay_pair = jnp.where(low[0:1, :], decay[2 * j:2 * j + 1, :], decay[2 * j + 1:2 * j + 2, :])
        ct[j] = decay_pair * ct[j] + _dot(jnp.concatenate(vws, axis=1), k_split)

    def store_first_group(res):
        q0_ref[...], k0_ref[...], vt0_ref[...], ot0_ref[...] = res["q"], res["k"], res["vt"], res["ot"]
        for c, stat in enumerate(res["stats"]):
            stat0_ref[c * 5 * H:(c + 1) * 5 * H, :] = jnp.concatenate(stat[0:5], axis=0)
            ccol0_ref[c * L:(c + 1) * L, :] = stat[5]

    def load_first_group():
        stats = [tuple(stat0_ref[(c * 5 + s) * H:(c * 5 + s + 1) * H, :] for s in range(5))
                 + (ccol0_ref[c * L:(c + 1) * L, :],) for c in range(cpg)]
        return {"q": q0_ref[...], "k": k0_ref[...], "vt": vt0_ref[...], "ot": ot0_ref[...], "stats": stats}

    @pl.when(i == 0)
    def _():
        res, first_steps = projection_steps(xn_ref[0:G, :])
        for step in first_steps:
            step()
        store_first_group(finalize(res))

    n_groups = tm // G
    cur = load_first_group()

    ct = [ct_ref[j] for j in range(pairs)]
    m = m_ref[...]
    units = [(g, c, j) for g in range(n_groups) for c in range(cpg) for j in range(pairs)]
    per_group = cpg * pairs
    projected = {0: cur}
    kq_queue = {}

    def queue_kq(t):
        g, c, j = units[t]
        kq_queue[t] = kq(projected[g], c, j)

    for t in range(MIXER_LOOKAHEAD):
        queue_kq(t)
    for t, (g, c, j) in enumerate(units):
        u = t % per_group
        if u == 0:
            nxt, steps = projection_steps(xn[(g + 1) * G:(g + 2) * G, :] if g + 1 < n_groups else xnn_ref[0:G, :])
            stats = projected[g]["stats"]
            m_prevs = []
            for cc in range(cpg):
                m_prevs.append(m)
                m = jnp.maximum(stats[cc][4] + m, stats[cc][3])
            m_prevs.append(m)
        if t == 0:
            prepare_next()
        if u < len(steps):
            steps[u]()
            if u + 1 == len(steps):
                if g + 1 < n_groups:
                    projected[g + 1] = finalize(nxt)
                else:
                    store_first_group(finalize(nxt))
        if t + MIXER_LOOKAHEAD < len(units):
            queue_kq(t + MIXER_LOOKAHEAD)
        scan_unit(projected[g], g, c, j, m_prevs[c], m_prevs[c + 1], ct, kq_queue.pop(t))
    m_ref[...] = m
    for j in range(pairs):
        ct_ref[j] = ct[j]
    wup_bf_ref[...] = wup_ref[...].astype(bf16)
    wdown_bf_ref[...] = wdown_ref[...].astype(bf16)


def _mlstm_mixer(x, gain, wt, gate_bias, hn_rep, w_up, w_down, seq):
    t, d = x.shape
    nq = M_HEADS * M_QK_DIM
    nv = M_HEADS * M_V_DIM
    tm = PROJ_TILE
    steps = t // tm
    slab = lambda w: pl.BlockSpec((w.shape[0], w.shape[1] // steps, w.shape[2]), lambda i: (0, i, 0))
    return pl.pallas_call(
        functools.partial(_mlstm_kernel, tiles_per_seq=seq // tm),
        grid=(steps,),
        in_specs=_proj_in_specs(tm, d, steps) + [
            _resident((1, d)), _resident(wt.shape), _resident(gate_bias.shape), _resident(hn_rep.shape),
            slab(w_up), slab(w_down)],
        out_specs=[pl.BlockSpec((nv, tm), lambda i: (0, i)), slab(w_up), slab(w_down)],
        out_shape=[jax.ShapeDtypeStruct((nv, t), bf16),
                   jax.ShapeDtypeStruct(w_up.shape, bf16), jax.ShapeDtypeStruct(w_down.shape, bf16)],
        scratch_shapes=[pltpu.VMEM((tm, d), bf16), pltpu.VMEM((tm, d), bf16),
                        pltpu.VMEM((M_HEADS // 2, M_V_DIM + BF16_ROWS, 2 * M_QK_DIM), f32),
                        pltpu.VMEM((M_HEADS, LANES), f32),
                        pltpu.VMEM((nq, MLSTM_GROUP), bf16), pltpu.VMEM((MLSTM_GROUP, nq), bf16),
                        pltpu.VMEM((nv, MLSTM_GROUP), bf16), pltpu.VMEM((nv, MLSTM_GROUP), bf16),
                        pltpu.VMEM((MLSTM_GROUP // M_CHUNK * 5 * M_HEADS, M_CHUNK), f32),
                        pltpu.VMEM((MLSTM_GROUP, LANES), f32)],
        compiler_params=_params(1),
        name="mlstm_mixer",
    )(x, x, gain, wt, gate_bias, hn_rep, w_up, w_down)


def _attn_proj_kernel(x0_ref, xnext_ref, g_ref, wt_ref, b_ref, qt_ref, k_ref, vt_ref, xn_ref, xnn_ref):
    tm = xnext_ref.shape[0]
    nkv = A_KV_HEADS * A_HEAD_DIM
    xn, prepare_next = _next_input_norm(pl.program_id(0), x0_ref, xnext_ref, g_ref, xn_ref, xnn_ref)
    nq = wt_ref.shape[0] - 2 * nkv
    proj = lambda r0, r1: _dot_nt(wt_ref[r0:r1, :], xn) + _tile_lanes(b_ref[r0:r1, :], tm // LANES)
    kv = proj(nq, nq + 2 * nkv)
    prepare_next()
    k_ref[...] = kv[0:nkv, :].T.astype(bf16)
    vt_ref[...] = kv[nkv:2 * nkv, :].astype(bf16)
    for r0 in range(0, nq, Q_ROWS_PER_DOT):
        q = proj(r0, r0 + Q_ROWS_PER_DOT)
        qt_ref[r0:r0 + Q_ROWS_PER_DOT, :] = (q * (A_HEAD_DIM ** -0.5 * LOG2E)).astype(bf16)


def _attn_proj(x, gain, wt, b_rep):
    t, d = x.shape
    nq = A_Q_HEADS * A_HEAD_DIM
    nkv = A_KV_HEADS * A_HEAD_DIM
    tm = PROJ_TILE
    row = lambda n: pl.BlockSpec((tm, n), lambda i: (i, 0))
    col = lambda n: pl.BlockSpec((n, tm), lambda i: (0, i))
    return pl.pallas_call(
        _attn_proj_kernel,
        grid=(t // tm,),
        in_specs=_proj_in_specs(tm, d, t // tm) + [_resident((1, d)), _resident(wt.shape), _resident(b_rep.shape)],
        out_specs=[col(nq), row(nkv), col(nkv)],
        out_shape=[jax.ShapeDtypeStruct((nq, t), bf16), jax.ShapeDtypeStruct((t, nkv), bf16),
                   jax.ShapeDtypeStruct((nkv, t), bf16)],
        scratch_shapes=[pltpu.VMEM((tm, d), bf16), pltpu.VMEM((tm, d), bf16)],
        compiler_params=_params(1),
        name="attn_proj",
    )(x, x, gain, wt, b_rep)


def _swa_kernel(qt_ref, kc_ref, kp_ref, vtc_ref, vtp_ref, sink_ref, out_ref):
    blk = A_BLOCK
    dh = A_HEAD_DIM
    first = pl.program_id(1) == 0
    ku = lax.broadcasted_iota(jnp.int32, (2 * blk, blk), 0)
    qi = lax.broadcasted_iota(jnp.int32, (2 * blk, blk), 1)
    diff = qi - (ku - blk)
    band = (diff >= 0) & (diff < WINDOW)
    bias = jnp.where(band | (ku == 0), 0.0, NEG_BIG)
    bias_first = jnp.where((band & (ku >= blk)) | (ku == 0), 0.0, NEG_BIG)
    krow = lax.broadcasted_iota(jnp.int32, (2 * blk, LANES), 0)
    klane = lax.broadcasted_iota(jnp.int32, (2 * blk, LANES), 1)
    k_aug = jnp.where((krow == 0) & (klane < 3), 1.0, 0.0).astype(bf16)
    vcol = lax.broadcasted_iota(jnp.int32, (dh, 2 * blk), 1)
    ones_rows = jnp.ones((BF16_ROWS, 2 * blk), bf16)

    hp_lanes = 2 * blk
    shared = {}

    def operands(bq, g):
        if (bq, g) not in shared:
            cols = slice(bq * blk, (bq + 1) * blk)
            if bq == 0:
                kcat = jnp.concatenate([kp_ref[...], kc_ref[cols, :]], axis=0)
                vtcat = jnp.concatenate([vtp_ref[...], vtc_ref[:, cols]], axis=1)
                b1 = jnp.where(first, bias_first, bias)
            else:
                kcat = kc_ref[(bq - 1) * blk:(bq + 1) * blk, :]
                vtcat = vtc_ref[:, (bq - 1) * blk:(bq + 1) * blk]
                b1 = bias
            in_group = (klane >= g * dh) & (klane < (g + 1) * dh) & (krow > 0)
            km = jnp.concatenate([jnp.where(in_group, kcat, jnp.zeros_like(kcat)), k_aug], axis=1)
            vt_g = jnp.where(vcol == 0, jnp.zeros((dh, 2 * blk), bf16), vtcat[g * dh:(g + 1) * dh, :])
            shared[(bq, g)] = km, jnp.concatenate([vt_g, ones_rows], axis=0), _tile_lanes(b1, 2)
        return shared[(bq, g)]

    def logits(bq, g, hp):
        cols = slice(bq * blk, (bq + 1) * blk)
        km, _, b2 = operands(bq, g)
        blocks = []
        for e in range(2):
            h = g * A_GROUP + 2 * hp + e
            pair = qt_ref[(h // 2) * LANES:(h // 2 + 1) * LANES, cols]
            if h % 2 != g:
                pair = jnp.concatenate([pair[dh:2 * dh, :], pair[0:dh, :]], axis=0)
            blocks.append(pair)
        sink_rows = sink_ref[g][:, hp * hp_lanes:(hp + 1) * hp_lanes]
        rhs = jnp.concatenate([jnp.concatenate(blocks, axis=1), sink_rows], axis=0)
        return _dot(km, rhs) + b2

    def finish(bq, g, hp, st):
        cols = slice(bq * blk, (bq + 1) * blk)
        _, vext, _ = operands(bq, g)
        p = jnp.exp2(st - jnp.max(st, axis=0, keepdims=True)).astype(bf16)
        oext = _dot(vext, p)
        o = (oext[0:dh, :] * (1.0 / oext[dh:dh + 1, :])).astype(bf16)
        for e in range(2):
            h = g * A_GROUP + 2 * hp + e
            out_ref[h * dh:(h + 1) * dh, cols] = o[:, e * blk:(e + 1) * blk]

    units = [(bq, g, hp) for bq in range(SWA_TILE // blk) for g in range(A_KV_HEADS) for hp in range(A_GROUP // 2)]
    queued = [logits(*unit) for unit in units[:SWA_LOOKAHEAD]]
    for u, unit in enumerate(units):
        if u + SWA_LOOKAHEAD < len(units):
            queued.append(logits(*units[u + SWA_LOOKAHEAD]))
        finish(*unit, queued[u])


def _swa(qt, k, vt, sink_aug, batch, seq):
    blk = A_BLOCK
    tq = SWA_TILE
    nq = A_Q_HEADS * A_HEAD_DIM
    nkv = A_KV_HEADS * A_HEAD_DIM
    nt = seq // tq
    per = tq // blk
    prev_idx = lambda b, i: b * nt * per + jnp.maximum(i * per - 1, 0)
    return pl.pallas_call(
        _swa_kernel,
        grid=(batch, nt),
        in_specs=[pl.BlockSpec((nq, tq), lambda b, i: (0, b * nt + i)),
                  pl.BlockSpec((tq, nkv), lambda b, i: (b * nt + i, 0)),
                  pl.BlockSpec((blk, nkv), lambda b, i: (prev_idx(b, i), 0)),
                  pl.BlockSpec((nkv, tq), lambda b, i: (0, b * nt + i)),
                  pl.BlockSpec((nkv, blk), lambda b, i: (0, prev_idx(b, i))),
                  _resident(sink_aug.shape)],
        out_specs=pl.BlockSpec((nq, tq), lambda b, i: (0, b * nt + i)),
        out_shape=jax.ShapeDtypeStruct((nq, batch * seq), bf16),
        compiler_params=_params(2),
        name="swa",
    )(qt, k, k, vt, vt, sink_aug)


def _sink_rows(sinks):
    parts = jnp.stack(_split3(sinks.astype(f32) * LOG2E), axis=0)
    per_lane = jnp.repeat(parts.reshape(3, A_KV_HEADS, A_GROUP), A_BLOCK, axis=2)
    return jnp.pad(per_lane.transpose(1, 0, 2), ((0, 0), (0, LANES - 3), (0, 0)))


def _out_ffn_kernel(h0_ref, at0_ref, hn_ref, atn_ref, wo_ref, wup_ref, wd_ref, vec_ref, conv_ref,
                    out_ref, act_ref, h1_ref, xn_ref, h1n_ref, xnn_ref, carry_ref, *, tiles_per_seq):
    tm = hn_ref.shape[0]
    dff = wd_ref.shape[0]
    i = pl.program_id(0)

    d = wo_ref.shape[1]
    out_cols = [slice(n0, n0 + OUT_CHUNK) for n0 in range(0, d, OUT_CHUNK)]

    def out_proj(at_ref, cols):
        return _dot_tn(at_ref[...], wo_ref[:, cols]) + vec_ref[0:1, cols]

    def front(h_ref, z_parts):
        h1 = h_ref[...] + _rmsnorm(jnp.concatenate(z_parts, axis=1), vec_ref[1:2, :])
        h1n_ref[...] = h1
        xnn_ref[...] = _rmsnorm(h1, vec_ref[2:3, :]).astype(bf16)

    @pl.when(i == 0)
    def _():
        front(h0_ref, [out_proj(at0_ref, cols) for cols in out_cols])

    @pl.when(i % tiles_per_seq == 0)
    def _():
        carry_ref[...] = jnp.zeros_like(carry_ref)

    xn_ref[...] = xnn_ref[...]
    h1_ref[...] = h1n_ref[...]
    top = lax.broadcasted_iota(jnp.int32, (8, FF_CHUNK), 0)
    z_parts = []
    for ci, c0 in enumerate(range(0, dff, FF_CHUNK)):
        if FRONT_AFTER_CHUNKS <= ci < FRONT_AFTER_CHUNKS + len(out_cols):
            z_parts.append(out_proj(atn_ref, out_cols[ci - FRONT_AFTER_CHUNKS]))
        if ci == FRONT_AFTER_CHUNKS + len(out_cols):
            front(hn_ref, z_parts)
        cs = slice(c0, c0 + FF_CHUNK)
        gate = _dot(xn_ref[...], wup_ref[:, cs])
        val = _dot(xn_ref[...], wup_ref[:, dff + c0:dff + c0 + FF_CHUNK])
        prev = carry_ref[0:8, cs]
        carry_ref[0:8, cs] = gate[tm - 8:tm, :]
        shifted = []
        for lag in (1, 2):
            rolled = pltpu.roll(gate, lag, axis=0)
            head = jnp.where(top < lag, pltpu.roll(prev, lag, axis=0), rolled[0:8, :])
            shifted.append(jnp.concatenate([head, rolled[8:, :]], axis=0))
        g1, g2 = shifted
        gc = conv_ref[3:4, cs] + conv_ref[0:1, cs] * g2 + conv_ref[1:2, cs] * g1 + conv_ref[2:3, cs] * gate
        act_ref[:, cs] = (gc * jax.nn.sigmoid(gc) * val).astype(bf16)
    y = _dot(act_ref[...], wd_ref[...])
    out_ref[...] = h1_ref[...] + _rmsnorm(y, vec_ref[3:4, :])


def _out_ffn(h, at, wo, wup, wd, vecs, conv, layer, seq):
    t, d = h.shape
    tm = TOKEN_TILE
    n = t // tm
    dff = wd.shape[1]
    nxt = lambda i: jnp.minimum(i + 1, n - 1)
    return pl.pallas_call(
        functools.partial(_out_ffn_kernel, tiles_per_seq=seq // tm),
        grid=(n,),
        in_specs=[pl.BlockSpec((tm, d), lambda i: (0, 0), pipeline_mode=pl.Buffered(1)),
                  pl.BlockSpec((at.shape[0], tm), lambda i: (0, 0), pipeline_mode=pl.Buffered(1)),
                  pl.BlockSpec((tm, d), lambda i: (nxt(i), 0)),
                  pl.BlockSpec((at.shape[0], tm), lambda i: (0, nxt(i))),
                  _resident(wo.shape), _layer_resident(wup.shape, layer), _layer_resident(wd.shape, layer),
                  _resident(vecs.shape), _resident(conv.shape)],
        out_specs=pl.BlockSpec((tm, d), lambda i: (i, 0)),
        out_shape=jax.ShapeDtypeStruct((t, d), f32),
        scratch_shapes=[pltpu.VMEM((tm, dff), bf16),
                        pltpu.VMEM((tm, d), f32), pltpu.VMEM((tm, d), bf16),
                        pltpu.VMEM((tm, d), f32), pltpu.VMEM((tm, d), bf16), pltpu.VMEM((16, dff), f32)],
        compiler_params=_params(1),
        name="out_ffn",
    )(h, at, h, at, wo, wup, wd, vecs, conv)


def kernel(x, m_w_in, m_gate_bias, m_head_norm, m_w_out, a_w_in, a_b_in, a_sinks, a_w_out, a_b_out,
           norm_mix_pre, norm_mix_post, norm_ffn_pre, norm_ffn_post, f_w_up, f_conv_w, f_conv_b, f_w_down):
    batch, seq, d = x.shape
    depth = norm_mix_pre.shape[0]
    h = x.reshape(batch * seq, d)
    row = lambda vec: vec.reshape(1, -1).astype(f32)
    lane_rep = lambda vec: jnp.broadcast_to(vec.astype(f32)[:, None], (vec.shape[0], LANES))

    for i in range(depth):
        j = i // 2
        if i % 2 == 0:
            mixed_t, w_up, w_down = _mlstm_mixer(
                h, row(norm_mix_pre[i]), m_w_in[j].T.astype(bf16), lane_rep(m_gate_bias[j].reshape(-1)),
                lane_rep(m_head_norm[j]), f_w_up, f_w_down, seq)
            w_out = m_w_out[j].astype(bf16)
            b_out = jnp.zeros((d,), f32)
        else:
            qt, k, vt = _attn_proj(h, row(norm_mix_pre[i]), a_w_in[j].T.astype(bf16), lane_rep(a_b_in[j]))
            mixed_t = _swa(qt, k, vt, _sink_rows(a_sinks[j]), batch, seq)
            w_out = a_w_out[j].astype(bf16)
            b_out = a_b_out[j]
        vecs = jnp.pad(jnp.stack([b_out, norm_mix_post[i], norm_ffn_pre[i], norm_ffn_post[i]]).astype(f32),
                       ((0, 4), (0, 0)))
        conv = jnp.pad(jnp.concatenate([f_conv_w[i], f_conv_b[i][None]], axis=0).astype(f32), ((0, 12), (0, 0)))
        h = _out_ffn(h, mixed_t, w_out, w_up, w_down, vecs, conv, i, seq)
    return h.reshape(batch, seq, d)
```

```python
import functools

import jax
import jax.numpy as jnp
from jax import lax
from jax.experimental import pallas as pl
from jax.experimental.pallas import tpu as pltpu

EPS = 1e-6
LANES = 128
BF16_ROWS = 16

M_HEADS = 8
M_QK_DIM = 64
M_V_DIM = 128
GATE_CAP = 15.0
M_CHUNK = 128
MLSTM_GROUP = 256
MIXER_LOOKAHEAD = 2

A_HEAD_DIM = 64
A_Q_HEADS = 16
A_KV_HEADS = 2
A_GROUP = A_Q_HEADS // A_KV_HEADS
WINDOW = 128
A_BLOCK = 128

LOG2E = 1.4426950408889634
NEG_BIG = -1e30

TOKEN_TILE = 512
PROJ_TILE = 1024
SWA_TILE = 512
ATTN_GROUP = 256
SWA_LOOKAHEAD = 4
FF_CHUNK = 256
Q_ROWS_PER_DOT = 256
OUT_CHUNK = 256
FRONT_AFTER_CHUNKS = 2
VMEM_LIMIT = 60 * 1024 * 1024

bf16 = jnp.bfloat16
f32 = jnp.float32


def _dot(a, b):
    return jnp.dot(a, b, preferred_element_type=f32)


def _dot_nt(a, b):
    return lax.dot_general(a, b, (((1,), (1,)), ((), ())), preferred_element_type=f32)


def _dot_tn(a, b):
    return lax.dot_general(a, b, (((0,), (0,)), ((), ())), preferred_element_type=f32)


def _rmsnorm(x, g):
    return x * lax.rsqrt(jnp.mean(x * x, axis=-1, keepdims=True) + EPS) * g


def _split3(x):
    hi = x.astype(bf16)
    r1 = x - hi.astype(f32)
    mid = r1.astype(bf16)
    lo = (r1 - mid.astype(f32)).astype(bf16)
    return hi, mid, lo


def _log_sigmoid(x):
    return jnp.minimum(x, 0.0) - jnp.log1p(jnp.exp(-jnp.abs(x)))


def _tile_lanes(x, reps):
    return jnp.concatenate([x] * reps, axis=1)


def _next_input_norm(i, x0_ref, xnext_ref, g_ref, xn_ref, xnn_ref):
    @pl.when(i == 0)
    def _():
        xnn_ref[...] = _rmsnorm(x0_ref[...], g_ref[...]).astype(bf16)

    xn_ref[...] = xnn_ref[...]

    def prepare_next():
        xnn_ref[...] = _rmsnorm(xnext_ref[...], g_ref[...]).astype(bf16)

    return xn_ref[...], prepare_next


def _sigmoid(x):
    return 0.5 * jnp.tanh(0.5 * x) + 0.5


def _proj_in_specs(tm, d, n_tiles):
    return [pl.BlockSpec((tm, d), lambda i: (0, 0), pipeline_mode=pl.Buffered(1)),
            pl.BlockSpec((tm, d), lambda i: (jnp.minimum(i + 1, n_tiles - 1), 0))]


def _resident(shape):
    nd = len(shape)
    return pl.BlockSpec(shape, lambda *_: (0,) * nd, pipeline_mode=pl.Buffered(1))


def _layer_resident(shape, layer):
    nd = len(shape) - 1
    return pl.BlockSpec((None,) + tuple(shape[1:]), lambda *_: (layer,) + (0,) * nd, pipeline_mode=pl.Buffered(1))


def _params(n_axes):
    return pltpu.CompilerParams(
        dimension_semantics=("arbitrary",) * n_axes, vmem_limit_bytes=VMEM_LIMIT)


def _mlstm_kernel(x0_ref, xnext_ref, g_ref, wt_ref, gb_ref, hn_ref, wup_ref, wdown_ref,
                  out_ref, wup_bf_ref, wdown_bf_ref, xn_ref, xnn_ref, ct_ref, m_ref,
                  q0_ref, k0_ref, vt0_ref, ot0_ref, stat0_ref, ccol0_ref, *, tiles_per_seq):
    tm = xnext_ref.shape[0]
    L = M_CHUNK
    H = M_HEADS
    dk, dv = M_QK_DIM, M_V_DIM
    nq = H * dk
    nv = H * dv
    pairs = H // 2
    G = MLSTM_GROUP
    cpg = G // L
    i = pl.program_id(0)

    @pl.when(i % tiles_per_seq == 0)
    def _():
        ct_ref[...] = jnp.zeros_like(ct_ref)
        m_ref[...] = jnp.zeros_like(m_ref)

    xn, prepare_next = _next_input_norm(i, x0_ref, xnext_ref, g_ref, xn_ref, xnn_ref)

    upper = lax.broadcasted_iota(jnp.int32, (L, L), 0) <= lax.broadcasted_iota(jnp.int32, (L, L), 1)
    triu = jnp.where(upper, 1.0, 0.0).astype(bf16)
    lane_h = lax.broadcasted_iota(jnp.int32, (H, L), 1)
    lane = lax.broadcasted_iota(jnp.int32, (L, LANES), 1)
    low = lane < dk
    ones_rows = jnp.ones((BF16_ROWS, L), bf16)
    zeros_half = jnp.zeros((dk, L), bf16)

    def projection_steps(xg):
        res = {}
        piece = lambda r0, r1: _dot_nt(wt_ref[r0:r1, :], xg)
        o0 = 2 * nq + nv

        def gates_and_o_hi():
            p = piece(o0 + nv // 2, o0 + nv + 2 * H)
            res["o_hi"] = _sigmoid(p[0:nv // 2, :]).astype(bf16)
            raw = p[nv // 2:, :] + _tile_lanes(gb_ref[...], G // LANES)
            capped = GATE_CAP * jnp.tanh(raw * (1.0 / GATE_CAP))
            res["grow"] = jnp.where(lax.broadcasted_iota(jnp.int32, raw.shape, 0) < H, capped, _log_sigmoid(capped))

        def o_lo():
            res["o_lo"] = _sigmoid(piece(o0, o0 + nv // 2)).astype(bf16)

        def k_and_stats():
            res["k"] = piece(nq, 2 * nq).T.astype(bf16)
            grow = res["grow"]
            stacked = jnp.concatenate([grow[:, c * L:(c + 1) * L] for c in range(cpg)], axis=0)
            sums = _dot(jnp.concatenate(_split3(stacked), axis=0), triu)
            n = cpg * 2 * H
            cum = sums[0:n] + sums[n:2 * n] + sums[2 * n:]
            stats = []
            for c in range(cpg):
                b_r = cum[c * 2 * H + H:(c + 1) * 2 * H, :]
                c_r = grow[0:H, c * L:(c + 1) * L] - b_r
                cmax = c_r
                shift = 1
                while shift < L:
                    cmax = jnp.where(lane_h >= shift, jnp.maximum(cmax, pltpu.roll(cmax, shift, axis=1)), cmax)
                    shift *= 2
                b_last = jnp.broadcast_to(b_r[:, L - 1:L], (H, L))
                tail = c_r + b_last
                tail_max = jnp.broadcast_to(jnp.max(tail, axis=-1, keepdims=True), (H, L))
                c_cols = jnp.concatenate([c_r * LOG2E, jnp.zeros((LANES - H, L), f32)], axis=0).T
                stats.append((b_r, cmax, tail, tail_max, b_last, c_cols))
            res["stats"] = stats

        def v_lo():
            res["v_lo"] = piece(2 * nq, 2 * nq + nv // 2).astype(bf16)

        def v_hi():
            res["v_hi"] = piece(2 * nq + nv // 2, 2 * nq + nv).astype(bf16)

        def q():
            res["q"] = (piece(0, nq) * (dk ** -0.5)).astype(bf16)

        return res, [gates_and_o_hi, o_lo, k_and_stats, v_lo, v_hi, q]

    def finalize(res):
        res["ot"] = jnp.concatenate([res.pop("o_lo"), res.pop("o_hi")], axis=0)
        res["vt"] = jnp.concatenate([res.pop("v_lo"), res.pop("v_hi")], axis=0)
        return res

    def kq(p, c, j):
        tok = slice(c * L, (c + 1) * L)
        kp = p["k"][tok, j * LANES:(j + 1) * LANES]
        qtp = p["q"][j * LANES:(j + 1) * LANES, tok]
        q_even = jnp.concatenate([qtp[0:dk, :], zeros_half], axis=0)
        q_odd = jnp.concatenate([zeros_half, qtp[dk:2 * dk, :]], axis=0)
        return kp, (q_even, q_odd), _dot(kp, jnp.concatenate([q_even, q_odd], axis=1))

    def scan_unit(p, g, c, j, m_prev, m_new, ct, kq_res):
        kp, q_eo, st = kq_res
        tok = slice(c * L, (c + 1) * L)
        out_tok = slice(g * G + c * L, g * G + (c + 1) * L)
        b_r, cmax, tail, _, b_last, c_cols = p["stats"][c]
        a_r = jnp.maximum(m_prev, cmax)
        a_r2 = a_r * LOG2E
        carry_w = jnp.exp(m_prev - a_r).astype(bf16)
        clamp_r = jnp.exp(-(a_r + b_r))
        ws_r = jnp.exp(tail - m_new).astype(bf16)
        decay = jnp.exp(b_last + m_prev - m_new)
        ct_pair_b = ct[j].astype(bf16)
        vws = []
        for e, q_e in enumerate(q_eo):
            h = 2 * j + e
            expo = jnp.where(upper, c_cols[:, h:h + 1] - a_r2[h:h + 1, :], NEG_BIG)
            wt = (st[:, e * L:(e + 1) * L] * jnp.exp2(expo)).astype(bf16)
            q_w = q_e * carry_w[h:h + 1, :]
            vext = jnp.concatenate([p["vt"][h * dv:(h + 1) * dv, tok], ones_rows], axis=0)
            nd = _dot(jnp.concatenate([vext, ct_pair_b], axis=1), jnp.concatenate([wt, q_w], axis=0))
            den = jnp.maximum(jnp.abs(nd[dv:dv + 1, :]), clamp_r[h:h + 1, :])
            inv = 1.0 / den
            num = nd[0:dv, :]
            scale = inv * lax.rsqrt(inv * inv * jnp.mean(num * num, axis=0, keepdims=True) + EPS)
            hn = num * scale * hn_ref[h * dv:(h + 1) * dv, :]
            gate_o = p["ot"][h * dv:(h + 1) * dv, tok].astype(f32)
            out_ref[h * dv:(h + 1) * dv, out_tok] = (gate_o * hn).astype(bf16)
            vws.append(vext * ws_r[h:h + 1, :])
        k_split = jnp.concatenate([jnp.where(low, kp, jnp.zeros_like(kp)),
                                   jnp.where(low, jnp.zeros_like(kp), kp)], axis=0)
        decay_pair = jnp.where(low[0:1, :], decay[2 * j:2 * j + 1, :], decay[2 * j + 1:2 * j + 2, :])
        ct[j] = decay_pair * ct[j] + _dot(jnp.concatenate(vws, axis=1), k_split)

    def store_first_group(res):
        q0_ref[...], k0_ref[...], vt0_ref[...], ot0_ref[...] = res["q"], res["k"], res["vt"], res["ot"]
        for c, stat in enumerate(res["stats"]):
            stat0_ref[c * 5 * H:(c + 1) * 5 * H, :] = jnp.concatenate(stat[0:5], axis=0)
            ccol0_ref[c * L:(c + 1) * L, :] = stat[5]

    def load_first_group():
        stats = [tuple(stat0_ref[(c * 5 + s) * H:(c * 5 + s + 1) * H, :] for s in range(5))
                 + (ccol0_ref[c * L:(c + 1) * L, :],) for c in range(cpg)]
        return {"q": q0_ref[...], "k": k0_ref[...], "vt": vt0_ref[...], "ot": ot0_ref[...], "stats": stats}

    @pl.when(i == 0)
    def _():
        res, first_steps = projection_steps(xn_ref[0:G, :])
        for step in first_steps:
            step()
        store_first_group(finalize(res))

    n_groups = tm // G
    cur = load_first_group()

    ct = [ct_ref[j] for j in range(pairs)]
    m = m_ref[...]
    units = [(g, c, j) for g in range(n_groups) for c in range(cpg) for j in range(pairs)]
    per_group = cpg * pairs
    projected = {0: cur}
    kq_queue = {}

    def queue_kq(t):
        g, c, j = units[t]
        kq_queue[t] = kq(projected[g], c, j)

    for t in range(MIXER_LOOKAHEAD):
        queue_kq(t)
    for t, (g, c, j) in enumerate(units):
        u = t % per_group
        if u == 0:
            nxt, steps = projection_steps(xn[(g + 1) * G:(g + 2) * G, :] if g + 1 < n_groups else xnn_ref[0:G, :])
            stats = projected[g]["stats"]
            m_prevs = []
            for cc in range(cpg):
                m_prevs.append(m)
                m = jnp.maximum(stats[cc][4] + m, stats[cc][3])
            m_prevs.append(m)
        if t == 0:
            prepare_next()
        if u < len(steps):
            steps[u]()
            if u + 1 == len(steps):
                if g + 1 < n_groups:
                    projected[g + 1] = finalize(nxt)
                else:
                    store_first_group(finalize(nxt))
        if t + MIXER_LOOKAHEAD < len(units):
            queue_kq(t + MIXER_LOOKAHEAD)
        scan_unit(projected[g], g, c, j, m_prevs[c], m_prevs[c + 1], ct, kq_queue.pop(t))
    m_ref[...] = m
    for j in range(pairs):
        ct_ref[j] = ct[j]
    wup_bf_ref[...] = wup_ref[...].astype(bf16)
    wdown_bf_ref[...] = wdown_ref[...].astype(bf16)


def _mlstm_mixer(x, gain, wt, gate_bias, hn_rep, w_up, w_down, seq):
    t, d = x.shape
    nq = M_HEADS * M_QK_DIM
    nv = M_HEADS * M_V_DIM
    tm = PROJ_TILE
    steps = t // tm
    slab = lambda w: pl.BlockSpec((w.shape[0], w.shape[1] // steps, w.shape[2]), lambda i: (0, i, 0))
    return pl.pallas_call(
        functools.partial(_mlstm_kernel, tiles_per_seq=seq // tm),
        grid=(steps,),
        in_specs=_proj_in_specs(tm, d, steps) + [
            _resident((1, d)), _resident(wt.shape), _resident(gate_bias.shape), _resident(hn_rep.shape),
            slab(w_up), slab(w_down)],
        out_specs=[pl.BlockSpec((nv, tm), lambda i: (0, i)), slab(w_up), slab(w_down)],
        out_shape=[jax.ShapeDtypeStruct((nv, t), bf16),
                   jax.ShapeDtypeStruct(w_up.shape, bf16), jax.ShapeDtypeStruct(w_down.shape, bf16)],
        scratch_shapes=[pltpu.VMEM((tm, d), bf16), pltpu.VMEM((tm, d), bf16),
                        pltpu.VMEM((M_HEADS // 2, M_V_DIM + BF16_ROWS, 2 * M_QK_DIM), f32),
                        pltpu.VMEM((M_HEADS, LANES), f32),
                        pltpu.VMEM((nq, MLSTM_GROUP), bf16), pltpu.VMEM((MLSTM_GROUP, nq), bf16),
                        pltpu.VMEM((nv, MLSTM_GROUP), bf16), pltpu.VMEM((nv, MLSTM_GROUP), bf16),
                        pltpu.VMEM((MLSTM_GROUP // M_CHUNK * 5 * M_HEADS, M_CHUNK), f32),
                        pltpu.VMEM((MLSTM_GROUP, LANES), f32)],
        compiler_params=_params(1),
        name="mlstm_mixer",
    )(x, x, gain, wt, gate_bias, hn_rep, w_up, w_down)


def _attn_kernel(x0_ref, xnext_ref, g_ref, wt_ref, b_ref, sink_ref, out_ref,
                 xn_ref, xnn_ref, kprev_ref, vtprev_ref, *, tiles_per_seq):
    tm = xnext_ref.shape[0]
    blk = A_BLOCK
    dh = A_HEAD_DIM
    nq = A_Q_HEADS * dh
    nkv = A_KV_HEADS * dh
    G = ATTN_GROUP
    bpg = G // blk
    i = pl.program_id(0)
    first = i % tiles_per_seq == 0

    @pl.when(i == 0)
    def _():
        kprev_ref[...] = jnp.zeros_like(kprev_ref)
        vtprev_ref[...] = jnp.zeros_like(vtprev_ref)

    xn, prepare_next = _next_input_norm(i, x0_ref, xnext_ref, g_ref, xn_ref, xnn_ref)

    ku = lax.broadcasted_iota(jnp.int32, (2 * blk, blk), 0)
    qi = lax.broadcasted_iota(jnp.int32, (2 * blk, blk), 1)
    diff = qi - (ku - blk)
    band = (diff >= 0) & (diff < WINDOW)
    bias = jnp.where(band | (ku == 0), 0.0, NEG_BIG)
    bias_first = jnp.where((band & (ku >= blk)) | (ku == 0), 0.0, NEG_BIG)
    krow = lax.broadcasted_iota(jnp.int32, (2 * blk, LANES), 0)
    klane = lax.broadcasted_iota(jnp.int32, (2 * blk, LANES), 1)
    k_aug = jnp.where((krow == 0) & (klane < 3), 1.0, 0.0).astype(bf16)
    vcol = lax.broadcasted_iota(jnp.int32, (dh, 2 * blk), 1)
    ones_rows = jnp.ones((BF16_ROWS, 2 * blk), bf16)
    hp_lanes = 2 * blk

    def projection_steps(xg):
        res = {"q": {}}
        piece = lambda r0, r1: _dot_nt(wt_ref[r0:r1, :], xg) + _tile_lanes(b_ref[r0:r1, :], G // LANES)

        def kv():
            p = piece(nq, nq + 2 * nkv)
            res["k"] = p[0:nkv, :].T.astype(bf16)
            res["vt"] = p[nkv:2 * nkv, :].astype(bf16)

        def q_rows(r0):
            def step():
                res["q"][r0] = (piece(r0, r0 + Q_ROWS_PER_DOT) * (dh ** -0.5 * LOG2E)).astype(bf16)
            return step

        return res, [kv] + [q_rows(r0) for r0 in range(0, nq, Q_ROWS_PER_DOT)]

    def finalize(res):
        res["qt"] = jnp.concatenate([res["q"][r0] for r0 in sorted(res["q"])], axis=0)
        return res

    projected = {}
    shared = {}

    def operands(g, bl, grp):
        if (g, bl, grp) not in shared:
            cols = slice(bl * blk, (bl + 1) * blk)
            cur = projected[g]
            if bl > 0:
                k_prev, vt_prev = cur["k"][(bl - 1) * blk:bl * blk, :], cur["vt"][:, (bl - 1) * blk:bl * blk]
            elif g > 0:
                k_prev, vt_prev = projected[g - 1]["k"][G - blk:G, :], projected[g - 1]["vt"][:, G - blk:G]
            else:
                k_prev, vt_prev = kprev_ref[...], vtprev_ref[...]
            b1 = jnp.where(first, bias_first, bias) if (g == 0 and bl == 0) else bias
            kcat = jnp.concatenate([k_prev, cur["k"][cols, :]], axis=0)
            vtcat = jnp.concatenate([vt_prev, cur["vt"][:, cols]], axis=1)
            in_group = (klane >= grp * dh) & (klane < (grp + 1) * dh) & (krow > 0)
            km = jnp.concatenate([jnp.where(in_group, kcat, jnp.zeros_like(kcat)), k_aug], axis=1)
            vt_g = jnp.where(vcol == 0, jnp.zeros((dh, 2 * blk), bf16), vtcat[grp * dh:(grp + 1) * dh, :])
            shared[(g, bl, grp)] = km, jnp.concatenate([vt_g, ones_rows], axis=0), _tile_lanes(b1, 2)
        return shared[(g, bl, grp)]

    def logits(g, bl, grp, hp):
        cols = slice(bl * blk, (bl + 1) * blk)
        km, _, b2 = operands(g, bl, grp)
        blocks = []
        for e in range(2):
            h = grp * A_GROUP + 2 * hp + e
            pair = projected[g]["qt"][(h // 2) * LANES:(h // 2 + 1) * LANES, cols]
            if h % 2 != grp:
                pair = jnp.concatenate([pair[dh:2 * dh, :], pair[0:dh, :]], axis=0)
            blocks.append(pair)
        sink_rows = sink_ref[grp][:, hp * hp_lanes:(hp + 1) * hp_lanes]
        rhs = jnp.concatenate([jnp.concatenate(blocks, axis=1), sink_rows], axis=0)
        return _dot(km, rhs) + b2

    def finish(g, bl, grp, hp, st):
        out_cols = slice(g * G + bl * blk, g * G + (bl + 1) * blk)
        _, vext, _ = operands(g, bl, grp)
        p = jnp.exp2(st - jnp.max(st, axis=0, keepdims=True)).astype(bf16)
        oext = _dot(vext, p)
        o = (oext[0:dh, :] * (1.0 / oext[dh:dh + 1, :])).astype(bf16)
        for e in range(2):
            h = grp * A_GROUP + 2 * hp + e
            out_ref[h * dh:(h + 1) * dh, out_cols] = o[:, e * blk:(e + 1) * blk]

    n_groups = tm // G
    res, steps = projection_steps(xn[0:G, :])
    for k_step, step in enumerate(steps):
        step()
        if k_step == 0:
            prepare_next()
    projected[0] = finalize(res)

    units = [(g, bl, grp, hp) for g in range(n_groups) for bl in range(bpg)
             for grp in range(A_KV_HEADS) for hp in range(A_GROUP // 2)]
    per_group = bpg * A_KV_HEADS * (A_GROUP // 2)
    queued = {}
    for t in range(SWA_LOOKAHEAD):
        queued[t] = logits(*units[t])
    for t, unit in enumerate(units):
        g = unit[0]
        u = t % per_group
        if u == 0:
            nxt, steps = projection_steps(xn[(g + 1) * G:(g + 2) * G, :]) if g + 1 < n_groups else (None, [])
        if u < len(steps):
            steps[u]()
            if u + 1 == len(steps):
                projected[g + 1] = finalize(nxt)
        if t + SWA_LOOKAHEAD < len(units):
            queued[t + SWA_LOOKAHEAD] = logits(*units[t + SWA_LOOKAHEAD])
        finish(*unit, queued.pop(t))
    last = projected[n_groups - 1]
    kprev_ref[...] = last["k"][G - blk:G, :]
    vtprev_ref[...] = last["vt"][:, G - blk:G]


def _attn_mixer(x, gain, wt, b_rep, sink_aug, seq):
    t, d = x.shape
    nq = A_Q_HEADS * A_HEAD_DIM
    nkv = A_KV_HEADS * A_HEAD_DIM
    tm = PROJ_TILE
    steps = t // tm
    return pl.pallas_call(
        functools.partial(_attn_kernel, tiles_per_seq=seq // tm),
        grid=(steps,),
        in_specs=_proj_in_specs(tm, d, steps) + [
            _resident((1, d)), _resident(wt.shape), _resident(b_rep.shape), _resident(sink_aug.shape)],
        out_specs=pl.BlockSpec((nq, tm), lambda i: (0, i)),
        out_shape=jax.ShapeDtypeStruct((nq, t), bf16),
        scratch_shapes=[pltpu.VMEM((tm, d), bf16), pltpu.VMEM((tm, d), bf16),
                        pltpu.VMEM((A_BLOCK, nkv), bf16), pltpu.VMEM((nkv, A_BLOCK), bf16)],
        compiler_params=_params(1),
        name="attn_mixer",
    )(x, x, gain, wt, b_rep, sink_aug)


def _attn_proj_kernel(x0_ref, xnext_ref, g_ref, wt_ref, b_ref, qt_ref, k_ref, vt_ref, xn_ref, xnn_ref):
    tm = xnext_ref.shape[0]
    nkv = A_KV_HEADS * A_HEAD_DIM
    xn, prepare_next = _next_input_norm(pl.program_id(0), x0_ref, xnext_ref, g_ref, xn_ref, xnn_ref)
    nq = wt_ref.shape[0] - 2 * nkv
    proj = lambda r0, r1: _dot_nt(wt_ref[r0:r1, :], xn) + _tile_lanes(b_ref[r0:r1, :], tm // LANES)
    kv = proj(nq, nq + 2 * nkv)
    prepare_next()
    k_ref[...] = kv[0:nkv, :].T.astype(bf16)
    vt_ref[...] = kv[nkv:2 * nkv, :].astype(bf16)
    for r0 in range(0, nq, Q_ROWS_PER_DOT):
        q = proj(r0, r0 + Q_ROWS_PER_DOT)
        qt_ref[r0:r0 + Q_ROWS_PER_DOT, :] = (q * (A_HEAD_DIM ** -0.5 * LOG2E)).astype(bf16)


def _attn_proj(x, gain, wt, b_rep):
    t, d = x.shape
    nq = A_Q_HEADS * A_HEAD_DIM
    nkv = A_KV_HEADS * A_HEAD_DIM
    tm = PROJ_TILE
    row = lambda n: pl.BlockSpec((tm, n), lambda i: (i, 0))
    col = lambda n: pl.BlockSpec((n, tm), lambda i: (0, i))
    return pl.pallas_call(
        _attn_proj_kernel,
        grid=(t // tm,),
        in_specs=_proj_in_specs(tm, d, t // tm) + [_resident((1, d)), _resident(wt.shape), _resident(b_rep.shape)],
        out_specs=[col(nq), row(nkv), col(nkv)],
        out_shape=[jax.ShapeDtypeStruct((nq, t), bf16), jax.ShapeDtypeStruct((t, nkv), bf16),
                   jax.ShapeDtypeStruct((nkv, t), bf16)],
        scratch_shapes=[pltpu.VMEM((tm, d), bf16), pltpu.VMEM((tm, d), bf16)],
        compiler_params=_params(1),
        name="attn_proj",
    )(x, x, gain, wt, b_rep)


def _swa_kernel(qt_ref, kc_ref, kp_ref, vtc_ref, vtp_ref, sink_ref, out_ref):
    blk = A_BLOCK
    dh = A_HEAD_DIM
    first = pl.program_id(1) == 0
    ku = lax.broadcasted_iota(jnp.int32, (2 * blk, blk), 0)
    qi = lax.broadcasted_iota(jnp.int32, (2 * blk, blk), 1)
    diff = qi - (ku - blk)
    band = (diff >= 0) & (diff < WINDOW)
    bias = jnp.where(band | (ku == 0), 0.0, NEG_BIG)
    bias_first = jnp.where((band & (ku >= blk)) | (ku == 0), 0.0, NEG_BIG)
    krow = lax.broadcasted_iota(jnp.int32, (2 * blk, LANES), 0)
    klane = lax.broadcasted_iota(jnp.int32, (2 * blk, LANES), 1)
    k_aug = jnp.where((krow == 0) & (klane < 3), 1.0, 0.0).astype(bf16)
    vcol = lax.broadcasted_iota(jnp.int32, (dh, 2 * blk), 1)
    ones_rows = jnp.ones((BF16_ROWS, 2 * blk), bf16)

    hp_lanes = 2 * blk
    shared = {}

    def operands(bq, g):
        if (bq, g) not in shared:
            cols = slice(bq * blk, (bq + 1) * blk)
            if bq == 0:
                kcat = jnp.concatenate([kp_ref[...], kc_ref[cols, :]], axis=0)
                vtcat = jnp.concatenate([vtp_ref[...], vtc_ref[:, cols]], axis=1)
                b1 = jnp.where(first, bias_first, bias)
            else:
                kcat = kc_ref[(bq - 1) * blk:(bq + 1) * blk, :]
                vtcat = vtc_ref[:, (bq - 1) * blk:(bq + 1) * blk]
                b1 = bias
            in_group = (klane >= g * dh) & (klane < (g + 1) * dh) & (krow > 0)
            km = jnp.concatenate([jnp.where(in_group, kcat, jnp.zeros_like(kcat)), k_aug], axis=1)
            vt_g = jnp.where(vcol == 0, jnp.zeros((dh, 2 * blk), bf16), vtcat[g * dh:(g + 1) * dh, :])
            shared[(bq, g)] = km, jnp.concatenate([vt_g, ones_rows], axis=0), _tile_lanes(b1, 2)
        return shared[(bq, g)]

    def logits(bq, g, hp):
        cols = slice(bq * blk, (bq + 1) * blk)
        km, _, b2 = operands(bq, g)
        blocks = []
        for e in range(2):
            h = g * A_GROUP + 2 * hp + e
            pair = qt_ref[(h // 2) * LANES:(h // 2 + 1) * LANES, cols]
            if h % 2 != g:
                pair = jnp.concatenate([pair[dh:2 * dh, :], pair[0:dh, :]], axis=0)
            blocks.append(pair)
        sink_rows = sink_ref[g][:, hp * hp_lanes:(hp + 1) * hp_lanes]
        rhs = jnp.concatenate([jnp.concatenate(blocks, axis=1), sink_rows], axis=0)
        return _dot(km, rhs) + b2

    def finish(bq, g, hp, st):
        cols = slice(bq * blk, (bq + 1) * blk)
        _, vext, _ = operands(bq, g)
        p = jnp.exp2(st - jnp.max(st, axis=0, keepdims=True)).astype(bf16)
        oext = _dot(vext, p)
        o = (oext[0:dh, :] * (1.0 / oext[dh:dh + 1, :])).astype(bf16)
        for e in range(2):
            h = g * A_GROUP + 2 * hp + e
            out_ref[h * dh:(h + 1) * dh, cols] = o[:, e * blk:(e + 1) * blk]

    units = [(bq, g, hp) for bq in range(SWA_TILE // blk) for g in range(A_KV_HEADS) for hp in range(A_GROUP // 2)]
    queued = [logits(*unit) for unit in units[:SWA_LOOKAHEAD]]
    for u, unit in enumerate(units):
        if u + SWA_LOOKAHEAD < len(units):
            queued.append(logits(*units[u + SWA_LOOKAHEAD]))
        finish(*unit, queued[u])


def _swa(qt, k, vt, sink_aug, batch, seq):
    blk = A_BLOCK
    tq = SWA_TILE
    nq = A_Q_HEADS * A_HEAD_DIM
    nkv = A_KV_HEADS * A_HEAD_DIM
    nt = seq // tq
    per = tq // blk
    prev_idx = lambda b, i: b * nt * per + jnp.maximum(i * per - 1, 0)
    return pl.pallas_call(
        _swa_kernel,
        grid=(batch, nt),
        in_specs=[pl.BlockSpec((nq, tq), lambda b, i: (0, b * nt + i)),
                  pl.BlockSpec((tq, nkv), lambda b, i: (b * nt + i, 0)),
                  pl.BlockSpec((blk, nkv), lambda b, i: (prev_idx(b, i), 0)),
                  pl.BlockSpec((nkv, tq), lambda b, i: (0, b * nt + i)),
                  pl.BlockSpec((nkv, blk), lambda b, i: (0, prev_idx(b, i))),
                  _resident(sink_aug.shape)],
        out_specs=pl.BlockSpec((nq, tq), lambda b, i: (0, b * nt + i)),
        out_shape=jax.ShapeDtypeStruct((nq, batch * seq), bf16),
        compiler_params=_params(2),
        name="swa",
    )(qt, k, k, vt, vt, sink_aug)


def _sink_rows(sinks):
    parts = jnp.stack(_split3(sinks.astype(f32) * LOG2E), axis=0)
    per_lane = jnp.repeat(parts.reshape(3, A_KV_HEADS, A_GROUP), A_BLOCK, axis=2)
    return jnp.pad(per_lane.transpose(1, 0, 2), ((0, 0), (0, LANES - 3), (0, 0)))


def _out_ffn_kernel(h0_ref, at0_ref, hn_ref, atn_ref, wo_ref, wup_ref, wd_ref, vec_ref, conv_ref,
                    out_ref, act_ref, h1_ref, xn_ref, h1n_ref, xnn_ref, carry_ref, *, tiles_per_seq):
    tm = hn_ref.shape[0]
    dff = wd_ref.shape[0]
    i = pl.program_id(0)

    d = wo_ref.shape[1]
    out_cols = [slice(n0, n0 + OUT_CHUNK) for n0 in range(0, d, OUT_CHUNK)]

    def out_proj(at_ref, cols):
        return _dot_tn(at_ref[...], wo_ref[:, cols]) + vec_ref[0:1, cols]

    def front(h_ref, z_parts):
        h1 = h_ref[...] + _rmsnorm(jnp.concatenate(z_parts, axis=1), vec_ref[1:2, :])
        h1n_ref[...] = h1
        xnn_ref[...] = _rmsnorm(h1, vec_ref[2:3, :]).astype(bf16)

    @pl.when(i == 0)
    def _():
        front(h0_ref, [out_proj(at0_ref, cols) for cols in out_cols])

    @pl.when(i % tiles_per_seq == 0)
    def _():
        carry_ref[...] = jnp.zeros_like(carry_ref)

    xn_ref[...] = xnn_ref[...]
    h1_ref[...] = h1n_ref[...]
    top = lax.broadcasted_iota(jnp.int32, (8, FF_CHUNK), 0)
    z_parts = []
    for ci, c0 in enumerate(range(0, dff, FF_CHUNK)):
        if FRONT_AFTER_CHUNKS <= ci < FRONT_AFTER_CHUNKS + len(out_cols):
            z_parts.append(out_proj(atn_ref, out_cols[ci - FRONT_AFTER_CHUNKS]))
        if ci == FRONT_AFTER_CHUNKS + len(out_cols):
            front(hn_ref, z_parts)
        cs = slice(c0, c0 + FF_CHUNK)
        gate = _dot(xn_ref[...], wup_ref[:, cs])
        val = _dot(xn_ref[...], wup_ref[:, dff + c0:dff + c0 + FF_CHUNK])
        prev = carry_ref[0:8, cs]
        carry_ref[0:8, cs] = gate[tm - 8:tm, :]
        shifted = []
        for lag in (1, 2):
            rolled = pltpu.roll(gate, lag, axis=0)
            head = jnp.where(top < lag, pltpu.roll(prev, lag, axis=0), rolled[0:8, :])
            shifted.append(jnp.concatenate([head, rolled[8:, :]], axis=0))
        g1, g2 = shifted
        gc = conv_ref[3:4, cs] + conv_ref[0:1, cs] * g2 + conv_ref[1:2, cs] * g1 + conv_ref[2:3, cs] * gate
        act_ref[:, cs] = (gc * jax.nn.sigmoid(gc) * val).astype(bf16)
    y = _dot(act_ref[...], wd_ref[...])
    out_ref[...] = h1_ref[...] + _rmsnorm(y, vec_ref[3:4, :])


def _out_ffn(h, at, wo, wup, wd, vecs, conv, layer, seq):
    t, d = h.shape
    tm = TOKEN_TILE
    n = t // tm
    dff = wd.shape[1]
    nxt = lambda i: jnp.minimum(i + 1, n - 1)
    return pl.pallas_call(
        functools.partial(_out_ffn_kernel, tiles_per_seq=seq // tm),
        grid=(n,),
        in_specs=[pl.BlockSpec((tm, d), lambda i: (0, 0), pipeline_mode=pl.Buffered(1)),
                  pl.BlockSpec((at.shape[0], tm), lambda i: (0, 0), pipeline_mode=pl.Buffered(1)),
                  pl.BlockSpec((tm, d), lambda i: (nxt(i), 0)),
                  pl.BlockSpec((at.shape[0], tm), lambda i: (0, nxt(i))),
                  _resident(wo.shape), _layer_resident(wup.shape, layer), _layer_resident(wd.shape, layer),
                  _resident(vecs.shape), _resident(conv.shape)],
        out_specs=pl.BlockSpec((tm, d), lambda i: (i, 0)),
        out_shape=jax.ShapeDtypeStruct((t, d), f32),
        scratch_shapes=[pltpu.VMEM((tm, dff), bf16),
                        pltpu.VMEM((tm, d), f32), pltpu.VMEM((tm, d), bf16),
                        pltpu.VMEM((tm, d), f32), pltpu.VMEM((tm, d), bf16), pltpu.VMEM((16, dff), f32)],
        compiler_params=_params(1),
        name="out_ffn",
    )(h, at, h, at, wo, wup, wd, vecs, conv)


def kernel(x, m_w_in, m_gate_bias, m_head_norm, m_w_out, a_w_in, a_b_in, a_sinks, a_w_out, a_b_out,
           norm_mix_pre, norm_mix_post, norm_ffn_pre, norm_ffn_post, f_w_up, f_conv_w, f_conv_b, f_w_down):
    batch, seq, d = x.shape
    depth = norm_mix_pre.shape[0]
    h = x.reshape(batch * seq, d)
    row = lambda vec: vec.reshape(1, -1).astype(f32)
    lane_rep = lambda vec: jnp.broadcast_to(vec.astype(f32)[:, None], (vec.shape[0], LANES))

    for i in range(depth):
        j = i // 2
        if i % 2 == 0:
            mixed_t, w_up, w_down = _mlstm_mixer(
                h, row(norm_mix_pre[i]), m_w_in[j].T.astype(bf16), lane_rep(m_gate_bias[j].reshape(-1)),
                lane_rep(m_head_norm[j]), f_w_up, f_w_down, seq)
            w_out = m_w_out[j].astype(bf16)
            b_out = jnp.zeros((d,), f32)
        else:
            mixed_t = _attn_mixer(h, row(norm_mix_pre[i]), a_w_in[j].T.astype(bf16), lane_rep(a_b_in[j]),
                                  _sink_rows(a_sinks[j]), seq)
            w_out = a_w_out[j].astype(bf16)
            b_out = a_b_out[j]
        vecs = jnp.pad(jnp.stack([b_out, norm_mix_post[i], norm_ffn_pre[i], norm_ffn_post[i]]).astype(f32),
                       ((0, 4), (0, 0)))
        conv = jnp.pad(jnp.concatenate([f_conv_w[i], f_conv_b[i][None]], axis=0).astype(f32), ((0, 12), (0, 0)))
        h = _out_ffn(h, mixed_t, w_out, w_up, w_down, vecs, conv, i, seq)
    return h.reshape(batch, seq, d)
```

```python
import functools

import jax
import jax.numpy as jnp
from jax import lax
from jax.experimental import pallas as pl
from jax.experimental.pallas import tpu as pltpu

EPS = 1e-6
LANES = 128
BF16_ROWS = 16

M_HEADS = 8
M_QK_DIM = 64
M_V_DIM = 128
GATE_CAP = 15.0
M_CHUNK = 128
MLSTM_GROUP = 256
MIXER_LOOKAHEAD = 2

A_HEAD_DIM = 64
A_Q_HEADS = 16
A_KV_HEADS = 2
A_GROUP = A_Q_HEADS // A_KV_HEADS
WINDOW = 128
A_BLOCK = 128

LOG2E = 1.4426950408889634
NEG_BIG = -1e30

TOKEN_TILE = 512
PROJ_TILE = 1024
ATTN_GROUP = 256
SWA_LOOKAHEAD = 4
FF_CHUNK = 256
Q_ROWS_PER_DOT = 256
OUT_CHUNK = 256
FRONT_AFTER_CHUNKS = 2
VMEM_LIMIT = 60 * 1024 * 1024

bf16 = jnp.bfloat16
f32 = jnp.float32


def _dot(a, b):
    return jnp.dot(a, b, preferred_element_type=f32)


def _dot_nt(a, b):
    return lax.dot_general(a, b, (((1,), (1,)), ((), ())), preferred_element_type=f32)


def _dot_tn(a, b):
    return lax.dot_general(a, b, (((0,), (0,)), ((), ())), preferred_element_type=f32)


def _rmsnorm(x, g):
    return x * lax.rsqrt(jnp.mean(x * x, axis=-1, keepdims=True) + EPS) * g


def _split3(x):
    hi = x.astype(bf16)
    r1 = x - hi.astype(f32)
    mid = r1.astype(bf16)
    lo = (r1 - mid.astype(f32)).astype(bf16)
    return hi, mid, lo


def _log_sigmoid(x):
    return jnp.minimum(x, 0.0) - jnp.log1p(jnp.exp(-jnp.abs(x)))


def _tile_lanes(x, reps):
    return jnp.concatenate([x] * reps, axis=1)


def _next_input_norm(i, x0_ref, xnext_ref, g_ref, xn_ref, xnn_ref):
    @pl.when(i == 0)
    def _():
        xnn_ref[...] = _rmsnorm(x0_ref[...], g_ref[...]).astype(bf16)

    xn_ref[...] = xnn_ref[...]

    def prepare_next():
        xnn_ref[...] = _rmsnorm(xnext_ref[...], g_ref[...]).astype(bf16)

    return xn_ref[...], prepare_next


def _sigmoid(x):
    return 0.5 * jnp.tanh(0.5 * x) + 0.5


def _proj_in_specs(tm, d, n_tiles):
    return [pl.BlockSpec((tm, d), lambda i: (0, 0), pipeline_mode=pl.Buffered(1)),
            pl.BlockSpec((tm, d), lambda i: (jnp.minimum(i + 1, n_tiles - 1), 0))]


def _resident(shape):
    nd = len(shape)
    return pl.BlockSpec(shape, lambda *_: (0,) * nd, pipeline_mode=pl.Buffered(1))


def _layer_resident(shape, layer):
    nd = len(shape) - 1
    return pl.BlockSpec((None,) + tuple(shape[1:]), lambda *_: (layer,) + (0,) * nd, pipeline_mode=pl.Buffered(1))


def _params(n_axes):
    return pltpu.CompilerParams(
        dimension_semantics=("arbitrary",) * n_axes, vmem_limit_bytes=VMEM_LIMIT)


def _mlstm_kernel(x0_ref, xnext_ref, g_ref, wt_ref, gb_ref, hn_ref, wup_ref, wdown_ref,
                  out_ref, wup_bf_ref, wdown_bf_ref, xn_ref, xnn_ref, ct_ref, m_ref,
                  q0_ref, k0_ref, vt0_ref, ot0_ref, stat0_ref, ccol0_ref, *, tiles_per_seq):
    tm = xnext_ref.shape[0]
    L = M_CHUNK
    H = M_HEADS
    dk, dv = M_QK_DIM, M_V_DIM
    nq = H * dk
    nv = H * dv
    pairs = H // 2
    G = MLSTM_GROUP
    cpg = G // L
    i = pl.program_id(0)

    @pl.when(i % tiles_per_seq == 0)
    def _():
        ct_ref[...] = jnp.zeros_like(ct_ref)
        m_ref[...] = jnp.zeros_like(m_ref)

    xn, prepare_next = _next_input_norm(i, x0_ref, xnext_ref, g_ref, xn_ref, xnn_ref)

    upper = lax.broadcasted_iota(jnp.int32, (L, L), 0) <= lax.broadcasted_iota(jnp.int32, (L, L), 1)
    triu = jnp.where(upper, 1.0, 0.0).astype(bf16)
    lane_h = lax.broadcasted_iota(jnp.int32, (H, L), 1)
    lane = lax.broadcasted_iota(jnp.int32, (L, LANES), 1)
    low = lane < dk
    ones_rows = jnp.ones((BF16_ROWS, L), bf16)
    zeros_half = jnp.zeros((dk, L), bf16)

    def projection_steps(xg):
        res = {}
        piece = lambda r0, r1: _dot_nt(wt_ref[r0:r1, :], xg)
        o0 = 2 * nq + nv

        def gates_and_o_hi():
            p = piece(o0 + nv // 2, o0 + nv + 2 * H)
            res["o_hi"] = _sigmoid(p[0:nv // 2, :]).astype(bf16)
            raw = p[nv // 2:, :] + _tile_lanes(gb_ref[...], G // LANES)
            capped = GATE_CAP * jnp.tanh(raw * (1.0 / GATE_CAP))
            res["grow"] = jnp.where(lax.broadcasted_iota(jnp.int32, raw.shape, 0) < H, capped, _log_sigmoid(capped))

        def o_lo():
            res["o_lo"] = _sigmoid(piece(o0, o0 + nv // 2)).astype(bf16)

        def k_and_stats():
            res["k"] = piece(nq, 2 * nq).T.astype(bf16)
            grow = res["grow"]
            stacked = jnp.concatenate([grow[:, c * L:(c + 1) * L] for c in range(cpg)], axis=0)
            sums = _dot(jnp.concatenate(_split3(stacked), axis=0), triu)
            n = cpg * 2 * H
            cum = sums[0:n] + sums[n:2 * n] + sums[2 * n:]
            stats = []
            for c in range(cpg):
                b_r = cum[c * 2 * H + H:(c + 1) * 2 * H, :]
                c_r = grow[0:H, c * L:(c + 1) * L] - b_r
                cmax = c_r
                shift = 1
                while shift < L:
                    cmax = jnp.where(lane_h >= shift, jnp.maximum(cmax, pltpu.roll(cmax, shift, axis=1)), cmax)
                    shift *= 2
                b_last = jnp.broadcast_to(b_r[:, L - 1:L], (H, L))
                tail = c_r + b_last
                tail_max = jnp.broadcast_to(jnp.max(tail, axis=-1, keepdims=True), (H, L))
                c_cols = jnp.concatenate([c_r * LOG2E, jnp.zeros((LANES - H, L), f32)], axis=0).T
                stats.append((b_r, cmax, tail, tail_max, b_last, c_cols))
            res["stats"] = stats

        def v_lo():
            res["v_lo"] = piece(2 * nq, 2 * nq + nv // 2).astype(bf16)

        def v_hi():
            res["v_hi"] = piece(2 * nq + nv // 2, 2 * nq + nv).astype(bf16)

        def q():
            res["q"] = (piece(0, nq) * (dk ** -0.5)).astype(bf16)

        return res, [gates_and_o_hi, o_lo, k_and_stats, v_lo, v_hi, q]

    def finalize(res):
        res["ot"] = jnp.concatenate([res.pop("o_lo"), res.pop("o_hi")], axis=0)
        res["vt"] = jnp.concatenate([res.pop("v_lo"), res.pop("v_hi")], axis=0)
        return res

    def kq(p, c, j):
        tok = slice(c * L, (c + 1) * L)
        kp = p["k"][tok, j * LANES:(j + 1) * LANES]
        qtp = p["q"][j * LANES:(j + 1) * LANES, tok]
        q_even = jnp.concatenate([qtp[0:dk, :], zeros_half], axis=0)
        q_odd = jnp.concatenate([zeros_half, qtp[dk:2 * dk, :]], axis=0)
        return kp, (q_even, q_odd), _dot(kp, jnp.concatenate([q_even, q_odd], axis=1))

    def scan_unit(p, g, c, j, m_prev, m_new, ct, kq_res):
        kp, q_eo, st = kq_res
        tok = slice(c * L, (c + 1) * L)
        out_tok = slice(g * G + c * L, g * G + (c + 1) * L)
        b_r, cmax, tail, _, b_last, c_cols = p["stats"][c]
        a_r = jnp.maximum(m_prev, cmax)
        a_r2 = a_r * LOG2E
        carry_w = jnp.exp(m_prev - a_r).astype(bf16)
        clamp_r = jnp.exp(-(a_r + b_r))
        ws_r = jnp.exp(tail - m_new).astype(bf16)
        decay = jnp.exp(b_last + m_prev - m_new)
        ct_pair_b = ct[j].astype(bf16)
        vws = []
        for e, q_e in enumerate(q_eo):
            h = 2 * j + e
            expo = jnp.where(upper, c_cols[:, h:h + 1] - a_r2[h:h + 1, :], NEG_BIG)
            wt = (st[:, e * L:(e + 1) * L] * jnp.exp2(expo)).astype(bf16)
            q_w = q_e * carry_w[h:h + 1, :]
            vext = jnp.concatenate([p["vt"][h * dv:(h + 1) * dv, tok], ones_rows], axis=0)
            nd = _dot(jnp.concatenate([vext, ct_pair_b], axis=1), jnp.concatenate([wt, q_w], axis=0))
            den = jnp.maximum(jnp.abs(nd[dv:dv + 1, :]), clamp_r[h:h + 1, :])
            inv = 1.0 / den
            num = nd[0:dv, :]
            scale = inv * lax.rsqrt(inv * inv * jnp.mean(num * num, axis=0, keepdims=True) + EPS)
            hn = num * scale * hn_ref[h * dv:(h + 1) * dv, :]
            gate_o = p["ot"][h * dv:(h + 1) * dv, tok].astype(f32)
            out_ref[h * dv:(h + 1) * dv, out_tok] = (gate_o * hn).astype(bf16)
            vws.append(vext * ws_r[h:h + 1, :])
        k_split = jnp.concatenate([jnp.where(low, kp, jnp.zeros_like(kp)),
                                   jnp.where(low, jnp.zeros_like(kp), kp)], axis=0)
        decay_pair = jnp.where(low[0:1, :], decay[2 * j:2 * j + 1, :], decay[2 * j + 1:2 * j + 2, :])
        ct[j] = decay_pair * ct[j] + _dot(jnp.concatenate(vws, axis=1), k_split)

    def store_first_group(res):
        q0_ref[...], k0_ref[...], vt0_ref[...], ot0_ref[...] = res["q"], res["k"], res["vt"], res["ot"]
        for c, stat in enumerate(res["stats"]):
            stat0_ref[c * 5 * H:(c + 1) * 5 * H, :] = jnp.concatenate(stat[0:5], axis=0)
            ccol0_ref[c * L:(c + 1) * L, :] = stat[5]

    def load_first_group():
        stats = [tuple(stat0_ref[(c * 5 + s) * H:(c * 5 + s + 1) * H, :] for s in range(5))
                 + (ccol0_ref[c * L:(c + 1) * L, :],) for c in range(cpg)]
        return {"q": q0_ref[...], "k": k0_ref[...], "vt": vt0_ref[...], "ot": ot0_ref[...], "stats": stats}

    @pl.when(i == 0)
    def _():
        res, first_steps = projection_steps(xn_ref[0:G, :])
        for step in first_steps:
            step()
        store_first_group(finalize(res))

    n_groups = tm // G
    cur = load_first_group()

    ct = [ct_ref[j] for j in range(pairs)]
    m = m_ref[...]
    units = [(g, c, j) for g in range(n_groups) for c in range(cpg) for j in range(pairs)]
    per_group = cpg * pairs
    projected = {0: cur}
    kq_queue = {}

    def queue_kq(t):
        g, c, j = units[t]
        kq_queue[t] = kq(projected[g], c, j)

    for t in range(MIXER_LOOKAHEAD):
        queue_kq(t)
    for t, (g, c, j) in enumerate(units):
        u = t % per_group
        if u == 0:
            nxt, steps = projection_steps(xn[(g + 1) * G:(g + 2) * G, :] if g + 1 < n_groups else xnn_ref[0:G, :])
            stats = projected[g]["stats"]
            m_prevs = []
            for cc in range(cpg):
                m_prevs.append(m)
                m = jnp.maximum(stats[cc][4] + m, stats[cc][3])
            m_prevs.append(m)
        if t == 0:
            prepare_next()
        if u < len(steps):
            steps[u]()
            if u + 1 == len(steps):
                if g + 1 < n_groups:
                    projected[g + 1] = finalize(nxt)
                else:
                    store_first_group(finalize(nxt))
        if t + MIXER_LOOKAHEAD < len(units):
            queue_kq(t + MIXER_LOOKAHEAD)
        scan_unit(projected[g], g, c, j, m_prevs[c], m_prevs[c + 1], ct, kq_queue.pop(t))
    m_ref[...] = m
    for j in range(pairs):
        ct_ref[j] = ct[j]
    wup_bf_ref[...] = wup_ref[...].astype(bf16)
    wdown_bf_ref[...] = wdown_ref[...].astype(bf16)


def _mlstm_mixer(x, gain, wt, gate_bias, hn_rep, w_up, w_down, seq):
    t, d = x.shape
    nq = M_HEADS * M_QK_DIM
    nv = M_HEADS * M_V_DIM
    tm = PROJ_TILE
    steps = t // tm
    slab = lambda w: pl.BlockSpec((w.shape[0], w.shape[1] // steps, w.shape[2]), lambda i: (0, i, 0))
    return pl.pallas_call(
        functools.partial(_mlstm_kernel, tiles_per_seq=seq // tm),
        grid=(steps,),
        in_specs=_proj_in_specs(tm, d, steps) + [
            _resident((1, d)), _resident(wt.shape), _resident(gate_bias.shape), _resident(hn_rep.shape),
            slab(w_up), slab(w_down)],
        out_specs=[pl.BlockSpec((nv, tm), lambda i: (0, i)), slab(w_up), slab(w_down)],
        out_shape=[jax.ShapeDtypeStruct((nv, t), bf16),
                   jax.ShapeDtypeStruct(w_up.shape, bf16), jax.ShapeDtypeStruct(w_down.shape, bf16)],
        scratch_shapes=[pltpu.VMEM((tm, d), bf16), pltpu.VMEM((tm, d), bf16),
                        pltpu.VMEM((M_HEADS // 2, M_V_DIM + BF16_ROWS, 2 * M_QK_DIM), f32),
                        pltpu.VMEM((M_HEADS, LANES), f32),
                        pltpu.VMEM((nq, MLSTM_GROUP), bf16), pltpu.VMEM((MLSTM_GROUP, nq), bf16),
                        pltpu.VMEM((nv, MLSTM_GROUP), bf16), pltpu.VMEM((nv, MLSTM_GROUP), bf16),
                        pltpu.VMEM((MLSTM_GROUP // M_CHUNK * 5 * M_HEADS, M_CHUNK), f32),
                        pltpu.VMEM((MLSTM_GROUP, LANES), f32)],
        compiler_params=_params(1),
        name="mlstm_mixer",
    )(x, x, gain, wt, gate_bias, hn_rep, w_up, w_down)


def _attn_kernel(x0_ref, xnext_ref, g_ref, wt_ref, b_ref, sink_ref, out_ref,
                 xn_ref, xnn_ref, kprev_ref, vtprev_ref, q0_ref, k0_ref, vt0_ref, *, tiles_per_seq):
    tm = xnext_ref.shape[0]
    blk = A_BLOCK
    dh = A_HEAD_DIM
    nq = A_Q_HEADS * dh
    nkv = A_KV_HEADS * dh
    G = ATTN_GROUP
    bpg = G // blk
    i = pl.program_id(0)
    first = i % tiles_per_seq == 0

    @pl.when(i == 0)
    def _():
        kprev_ref[...] = jnp.zeros_like(kprev_ref)
        vtprev_ref[...] = jnp.zeros_like(vtprev_ref)

    xn, prepare_next = _next_input_norm(i, x0_ref, xnext_ref, g_ref, xn_ref, xnn_ref)

    ku = lax.broadcasted_iota(jnp.int32, (2 * blk, blk), 0)
    qi = lax.broadcasted_iota(jnp.int32, (2 * blk, blk), 1)
    diff = qi - (ku - blk)
    band = (diff >= 0) & (diff < WINDOW)
    bias = jnp.where(band | (ku == 0), 0.0, NEG_BIG)
    bias_first = jnp.where((band & (ku >= blk)) | (ku == 0), 0.0, NEG_BIG)
    krow = lax.broadcasted_iota(jnp.int32, (2 * blk, LANES), 0)
    klane = lax.broadcasted_iota(jnp.int32, (2 * blk, LANES), 1)
    k_aug = jnp.where((krow == 0) & (klane < 3), 1.0, 0.0).astype(bf16)
    vcol = lax.broadcasted_iota(jnp.int32, (dh, 2 * blk), 1)
    ones_rows = jnp.ones((BF16_ROWS, 2 * blk), bf16)
    hp_lanes = 2 * blk

    def projection_steps(xg):
        res = {"q": {}}
        piece = lambda r0, r1: _dot_nt(wt_ref[r0:r1, :], xg) + _tile_lanes(b_ref[r0:r1, :], G // LANES)

        def kv():
            p = piece(nq, nq + 2 * nkv)
            res["k"] = p[0:nkv, :].T.astype(bf16)
            res["vt"] = p[nkv:2 * nkv, :].astype(bf16)

        def q_rows(r0):
            def step():
                res["q"][r0] = (piece(r0, r0 + Q_ROWS_PER_DOT) * (dh ** -0.5 * LOG2E)).astype(bf16)
            return step

        return res, [kv] + [q_rows(r0) for r0 in range(0, nq, Q_ROWS_PER_DOT)]

    def finalize(res):
        res["qt"] = jnp.concatenate([res["q"][r0] for r0 in sorted(res["q"])], axis=0)
        return res

    projected = {}
    shared = {}

    def operands(g, bl, grp):
        if (g, bl, grp) not in shared:
            cols = slice(bl * blk, (bl + 1) * blk)
            cur = projected[g]
            if bl > 0:
                k_prev, vt_prev = cur["k"][(bl - 1) * blk:bl * blk, :], cur["vt"][:, (bl - 1) * blk:bl * blk]
            elif g > 0:
                k_prev, vt_prev = projected[g - 1]["k"][G - blk:G, :], projected[g - 1]["vt"][:, G - blk:G]
            else:
                k_prev, vt_prev = kprev_ref[...], vtprev_ref[...]
            b1 = jnp.where(first, bias_first, bias) if (g == 0 and bl == 0) else bias
            kcat = jnp.concatenate([k_prev, cur["k"][cols, :]], axis=0)
            vtcat = jnp.concatenate([vt_prev, cur["vt"][:, cols]], axis=1)
            in_group = (klane >= grp * dh) & (klane < (grp + 1) * dh) & (krow > 0)
            km = jnp.concatenate([jnp.where(in_group, kcat, jnp.zeros_like(kcat)), k_aug], axis=1)
            vt_g = jnp.where(vcol == 0, jnp.zeros((dh, 2 * blk), bf16), vtcat[grp * dh:(grp + 1) * dh, :])
            shared[(g, bl, grp)] = km, jnp.concatenate([vt_g, ones_rows], axis=0), _tile_lanes(b1, 2)
        return shared[(g, bl, grp)]

    def logits(g, bl, grp, hp):
        cols = slice(bl * blk, (bl + 1) * blk)
        km, _, b2 = operands(g, bl, grp)
        blocks = []
        for e in range(2):
            h = grp * A_GROUP + 2 * hp + e
            pair = projected[g]["qt"][(h // 2) * LANES:(h // 2 + 1) * LANES, cols]
            if h % 2 != grp:
                pair = jnp.concatenate([pair[dh:2 * dh, :], pair[0:dh, :]], axis=0)
            blocks.append(pair)
        sink_rows = sink_ref[grp][:, hp * hp_lanes:(hp + 1) * hp_lanes]
        rhs = jnp.concatenate([jnp.concatenate(blocks, axis=1), sink_rows], axis=0)
        return _dot(km, rhs) + b2

    def finish(g, bl, grp, hp, st):
        out_cols = slice(g * G + bl * blk, g * G + (bl + 1) * blk)
        _, vext, _ = operands(g, bl, grp)
        p = jnp.exp2(st - jnp.max(st, axis=0, keepdims=True)).astype(bf16)
        oext = _dot(vext, p)
        o = (oext[0:dh, :] * (1.0 / oext[dh:dh + 1, :])).astype(bf16)
        for e in range(2):
            h = grp * A_GROUP + 2 * hp + e
            out_ref[h * dh:(h + 1) * dh, out_cols] = o[:, e * blk:(e + 1) * blk]

    def store_first_group(res):
        q0_ref[...], k0_ref[...], vt0_ref[...] = res["qt"], res["k"], res["vt"]

    @pl.when(i == 0)
    def _():
        res, first_steps = projection_steps(xn_ref[0:G, :])
        for step in first_steps:
            step()
        store_first_group(finalize(res))

    n_groups = tm // G
    projected[0] = {"qt": q0_ref[...], "k": k0_ref[...], "vt": vt0_ref[...]}

    units = [(g, bl, grp, hp) for g in range(n_groups) for bl in range(bpg)
             for grp in range(A_KV_HEADS) for hp in range(A_GROUP // 2)]
    per_group = bpg * A_KV_HEADS * (A_GROUP // 2)
    queued = {}
    for t in range(SWA_LOOKAHEAD):
        queued[t] = logits(*units[t])
    for t, unit in enumerate(units):
        g = unit[0]
        u = t % per_group
        if u == 0:
            nxt, steps = projection_steps(xn[(g + 1) * G:(g + 2) * G, :] if g + 1 < n_groups else xnn_ref[0:G, :])
        if t == 0:
            prepare_next()
        if u < len(steps):
            steps[u]()
            if u + 1 == len(steps):
                if g + 1 < n_groups:
                    projected[g + 1] = finalize(nxt)
                else:
                    store_first_group(finalize(nxt))
        if t + SWA_LOOKAHEAD < len(units):
            queued[t + SWA_LOOKAHEAD] = logits(*units[t + SWA_LOOKAHEAD])
        finish(*unit, queued.pop(t))
    last = projected[n_groups - 1]
    kprev_ref[...] = last["k"][G - blk:G, :]
    vtprev_ref[...] = last["vt"][:, G - blk:G]


def _attn_mixer(x, gain, wt, b_rep, sink_aug, seq):
    t, d = x.shape
    nq = A_Q_HEADS * A_HEAD_DIM
    nkv = A_KV_HEADS * A_HEAD_DIM
    tm = PROJ_TILE
    steps = t // tm
    return pl.pallas_call(
        functools.partial(_attn_kernel, tiles_per_seq=seq // tm),
        grid=(steps,),
        in_specs=_proj_in_specs(tm, d, steps) + [
            _resident((1, d)), _resident(wt.shape), _resident(b_rep.shape), _resident(sink_aug.shape)],
        out_specs=pl.BlockSpec((nq, tm), lambda i: (0, i)),
        out_shape=jax.ShapeDtypeStruct((nq, t), bf16),
        scratch_shapes=[pltpu.VMEM((tm, d), bf16), pltpu.VMEM((tm, d), bf16),
                        pltpu.VMEM((A_BLOCK, nkv), bf16), pltpu.VMEM((nkv, A_BLOCK), bf16),
                        pltpu.VMEM((nq, ATTN_GROUP), bf16), pltpu.VMEM((ATTN_GROUP, nkv), bf16),
                        pltpu.VMEM((nkv, ATTN_GROUP), bf16)],
        compiler_params=_params(1),
        name="attn_mixer",
    )(x, x, gain, wt, b_rep, sink_aug)


def _sink_rows(sinks):
    parts = jnp.stack(_split3(sinks.astype(f32) * LOG2E), axis=0)
    per_lane = jnp.repeat(parts.reshape(3, A_KV_HEADS, A_GROUP), A_BLOCK, axis=2)
    return jnp.pad(per_lane.transpose(1, 0, 2), ((0, 0), (0, LANES - 3), (0, 0)))


def _out_ffn_kernel(h0_ref, at0_ref, hn_ref, atn_ref, wo_ref, wup_ref, wd_ref, vec_ref, conv_ref,
                    out_ref, act_ref, h1_ref, xn_ref, h1n_ref, xnn_ref, carry_ref, *, tiles_per_seq):
    tm = hn_ref.shape[0]
    dff = wd_ref.shape[0]
    i = pl.program_id(0)

    d = wo_ref.shape[1]
    out_cols = [slice(n0, n0 + OUT_CHUNK) for n0 in range(0, d, OUT_CHUNK)]

    def out_proj(at_ref, cols):
        return _dot_tn(at_ref[...], wo_ref[:, cols]) + vec_ref[0:1, cols]

    def front(h_ref, z_parts):
        h1 = h_ref[...] + _rmsnorm(jnp.concatenate(z_parts, axis=1), vec_ref[1:2, :])
        h1n_ref[...] = h1
        xnn_ref[...] = _rmsnorm(h1, vec_ref[2:3, :]).astype(bf16)

    @pl.when(i == 0)
    def _():
        front(h0_ref, [out_proj(at0_ref, cols) for cols in out_cols])

    @pl.when(i % tiles_per_seq == 0)
    def _():
        carry_ref[...] = jnp.zeros_like(carry_ref)

    xn_ref[...] = xnn_ref[...]
    h1_ref[...] = h1n_ref[...]
    top = lax.broadcasted_iota(jnp.int32, (8, FF_CHUNK), 0)
    z_parts = []
    for ci, c0 in enumerate(range(0, dff, FF_CHUNK)):
        if FRONT_AFTER_CHUNKS <= ci < FRONT_AFTER_CHUNKS + len(out_cols):
            z_parts.append(out_proj(atn_ref, out_cols[ci - FRONT_AFTER_CHUNKS]))
        if ci == FRONT_AFTER_CHUNKS + len(out_cols):
            front(hn_ref, z_parts)
        cs = slice(c0, c0 + FF_CHUNK)
        gate = _dot(xn_ref[...], wup_ref[:, cs])
        val = _dot(xn_ref[...], wup_ref[:, dff + c0:dff + c0 + FF_CHUNK])
        prev = carry_ref[0:8, cs]
        carry_ref[0:8, cs] = gate[tm - 8:tm, :]
        shifted = []
        for lag in (1, 2):
            rolled = pltpu.roll(gate, lag, axis=0)
            head = jnp.where(top < lag, pltpu.roll(prev, lag, axis=0), rolled[0:8, :])
            shifted.append(jnp.concatenate([head, rolled[8:, :]], axis=0))
        g1, g2 = shifted
        gc = conv_ref[3:4, cs] + conv_ref[0:1, cs] * g2 + conv_ref[1:2, cs] * g1 + conv_ref[2:3, cs] * gate
        act_ref[:, cs] = (gc * jax.nn.sigmoid(gc) * val).astype(bf16)
    y = _dot(act_ref[...], wd_ref[...])
    out_ref[...] = h1_ref[...] + _rmsnorm(y, vec_ref[3:4, :])


def _out_ffn(h, at, wo, wup, wd, vecs, conv, layer, seq):
    t, d = h.shape
    tm = TOKEN_TILE
    n = t // tm
    dff = wd.shape[1]
    nxt = lambda i: jnp.minimum(i + 1, n - 1)
    return pl.pallas_call(
        functools.partial(_out_ffn_kernel, tiles_per_seq=seq // tm),
        grid=(n,),
        in_specs=[pl.BlockSpec((tm, d), lambda i: (0, 0), pipeline_mode=pl.Buffered(1)),
                  pl.BlockSpec((at.shape[0], tm), lambda i: (0, 0), pipeline_mode=pl.Buffered(1)),
                  pl.BlockSpec((tm, d), lambda i: (nxt(i), 0)),
                  pl.BlockSpec((at.shape[0], tm), lambda i: (0, nxt(i))),
                  _resident(wo.shape), _layer_resident(wup.shape, layer), _layer_resident(wd.shape, layer),
                  _resident(vecs.shape), _resident(conv.shape)],
        out_specs=pl.BlockSpec((tm, d), lambda i: (i, 0)),
        out_shape=jax.ShapeDtypeStruct((t, d), f32),
        scratch_shapes=[pltpu.VMEM((tm, dff), bf16),
                        pltpu.VMEM((tm, d), f32), pltpu.VMEM((tm, d), bf16),
                        pltpu.VMEM((tm, d), f32), pltpu.VMEM((tm, d), bf16), pltpu.VMEM((16, dff), f32)],
        compiler_params=_params(1),
        name="out_ffn",
    )(h, at, h, at, wo, wup, wd, vecs, conv)


def kernel(x, m_w_in, m_gate_bias, m_head_norm, m_w_out, a_w_in, a_b_in, a_sinks, a_w_out, a_b_out,
           norm_mix_pre, norm_mix_post, norm_ffn_pre, norm_ffn_post, f_w_up, f_conv_w, f_conv_b, f_w_down):
    batch, seq, d = x.shape
    depth = norm_mix_pre.shape[0]
    h = x.reshape(batch * seq, d)
    row = lambda vec: vec.reshape(1, -1).astype(f32)
    lane_rep = lambda vec: jnp.broadcast_to(vec.astype(f32)[:, None], (vec.shape[0], LANES))

    for i in range(depth):
        j = i // 2
        if i % 2 == 0:
            mixed_t, w_up, w_down = _mlstm_mixer(
                h, row(norm_mix_pre[i]), m_w_in[j].T.astype(bf16), lane_rep(m_gate_bias[j].reshape(-1)),
                lane_rep(m_head_norm[j]), f_w_up, f_w_down, seq)
            w_out = m_w_out[j].astype(bf16)
            b_out = jnp.zeros((d,), f32)
        else:
            mixed_t = _attn_mixer(h, row(norm_mix_pre[i]), a_w_in[j].T.astype(bf16), lane_rep(a_b_in[j]),
                                  _sink_rows(a_sinks[j]), seq)
            w_out = a_w_out[j].astype(bf16)
            b_out = a_b_out[j]
        vecs = jnp.pad(jnp.stack([b_out, norm_mix_post[i], norm_ffn_pre[i], norm_ffn_post[i]]).astype(f32),
                       ((0, 4), (0, 0)))
        conv = jnp.pad(jnp.concatenate([f_conv_w[i], f_conv_b[i][None]], axis=0).astype(f32), ((0, 12), (0, 0)))
        h = _out_ffn(h, mixed_t, w_out, w_up, w_down, vecs, conv, i, seq)
    return h.reshape(batch, seq, d)
```

```python
import functools

import jax
import jax.numpy as jnp
from jax import lax
from jax.experimental import pallas as pl
from jax.experimental.pallas import tpu as pltpu

EPS = 1e-6
LANES = 128
BF16_ROWS = 16

M_HEADS = 8
M_QK_DIM = 64
M_V_DIM = 128
GATE_CAP = 15.0
M_CHUNK = 128
MLSTM_GROUP = 256
MIXER_LOOKAHEAD = 2

A_HEAD_DIM = 64
A_Q_HEADS = 16
A_KV_HEADS = 2
A_GROUP = A_Q_HEADS // A_KV_HEADS
WINDOW = 128
A_BLOCK = 128

LOG2E = 1.4426950408889634
NEG_BIG = -1e30

TOKEN_TILE = 512
PROJ_TILE = 1024
ATTN_GROUP = 256
SWA_LOOKAHEAD = 4
FF_CHUNK = 256
Q_ROWS_PER_DOT = 256
OUT_CHUNK = 256
FRONT_AFTER_CHUNKS = 2
VMEM_LIMIT = 60 * 1024 * 1024

bf16 = jnp.bfloat16
f32 = jnp.float32


def _dot(a, b):
    return jnp.dot(a, b, preferred_element_type=f32)


def _dot_nt(a, b):
    return lax.dot_general(a, b, (((1,), (1,)), ((), ())), preferred_element_type=f32)


def _dot_tn(a, b):
    return lax.dot_general(a, b, (((0,), (0,)), ((), ())), preferred_element_type=f32)


def _rmsnorm(x, g):
    return x * lax.rsqrt(jnp.mean(x * x, axis=-1, keepdims=True) + EPS) * g


def _split3(x):
    hi = x.astype(bf16)
    r1 = x - hi.astype(f32)
    mid = r1.astype(bf16)
    lo = (r1 - mid.astype(f32)).astype(bf16)
    return hi, mid, lo


def _log_sigmoid(x):
    return jnp.minimum(x, 0.0) - jnp.log1p(jnp.exp(-jnp.abs(x)))


def _tile_lanes(x, reps):
    return jnp.concatenate([x] * reps, axis=1)


def _next_input_norm(i, x0_ref, xnext_ref, g_ref, xn_ref, xnn_ref):
    @pl.when(i == 0)
    def _():
        xnn_ref[...] = _rmsnorm(x0_ref[...], g_ref[...]).astype(bf16)

    xn_ref[...] = xnn_ref[...]

    def prepare_next():
        xnn_ref[...] = _rmsnorm(xnext_ref[...], g_ref[...]).astype(bf16)

    return xn_ref[...], prepare_next


def _sigmoid(x):
    return 0.5 * jnp.tanh(0.5 * x) + 0.5


def _proj_in_specs(tm, d, n_tiles):
    return [pl.BlockSpec((tm, d), lambda i: (0, 0), pipeline_mode=pl.Buffered(1)),
            pl.BlockSpec((tm, d), lambda i: (jnp.minimum(i + 1, n_tiles - 1), 0))]


def _resident(shape):
    nd = len(shape)
    return pl.BlockSpec(shape, lambda *_: (0,) * nd, pipeline_mode=pl.Buffered(1))


def _layer_resident(shape, layer):
    nd = len(shape) - 1
    return pl.BlockSpec((None,) + tuple(shape[1:]), lambda *_: (layer,) + (0,) * nd, pipeline_mode=pl.Buffered(1))


def _params(n_axes):
    return pltpu.CompilerParams(
        dimension_semantics=("arbitrary",) * n_axes, vmem_limit_bytes=VMEM_LIMIT)


def _mlstm_kernel(x0_ref, xnext_ref, g_ref, wt_ref, gb_ref, hn_ref, wup_ref, wdown_ref,
                  out_ref, wup_bf_ref, wdown_bf_ref, xn_ref, xnn_ref, ct_ref, m_ref,
                  q0_ref, k0_ref, vt0_ref, ot0_ref, stat0_ref, ccol0_ref, *, tiles_per_seq):
    tm = xnext_ref.shape[0]
    L = M_CHUNK
    H = M_HEADS
    dk, dv = M_QK_DIM, M_V_DIM
    nq = H * dk
    nv = H * dv
    pairs = H // 2
    G = MLSTM_GROUP
    cpg = G // L
    i = pl.program_id(0)

    @pl.when(i % tiles_per_seq == 0)
    def _():
        ct_ref[...] = jnp.zeros_like(ct_ref)
        m_ref[...] = jnp.zeros_like(m_ref)

    xn, prepare_next = _next_input_norm(i, x0_ref, xnext_ref, g_ref, xn_ref, xnn_ref)

    upper = lax.broadcasted_iota(jnp.int32, (L, L), 0) <= lax.broadcasted_iota(jnp.int32, (L, L), 1)
    triu = jnp.where(upper, 1.0, 0.0).astype(bf16)
    lane_h = lax.broadcasted_iota(jnp.int32, (H, L), 1)
    lane = lax.broadcasted_iota(jnp.int32, (L, LANES), 1)
    low = lane < dk
    ones_rows = jnp.ones((BF16_ROWS, L), bf16)
    zeros_half = jnp.zeros((dk, L), bf16)

    def projection_steps(xg):
        res = {}
        piece = lambda r0, r1: _dot_nt(wt_ref[r0:r1, :], xg)
        o0 = 2 * nq + nv

        def gates_and_o_hi():
            p = piece(o0 + nv // 2, o0 + nv + 2 * H)
            res["o_hi"] = _sigmoid(p[0:nv // 2, :]).astype(bf16)
            raw = p[nv // 2:, :] + _tile_lanes(gb_ref[...], G // LANES)
            capped = GATE_CAP * jnp.tanh(raw * (1.0 / GATE_CAP))
            res["grow"] = jnp.where(lax.broadcasted_iota(jnp.int32, raw.shape, 0) < H, capped, _log_sigmoid(capped))

        def o_lo():
            res["o_lo"] = _sigmoid(piece(o0, o0 + nv // 2)).astype(bf16)

        def k_and_stats():
            res["k"] = piece(nq, 2 * nq).T.astype(bf16)
            grow = res["grow"]
            stacked = jnp.concatenate([grow[:, c * L:(c + 1) * L] for c in range(cpg)], axis=0)
            sums = _dot(jnp.concatenate(_split3(stacked), axis=0), triu)
            n = cpg * 2 * H
            cum = sums[0:n] + sums[n:2 * n] + sums[2 * n:]
            stats = []
            for c in range(cpg):
                b_r = cum[c * 2 * H + H:(c + 1) * 2 * H, :]
                c_r = grow[0:H, c * L:(c + 1) * L] - b_r
                cmax = c_r
                shift = 1
                while shift < L:
                    cmax = jnp.where(lane_h >= shift, jnp.maximum(cmax, pltpu.roll(cmax, shift, axis=1)), cmax)
                    shift *= 2
                b_last = jnp.broadcast_to(b_r[:, L - 1:L], (H, L))
                tail = c_r + b_last
                tail_max = jnp.broadcast_to(jnp.max(tail, axis=-1, keepdims=True), (H, L))
                c_cols = jnp.concatenate([c_r * LOG2E, jnp.zeros((LANES - H, L), f32)], axis=0).T
                stats.append((b_r, cmax, tail, tail_max, b_last, c_cols))
            res["stats"] = stats

        def v_lo():
            res["v_lo"] = piece(2 * nq, 2 * nq + nv // 2).astype(bf16)

        def v_hi():
            res["v_hi"] = piece(2 * nq + nv // 2, 2 * nq + nv).astype(bf16)

        def q():
            res["q"] = (piece(0, nq) * (dk ** -0.5)).astype(bf16)

        return res, [gates_and_o_hi, o_lo, k_and_stats, v_lo, v_hi, q]

    def finalize(res):
        res["ot"] = jnp.concatenate([res.pop("o_lo"), res.pop("o_hi")], axis=0)
        res["vt"] = jnp.concatenate([res.pop("v_lo"), res.pop("v_hi")], axis=0)
        return res

    def kq(p, c, j):
        tok = slice(c * L, (c + 1) * L)
        kp = p["k"][tok, j * LANES:(j + 1) * LANES]
        qtp = p["q"][j * LANES:(j + 1) * LANES, tok]
        q_even = jnp.concatenate([qtp[0:dk, :], zeros_half], axis=0)
        q_odd = jnp.concatenate([zeros_half, qtp[dk:2 * dk, :]], axis=0)
        return kp, (q_even, q_odd), _dot(kp, jnp.concatenate([q_even, q_odd], axis=1))

    def scan_unit(p, g, c, j, m_prev, m_new, ct, kq_res):
        kp, q_eo, st = kq_res
        tok = slice(c * L, (c + 1) * L)
        out_tok = slice(g * G + c * L, g * G + (c + 1) * L)
        b_r, cmax, tail, _, b_last, c_cols = p["stats"][c]
        a_r = jnp.maximum(m_prev, cmax)
        a_r2 = a_r * LOG2E
        carry_w = jnp.exp(m_prev - a_r).astype(bf16)
        clamp_r = jnp.exp(-(a_r + b_r))
        ws_r = jnp.exp(tail - m_new).astype(bf16)
        decay = jnp.exp(b_last + m_prev - m_new)
        ct_pair_b = ct[j].astype(bf16)
        vws = []
        for e, q_e in enumerate(q_eo):
            h = 2 * j + e
            expo = jnp.where(upper, c_cols[:, h:h + 1] - a_r2[h:h + 1, :], NEG_BIG)
            wt = (st[:, e * L:(e + 1) * L] * jnp.exp2(expo)).astype(bf16)
            q_w = q_e * carry_w[h:h + 1, :]
            vext = jnp.concatenate([p["vt"][h * dv:(h + 1) * dv, tok], ones_rows], axis=0)
            nd = _dot(jnp.concatenate([vext, ct_pair_b], axis=1), jnp.concatenate([wt, q_w], axis=0))
            den = jnp.maximum(jnp.abs(nd[dv:dv + 1, :]), clamp_r[h:h + 1, :])
            inv = 1.0 / den
            num = nd[0:dv, :]
            scale = inv * lax.rsqrt(inv * inv * jnp.mean(num * num, axis=0, keepdims=True) + EPS)
            hn = num * scale * hn_ref[h * dv:(h + 1) * dv, :]
            gate_o = p["ot"][h * dv:(h + 1) * dv, tok].astype(f32)
            out_ref[h * dv:(h + 1) * dv, out_tok] = (gate_o * hn).astype(bf16)
            vws.append(vext * ws_r[h:h + 1, :])
        k_split = jnp.concatenate([jnp.where(low, kp, jnp.zeros_like(kp)),
                                   jnp.where(low, jnp.zeros_like(kp), kp)], axis=0)
        decay_pair = jnp.where(low[0:1, :], decay[2 * j:2 * j + 1, :], decay[2 * j + 1:2 * j + 2, :])
        ct[j] = decay_pair * ct[j] + _dot(jnp.concatenate(vws, axis=1), k_split)

    def store_first_group(res):
        q0_ref[...], k0_ref[...], vt0_ref[...], ot0_ref[...] = res["q"], res["k"], res["vt"], res["ot"]
        for c, stat in enumerate(res["stats"]):
            stat0_ref[c * 5 * H:(c + 1) * 5 * H, :] = jnp.concatenate(stat[0:5], axis=0)
            ccol0_ref[c * L:(c + 1) * L, :] = stat[5]

    def load_first_group():
        stats = [tuple(stat0_ref[(c * 5 + s) * H:(c * 5 + s + 1) * H, :] for s in range(5))
                 + (ccol0_ref[c * L:(c + 1) * L, :],) for c in range(cpg)]
        return {"q": q0_ref[...], "k": k0_ref[...], "vt": vt0_ref[...], "ot": ot0_ref[...], "stats": stats}

    @pl.when(i == 0)
    def _():
        res, first_steps = projection_steps(xn_ref[0:G, :])
        for step in first_steps:
            step()
        store_first_group(finalize(res))

    n_groups = tm // G
    cur = load_first_group()

    ct = [ct_ref[j] for j in range(pairs)]
    m = m_ref[...]
    units = [(g, c, j) for g in range(n_groups) for c in range(cpg) for j in range(pairs)]
    per_group = cpg * pairs
    projected = {0: cur}
    kq_queue = {}

    def queue_kq(t):
        g, c, j = units[t]
        kq_queue[t] = kq(projected[g], c, j)

    for t in range(MIXER_LOOKAHEAD):
        queue_kq(t)
    for t, (g, c, j) in enumerate(units):
        u = t % per_group
        if u == 0:
            nxt, steps = projection_steps(xn[(g + 1) * G:(g + 2) * G, :] if g + 1 < n_groups else xnn_ref[0:G, :])
            stats = projected[g]["stats"]
            m_prevs = []
            for cc in range(cpg):
                m_prevs.append(m)
                m = jnp.maximum(stats[cc][4] + m, stats[cc][3])
            m_prevs.append(m)
        if t == 0:
            prepare_next()
        if u < len(steps):
            steps[u]()
            if u + 1 == len(steps):
                if g + 1 < n_groups:
                    projected[g + 1] = finalize(nxt)
                else:
                    store_first_group(finalize(nxt))
        if t + MIXER_LOOKAHEAD < len(units):
            queue_kq(t + MIXER_LOOKAHEAD)
        scan_unit(projected[g], g, c, j, m_prevs[c], m_prevs[c + 1], ct, kq_queue.pop(t))
    m_ref[...] = m
    for j in range(pairs):
        ct_ref[j] = ct[j]
    wup_bf_ref[...] = wup_ref[...].astype(bf16)
    wdown_bf_ref[...] = wdown_ref[...].astype(bf16)


def _mlstm_mixer(x, gain, wt, gate_bias, hn_rep, w_up, w_down, seq):
    t, d = x.shape
    nq = M_HEADS * M_QK_DIM
    nv = M_HEADS * M_V_DIM
    tm = PROJ_TILE
    steps = t // tm
    slab = lambda w: pl.BlockSpec((w.shape[0], w.shape[1] // steps, w.shape[2]), lambda i: (0, i, 0))
    return pl.pallas_call(
        functools.partial(_mlstm_kernel, tiles_per_seq=seq // tm),
        grid=(steps,),
        in_specs=_proj_in_specs(tm, d, steps) + [
            _resident((1, d)), _resident(wt.shape), _resident(gate_bias.shape), _resident(hn_rep.shape),
            slab(w_up), slab(w_down)],
        out_specs=[pl.BlockSpec((nv, tm), lambda i: (0, i)), slab(w_up), slab(w_down)],
        out_shape=[jax.ShapeDtypeStruct((nv, t), bf16),
                   jax.ShapeDtypeStruct(w_up.shape, bf16), jax.ShapeDtypeStruct(w_down.shape, bf16)],
        scratch_shapes=[pltpu.VMEM((tm, d), bf16), pltpu.VMEM((tm, d), bf16),
                        pltpu.VMEM((M_HEADS // 2, M_V_DIM + BF16_ROWS, 2 * M_QK_DIM), f32),
                        pltpu.VMEM((M_HEADS, LANES), f32),
                        pltpu.VMEM((nq, MLSTM_GROUP), bf16), pltpu.VMEM((MLSTM_GROUP, nq), bf16),
                        pltpu.VMEM((nv, MLSTM_GROUP), bf16), pltpu.VMEM((nv, MLSTM_GROUP), bf16),
                        pltpu.VMEM((MLSTM_GROUP // M_CHUNK * 5 * M_HEADS, M_CHUNK), f32),
                        pltpu.VMEM((MLSTM_GROUP, LANES), f32)],
        compiler_params=_params(1),
        name="mlstm_mixer",
    )(x, x, gain, wt, gate_bias, hn_rep, w_up, w_down)


def _attn_kernel(x0_ref, xnext_ref, g_ref, wt_ref, b_ref, sink_ref, out_ref,
                 xn_ref, xnn_ref, kprev_ref, vtprev_ref, *, tiles_per_seq):
    tm = xnext_ref.shape[0]
    blk = A_BLOCK
    dh = A_HEAD_DIM
    nq = A_Q_HEADS * dh
    nkv = A_KV_HEADS * dh
    G = ATTN_GROUP
    bpg = G // blk
    i = pl.program_id(0)
    first = i % tiles_per_seq == 0

    @pl.when(i == 0)
    def _():
        kprev_ref[...] = jnp.zeros_like(kprev_ref)
        vtprev_ref[...] = jnp.zeros_like(vtprev_ref)

    xn, prepare_next = _next_input_norm(i, x0_ref, xnext_ref, g_ref, xn_ref, xnn_ref)

    ku = lax.broadcasted_iota(jnp.int32, (2 * blk, blk), 0)
    qi = lax.broadcasted_iota(jnp.int32, (2 * blk, blk), 1)
    diff = qi - (ku - blk)
    band = (diff >= 0) & (diff < WINDOW)
    bias = jnp.where(band | (ku == 0), 0.0, NEG_BIG)
    bias_first = jnp.where((band & (ku >= blk)) | (ku == 0), 0.0, NEG_BIG)
    krow = lax.broadcasted_iota(jnp.int32, (2 * blk, LANES), 0)
    klane = lax.broadcasted_iota(jnp.int32, (2 * blk, LANES), 1)
    k_aug = jnp.where((krow == 0) & (klane < 3), 1.0, 0.0).astype(bf16)
    vcol = lax.broadcasted_iota(jnp.int32, (dh, 2 * blk), 1)
    ones_rows = jnp.ones((BF16_ROWS, 2 * blk), bf16)
    hp_lanes = 2 * blk

    def projection_steps(xg):
        res = {"q": {}}
        piece = lambda r0, r1: _dot_nt(wt_ref[r0:r1, :], xg) + _tile_lanes(b_ref[r0:r1, :], G // LANES)

        def kv():
            p = piece(nq, nq + 2 * nkv)
            res["k"] = p[0:nkv, :].T.astype(bf16)
            res["vt"] = p[nkv:2 * nkv, :].astype(bf16)

        def q_rows(r0):
            def step():
                res["q"][r0] = (piece(r0, r0 + Q_ROWS_PER_DOT) * (dh ** -0.5 * LOG2E)).astype(bf16)
            return step

        return res, [kv] + [q_rows(r0) for r0 in range(0, nq, Q_ROWS_PER_DOT)]

    def finalize(res):
        res["qt"] = jnp.concatenate([res["q"][r0] for r0 in sorted(res["q"])], axis=0)
        return res

    projected = {}
    shared = {}

    def operands(g, bl, grp):
        if (g, bl, grp) not in shared:
            cols = slice(bl * blk, (bl + 1) * blk)
            cur = projected[g]
            if bl > 0:
                k_prev, vt_prev = cur["k"][(bl - 1) * blk:bl * blk, :], cur["vt"][:, (bl - 1) * blk:bl * blk]
            elif g > 0:
                k_prev, vt_prev = projected[g - 1]["k"][G - blk:G, :], projected[g - 1]["vt"][:, G - blk:G]
            else:
                k_prev, vt_prev = kprev_ref[...], vtprev_ref[...]
            b1 = jnp.where(first, bias_first, bias) if (g == 0 and bl == 0) else bias
            kcat = jnp.concatenate([k_prev, cur["k"][cols, :]], axis=0)
            vtcat = jnp.concatenate([vt_prev, cur["vt"][:, cols]], axis=1)
            in_group = (klane >= grp * dh) & (klane < (grp + 1) * dh) & (krow > 0)
            km = jnp.concatenate([jnp.where(in_group, kcat, jnp.zeros_like(kcat)), k_aug], axis=1)
            vt_g = jnp.where(vcol == 0, jnp.zeros((dh, 2 * blk), bf16), vtcat[grp * dh:(grp + 1) * dh, :])
            shared[(g, bl, grp)] = km, jnp.concatenate([vt_g, ones_rows], axis=0), _tile_lanes(b1, 2)
        return shared[(g, bl, grp)]

    def logits(g, bl, grp, hp):
        cols = slice(bl * blk, (bl + 1) * blk)
        km, _, b2 = operands(g, bl, grp)
        blocks = []
        for e in range(2):
            h = grp * A_GROUP + 2 * hp + e
            pair = projected[g]["qt"][(h // 2) * LANES:(h // 2 + 1) * LANES, cols]
            if h % 2 != grp:
                pair = jnp.concatenate([pair[dh:2 * dh, :], pair[0:dh, :]], axis=0)
            blocks.append(pair)
        sink_rows = sink_ref[grp][:, hp * hp_lanes:(hp + 1) * hp_lanes]
        rhs = jnp.concatenate([jnp.concatenate(blocks, axis=1), sink_rows], axis=0)
        return _dot(km, rhs) + b2

    def finish(g, bl, grp, hp, st):
        out_cols = slice(g * G + bl * blk, g * G + (bl + 1) * blk)
        _, vext, _ = operands(g, bl, grp)
        p = jnp.exp2(st - jnp.max(st, axis=0, keepdims=True)).astype(bf16)
        oext = _dot(vext, p)
        o = (oext[0:dh, :] * (1.0 / oext[dh:dh + 1, :])).astype(bf16)
        for e in range(2):
            h = grp * A_GROUP + 2 * hp + e
            out_ref[h * dh:(h + 1) * dh, out_cols] = o[:, e * blk:(e + 1) * blk]

    n_groups = tm // G
    res, steps = projection_steps(xn[0:G, :])
    for k_step, step in enumerate(steps):
        step()
        if k_step == 0:
            prepare_next()
    projected[0] = finalize(res)

    units = [(g, bl, grp, hp) for g in range(n_groups) for bl in range(bpg)
             for grp in range(A_KV_HEADS) for hp in range(A_GROUP // 2)]
    per_group = bpg * A_KV_HEADS * (A_GROUP // 2)
    queued = {}
    for t in range(SWA_LOOKAHEAD):
        queued[t] = logits(*units[t])
    for t, unit in enumerate(units):
        g = unit[0]
        u = t % per_group
        if u == 0:
            nxt, steps = projection_steps(xn[(g + 1) * G:(g + 2) * G, :]) if g + 1 < n_groups else (None, [])
        if u < len(steps):
            steps[u]()
            if u + 1 == len(steps):
                projected[g + 1] = finalize(nxt)
        if t + SWA_LOOKAHEAD < len(units):
            queued[t + SWA_LOOKAHEAD] = logits(*units[t + SWA_LOOKAHEAD])
        finish(*unit, queued.pop(t))
    last = projected[n_groups - 1]
    kprev_ref[...] = last["k"][G - blk:G, :]
    vtprev_ref[...] = last["vt"][:, G - blk:G]


def _attn_mixer(x, gain, wt, b_rep, sink_aug, seq):
    t, d = x.shape
    nq = A_Q_HEADS * A_HEAD_DIM
    nkv = A_KV_HEADS * A_HEAD_DIM
    tm = PROJ_TILE
    steps = t // tm
    return pl.pallas_call(
        functools.partial(_attn_kernel, tiles_per_seq=seq // tm),
        grid=(steps,),
        in_specs=_proj_in_specs(tm, d, steps) + [
            _resident((1, d)), _resident(wt.shape), _resident(b_rep.shape), _resident(sink_aug.shape)],
        out_specs=pl.BlockSpec((nq, tm), lambda i: (0, i)),
        out_shape=jax.ShapeDtypeStruct((nq, t), bf16),
        scratch_shapes=[pltpu.VMEM((tm, d), bf16), pltpu.VMEM((tm, d), bf16),
                        pltpu.VMEM((A_BLOCK, nkv), bf16), pltpu.VMEM((nkv, A_BLOCK), bf16)],
        compiler_params=_params(1),
        name="attn_mixer",
    )(x, x, gain, wt, b_rep, sink_aug)


def _sink_rows(sinks):
    parts = jnp.stack(_split3(sinks.astype(f32) * LOG2E), axis=0)
    per_lane = jnp.repeat(parts.reshape(3, A_KV_HEADS, A_GROUP), A_BLOCK, axis=2)
    return jnp.pad(per_lane.transpose(1, 0, 2), ((0, 0), (0, LANES - 3), (0, 0)))


def _out_ffn_kernel(h0_ref, at0_ref, hn_ref, atn_ref, wo_ref, wup_ref, wd_ref, vec_ref, conv_ref,
                    out_ref, act_ref, h1_ref, xn_ref, h1n_ref, xnn_ref, carry_ref, *, tiles_per_seq):
    tm = hn_ref.shape[0]
    dff = wd_ref.shape[0]
    i = pl.program_id(0)

    d = wo_ref.shape[1]
    out_cols = [slice(n0, n0 + OUT_CHUNK) for n0 in range(0, d, OUT_CHUNK)]

    def out_proj(at_ref, cols):
        return _dot_tn(at_ref[...], wo_ref[:, cols]) + vec_ref[0:1, cols]

    def front(h_ref, z_parts):
        h1 = h_ref[...] + _rmsnorm(jnp.concatenate(z_parts, axis=1), vec_ref[1:2, :])
        h1n_ref[...] = h1
        xnn_ref[...] = _rmsnorm(h1, vec_ref[2:3, :]).astype(bf16)

    @pl.when(i == 0)
    def _():
        front(h0_ref, [out_proj(at0_ref, cols) for cols in out_cols])

    @pl.when(i % tiles_per_seq == 0)
    def _():
        carry_ref[...] = jnp.zeros_like(carry_ref)

    xn_ref[...] = xnn_ref[...]
    h1_ref[...] = h1n_ref[...]
    top = lax.broadcasted_iota(jnp.int32, (8, FF_CHUNK), 0)
    z_parts = []
    for ci, c0 in enumerate(range(0, dff, FF_CHUNK)):
        if FRONT_AFTER_CHUNKS <= ci < FRONT_AFTER_CHUNKS + len(out_cols):
            z_parts.append(out_proj(atn_ref, out_cols[ci - FRONT_AFTER_CHUNKS]))
        if ci == FRONT_AFTER_CHUNKS + len(out_cols):
            front(hn_ref, z_parts)
        cs = slice(c0, c0 + FF_CHUNK)
        gate = _dot(xn_ref[...], wup_ref[:, cs])
        val = _dot(xn_ref[...], wup_ref[:, dff + c0:dff + c0 + FF_CHUNK])
        prev = carry_ref[0:8, cs]
        carry_ref[0:8, cs] = gate[tm - 8:tm, :]
        shifted = []
        for lag in (1, 2):
            rolled = pltpu.roll(gate, lag, axis=0)
            head = jnp.where(top < lag, pltpu.roll(prev, lag, axis=0), rolled[0:8, :])
            shifted.append(jnp.concatenate([head, rolled[8:, :]], axis=0))
        g1, g2 = shifted
        gc = conv_ref[3:4, cs] + conv_ref[0:1, cs] * g2 + conv_ref[1:2, cs] * g1 + conv_ref[2:3, cs] * gate
        act_ref[:, cs] = (gc * jax.nn.sigmoid(gc) * val).astype(bf16)
    y = _dot(act_ref[...], wd_ref[...])
    out_ref[...] = h1_ref[...] + _rmsnorm(y, vec_ref[3:4, :])


def _out_ffn(h, at, wo, wup, wd, vecs, conv, layer, seq):
    t, d = h.shape
    tm = TOKEN_TILE
    n = t // tm
    dff = wd.shape[1]
    nxt = lambda i: jnp.minimum(i + 1, n - 1)
    return pl.pallas_call(
        functools.partial(_out_ffn_kernel, tiles_per_seq=seq // tm),
        grid=(n,),
        in_specs=[pl.BlockSpec((tm, d), lambda i: (0, 0), pipeline_mode=pl.Buffered(1)),
                  pl.BlockSpec((at.shape[0], tm), lambda i: (0, 0), pipeline_mode=pl.Buffered(1)),
                  pl.BlockSpec((tm, d), lambda i: (nxt(i), 0)),
                  pl.BlockSpec((at.shape[0], tm), lambda i: (0, nxt(i))),
                  _resident(wo.shape), _layer_resident(wup.shape, layer), _layer_resident(wd.shape, layer),
                  _resident(vecs.shape), _resident(conv.shape)],
        out_specs=pl.BlockSpec((tm, d), lambda i: (i, 0)),
        out_shape=jax.ShapeDtypeStruct((t, d), f32),
        scratch_shapes=[pltpu.VMEM((tm, dff), bf16),
                        pltpu.VMEM((tm, d), f32), pltpu.VMEM((tm, d), bf16),
                        pltpu.VMEM((tm, d), f32), pltpu.VMEM((tm, d), bf16), pltpu.VMEM((16, dff), f32)],
        compiler_params=_params(1),
        name="out_ffn",
    )(h, at, h, at, wo, wup, wd, vecs, conv)


def kernel(x, m_w_in, m_gate_bias, m_head_norm, m_w_out, a_w_in, a_b_in, a_sinks, a_w_out, a_b_out,
           norm_mix_pre, norm_mix_post, norm_ffn_pre, norm_ffn_post, f_w_up, f_conv_w, f_conv_b, f_w_down):
    batch, seq, d = x.shape
    depth = norm_mix_pre.shape[0]
    h = x.reshape(batch * seq, d)
    row = lambda vec: vec.reshape(1, -1).astype(f32)
    lane_rep = lambda vec: jnp.broadcast_to(vec.astype(f32)[:, None], (vec.shape[0], LANES))

    for i in range(depth):
        j = i // 2
        if i % 2 == 0:
            mixed_t, w_up, w_down = _mlstm_mixer(
                h, row(norm_mix_pre[i]), m_w_in[j].T.astype(bf16), lane_rep(m_gate_bias[j].reshape(-1)),
                lane_rep(m_head_norm[j]), f_w_up, f_w_down, seq)
            w_out = m_w_out[j].astype(bf16)
            b_out = jnp.zeros((d,), f32)
        else:
            mixed_t = _attn_mixer(h, row(norm_mix_pre[i]), a_w_in[j].T.astype(bf16), lane_rep(a_b_in[j]),
                                  _sink_rows(a_sinks[j]), seq)
            w_out = a_w_out[j].astype(bf16)
            b_out = a_b_out[j]
        vecs = jnp.pad(jnp.stack([b_out, norm_mix_post[i], norm_ffn_pre[i], norm_ffn_post[i]]).astype(f32),
                       ((0, 4), (0, 0)))
        conv = jnp.pad(jnp.concatenate([f_conv_w[i], f_conv_b[i][None]], axis=0).astype(f32), ((0, 12), (0, 0)))
        h = _out_ffn(h, mixed_t, w_out, w_up, w_down, vecs, conv, i, seq)
    return h.reshape(batch, seq, d)
```

```python
import functools

import jax
import jax.numpy as jnp
from jax import lax
from jax.experimental import pallas as pl
from jax.experimental.pallas import tpu as pltpu

EPS = 1e-6
LANES = 128
BF16_ROWS = 16

M_HEADS = 8
M_QK_DIM = 64
M_V_DIM = 128
GATE_CAP = 15.0
M_CHUNK = 128
MLSTM_GROUP = 256
MIXER_LOOKAHEAD = 2

A_HEAD_DIM = 64
A_Q_HEADS = 16
A_KV_HEADS = 2
A_GROUP = A_Q_HEADS // A_KV_HEADS
WINDOW = 128
A_BLOCK = 128

LOG2E = 1.4426950408889634
NEG_BIG = -1e30

TOKEN_TILE = 512
PROJ_TILE = 1024
ATTN_TILE = 2048
ATTN_GROUP = 256
SWA_LOOKAHEAD = 4
FF_CHUNK = 256
Q_ROWS_PER_DOT = 256
OUT_CHUNK = 256
FRONT_AFTER_CHUNKS = 2
VMEM_LIMIT = 60 * 1024 * 1024

bf16 = jnp.bfloat16
f32 = jnp.float32


def _dot(a, b):
    return jnp.dot(a, b, preferred_element_type=f32)


def _dot_nt(a, b):
    return lax.dot_general(a, b, (((1,), (1,)), ((), ())), preferred_element_type=f32)


def _dot_tn(a, b):
    return lax.dot_general(a, b, (((0,), (0,)), ((), ())), preferred_element_type=f32)


def _rmsnorm(x, g):
    return x * lax.rsqrt(jnp.mean(x * x, axis=-1, keepdims=True) + EPS) * g


def _split3(x):
    hi = x.astype(bf16)
    r1 = x - hi.astype(f32)
    mid = r1.astype(bf16)
    lo = (r1 - mid.astype(f32)).astype(bf16)
    return hi, mid, lo


def _log_sigmoid(x):
    return jnp.minimum(x, 0.0) - jnp.log1p(jnp.exp(-jnp.abs(x)))


def _tile_lanes(x, reps):
    return jnp.concatenate([x] * reps, axis=1)


def _next_input_norm(i, x0_ref, xnext_ref, g_ref, xn_ref, xnn_ref):
    @pl.when(i == 0)
    def _():
        xnn_ref[...] = _rmsnorm(x0_ref[...], g_ref[...]).astype(bf16)

    xn_ref[...] = xnn_ref[...]

    def prepare_next():
        xnn_ref[...] = _rmsnorm(xnext_ref[...], g_ref[...]).astype(bf16)

    return xn_ref[...], prepare_next


def _sigmoid(x):
    return 0.5 * jnp.tanh(0.5 * x) + 0.5


def _proj_in_specs(tm, d, n_tiles):
    return [pl.BlockSpec((tm, d), lambda i: (0, 0), pipeline_mode=pl.Buffered(1)),
            pl.BlockSpec((tm, d), lambda i: (jnp.minimum(i + 1, n_tiles - 1), 0))]


def _resident(shape):
    nd = len(shape)
    return pl.BlockSpec(shape, lambda *_: (0,) * nd, pipeline_mode=pl.Buffered(1))


def _layer_resident(shape, layer):
    nd = len(shape) - 1
    return pl.BlockSpec((None,) + tuple(shape[1:]), lambda *_: (layer,) + (0,) * nd, pipeline_mode=pl.Buffered(1))


def _params(n_axes):
    return pltpu.CompilerParams(
        dimension_semantics=("arbitrary",) * n_axes, vmem_limit_bytes=VMEM_LIMIT)


def _mlstm_kernel(x0_ref, xnext_ref, g_ref, wt_ref, gb_ref, hn_ref, wup_ref, wdown_ref,
                  out_ref, wup_bf_ref, wdown_bf_ref, xn_ref, xnn_ref, ct_ref, m_ref,
                  q0_ref, k0_ref, vt0_ref, ot0_ref, stat0_ref, ccol0_ref, *, tiles_per_seq):
    tm = xnext_ref.shape[0]
    L = M_CHUNK
    H = M_HEADS
    dk, dv = M_QK_DIM, M_V_DIM
    nq = H * dk
    nv = H * dv
    pairs = H // 2
    G = MLSTM_GROUP
    cpg = G // L
    i = pl.program_id(0)

    @pl.when(i % tiles_per_seq == 0)
    def _():
        ct_ref[...] = jnp.zeros_like(ct_ref)
        m_ref[...] = jnp.zeros_like(m_ref)

    xn, prepare_next = _next_input_norm(i, x0_ref, xnext_ref, g_ref, xn_ref, xnn_ref)

    upper = lax.broadcasted_iota(jnp.int32, (L, L), 0) <= lax.broadcasted_iota(jnp.int32, (L, L), 1)
    triu = jnp.where(upper, 1.0, 0.0).astype(bf16)
    lane_h = lax.broadcasted_iota(jnp.int32, (H, L), 1)
    lane = lax.broadcasted_iota(jnp.int32, (L, LANES), 1)
    low = lane < dk
    ones_rows = jnp.ones((BF16_ROWS, L), bf16)
    zeros_half = jnp.zeros((dk, L), bf16)

    def projection_steps(xg):
        res = {}
        piece = lambda r0, r1: _dot_nt(wt_ref[r0:r1, :], xg)
        o0 = 2 * nq + nv

        def gates_and_o_hi():
            p = piece(o0 + nv // 2, o0 + nv + 2 * H)
            res["o_hi"] = _sigmoid(p[0:nv // 2, :]).astype(bf16)
            raw = p[nv // 2:, :] + _tile_lanes(gb_ref[...], G // LANES)
            capped = GATE_CAP * jnp.tanh(raw * (1.0 / GATE_CAP))
            res["grow"] = jnp.where(lax.broadcasted_iota(jnp.int32, raw.shape, 0) < H, capped, _log_sigmoid(capped))

        def o_lo():
            res["o_lo"] = _sigmoid(piece(o0, o0 + nv // 2)).astype(bf16)

        def k_and_stats():
            res["k"] = piece(nq, 2 * nq).T.astype(bf16)
            grow = res["grow"]
            stacked = jnp.concatenate([grow[:, c * L:(c + 1) * L] for c in range(cpg)], axis=0)
            sums = _dot(jnp.concatenate(_split3(stacked), axis=0), triu)
            n = cpg * 2 * H
            cum = sums[0:n] + sums[n:2 * n] + sums[2 * n:]
            stats = []
            for c in range(cpg):
                b_r = cum[c * 2 * H + H:(c + 1) * 2 * H, :]
                c_r = grow[0:H, c * L:(c + 1) * L] - b_r
                cmax = c_r
                shift = 1
                while shift < L:
                    cmax = jnp.where(lane_h >= shift, jnp.maximum(cmax, pltpu.roll(cmax, shift, axis=1)), cmax)
                    shift *= 2
                b_last = jnp.broadcast_to(b_r[:, L - 1:L], (H, L))
                tail = c_r + b_last
                tail_max = jnp.broadcast_to(jnp.max(tail, axis=-1, keepdims=True), (H, L))
                c_cols = jnp.concatenate([c_r * LOG2E, jnp.zeros((LANES - H, L), f32)], axis=0).T
                stats.append((b_r, cmax, tail, tail_max, b_last, c_cols))
            res["stats"] = stats

        def v_lo():
            res["v_lo"] = piece(2 * nq, 2 * nq + nv // 2).astype(bf16)

        def v_hi():
            res["v_hi"] = piece(2 * nq + nv // 2, 2 * nq + nv).astype(bf16)

        def q():
            res["q"] = (piece(0, nq) * (dk ** -0.5)).astype(bf16)

        return res, [gates_and_o_hi, o_lo, k_and_stats, v_lo, v_hi, q]

    def finalize(res):
        res["ot"] = jnp.concatenate([res.pop("o_lo"), res.pop("o_hi")], axis=0)
        res["vt"] = jnp.concatenate([res.pop("v_lo"), res.pop("v_hi")], axis=0)
        return res

    def kq(p, c, j):
        tok = slice(c * L, (c + 1) * L)
        kp = p["k"][tok, j * LANES:(j + 1) * LANES]
        qtp = p["q"][j * LANES:(j + 1) * LANES, tok]
        q_even = jnp.concatenate([qtp[0:dk, :], zeros_half], axis=0)
        q_odd = jnp.concatenate([zeros_half, qtp[dk:2 * dk, :]], axis=0)
        return kp, (q_even, q_odd), _dot(kp, jnp.concatenate([q_even, q_odd], axis=1))

    def scan_unit(p, g, c, j, m_prev, m_new, ct, kq_res):
        kp, q_eo, st = kq_res
        tok = slice(c * L, (c + 1) * L)
        out_tok = slice(g * G + c * L, g * G + (c + 1) * L)
        b_r, cmax, tail, _, b_last, c_cols = p["stats"][c]
        a_r = jnp.maximum(m_prev, cmax)
        a_r2 = a_r * LOG2E
        carry_w = jnp.exp(m_prev - a_r).astype(bf16)
        clamp_r = jnp.exp(-(a_r + b_r))
        ws_r = jnp.exp(tail - m_new).astype(bf16)
        decay = jnp.exp(b_last + m_prev - m_new)
        ct_pair_b = ct[j].astype(bf16)
        vws = []
        for e, q_e in enumerate(q_eo):
            h = 2 * j + e
            expo = jnp.where(upper, c_cols[:, h:h + 1] - a_r2[h:h + 1, :], NEG_BIG)
            wt = (st[:, e * L:(e + 1) * L] * jnp.exp2(expo)).astype(bf16)
            q_w = q_e * carry_w[h:h + 1, :]
            vext = jnp.concatenate([p["vt"][h * dv:(h + 1) * dv, tok], ones_rows], axis=0)
            nd = _dot(jnp.concatenate([vext, ct_pair_b], axis=1), jnp.concatenate([wt, q_w], axis=0))
            den = jnp.maximum(jnp.abs(nd[dv:dv + 1, :]), clamp_r[h:h + 1, :])
            inv = 1.0 / den
            num = nd[0:dv, :]
            scale = inv * lax.rsqrt(inv * inv * jnp.mean(num * num, axis=0, keepdims=True) + EPS)
            hn = num * scale * hn_ref[h * dv:(h + 1) * dv, :]
            gate_o = p["ot"][h * dv:(h + 1) * dv, tok].astype(f32)
            out_ref[h * dv:(h + 1) * dv, out_tok] = (gate_o * hn).astype(bf16)
            vws.append(vext * ws_r[h:h + 1, :])
        k_split = jnp.concatenate([jnp.where(low, kp, jnp.zeros_like(kp)),
                                   jnp.where(low, jnp.zeros_like(kp), kp)], axis=0)
        decay_pair = jnp.where(low[0:1, :], decay[2 * j:2 * j + 1, :], decay[2 * j + 1:2 * j + 2, :])
        ct[j] = decay_pair * ct[j] + _dot(jnp.concatenate(vws, axis=1), k_split)

    def store_first_group(res):
        q0_ref[...], k0_ref[...], vt0_ref[...], ot0_ref[...] = res["q"], res["k"], res["vt"], res["ot"]
        for c, stat in enumerate(res["stats"]):
            stat0_ref[c * 5 * H:(c + 1) * 5 * H, :] = jnp.concatenate(stat[0:5], axis=0)
            ccol0_ref[c * L:(c + 1) * L, :] = stat[5]

    def load_first_group():
        stats = [tuple(stat0_ref[(c * 5 + s) * H:(c * 5 + s + 1) * H, :] for s in range(5))
                 + (ccol0_ref[c * L:(c + 1) * L, :],) for c in range(cpg)]
        return {"q": q0_ref[...], "k": k0_ref[...], "vt": vt0_ref[...], "ot": ot0_ref[...], "stats": stats}

    @pl.when(i == 0)
    def _():
        res, first_steps = projection_steps(xn_ref[0:G, :])
        for step in first_steps:
            step()
        store_first_group(finalize(res))

    n_groups = tm // G
    cur = load_first_group()

    ct = [ct_ref[j] for j in range(pairs)]
    m = m_ref[...]
    units = [(g, c, j) for g in range(n_groups) for c in range(cpg) for j in range(pairs)]
    per_group = cpg * pairs
    projected = {0: cur}
    kq_queue = {}

    def queue_kq(t):
        g, c, j = units[t]
        kq_queue[t] = kq(projected[g], c, j)

    for t in range(MIXER_LOOKAHEAD):
        queue_kq(t)
    for t, (g, c, j) in enumerate(units):
        u = t % per_group
        if u == 0:
            nxt, steps = projection_steps(xn[(g + 1) * G:(g + 2) * G, :] if g + 1 < n_groups else xnn_ref[0:G, :])
            stats = projected[g]["stats"]
            m_prevs = []
            for cc in range(cpg):
                m_prevs.append(m)
                m = jnp.maximum(stats[cc][4] + m, stats[cc][3])
            m_prevs.append(m)
        if t == 0:
            prepare_next()
        if u < len(steps):
            steps[u]()
            if u + 1 == len(steps):
                if g + 1 < n_groups:
                    projected[g + 1] = finalize(nxt)
                else:
                    store_first_group(finalize(nxt))
        if t + MIXER_LOOKAHEAD < len(units):
            queue_kq(t + MIXER_LOOKAHEAD)
        scan_unit(projected[g], g, c, j, m_prevs[c], m_prevs[c + 1], ct, kq_queue.pop(t))
    m_ref[...] = m
    for j in range(pairs):
        ct_ref[j] = ct[j]
    wup_bf_ref[...] = wup_ref[...].astype(bf16)
    wdown_bf_ref[...] = wdown_ref[...].astype(bf16)


def _mlstm_mixer(x, gain, wt, gate_bias, hn_rep, w_up, w_down, seq):
    t, d = x.shape
    nq = M_HEADS * M_QK_DIM
    nv = M_HEADS * M_V_DIM
    tm = PROJ_TILE
    steps = t // tm
    slab = lambda w: pl.BlockSpec((w.shape[0], w.shape[1] // steps, w.shape[2]), lambda i: (0, i, 0))
    return pl.pallas_call(
        functools.partial(_mlstm_kernel, tiles_per_seq=seq // tm),
        grid=(steps,),
        in_specs=_proj_in_specs(tm, d, steps) + [
            _resident((1, d)), _resident(wt.shape), _resident(gate_bias.shape), _resident(hn_rep.shape),
            slab(w_up), slab(w_down)],
        out_specs=[pl.BlockSpec((nv, tm), lambda i: (0, i)), slab(w_up), slab(w_down)],
        out_shape=[jax.ShapeDtypeStruct((nv, t), bf16),
                   jax.ShapeDtypeStruct(w_up.shape, bf16), jax.ShapeDtypeStruct(w_down.shape, bf16)],
        scratch_shapes=[pltpu.VMEM((tm, d), bf16), pltpu.VMEM((tm, d), bf16),
                        pltpu.VMEM((M_HEADS // 2, M_V_DIM + BF16_ROWS, 2 * M_QK_DIM), f32),
                        pltpu.VMEM((M_HEADS, LANES), f32),
                        pltpu.VMEM((nq, MLSTM_GROUP), bf16), pltpu.VMEM((MLSTM_GROUP, nq), bf16),
                        pltpu.VMEM((nv, MLSTM_GROUP), bf16), pltpu.VMEM((nv, MLSTM_GROUP), bf16),
                        pltpu.VMEM((MLSTM_GROUP // M_CHUNK * 5 * M_HEADS, M_CHUNK), f32),
                        pltpu.VMEM((MLSTM_GROUP, LANES), f32)],
        compiler_params=_params(1),
        name="mlstm_mixer",
    )(x, x, gain, wt, gate_bias, hn_rep, w_up, w_down)


def _attn_kernel(x0_ref, xnext_ref, g_ref, wt_ref, b_ref, sink_ref, out_ref,
                 xn_ref, xnn_ref, kprev_ref, vtprev_ref, *, tiles_per_seq):
    tm = xnext_ref.shape[0]
    blk = A_BLOCK
    dh = A_HEAD_DIM
    nq = A_Q_HEADS * dh
    nkv = A_KV_HEADS * dh
    G = ATTN_GROUP
    bpg = G // blk
    i = pl.program_id(0)
    first = i % tiles_per_seq == 0

    @pl.when(i == 0)
    def _():
        kprev_ref[...] = jnp.zeros_like(kprev_ref)
        vtprev_ref[...] = jnp.zeros_like(vtprev_ref)

    xn, prepare_next = _next_input_norm(i, x0_ref, xnext_ref, g_ref, xn_ref, xnn_ref)

    ku = lax.broadcasted_iota(jnp.int32, (2 * blk, blk), 0)
    qi = lax.broadcasted_iota(jnp.int32, (2 * blk, blk), 1)
    diff = qi - (ku - blk)
    band = (diff >= 0) & (diff < WINDOW)
    bias = jnp.where(band | (ku == 0), 0.0, NEG_BIG)
    bias_first = jnp.where((band & (ku >= blk)) | (ku == 0), 0.0, NEG_BIG)
    krow = lax.broadcasted_iota(jnp.int32, (2 * blk, LANES), 0)
    klane = lax.broadcasted_iota(jnp.int32, (2 * blk, LANES), 1)
    k_aug = jnp.where((krow == 0) & (klane < 3), 1.0, 0.0).astype(bf16)
    vcol = lax.broadcasted_iota(jnp.int32, (dh, 2 * blk), 1)
    ones_rows = jnp.ones((BF16_ROWS, 2 * blk), bf16)
    hp_lanes = 2 * blk

    def projection_steps(xg):
        res = {"q": {}}
        piece = lambda r0, r1: _dot_nt(wt_ref[r0:r1, :], xg) + _tile_lanes(b_ref[r0:r1, :], G // LANES)

        def kv():
            p = piece(nq, nq + 2 * nkv)
            res["k"] = p[0:nkv, :].T.astype(bf16)
            res["vt"] = p[nkv:2 * nkv, :].astype(bf16)

        def q_rows(r0):
            def step():
                res["q"][r0] = (piece(r0, r0 + Q_ROWS_PER_DOT) * (dh ** -0.5 * LOG2E)).astype(bf16)
            return step

        return res, [kv] + [q_rows(r0) for r0 in range(0, nq, Q_ROWS_PER_DOT)]

    def finalize(res):
        res["qt"] = jnp.concatenate([res["q"][r0] for r0 in sorted(res["q"])], axis=0)
        return res

    projected = {}
    shared = {}

    def operands(g, bl, grp):
        if (g, bl, grp) not in shared:
            cols = slice(bl * blk, (bl + 1) * blk)
            cur = projected[g]
            if bl > 0:
                k_prev, vt_prev = cur["k"][(bl - 1) * blk:bl * blk, :], cur["vt"][:, (bl - 1) * blk:bl * blk]
            elif g > 0:
                k_prev, vt_prev = projected[g - 1]["k"][G - blk:G, :], projected[g - 1]["vt"][:, G - blk:G]
            else:
                k_prev, vt_prev = kprev_ref[...], vtprev_ref[...]
            b1 = jnp.where(first, bias_first, bias) if (g == 0 and bl == 0) else bias
            kcat = jnp.concatenate([k_prev, cur["k"][cols, :]], axis=0)
            vtcat = jnp.concatenate([vt_prev, cur["vt"][:, cols]], axis=1)
            in_group = (klane >= grp * dh) & (klane < (grp + 1) * dh) & (krow > 0)
            km = jnp.concatenate([jnp.where(in_group, kcat, jnp.zeros_like(kcat)), k_aug], axis=1)
            vt_g = jnp.where(vcol == 0, jnp.zeros((dh, 2 * blk), bf16), vtcat[grp * dh:(grp + 1) * dh, :])
            shared[(g, bl, grp)] = km, jnp.concatenate([vt_g, ones_rows], axis=0), _tile_lanes(b1, 2)
        return shared[(g, bl, grp)]

    def logits(g, bl, grp, hp):
        cols = slice(bl * blk, (bl + 1) * blk)
        km, _, b2 = operands(g, bl, grp)
        blocks = []
        for e in range(2):
            h = grp * A_GROUP + 2 * hp + e
            pair = projected[g]["qt"][(h // 2) * LANES:(h // 2 + 1) * LANES, cols]
            if h % 2 != grp:
                pair = jnp.concatenate([pair[dh:2 * dh, :], pair[0:dh, :]], axis=0)
            blocks.append(pair)
        sink_rows = sink_ref[grp][:, hp * hp_lanes:(hp + 1) * hp_lanes]
        rhs = jnp.concatenate([jnp.concatenate(blocks, axis=1), sink_rows], axis=0)
        return _dot(km, rhs) + b2

    def finish(g, bl, grp, hp, st):
        out_cols = slice(g * G + bl * blk, g * G + (bl + 1) * blk)
        _, vext, _ = operands(g, bl, grp)
        p = jnp.exp2(st - jnp.max(st, axis=0, keepdims=True)).astype(bf16)
        oext = _dot(vext, p)
        o = (oext[0:dh, :] * (1.0 / oext[dh:dh + 1, :])).astype(bf16)
        for e in range(2):
            h = grp * A_GROUP + 2 * hp + e
            out_ref[h * dh:(h + 1) * dh, out_cols] = o[:, e * blk:(e + 1) * blk]

    n_groups = tm // G
    res, steps = projection_steps(xn[0:G, :])
    for k_step, step in enumerate(steps):
        step()
        if k_step == 0:
            prepare_next()
    projected[0] = finalize(res)

    units = [(g, bl, grp, hp) for g in range(n_groups) for bl in range(bpg)
             for grp in range(A_KV_HEADS) for hp in range(A_GROUP // 2)]
    per_group = bpg * A_KV_HEADS * (A_GROUP // 2)
    queued = {}
    for t in range(SWA_LOOKAHEAD):
        queued[t] = logits(*units[t])
    for t, unit in enumerate(units):
        g = unit[0]
        u = t % per_group
        if u == 0:
            nxt, steps = projection_steps(xn[(g + 1) * G:(g + 2) * G, :]) if g + 1 < n_groups else (None, [])
        if u < len(steps):
            steps[u]()
            if u + 1 == len(steps):
                projected[g + 1] = finalize(nxt)
        if t + SWA_LOOKAHEAD < len(units):
            queued[t + SWA_LOOKAHEAD] = logits(*units[t + SWA_LOOKAHEAD])
        finish(*unit, queued.pop(t))
    last = projected[n_groups - 1]
    kprev_ref[...] = last["k"][G - blk:G, :]
    vtprev_ref[...] = last["vt"][:, G - blk:G]


def _attn_mixer(x, gain, wt, b_rep, sink_aug, seq):
    t, d = x.shape
    nq = A_Q_HEADS * A_HEAD_DIM
    nkv = A_KV_HEADS * A_HEAD_DIM
    tm = ATTN_TILE
    steps = t // tm
    return pl.pallas_call(
        functools.partial(_attn_kernel, tiles_per_seq=seq // tm),
        grid=(steps,),
        in_specs=_proj_in_specs(tm, d, steps) + [
            _resident((1, d)), _resident(wt.shape), _resident(b_rep.shape), _resident(sink_aug.shape)],
        out_specs=pl.BlockSpec((nq, tm), lambda i: (0, i)),
        out_shape=jax.ShapeDtypeStruct((nq, t), bf16),
        scratch_shapes=[pltpu.VMEM((tm, d), bf16), pltpu.VMEM((tm, d), bf16),
                        pltpu.VMEM((A_BLOCK, nkv), bf16), pltpu.VMEM((nkv, A_BLOCK), bf16)],
        compiler_params=_params(1),
        name="attn_mixer",
    )(x, x, gain, wt, b_rep, sink_aug)


def _sink_rows(sinks):
    parts = jnp.stack(_split3(sinks.astype(f32) * LOG2E), axis=0)
    per_lane = jnp.repeat(parts.reshape(3, A_KV_HEADS, A_GROUP), A_BLOCK, axis=2)
    return jnp.pad(per_lane.transpose(1, 0, 2), ((0, 0), (0, LANES - 3), (0, 0)))


def _out_ffn_kernel(h0_ref, at0_ref, hn_ref, atn_ref, wo_ref, wup_ref, wd_ref, vec_ref, conv_ref,
                    out_ref, act_ref, h1_ref, xn_ref, h1n_ref, xnn_ref, carry_ref, *, tiles_per_seq):
    tm = hn_ref.shape[0]
    dff = wd_ref.shape[0]
    i = pl.program_id(0)

    d = wo_ref.shape[1]
    out_cols = [slice(n0, n0 + OUT_CHUNK) for n0 in range(0, d, OUT_CHUNK)]

    def out_proj(at_ref, cols):
        return _dot_tn(at_ref[...], wo_ref[:, cols]) + vec_ref[0:1, cols]

    def front(h_ref, z_parts):
        h1 = h_ref[...] + _rmsnorm(jnp.concatenate(z_parts, axis=1), vec_ref[1:2, :])
        h1n_ref[...] = h1
        xnn_ref[...] = _rmsnorm(h1, vec_ref[2:3, :]).astype(bf16)

    @pl.when(i == 0)
    def _():
        front(h0_ref, [out_proj(at0_ref, cols) for cols in out_cols])

    @pl.when(i % tiles_per_seq == 0)
    def _():
        carry_ref[...] = jnp.zeros_like(carry_ref)

    xn_ref[...] = xnn_ref[...]
    h1_ref[...] = h1n_ref[...]
    top = lax.broadcasted_iota(jnp.int32, (8, FF_CHUNK), 0)
    z_parts = []
    for ci, c0 in enumerate(range(0, dff, FF_CHUNK)):
        if FRONT_AFTER_CHUNKS <= ci < FRONT_AFTER_CHUNKS + len(out_cols):
            z_parts.append(out_proj(atn_ref, out_cols[ci - FRONT_AFTER_CHUNKS]))
        if ci == FRONT_AFTER_CHUNKS + len(out_cols):
            front(hn_ref, z_parts)
        cs = slice(c0, c0 + FF_CHUNK)
        gate = _dot(xn_ref[...], wup_ref[:, cs])
        val = _dot(xn_ref[...], wup_ref[:, dff + c0:dff + c0 + FF_CHUNK])
        prev = carry_ref[0:8, cs]
        carry_ref[0:8, cs] = gate[tm - 8:tm, :]
        shifted = []
        for lag in (1, 2):
            rolled = pltpu.roll(gate, lag, axis=0)
            head = jnp.where(top < lag, pltpu.roll(prev, lag, axis=0), rolled[0:8, :])
            shifted.append(jnp.concatenate([head, rolled[8:, :]], axis=0))
        g1, g2 = shifted
        gc = conv_ref[3:4, cs] + conv_ref[0:1, cs] * g2 + conv_ref[1:2, cs] * g1 + conv_ref[2:3, cs] * gate
        act_ref[:, cs] = (gc * jax.nn.sigmoid(gc) * val).astype(bf16)
    y = _dot(act_ref[...], wd_ref[...])
    out_ref[...] = h1_ref[...] + _rmsnorm(y, vec_ref[3:4, :])


def _out_ffn(h, at, wo, wup, wd, vecs, conv, layer, seq):
    t, d = h.shape
    tm = TOKEN_TILE
    n = t // tm
    dff = wd.shape[1]
    nxt = lambda i: jnp.minimum(i + 1, n - 1)
    return pl.pallas_call(
        functools.partial(_out_ffn_kernel, tiles_per_seq=seq // tm),
        grid=(n,),
        in_specs=[pl.BlockSpec((tm, d), lambda i: (0, 0), pipeline_mode=pl.Buffered(1)),
                  pl.BlockSpec((at.shape[0], tm), lambda i: (0, 0), pipeline_mode=pl.Buffered(1)),
                  pl.BlockSpec((tm, d), lambda i: (nxt(i), 0)),
                  pl.BlockSpec((at.shape[0], tm), lambda i: (0, nxt(i))),
                  _resident(wo.shape), _layer_resident(wup.shape, layer), _layer_resident(wd.shape, layer),
                  _resident(vecs.shape), _resident(conv.shape)],
        out_specs=pl.BlockSpec((tm, d), lambda i: (i, 0)),
        out_shape=jax.ShapeDtypeStruct((t, d), f32),
        scratch_shapes=[pltpu.VMEM((tm, dff), bf16),
                        pltpu.VMEM((tm, d), f32), pltpu.VMEM((tm, d), bf16),
                        pltpu.VMEM((tm, d), f32), pltpu.VMEM((tm, d), bf16), pltpu.VMEM((16, dff), f32)],
        compiler_params=_params(1),
        name="out_ffn",
    )(h, at, h, at, wo, wup, wd, vecs, conv)


def kernel(x, m_w_in, m_gate_bias, m_head_norm, m_w_out, a_w_in, a_b_in, a_sinks, a_w_out, a_b_out,
           norm_mix_pre, norm_mix_post, norm_ffn_pre, norm_ffn_post, f_w_up, f_conv_w, f_conv_b, f_w_down):
    batch, seq, d = x.shape
    depth = norm_mix_pre.shape[0]
    h = x.reshape(batch * seq, d)
    row = lambda vec: vec.reshape(1, -1).astype(f32)
    lane_rep = lambda vec: jnp.broadcast_to(vec.astype(f32)[:, None], (vec.shape[0], LANES))

    for i in range(depth):
        j = i // 2
        if i % 2 == 0:
            mixed_t, w_up, w_down = _mlstm_mixer(
                h, row(norm_mix_pre[i]), m_w_in[j].T.astype(bf16), lane_rep(m_gate_bias[j].reshape(-1)),
                lane_rep(m_head_norm[j]), f_w_up, f_w_down, seq)
            w_out = m_w_out[j].astype(bf16)
            b_out = jnp.zeros((d,), f32)
        else:
            mixed_t = _attn_mixer(h, row(norm_mix_pre[i]), a_w_in[j].T.astype(bf16), lane_rep(a_b_in[j]),
                                  _sink_rows(a_sinks[j]), seq)
            w_out = a_w_out[j].astype(bf16)
            b_out = a_b_out[j]
        vecs = jnp.pad(jnp.stack([b_out, norm_mix_post[i], norm_ffn_pre[i], norm_ffn_post[i]]).astype(f32),
                       ((0, 4), (0, 0)))
        conv = jnp.pad(jnp.concatenate([f_conv_w[i], f_conv_b[i][None]], axis=0).astype(f32), ((0, 12), (0, 0)))
        h = _out_ffn(h, mixed_t, w_out, w_up, w_down, vecs, conv, i, seq)
    return h.reshape(batch, seq, d)
```

```python
import functools

import jax
import jax.numpy as jnp
from jax import lax
from jax.experimental import pallas as pl
from jax.experimental.pallas import tpu as pltpu

EPS = 1e-6
LANES = 128
BF16_ROWS = 16

M_HEADS = 8
M_QK_DIM = 64
M_V_DIM = 128
GATE_CAP = 15.0
M_CHUNK = 128
MLSTM_GROUP = 256
MIXER_LOOKAHEAD = 2

A_HEAD_DIM = 64
A_Q_HEADS = 16
A_KV_HEADS = 2
A_GROUP = A_Q_HEADS // A_KV_HEADS
WINDOW = 128
A_BLOCK = 128

LOG2E = 1.4426950408889634
NEG_BIG = -1e30

TOKEN_TILE = 512
PROJ_TILE = 1024
ATTN_GROUP = 256
SWA_LOOKAHEAD = 5
FF_CHUNK = 256
Q_ROWS_PER_DOT = 256
OUT_CHUNK = 256
FRONT_AFTER_CHUNKS = 2
VMEM_LIMIT = 60 * 1024 * 1024

bf16 = jnp.bfloat16
f32 = jnp.float32


def _dot(a, b):
    return jnp.dot(a, b, preferred_element_type=f32)


def _dot_nt(a, b):
    return lax.dot_general(a, b, (((1,), (1,)), ((), ())), preferred_element_type=f32)


def _dot_tn(a, b):
    return lax.dot_general(a, b, (((0,), (0,)), ((), ())), preferred_element_type=f32)


def _rmsnorm(x, g):
    return x * lax.rsqrt(jnp.mean(x * x, axis=-1, keepdims=True) + EPS) * g


def _split3(x):
    hi = x.astype(bf16)
    r1 = x - hi.astype(f32)
    mid = r1.astype(bf16)
    lo = (r1 - mid.astype(f32)).astype(bf16)
    return hi, mid, lo


def _log_sigmoid(x):
    return jnp.minimum(x, 0.0) - jnp.log1p(jnp.exp(-jnp.abs(x)))


def _tile_lanes(x, reps):
    return jnp.concatenate([x] * reps, axis=1)


def _next_input_norm(i, x0_ref, xnext_ref, g_ref, xn_ref, xnn_ref):
    @pl.when(i == 0)
    def _():
        xnn_ref[...] = _rmsnorm(x0_ref[...], g_ref[...]).astype(bf16)

    xn_ref[...] = xnn_ref[...]

    def prepare_next():
        xnn_ref[...] = _rmsnorm(xnext_ref[...], g_ref[...]).astype(bf16)

    return xn_ref[...], prepare_next


def _sigmoid(x):
    return 0.5 * jnp.tanh(0.5 * x) + 0.5


def _proj_in_specs(tm, d, n_tiles):
    return [pl.BlockSpec((tm, d), lambda i: (0, 0), pipeline_mode=pl.Buffered(1)),
            pl.BlockSpec((tm, d), lambda i: (jnp.minimum(i + 1, n_tiles - 1), 0))]


def _resident(shape):
    nd = len(shape)
    return pl.BlockSpec(shape, lambda *_: (0,) * nd, pipeline_mode=pl.Buffered(1))


def _layer_resident(shape, layer):
    nd = len(shape) - 1
    return pl.BlockSpec((None,) + tuple(shape[1:]), lambda *_: (layer,) + (0,) * nd, pipeline_mode=pl.Buffered(1))


def _params(n_axes):
    return pltpu.CompilerParams(
        dimension_semantics=("arbitrary",) * n_axes, vmem_limit_bytes=VMEM_LIMIT)


def _mlstm_kernel(x0_ref, xnext_ref, g_ref, wt_ref, gb_ref, hn_ref, wup_ref, wdown_ref,
                  out_ref, wup_bf_ref, wdown_bf_ref, xn_ref, xnn_ref, ct_ref, m_ref,
                  q0_ref, k0_ref, vt0_ref, ot0_ref, stat0_ref, ccol0_ref, *, tiles_per_seq):
    tm = xnext_ref.shape[0]
    L = M_CHUNK
    H = M_HEADS
    dk, dv = M_QK_DIM, M_V_DIM
    nq = H * dk
    nv = H * dv
    pairs = H // 2
    G = MLSTM_GROUP
    cpg = G // L
    i = pl.program_id(0)

    @pl.when(i % tiles_per_seq == 0)
    def _():
        ct_ref[...] = jnp.zeros_like(ct_ref)
        m_ref[...] = jnp.zeros_like(m_ref)

    xn, prepare_next = _next_input_norm(i, x0_ref, xnext_ref, g_ref, xn_ref, xnn_ref)

    upper = lax.broadcasted_iota(jnp.int32, (L, L), 0) <= lax.broadcasted_iota(jnp.int32, (L, L), 1)
    triu = jnp.where(upper, 1.0, 0.0).astype(bf16)
    lane_h = lax.broadcasted_iota(jnp.int32, (H, L), 1)
    lane = lax.broadcasted_iota(jnp.int32, (L, LANES), 1)
    low = lane < dk
    ones_rows = jnp.ones((BF16_ROWS, L), bf16)
    zeros_half = jnp.zeros((dk, L), bf16)

    def projection_steps(xg):
        res = {}
        piece = lambda r0, r1: _dot_nt(wt_ref[r0:r1, :], xg)
        o0 = 2 * nq + nv

        def gates_and_o_hi():
            p = piece(o0 + nv // 2, o0 + nv + 2 * H)
            res["o_hi"] = _sigmoid(p[0:nv // 2, :]).astype(bf16)
            raw = p[nv // 2:, :] + _tile_lanes(gb_ref[...], G // LANES)
            capped = GATE_CAP * jnp.tanh(raw * (1.0 / GATE_CAP))
            res["grow"] = jnp.where(lax.broadcasted_iota(jnp.int32, raw.shape, 0) < H, capped, _log_sigmoid(capped))

        def o_lo():
            res["o_lo"] = _sigmoid(piece(o0, o0 + nv // 2)).astype(bf16)

        def k_and_stats():
            res["k"] = piece(nq, 2 * nq).T.astype(bf16)
            grow = res["grow"]
            stacked = jnp.concatenate([grow[:, c * L:(c + 1) * L] for c in range(cpg)], axis=0)
            sums = _dot(jnp.concatenate(_split3(stacked), axis=0), triu)
            n = cpg * 2 * H
            cum = sums[0:n] + sums[n:2 * n] + sums[2 * n:]
            stats = []
            for c in range(cpg):
                b_r = cum[c * 2 * H + H:(c + 1) * 2 * H, :]
                c_r = grow[0:H, c * L:(c + 1) * L] - b_r
                cmax = c_r
                shift = 1
                while shift < L:
                    cmax = jnp.where(lane_h >= shift, jnp.maximum(cmax, pltpu.roll(cmax, shift, axis=1)), cmax)
                    shift *= 2
                b_last = jnp.broadcast_to(b_r[:, L - 1:L], (H, L))
                tail = c_r + b_last
                tail_max = jnp.broadcast_to(jnp.max(tail, axis=-1, keepdims=True), (H, L))
                c_cols = jnp.concatenate([c_r * LOG2E, jnp.zeros((LANES - H, L), f32)], axis=0).T
                stats.append((b_r, cmax, tail, tail_max, b_last, c_cols))
            res["stats"] = stats

        def v_lo():
            res["v_lo"] = piece(2 * nq, 2 * nq + nv // 2).astype(bf16)

        def v_hi():
            res["v_hi"] = piece(2 * nq + nv // 2, 2 * nq + nv).astype(bf16)

        def q():
            res["q"] = (piece(0, nq) * (dk ** -0.5)).astype(bf16)

        return res, [gates_and_o_hi, o_lo, k_and_stats, v_lo, v_hi, q]

    def finalize(res):
        res["ot"] = jnp.concatenate([res.pop("o_lo"), res.pop("o_hi")], axis=0)
        res["vt"] = jnp.concatenate([res.pop("v_lo"), res.pop("v_hi")], axis=0)
        return res

    def kq(p, c, j):
        tok = slice(c * L, (c + 1) * L)
        kp = p["k"][tok, j * LANES:(j + 1) * LANES]
        qtp = p["q"][j * LANES:(j + 1) * LANES, tok]
        q_even = jnp.concatenate([qtp[0:dk, :], zeros_half], axis=0)
        q_odd = jnp.concatenate([zeros_half, qtp[dk:2 * dk, :]], axis=0)
        return kp, (q_even, q_odd), _dot(kp, jnp.concatenate([q_even, q_odd], axis=1))

    def scan_unit(p, g, c, j, m_prev, m_new, ct, kq_res):
        kp, q_eo, st = kq_res
        tok = slice(c * L, (c + 1) * L)
        out_tok = slice(g * G + c * L, g * G + (c + 1) * L)
        b_r, cmax, tail, _, b_last, c_cols = p["stats"][c]
        a_r = jnp.maximum(m_prev, cmax)
        a_r2 = a_r * LOG2E
        carry_w = jnp.exp(m_prev - a_r).astype(bf16)
        clamp_r = jnp.exp(-(a_r + b_r))
        ws_r = jnp.exp(tail - m_new).astype(bf16)
        decay = jnp.exp(b_last + m_prev - m_new)
        ct_pair_b = ct[j].astype(bf16)
        vws = []
        for e, q_e in enumerate(q_eo):
            h = 2 * j + e
            expo = jnp.where(upper, c_cols[:, h:h + 1] - a_r2[h:h + 1, :], NEG_BIG)
            wt = (st[:, e * L:(e + 1) * L] * jnp.exp2(expo)).astype(bf16)
            q_w = q_e * carry_w[h:h + 1, :]
            vext = jnp.concatenate([p["vt"][h * dv:(h + 1) * dv, tok], ones_rows], axis=0)
            nd = _dot(jnp.concatenate([vext, ct_pair_b], axis=1), jnp.concatenate([wt, q_w], axis=0))
            den = jnp.maximum(jnp.abs(nd[dv:dv + 1, :]), clamp_r[h:h + 1, :])
            inv = 1.0 / den
            num = nd[0:dv, :]
            scale = inv * lax.rsqrt(inv * inv * jnp.mean(num * num, axis=0, keepdims=True) + EPS)
            hn = num * scale * hn_ref[h * dv:(h + 1) * dv, :]
            gate_o = p["ot"][h * dv:(h + 1) * dv, tok].astype(f32)
            out_ref[h * dv:(h + 1) * dv, out_tok] = (gate_o * hn).astype(bf16)
            vws.append(vext * ws_r[h:h + 1, :])
        k_split = jnp.concatenate([jnp.where(low, kp, jnp.zeros_like(kp)),
                                   jnp.where(low, jnp.zeros_like(kp), kp)], axis=0)
        decay_pair = jnp.where(low[0:1, :], decay[2 * j:2 * j + 1, :], decay[2 * j + 1:2 * j + 2, :])
        ct[j] = decay_pair * ct[j] + _dot(jnp.concatenate(vws, axis=1), k_split)

    def store_first_group(res):
        q0_ref[...], k0_ref[...], vt0_ref[...], ot0_ref[...] = res["q"], res["k"], res["vt"], res["ot"]
        for c, stat in enumerate(res["stats"]):
            stat0_ref[c * 5 * H:(c + 1) * 5 * H, :] = jnp.concatenate(stat[0:5], axis=0)
            ccol0_ref[c * L:(c + 1) * L, :] = stat[5]

    def load_first_group():
        stats = [tuple(stat0_ref[(c * 5 + s) * H:(c * 5 + s + 1) * H, :] for s in range(5))
                 + (ccol0_ref[c * L:(c + 1) * L, :],) for c in range(cpg)]
        return {"q": q0_ref[...], "k": k0_ref[...], "vt": vt0_ref[...], "ot": ot0_ref[...], "stats": stats}

    @pl.when(i == 0)
    def _():
        res, first_steps = projection_steps(xn_ref[0:G, :])
        for step in first_steps:
            step()
        store_first_group(finalize(res))

    n_groups = tm // G
    cur = load_first_group()

    ct = [ct_ref[j] for j in range(pairs)]
    m = m_ref[...]
    units = [(g, c, j) for g in range(n_groups) for c in range(cpg) for j in range(pairs)]
    per_group = cpg * pairs
    projected = {0: cur}
    kq_queue = {}

    def queue_kq(t):
        g, c, j = units[t]
        kq_queue[t] = kq(projected[g], c, j)

    for t in range(MIXER_LOOKAHEAD):
        queue_kq(t)
    for t, (g, c, j) in enumerate(units):
        u = t % per_group
        if u == 0:
            nxt, steps = projection_steps(xn[(g + 1) * G:(g + 2) * G, :] if g + 1 < n_groups else xnn_ref[0:G, :])
            stats = projected[g]["stats"]
            m_prevs = []
            for cc in range(cpg):
                m_prevs.append(m)
                m = jnp.maximum(stats[cc][4] + m, stats[cc][3])
            m_prevs.append(m)
        if t == 0:
            prepare_next()
        if u < len(steps):
            steps[u]()
            if u + 1 == len(steps):
                if g + 1 < n_groups:
                    projected[g + 1] = finalize(nxt)
                else:
                    store_first_group(finalize(nxt))
        if t + MIXER_LOOKAHEAD < len(units):
            queue_kq(t + MIXER_LOOKAHEAD)
        scan_unit(projected[g], g, c, j, m_prevs[c], m_prevs[c + 1], ct, kq_queue.pop(t))
    m_ref[...] = m
    for j in range(pairs):
        ct_ref[j] = ct[j]
    wup_bf_ref[...] = wup_ref[...].astype(bf16)
    wdown_bf_ref[...] = wdown_ref[...].astype(bf16)


def _mlstm_mixer(x, gain, wt, gate_bias, hn_rep, w_up, w_down, seq):
    t, d = x.shape
    nq = M_HEADS * M_QK_DIM
    nv = M_HEADS * M_V_DIM
    tm = PROJ_TILE
    steps = t // tm
    slab = lambda w: pl.BlockSpec((w.shape[0], w.shape[1] // steps, w.shape[2]), lambda i: (0, i, 0))
    return pl.pallas_call(
        functools.partial(_mlstm_kernel, tiles_per_seq=seq // tm),
        grid=(steps,),
        in_specs=_proj_in_specs(tm, d, steps) + [
            _resident((1, d)), _resident(wt.shape), _resident(gate_bias.shape), _resident(hn_rep.shape),
            slab(w_up), slab(w_down)],
        out_specs=[pl.BlockSpec((nv, tm), lambda i: (0, i)), slab(w_up), slab(w_down)],
        out_shape=[jax.ShapeDtypeStruct((nv, t), bf16),
                   jax.ShapeDtypeStruct(w_up.shape, bf16), jax.ShapeDtypeStruct(w_down.shape, bf16)],
        scratch_shapes=[pltpu.VMEM((tm, d), bf16), pltpu.VMEM((tm, d), bf16),
                        pltpu.VMEM((M_HEADS // 2, M_V_DIM + BF16_ROWS, 2 * M_QK_DIM), f32),
                        pltpu.VMEM((M_HEADS, LANES), f32),
                        pltpu.VMEM((nq, MLSTM_GROUP), bf16), pltpu.VMEM((MLSTM_GROUP, nq), bf16),
                        pltpu.VMEM((nv, MLSTM_GROUP), bf16), pltpu.VMEM((nv, MLSTM_GROUP), bf16),
                        pltpu.VMEM((MLSTM_GROUP // M_CHUNK * 5 * M_HEADS, M_CHUNK), f32),
                        pltpu.VMEM((MLSTM_GROUP, LANES), f32)],
        compiler_params=_params(1),
        name="mlstm_mixer",
    )(x, x, gain, wt, gate_bias, hn_rep, w_up, w_down)


def _attn_kernel(x0_ref, xnext_ref, g_ref, wt_ref, b_ref, sink_ref, out_ref,
                 xn_ref, xnn_ref, kprev_ref, vtprev_ref, *, tiles_per_seq):
    tm = xnext_ref.shape[0]
    blk = A_BLOCK
    dh = A_HEAD_DIM
    nq = A_Q_HEADS * dh
    nkv = A_KV_HEADS * dh
    G = ATTN_GROUP
    bpg = G // blk
    i = pl.program_id(0)
    first = i % tiles_per_seq == 0

    @pl.when(i == 0)
    def _():
        kprev_ref[...] = jnp.zeros_like(kprev_ref)
        vtprev_ref[...] = jnp.zeros_like(vtprev_ref)

    xn, prepare_next = _next_input_norm(i, x0_ref, xnext_ref, g_ref, xn_ref, xnn_ref)

    ku = lax.broadcasted_iota(jnp.int32, (2 * blk, blk), 0)
    qi = lax.broadcasted_iota(jnp.int32, (2 * blk, blk), 1)
    diff = qi - (ku - blk)
    band = (diff >= 0) & (diff < WINDOW)
    bias = jnp.where(band | (ku == 0), 0.0, NEG_BIG)
    bias_first = jnp.where((band & (ku >= blk)) | (ku == 0), 0.0, NEG_BIG)
    krow = lax.broadcasted_iota(jnp.int32, (2 * blk, LANES), 0)
    klane = lax.broadcasted_iota(jnp.int32, (2 * blk, LANES), 1)
    k_aug = jnp.where((krow == 0) & (klane < 3), 1.0, 0.0).astype(bf16)
    vcol = lax.broadcasted_iota(jnp.int32, (dh, 2 * blk), 1)
    ones_rows = jnp.ones((BF16_ROWS, 2 * blk), bf16)
    hp_lanes = 2 * blk

    def projection_steps(xg):
        res = {"q": {}}
        piece = lambda r0, r1: _dot_nt(wt_ref[r0:r1, :], xg) + _tile_lanes(b_ref[r0:r1, :], G // LANES)

        def kv():
            p = piece(nq, nq + 2 * nkv)
            res["k"] = p[0:nkv, :].T.astype(bf16)
            res["vt"] = p[nkv:2 * nkv, :].astype(bf16)

        def q_rows(r0):
            def step():
                res["q"][r0] = (piece(r0, r0 + Q_ROWS_PER_DOT) * (dh ** -0.5 * LOG2E)).astype(bf16)
            return step

        return res, [kv] + [q_rows(r0) for r0 in range(0, nq, Q_ROWS_PER_DOT)]

    def finalize(res):
        res["qt"] = jnp.concatenate([res["q"][r0] for r0 in sorted(res["q"])], axis=0)
        return res

    projected = {}
    shared = {}

    def operands(g, bl, grp):
        if (g, bl, grp) not in shared:
            cols = slice(bl * blk, (bl + 1) * blk)
            cur = projected[g]
            if bl > 0:
                k_prev, vt_prev = cur["k"][(bl - 1) * blk:bl * blk, :], cur["vt"][:, (bl - 1) * blk:bl * blk]
            elif g > 0:
                k_prev, vt_prev = projected[g - 1]["k"][G - blk:G, :], projected[g - 1]["vt"][:, G - blk:G]
            else:
                k_prev, vt_prev = kprev_ref[...], vtprev_ref[...]
            b1 = jnp.where(first, bias_first, bias) if (g == 0 and bl == 0) else bias
            kcat = jnp.concatenate([k_prev, cur["k"][cols, :]], axis=0)
            vtcat = jnp.concatenate([vt_prev, cur["vt"][:, cols]], axis=1)
            in_group = (klane >= grp * dh) & (klane < (grp + 1) * dh) & (krow > 0)
            km = jnp.concatenate([jnp.where(in_group, kcat, jnp.zeros_like(kcat)), k_aug], axis=1)
            vt_g = jnp.where(vcol == 0, jnp.zeros((dh, 2 * blk), bf16), vtcat[grp * dh:(grp + 1) * dh, :])
            shared[(g, bl, grp)] = km, jnp.concatenate([vt_g, ones_rows], axis=0), _tile_lanes(b1, 2)
        return shared[(g, bl, grp)]

    def logits(g, bl, grp, hp):
        cols = slice(bl * blk, (bl + 1) * blk)
        km, _, b2 = operands(g, bl, grp)
        blocks = []
        for e in range(2):
            h = grp * A_GROUP + 2 * hp + e
            pair = projected[g]["qt"][(h // 2) * LANES:(h // 2 + 1) * LANES, cols]
            if h % 2 != grp:
                pair = jnp.concatenate([pair[dh:2 * dh, :], pair[0:dh, :]], axis=0)
            blocks.append(pair)
        sink_rows = sink_ref[grp][:, hp * hp_lanes:(hp + 1) * hp_lanes]
        rhs = jnp.concatenate([jnp.concatenate(blocks, axis=1), sink_rows], axis=0)
        return _dot(km, rhs) + b2

    def finish(g, bl, grp, hp, st):
        out_cols = slice(g * G + bl * blk, g * G + (bl + 1) * blk)
        _, vext, _ = operands(g, bl, grp)
        p = jnp.exp2(st - jnp.max(st, axis=0, keepdims=True)).astype(bf16)
        oext = _dot(vext, p)
        o = (oext[0:dh, :] * (1.0 / oext[dh:dh + 1, :])).astype(bf16)
        for e in range(2):
            h = grp * A_GROUP + 2 * hp + e
            out_ref[h * dh:(h + 1) * dh, out_cols] = o[:, e * blk:(e + 1) * blk]

    n_groups = tm // G
    res, steps = projection_steps(xn[0:G, :])
    for k_step, step in enumerate(steps):
        step()
        if k_step == 0:
            prepare_next()
    projected[0] = finalize(res)

    units = [(g, bl, grp, hp) for g in range(n_groups) for bl in range(bpg)
             for grp in range(A_KV_HEADS) for hp in range(A_GROUP // 2)]
    per_group = bpg * A_KV_HEADS * (A_GROUP // 2)
    queued = {}
    for t in range(SWA_LOOKAHEAD):
        queued[t] = logits(*units[t])
    for t, unit in enumerate(units):
        g = unit[0]
        u = t % per_group
        if u == 0:
            nxt, steps = projection_steps(xn[(g + 1) * G:(g + 2) * G, :]) if g + 1 < n_groups else (None, [])
        if u < len(steps):
            steps[u]()
            if u + 1 == len(steps):
                projected[g + 1] = finalize(nxt)
        if t + SWA_LOOKAHEAD < len(units):
            queued[t + SWA_LOOKAHEAD] = logits(*units[t + SWA_LOOKAHEAD])
        finish(*unit, queued.pop(t))
    last = projected[n_groups - 1]
    kprev_ref[...] = last["k"][G - blk:G, :]
    vtprev_ref[...] = last["vt"][:, G - blk:G]


def _attn_mixer(x, gain, wt, b_rep, sink_aug, seq):
    t, d = x.shape
    nq = A_Q_HEADS * A_HEAD_DIM
    nkv = A_KV_HEADS * A_HEAD_DIM
    tm = PROJ_TILE
    steps = t // tm
    return pl.pallas_call(
        functools.partial(_attn_kernel, tiles_per_seq=seq // tm),
        grid=(steps,),
        in_specs=_proj_in_specs(tm, d, steps) + [
            _resident((1, d)), _resident(wt.shape), _resident(b_rep.shape), _resident(sink_aug.shape)],
        out_specs=pl.BlockSpec((nq, tm), lambda i: (0, i)),
        out_shape=jax.ShapeDtypeStruct((nq, t), bf16),
        scratch_shapes=[pltpu.VMEM((tm, d), bf16), pltpu.VMEM((tm, d), bf16),
                        pltpu.VMEM((A_BLOCK, nkv), bf16), pltpu.VMEM((nkv, A_BLOCK), bf16)],
        compiler_params=_params(1),
        name="attn_mixer",
    )(x, x, gain, wt, b_rep, sink_aug)


def _sink_rows(sinks):
    parts = jnp.stack(_split3(sinks.astype(f32) * LOG2E), axis=0)
    per_lane = jnp.repeat(parts.reshape(3, A_KV_HEADS, A_GROUP), A_BLOCK, axis=2)
    return jnp.pad(per_lane.transpose(1, 0, 2), ((0, 0), (0, LANES - 3), (0, 0)))


def _out_ffn_kernel(h0_ref, at0_ref, hn_ref, atn_ref, wo_ref, wup_ref, wd_ref, vec_ref, conv_ref,
                    out_ref, act_ref, h1_ref, xn_ref, h1n_ref, xnn_ref, carry_ref, *, tiles_per_seq):
    tm = hn_ref.shape[0]
    dff = wd_ref.shape[0]
    i = pl.program_id(0)

    d = wo_ref.shape[1]
    out_cols = [slice(n0, n0 + OUT_CHUNK) for n0 in range(0, d, OUT_CHUNK)]

    def out_proj(at_ref, cols):
        return _dot_tn(at_ref[...], wo_ref[:, cols]) + vec_ref[0:1, cols]

    def front(h_ref, z_parts):
        h1 = h_ref[...] + _rmsnorm(jnp.concatenate(z_parts, axis=1), vec_ref[1:2, :])
        h1n_ref[...] = h1
        xnn_ref[...] = _rmsnorm(h1, vec_ref[2:3, :]).astype(bf16)

    @pl.when(i == 0)
    def _():
        front(h0_ref, [out_proj(at0_ref, cols) for cols in out_cols])

    @pl.when(i % tiles_per_seq == 0)
    def _():
        carry_ref[...] = jnp.zeros_like(carry_ref)

    xn_ref[...] = xnn_ref[...]
    h1_ref[...] = h1n_ref[...]
    top = lax.broadcasted_iota(jnp.int32, (8, FF_CHUNK), 0)
    z_parts = []
    for ci, c0 in enumerate(range(0, dff, FF_CHUNK)):
        if FRONT_AFTER_CHUNKS <= ci < FRONT_AFTER_CHUNKS + len(out_cols):
            z_parts.append(out_proj(atn_ref, out_cols[ci - FRONT_AFTER_CHUNKS]))
        if ci == FRONT_AFTER_CHUNKS + len(out_cols):
            front(hn_ref, z_parts)
        cs = slice(c0, c0 + FF_CHUNK)
        gate = _dot(xn_ref[...], wup_ref[:, cs])
        val = _dot(xn_ref[...], wup_ref[:, dff + c0:dff + c0 + FF_CHUNK])
        prev = carry_ref[0:8, cs]
        carry_ref[0:8, cs] = gate[tm - 8:tm, :]
        shifted = []
        for lag in (1, 2):
            rolled = pltpu.roll(gate, lag, axis=0)
            head = jnp.where(top < lag, pltpu.roll(prev, lag, axis=0), rolled[0:8, :])
            shifted.append(jnp.concatenate([head, rolled[8:, :]], axis=0))
        g1, g2 = shifted
        gc = conv_ref[3:4, cs] + conv_ref[0:1, cs] * g2 + conv_ref[1:2, cs] * g1 + conv_ref[2:3, cs] * gate
        act_ref[:, cs] = (gc * jax.nn.sigmoid(gc) * val).astype(bf16)
    y = _dot(act_ref[...], wd_ref[...])
    out_ref[...] = h1_ref[...] + _rmsnorm(y, vec_ref[3:4, :])


def _out_ffn(h, at, wo, wup, wd, vecs, conv, layer, seq):
    t, d = h.shape
    tm = TOKEN_TILE
    n = t // tm
    dff = wd.shape[1]
    nxt = lambda i: jnp.minimum(i + 1, n - 1)
    return pl.pallas_call(
        functools.partial(_out_ffn_kernel, tiles_per_seq=seq // tm),
        grid=(n,),
        in_specs=[pl.BlockSpec((tm, d), lambda i: (0, 0), pipeline_mode=pl.Buffered(1)),
                  pl.BlockSpec((at.shape[0], tm), lambda i: (0, 0), pipeline_mode=pl.Buffered(1)),
                  pl.BlockSpec((tm, d), lambda i: (nxt(i), 0)),
                  pl.BlockSpec((at.shape[0], tm), lambda i: (0, nxt(i))),
                  _resident(wo.shape), _layer_resident(wup.shape, layer), _layer_resident(wd.shape, layer),
                  _resident(vecs.shape), _resident(conv.shape)],
        out_specs=pl.BlockSpec((tm, d), lambda i: (i, 0)),
        out_shape=jax.ShapeDtypeStruct((t, d), f32),
        scratch_shapes=[pltpu.VMEM((tm, dff), bf16),
                        pltpu.VMEM((tm, d), f32), pltpu.VMEM((tm, d), bf16),
                        pltpu.VMEM((tm, d), f32), pltpu.VMEM((tm, d), bf16), pltpu.VMEM((16, dff), f32)],
        compiler_params=_params(1),
        name="out_ffn",
    )(h, at, h, at, wo, wup, wd, vecs, conv)


def kernel(x, m_w_in, m_gate_bias, m_head_norm, m_w_out, a_w_in, a_b_in, a_sinks, a_w_out, a_b_out,
           norm_mix_pre, norm_mix_post, norm_ffn_pre, norm_ffn_post, f_w_up, f_conv_w, f_conv_b, f_w_down):
    batch, seq, d = x.shape
    depth = norm_mix_pre.shape[0]
    h = x.reshape(batch * seq, d)
    row = lambda vec: vec.reshape(1, -1).astype(f32)
    lane_rep = lambda vec: jnp.broadcast_to(vec.astype(f32)[:, None], (vec.shape[0], LANES))

    for i in range(depth):
        j = i // 2
        if i % 2 == 0:
            mixed_t, w_up, w_down = _mlstm_mixer(
                h, row(norm_mix_pre[i]), m_w_in[j].T.astype(bf16), lane_rep(m_gate_bias[j].reshape(-1)),
                lane_rep(m_head_norm[j]), f_w_up, f_w_down, seq)
            w_out = m_w_out[j].astype(bf16)
            b_out = jnp.zeros((d,), f32)
        else:
            mixed_t = _attn_mixer(h, row(norm_mix_pre[i]), a_w_in[j].T.astype(bf16), lane_rep(a_b_in[j]),
                                  _sink_rows(a_sinks[j]), seq)
            w_out = a_w_out[j].astype(bf16)
            b_out = a_b_out[j]
        vecs = jnp.pad(jnp.stack([b_out, norm_mix_post[i], norm_ffn_pre[i], norm_ffn_post[i]]).astype(f32),
                       ((0, 4), (0, 0)))
        conv = jnp.pad(jnp.concatenate([f_conv_w[i], f_conv_b[i][None]], axis=0).astype(f32), ((0, 12), (0, 0)))
        h = _out_ffn(h, mixed_t, w_out, w_up, w_down, vecs, conv, i, seq)
    return h.reshape(batch, seq, d)
```

```python
import functools

import jax
import jax.numpy as jnp
from jax import lax
from jax.experimental import pallas as pl
from jax.experimental.pallas import tpu as pltpu

EPS = 1e-6
LANES = 128
BF16_ROWS = 16

M_HEADS = 8
M_QK_DIM = 64
M_V_DIM = 128
GATE_CAP = 15.0
M_CHUNK = 128
MLSTM_GROUP = 256
MIXER_LOOKAHEAD = 2

A_HEAD_DIM = 64
A_Q_HEADS = 16
A_KV_HEADS = 2
A_GROUP = A_Q_HEADS // A_KV_HEADS
WINDOW = 128
A_BLOCK = 128

LOG2E = 1.4426950408889634
NEG_BIG = -1e30

TOKEN_TILE = 512
PROJ_TILE = 1024
ATTN_GROUP = 256
SWA_LOOKAHEAD = 5
FF_CHUNK = 256
Q_ROWS_PER_DOT = 256
OUT_CHUNK = 256
NORM_ROWS = 128
FRONT_AFTER_CHUNKS = 2
VMEM_LIMIT = 60 * 1024 * 1024

bf16 = jnp.bfloat16
f32 = jnp.float32


def _dot(a, b):
    return jnp.dot(a, b, preferred_element_type=f32)


def _dot_nt(a, b):
    return lax.dot_general(a, b, (((1,), (1,)), ((), ())), preferred_element_type=f32)


def _dot_tn(a, b):
    return lax.dot_general(a, b, (((0,), (0,)), ((), ())), preferred_element_type=f32)


def _rmsnorm(x, g):
    return x * lax.rsqrt(jnp.mean(x * x, axis=-1, keepdims=True) + EPS) * g


def _split3(x):
    hi = x.astype(bf16)
    r1 = x - hi.astype(f32)
    mid = r1.astype(bf16)
    lo = (r1 - mid.astype(f32)).astype(bf16)
    return hi, mid, lo


def _log_sigmoid(x):
    return jnp.minimum(x, 0.0) - jnp.log1p(jnp.exp(-jnp.abs(x)))


def _tile_lanes(x, reps):
    return jnp.concatenate([x] * reps, axis=1)


def _next_input_norm(i, x0_ref, xnext_ref, g_ref, xn_ref, xnn_ref):
    @pl.when(i == 0)
    def _():
        xnn_ref[...] = _rmsnorm(x0_ref[...], g_ref[...]).astype(bf16)

    xn_ref[...] = xnn_ref[...]

    def prepare_next():
        xnn_ref[...] = _rmsnorm(xnext_ref[...], g_ref[...]).astype(bf16)

    return xn_ref[...], prepare_next


def _sigmoid(x):
    return 0.5 * jnp.tanh(0.5 * x) + 0.5


def _proj_in_specs(tm, d, n_tiles):
    return [pl.BlockSpec((tm, d), lambda i: (0, 0), pipeline_mode=pl.Buffered(1)),
            pl.BlockSpec((tm, d), lambda i: (jnp.minimum(i + 1, n_tiles - 1), 0))]


def _resident(shape):
    nd = len(shape)
    return pl.BlockSpec(shape, lambda *_: (0,) * nd, pipeline_mode=pl.Buffered(1))


def _layer_resident(shape, layer):
    nd = len(shape) - 1
    return pl.BlockSpec((None,) + tuple(shape[1:]), lambda *_: (layer,) + (0,) * nd, pipeline_mode=pl.Buffered(1))


def _params(n_axes):
    return pltpu.CompilerParams(
        dimension_semantics=("arbitrary",) * n_axes, vmem_limit_bytes=VMEM_LIMIT)


def _mlstm_kernel(x0_ref, xnext_ref, g_ref, wt_ref, gb_ref, hn_ref, wup_ref, wdown_ref,
                  out_ref, wup_bf_ref, wdown_bf_ref, xn_ref, xnn_ref, ct_ref, m_ref,
                  q0_ref, k0_ref, vt0_ref, ot0_ref, stat0_ref, ccol0_ref, *, tiles_per_seq):
    tm = xnext_ref.shape[0]
    L = M_CHUNK
    H = M_HEADS
    dk, dv = M_QK_DIM, M_V_DIM
    nq = H * dk
    nv = H * dv
    pairs = H // 2
    G = MLSTM_GROUP
    cpg = G // L
    i = pl.program_id(0)

    @pl.when(i % tiles_per_seq == 0)
    def _():
        ct_ref[...] = jnp.zeros_like(ct_ref)
        m_ref[...] = jnp.zeros_like(m_ref)

    xn, prepare_next = _next_input_norm(i, x0_ref, xnext_ref, g_ref, xn_ref, xnn_ref)

    upper = lax.broadcasted_iota(jnp.int32, (L, L), 0) <= lax.broadcasted_iota(jnp.int32, (L, L), 1)
    triu = jnp.where(upper, 1.0, 0.0).astype(bf16)
    lane_h = lax.broadcasted_iota(jnp.int32, (H, L), 1)
    lane = lax.broadcasted_iota(jnp.int32, (L, LANES), 1)
    low = lane < dk
    ones_rows = jnp.ones((BF16_ROWS, L), bf16)
    zeros_half = jnp.zeros((dk, L), bf16)

    def projection_steps(xg):
        res = {}
        piece = lambda r0, r1: _dot_nt(wt_ref[r0:r1, :], xg)
        o0 = 2 * nq + nv

        def gates_and_o_hi():
            p = piece(o0 + nv // 2, o0 + nv + 2 * H)
            res["o_hi"] = _sigmoid(p[0:nv // 2, :]).astype(bf16)
            raw = p[nv // 2:, :] + _tile_lanes(gb_ref[...], G // LANES)
            capped = GATE_CAP * jnp.tanh(raw * (1.0 / GATE_CAP))
            res["grow"] = jnp.where(lax.broadcasted_iota(jnp.int32, raw.shape, 0) < H, capped, _log_sigmoid(capped))

        def o_lo():
            res["o_lo"] = _sigmoid(piece(o0, o0 + nv // 2)).astype(bf16)

        def k_and_stats():
            res["k"] = piece(nq, 2 * nq).T.astype(bf16)
            grow = res["grow"]
            stacked = jnp.concatenate([grow[:, c * L:(c + 1) * L] for c in range(cpg)], axis=0)
            sums = _dot(jnp.concatenate(_split3(stacked), axis=0), triu)
            n = cpg * 2 * H
            cum = sums[0:n] + sums[n:2 * n] + sums[2 * n:]
            stats = []
            for c in range(cpg):
                b_r = cum[c * 2 * H + H:(c + 1) * 2 * H, :]
                c_r = grow[0:H, c * L:(c + 1) * L] - b_r
                cmax = c_r
                shift = 1
                while shift < L:
                    cmax = jnp.where(lane_h >= shift, jnp.maximum(cmax, pltpu.roll(cmax, shift, axis=1)), cmax)
                    shift *= 2
                b_last = jnp.broadcast_to(b_r[:, L - 1:L], (H, L))
                tail = c_r + b_last
                tail_max = jnp.broadcast_to(jnp.max(tail, axis=-1, keepdims=True), (H, L))
                c_cols = jnp.concatenate([c_r * LOG2E, jnp.zeros((LANES - H, L), f32)], axis=0).T
                stats.append((b_r, cmax, tail, tail_max, b_last, c_cols))
            res["stats"] = stats

        def v_lo():
            res["v_lo"] = piece(2 * nq, 2 * nq + nv // 2).astype(bf16)

        def v_hi():
            res["v_hi"] = piece(2 * nq + nv // 2, 2 * nq + nv).astype(bf16)

        def q():
            res["q"] = (piece(0, nq) * (dk ** -0.5)).astype(bf16)

        return res, [gates_and_o_hi, o_lo, k_and_stats, v_lo, v_hi, q]

    def finalize(res):
        res["ot"] = jnp.concatenate([res.pop("o_lo"), res.pop("o_hi")], axis=0)
        res["vt"] = jnp.concatenate([res.pop("v_lo"), res.pop("v_hi")], axis=0)
        return res

    def kq(p, c, j):
        tok = slice(c * L, (c + 1) * L)
        kp = p["k"][tok, j * LANES:(j + 1) * LANES]
        qtp = p["q"][j * LANES:(j + 1) * LANES, tok]
        q_even = jnp.concatenate([qtp[0:dk, :], zeros_half], axis=0)
        q_odd = jnp.concatenate([zeros_half, qtp[dk:2 * dk, :]], axis=0)
        return kp, (q_even, q_odd), _dot(kp, jnp.concatenate([q_even, q_odd], axis=1))

    def scan_unit(p, g, c, j, m_prev, m_new, ct, kq_res):
        kp, q_eo, st = kq_res
        tok = slice(c * L, (c + 1) * L)
        out_tok = slice(g * G + c * L, g * G + (c + 1) * L)
        b_r, cmax, tail, _, b_last, c_cols = p["stats"][c]
        a_r = jnp.maximum(m_prev, cmax)
        a_r2 = a_r * LOG2E
        carry_w = jnp.exp(m_prev - a_r).astype(bf16)
        clamp_r = jnp.exp(-(a_r + b_r))
        ws_r = jnp.exp(tail - m_new).astype(bf16)
        decay = jnp.exp(b_last + m_prev - m_new)
        ct_pair_b = ct[j].astype(bf16)
        vws = []
        for e, q_e in enumerate(q_eo):
            h = 2 * j + e
            expo = jnp.where(upper, c_cols[:, h:h + 1] - a_r2[h:h + 1, :], NEG_BIG)
            wt = (st[:, e * L:(e + 1) * L] * jnp.exp2(expo)).astype(bf16)
            q_w = q_e * carry_w[h:h + 1, :]
            vext = jnp.concatenate([p["vt"][h * dv:(h + 1) * dv, tok], ones_rows], axis=0)
            nd = _dot(jnp.concatenate([vext, ct_pair_b], axis=1), jnp.concatenate([wt, q_w], axis=0))
            den = jnp.maximum(jnp.abs(nd[dv:dv + 1, :]), clamp_r[h:h + 1, :])
            inv = 1.0 / den
            num = nd[0:dv, :]
            scale = inv * lax.rsqrt(inv * inv * jnp.mean(num * num, axis=0, keepdims=True) + EPS)
            hn = num * scale * hn_ref[h * dv:(h + 1) * dv, :]
            gate_o = p["ot"][h * dv:(h + 1) * dv, tok].astype(f32)
            out_ref[h * dv:(h + 1) * dv, out_tok] = (gate_o * hn).astype(bf16)
            vws.append(vext * ws_r[h:h + 1, :])
        k_split = jnp.concatenate([jnp.where(low, kp, jnp.zeros_like(kp)),
                                   jnp.where(low, jnp.zeros_like(kp), kp)], axis=0)
        decay_pair = jnp.where(low[0:1, :], decay[2 * j:2 * j + 1, :], decay[2 * j + 1:2 * j + 2, :])
        ct[j] = decay_pair * ct[j] + _dot(jnp.concatenate(vws, axis=1), k_split)

    def store_first_group(res):
        q0_ref[...], k0_ref[...], vt0_ref[...], ot0_ref[...] = res["q"], res["k"], res["vt"], res["ot"]
        for c, stat in enumerate(res["stats"]):
            stat0_ref[c * 5 * H:(c + 1) * 5 * H, :] = jnp.concatenate(stat[0:5], axis=0)
            ccol0_ref[c * L:(c + 1) * L, :] = stat[5]

    def load_first_group():
        stats = [tuple(stat0_ref[(c * 5 + s) * H:(c * 5 + s + 1) * H, :] for s in range(5))
                 + (ccol0_ref[c * L:(c + 1) * L, :],) for c in range(cpg)]
        return {"q": q0_ref[...], "k": k0_ref[...], "vt": vt0_ref[...], "ot": ot0_ref[...], "stats": stats}

    @pl.when(i == 0)
    def _():
        res, first_steps = projection_steps(xn_ref[0:G, :])
        for step in first_steps:
            step()
        store_first_group(finalize(res))

    n_groups = tm // G
    cur = load_first_group()

    ct = [ct_ref[j] for j in range(pairs)]
    m = m_ref[...]
    units = [(g, c, j) for g in range(n_groups) for c in range(cpg) for j in range(pairs)]
    per_group = cpg * pairs
    projected = {0: cur}
    kq_queue = {}

    def queue_kq(t):
        g, c, j = units[t]
        kq_queue[t] = kq(projected[g], c, j)

    for t in range(MIXER_LOOKAHEAD):
        queue_kq(t)
    for t, (g, c, j) in enumerate(units):
        u = t % per_group
        if u == 0:
            nxt, steps = projection_steps(xn[(g + 1) * G:(g + 2) * G, :] if g + 1 < n_groups else xnn_ref[0:G, :])
            stats = projected[g]["stats"]
            m_prevs = []
            for cc in range(cpg):
                m_prevs.append(m)
                m = jnp.maximum(stats[cc][4] + m, stats[cc][3])
            m_prevs.append(m)
        if t == 0:
            prepare_next()
        if u < len(steps):
            steps[u]()
            if u + 1 == len(steps):
                if g + 1 < n_groups:
                    projected[g + 1] = finalize(nxt)
                else:
                    store_first_group(finalize(nxt))
        if t + MIXER_LOOKAHEAD < len(units):
            queue_kq(t + MIXER_LOOKAHEAD)
        scan_unit(projected[g], g, c, j, m_prevs[c], m_prevs[c + 1], ct, kq_queue.pop(t))
    m_ref[...] = m
    for j in range(pairs):
        ct_ref[j] = ct[j]
    wup_bf_ref[...] = wup_ref[...].astype(bf16)
    wdown_bf_ref[...] = wdown_ref[...].astype(bf16)


def _mlstm_mixer(x, gain, wt, gate_bias, hn_rep, w_up, w_down, seq):
    t, d = x.shape
    nq = M_HEADS * M_QK_DIM
    nv = M_HEADS * M_V_DIM
    tm = PROJ_TILE
    steps = t // tm
    slab = lambda w: pl.BlockSpec((w.shape[0], w.shape[1] // steps, w.shape[2]), lambda i: (0, i, 0))
    return pl.pallas_call(
        functools.partial(_mlstm_kernel, tiles_per_seq=seq // tm),
        grid=(steps,),
        in_specs=_proj_in_specs(tm, d, steps) + [
            _resident((1, d)), _resident(wt.shape), _resident(gate_bias.shape), _resident(hn_rep.shape),
            slab(w_up), slab(w_down)],
        out_specs=[pl.BlockSpec((nv, tm), lambda i: (0, i)), slab(w_up), slab(w_down)],
        out_shape=[jax.ShapeDtypeStruct((nv, t), bf16),
                   jax.ShapeDtypeStruct(w_up.shape, bf16), jax.ShapeDtypeStruct(w_down.shape, bf16)],
        scratch_shapes=[pltpu.VMEM((tm, d), bf16), pltpu.VMEM((tm, d), bf16),
                        pltpu.VMEM((M_HEADS // 2, M_V_DIM + BF16_ROWS, 2 * M_QK_DIM), f32),
                        pltpu.VMEM((M_HEADS, LANES), f32),
                        pltpu.VMEM((nq, MLSTM_GROUP), bf16), pltpu.VMEM((MLSTM_GROUP, nq), bf16),
                        pltpu.VMEM((nv, MLSTM_GROUP), bf16), pltpu.VMEM((nv, MLSTM_GROUP), bf16),
                        pltpu.VMEM((MLSTM_GROUP // M_CHUNK * 5 * M_HEADS, M_CHUNK), f32),
                        pltpu.VMEM((MLSTM_GROUP, LANES), f32)],
        compiler_params=_params(1),
        name="mlstm_mixer",
    )(x, x, gain, wt, gate_bias, hn_rep, w_up, w_down)


def _attn_kernel(x0_ref, xnext_ref, g_ref, wt_ref, b_ref, sink_ref, out_ref,
                 xn_ref, xnn_ref, kprev_ref, vtprev_ref, *, tiles_per_seq):
    tm = xnext_ref.shape[0]
    blk = A_BLOCK
    dh = A_HEAD_DIM
    nq = A_Q_HEADS * dh
    nkv = A_KV_HEADS * dh
    G = ATTN_GROUP
    bpg = G // blk
    i = pl.program_id(0)
    first = i % tiles_per_seq == 0

    @pl.when(i == 0)
    def _():
        kprev_ref[...] = jnp.zeros_like(kprev_ref)
        vtprev_ref[...] = jnp.zeros_like(vtprev_ref)

    xn, prepare_next = _next_input_norm(i, x0_ref, xnext_ref, g_ref, xn_ref, xnn_ref)

    ku = lax.broadcasted_iota(jnp.int32, (2 * blk, blk), 0)
    qi = lax.broadcasted_iota(jnp.int32, (2 * blk, blk), 1)
    diff = qi - (ku - blk)
    band = (diff >= 0) & (diff < WINDOW)
    bias = jnp.where(band | (ku == 0), 0.0, NEG_BIG)
    bias_first = jnp.where((band & (ku >= blk)) | (ku == 0), 0.0, NEG_BIG)
    krow = lax.broadcasted_iota(jnp.int32, (2 * blk, LANES), 0)
    klane = lax.broadcasted_iota(jnp.int32, (2 * blk, LANES), 1)
    k_aug = jnp.where((krow == 0) & (klane < 3), 1.0, 0.0).astype(bf16)
    vcol = lax.broadcasted_iota(jnp.int32, (dh, 2 * blk), 1)
    ones_rows = jnp.ones((BF16_ROWS, 2 * blk), bf16)
    hp_lanes = 2 * blk

    def projection_steps(xg):
        res = {"q": {}}
        piece = lambda r0, r1: _dot_nt(wt_ref[r0:r1, :], xg) + _tile_lanes(b_ref[r0:r1, :], G // LANES)

        def kv():
            p = piece(nq, nq + 2 * nkv)
            res["k"] = p[0:nkv, :].T.astype(bf16)
            res["vt"] = p[nkv:2 * nkv, :].astype(bf16)

        def q_rows(r0):
            def step():
                res["q"][r0] = (piece(r0, r0 + Q_ROWS_PER_DOT) * (dh ** -0.5 * LOG2E)).astype(bf16)
            return step

        return res, [kv] + [q_rows(r0) for r0 in range(0, nq, Q_ROWS_PER_DOT)]

    def finalize(res):
        res["qt"] = jnp.concatenate([res["q"][r0] for r0 in sorted(res["q"])], axis=0)
        return res

    projected = {}
    shared = {}

    def operands(g, bl, grp):
        if (g, bl, grp) not in shared:
            cols = slice(bl * blk, (bl + 1) * blk)
            cur = projected[g]
            if bl > 0:
                k_prev, vt_prev = cur["k"][(bl - 1) * blk:bl * blk, :], cur["vt"][:, (bl - 1) * blk:bl * blk]
            elif g > 0:
                k_prev, vt_prev = projected[g - 1]["k"][G - blk:G, :], projected[g - 1]["vt"][:, G - blk:G]
            else:
                k_prev, vt_prev = kprev_ref[...], vtprev_ref[...]
            b1 = jnp.where(first, bias_first, bias) if (g == 0 and bl == 0) else bias
            kcat = jnp.concatenate([k_prev, cur["k"][cols, :]], axis=0)
            vtcat = jnp.concatenate([vt_prev, cur["vt"][:, cols]], axis=1)
            in_group = (klane >= grp * dh) & (klane < (grp + 1) * dh) & (krow > 0)
            km = jnp.concatenate([jnp.where(in_group, kcat, jnp.zeros_like(kcat)), k_aug], axis=1)
            vt_g = jnp.where(vcol == 0, jnp.zeros((dh, 2 * blk), bf16), vtcat[grp * dh:(grp + 1) * dh, :])
            shared[(g, bl, grp)] = km, jnp.concatenate([vt_g, ones_rows], axis=0), _tile_lanes(b1, 2)
        return shared[(g, bl, grp)]

    def logits(g, bl, grp, hp):
        cols = slice(bl * blk, (bl + 1) * blk)
        km, _, b2 = operands(g, bl, grp)
        blocks = []
        for e in range(2):
            h = grp * A_GROUP + 2 * hp + e
            pair = projected[g]["qt"][(h // 2) * LANES:(h // 2 + 1) * LANES, cols]
            if h % 2 != grp:
                pair = jnp.concatenate([pair[dh:2 * dh, :], pair[0:dh, :]], axis=0)
            blocks.append(pair)
        sink_rows = sink_ref[grp][:, hp * hp_lanes:(hp + 1) * hp_lanes]
        rhs = jnp.concatenate([jnp.concatenate(blocks, axis=1), sink_rows], axis=0)
        return _dot(km, rhs) + b2

    def finish(g, bl, grp, hp, st):
        out_cols = slice(g * G + bl * blk, g * G + (bl + 1) * blk)
        _, vext, _ = operands(g, bl, grp)
        p = jnp.exp2(st - jnp.max(st, axis=0, keepdims=True)).astype(bf16)
        oext = _dot(vext, p)
        o = (oext[0:dh, :] * (1.0 / oext[dh:dh + 1, :])).astype(bf16)
        for e in range(2):
            h = grp * A_GROUP + 2 * hp + e
            out_ref[h * dh:(h + 1) * dh, out_cols] = o[:, e * blk:(e + 1) * blk]

    n_groups = tm // G
    res, steps = projection_steps(xn[0:G, :])
    for k_step, step in enumerate(steps):
        step()
        if k_step == 0:
            prepare_next()
    projected[0] = finalize(res)

    units = [(g, bl, grp, hp) for g in range(n_groups) for bl in range(bpg)
             for grp in range(A_KV_HEADS) for hp in range(A_GROUP // 2)]
    per_group = bpg * A_KV_HEADS * (A_GROUP // 2)
    queued = {}
    for t in range(SWA_LOOKAHEAD):
        queued[t] = logits(*units[t])
    for t, unit in enumerate(units):
        g = unit[0]
        u = t % per_group
        if u == 0:
            nxt, steps = projection_steps(xn[(g + 1) * G:(g + 2) * G, :]) if g + 1 < n_groups else (None, [])
        if u < len(steps):
            steps[u]()
            if u + 1 == len(steps):
                projected[g + 1] = finalize(nxt)
        if t + SWA_LOOKAHEAD < len(units):
            queued[t + SWA_LOOKAHEAD] = logits(*units[t + SWA_LOOKAHEAD])
        finish(*unit, queued.pop(t))
    last = projected[n_groups - 1]
    kprev_ref[...] = last["k"][G - blk:G, :]
    vtprev_ref[...] = last["vt"][:, G - blk:G]


def _attn_mixer(x, gain, wt, b_rep, sink_aug, seq):
    t, d = x.shape
    nq = A_Q_HEADS * A_HEAD_DIM
    nkv = A_KV_HEADS * A_HEAD_DIM
    tm = PROJ_TILE
    steps = t // tm
    return pl.pallas_call(
        functools.partial(_attn_kernel, tiles_per_seq=seq // tm),
        grid=(steps,),
        in_specs=_proj_in_specs(tm, d, steps) + [
            _resident((1, d)), _resident(wt.shape), _resident(b_rep.shape), _resident(sink_aug.shape)],
        out_specs=pl.BlockSpec((nq, tm), lambda i: (0, i)),
        out_shape=jax.ShapeDtypeStruct((nq, t), bf16),
        scratch_shapes=[pltpu.VMEM((tm, d), bf16), pltpu.VMEM((tm, d), bf16),
                        pltpu.VMEM((A_BLOCK, nkv), bf16), pltpu.VMEM((nkv, A_BLOCK), bf16)],
        compiler_params=_params(1),
        name="attn_mixer",
    )(x, x, gain, wt, b_rep, sink_aug)


def _sink_rows(sinks):
    parts = jnp.stack(_split3(sinks.astype(f32) * LOG2E), axis=0)
    per_lane = jnp.repeat(parts.reshape(3, A_KV_HEADS, A_GROUP), A_BLOCK, axis=2)
    return jnp.pad(per_lane.transpose(1, 0, 2), ((0, 0), (0, LANES - 3), (0, 0)))


def _out_ffn_kernel(h0_ref, at0_ref, hn_ref, atn_ref, wo_ref, wup_ref, wd_ref, vec_ref, conv_ref,
                    out_ref, act_ref, h1_ref, xn_ref, h1n_ref, xnn_ref, carry_ref, *, tiles_per_seq):
    tm = hn_ref.shape[0]
    dff = wd_ref.shape[0]
    i = pl.program_id(0)

    d = wo_ref.shape[1]
    out_cols = [slice(n0, n0 + OUT_CHUNK) for n0 in range(0, d, OUT_CHUNK)]

    def out_proj(at_ref, cols):
        return _dot_tn(at_ref[...], wo_ref[:, cols]) + vec_ref[0:1, cols]

    def front(h_ref, z_parts):
        for r0 in range(0, tm, NORM_ROWS):
            rows = slice(r0, r0 + NORM_ROWS)
            z = jnp.concatenate([p[rows, :] for p in z_parts], axis=1)
            h1 = h_ref[rows, :] + _rmsnorm(z, vec_ref[1:2, :])
            h1n_ref[rows, :] = h1
            xnn_ref[rows, :] = _rmsnorm(h1, vec_ref[2:3, :]).astype(bf16)

    @pl.when(i == 0)
    def _():
        front(h0_ref, [out_proj(at0_ref, cols) for cols in out_cols])

    @pl.when(i % tiles_per_seq == 0)
    def _():
        carry_ref[...] = jnp.zeros_like(carry_ref)

    xn_ref[...] = xnn_ref[...]
    h1_ref[...] = h1n_ref[...]
    top = lax.broadcasted_iota(jnp.int32, (8, FF_CHUNK), 0)
    z_parts = []
    for ci, c0 in enumerate(range(0, dff, FF_CHUNK)):
        if FRONT_AFTER_CHUNKS <= ci < FRONT_AFTER_CHUNKS + len(out_cols):
            z_parts.append(out_proj(atn_ref, out_cols[ci - FRONT_AFTER_CHUNKS]))
        if ci == FRONT_AFTER_CHUNKS + len(out_cols):
            front(hn_ref, z_parts)
        cs = slice(c0, c0 + FF_CHUNK)
        gate = _dot(xn_ref[...], wup_ref[:, cs])
        val = _dot(xn_ref[...], wup_ref[:, dff + c0:dff + c0 + FF_CHUNK])
        prev = carry_ref[0:8, cs]
        carry_ref[0:8, cs] = gate[tm - 8:tm, :]
        shifted = []
        for lag in (1, 2):
            rolled = pltpu.roll(gate, lag, axis=0)
            head = jnp.where(top < lag, pltpu.roll(prev, lag, axis=0), rolled[0:8, :])
            shifted.append(jnp.concatenate([head, rolled[8:, :]], axis=0))
        g1, g2 = shifted
        gc = conv_ref[3:4, cs] + conv_ref[0:1, cs] * g2 + conv_ref[1:2, cs] * g1 + conv_ref[2:3, cs] * gate
        act_ref[:, cs] = (gc * jax.nn.sigmoid(gc) * val).astype(bf16)
    y = _dot(act_ref[...], wd_ref[...])
    for r0 in range(0, tm, NORM_ROWS):
        rows = slice(r0, r0 + NORM_ROWS)
        out_ref[rows, :] = h1_ref[rows, :] + _rmsnorm(y[rows, :], vec_ref[3:4, :])


def _out_ffn(h, at, wo, wup, wd, vecs, conv, layer, seq):
    t, d = h.shape
    tm = TOKEN_TILE
    n = t // tm
    dff = wd.shape[1]
    nxt = lambda i: jnp.minimum(i + 1, n - 1)
    return pl.pallas_call(
        functools.partial(_out_ffn_kernel, tiles_per_seq=seq // tm),
        grid=(n,),
        in_specs=[pl.BlockSpec((tm, d), lambda i: (0, 0), pipeline_mode=pl.Buffered(1)),
                  pl.BlockSpec((at.shape[0], tm), lambda i: (0, 0), pipeline_mode=pl.Buffered(1)),
                  pl.BlockSpec((tm, d), lambda i: (nxt(i), 0)),
                  pl.BlockSpec((at.shape[0], tm), lambda i: (0, nxt(i))),
                  _resident(wo.shape), _layer_resident(wup.shape, layer), _layer_resident(wd.shape, layer),
                  _resident(vecs.shape), _resident(conv.shape)],
        out_specs=pl.BlockSpec((tm, d), lambda i: (i, 0)),
        out_shape=jax.ShapeDtypeStruct((t, d), f32),
        scratch_shapes=[pltpu.VMEM((tm, dff), bf16),
                        pltpu.VMEM((tm, d), f32), pltpu.VMEM((tm, d), bf16),
                        pltpu.VMEM((tm, d), f32), pltpu.VMEM((tm, d), bf16), pltpu.VMEM((16, dff), f32)],
        compiler_params=_params(1),
        name="out_ffn",
    )(h, at, h, at, wo, wup, wd, vecs, conv)


def kernel(x, m_w_in, m_gate_bias, m_head_norm, m_w_out, a_w_in, a_b_in, a_sinks, a_w_out, a_b_out,
           norm_mix_pre, norm_mix_post, norm_ffn_pre, norm_ffn_post, f_w_up, f_conv_w, f_conv_b, f_w_down):
    batch, seq, d = x.shape
    depth = norm_mix_pre.shape[0]
    h = x.reshape(batch * seq, d)
    row = lambda vec: vec.reshape(1, -1).astype(f32)
    lane_rep = lambda vec: jnp.broadcast_to(vec.astype(f32)[:, None], (vec.shape[0], LANES))

    for i in range(depth):
        j = i // 2
        if i % 2 == 0:
            mixed_t, w_up, w_down = _mlstm_mixer(
                h, row(norm_mix_pre[i]), m_w_in[j].T.astype(bf16), lane_rep(m_gate_bias[j].reshape(-1)),
                lane_rep(m_head_norm[j]), f_w_up, f_w_down, seq)
            w_out = m_w_out[j].astype(bf16)
            b_out = jnp.zeros((d,), f32)
        else:
            mixed_t = _attn_mixer(h, row(norm_mix_pre[i]), a_w_in[j].T.astype(bf16), lane_rep(a_b_in[j]),
                                  _sink_rows(a_sinks[j]), seq)
            w_out = a_w_out[j].astype(bf16)
            b_out = a_b_out[j]
        vecs = jnp.pad(jnp.stack([b_out, norm_mix_post[i], norm_ffn_pre[i], norm_ffn_post[i]]).astype(f32),
                       ((0, 4), (0, 0)))
        conv = jnp.pad(jnp.concatenate([f_conv_w[i], f_conv_b[i][None]], axis=0).astype(f32), ((0, 12), (0, 0)))
        h = _out_ffn(h, mixed_t, w_out, w_up, w_down, vecs, conv, i, seq)
    return h.reshape(batch, seq, d)
```

```python
import functools

import jax
import jax.numpy as jnp
from jax import lax
from jax.experimental import pallas as pl
from jax.experimental.pallas import tpu as pltpu

EPS = 1e-6
LANES = 128
BF16_ROWS = 16

M_HEADS = 8
M_QK_DIM = 64
M_V_DIM = 128
GATE_CAP = 15.0
M_CHUNK = 128
MLSTM_GROUP = 256
MIXER_LOOKAHEAD = 2

A_HEAD_DIM = 64
A_Q_HEADS = 16
A_KV_HEADS = 2
A_GROUP = A_Q_HEADS // A_KV_HEADS
WINDOW = 128
A_BLOCK = 128

LOG2E = 1.4426950408889634
NEG_BIG = -1e30

TOKEN_TILE = 512
PROJ_TILE = 1024
ATTN_GROUP = 256
SWA_LOOKAHEAD = 5
FF_CHUNK = 256
Q_ROWS_PER_DOT = 256
OUT_CHUNK = 256
NORM_ROWS = 128
FRONT_AFTER_CHUNKS = 2
VMEM_LIMIT = 60 * 1024 * 1024

bf16 = jnp.bfloat16
f32 = jnp.float32


def _dot(a, b):
    return jnp.dot(a, b, preferred_element_type=f32)


def _dot_nt(a, b):
    return lax.dot_general(a, b, (((1,), (1,)), ((), ())), preferred_element_type=f32)


def _dot_tn(a, b):
    return lax.dot_general(a, b, (((0,), (0,)), ((), ())), preferred_element_type=f32)


def _rmsnorm(x, g):
    return x * lax.rsqrt(jnp.mean(x * x, axis=-1, keepdims=True) + EPS) * g


def _split3(x):
    hi = x.astype(bf16)
    r1 = x - hi.astype(f32)
    mid = r1.astype(bf16)
    lo = (r1 - mid.astype(f32)).astype(bf16)
    return hi, mid, lo


def _log_sigmoid(x):
    return jnp.minimum(x, 0.0) - jnp.log1p(jnp.exp(-jnp.abs(x)))


def _tile_lanes(x, reps):
    return jnp.concatenate([x] * reps, axis=1)


def _next_input_norm(i, x0_ref, xnext_ref, g_ref, xn_ref, xnn_ref):
    @pl.when(i == 0)
    def _():
        xnn_ref[...] = _rmsnorm(x0_ref[...], g_ref[...]).astype(bf16)

    xn_ref[...] = xnn_ref[...]

    def prepare_next():
        xnn_ref[...] = _rmsnorm(xnext_ref[...], g_ref[...]).astype(bf16)

    return xn_ref[...], prepare_next


def _sigmoid(x):
    return 0.5 * jnp.tanh(0.5 * x) + 0.5


def _proj_in_specs(tm, d, n_tiles):
    return [pl.BlockSpec((tm, d), lambda i: (0, 0), pipeline_mode=pl.Buffered(1)),
            pl.BlockSpec((tm, d), lambda i: (jnp.minimum(i + 1, n_tiles - 1), 0))]


def _resident(shape):
    nd = len(shape)
    return pl.BlockSpec(shape, lambda *_: (0,) * nd, pipeline_mode=pl.Buffered(1))


def _layer_resident(shape, layer):
    nd = len(shape) - 1
    return pl.BlockSpec((None,) + tuple(shape[1:]), lambda *_: (layer,) + (0,) * nd, pipeline_mode=pl.Buffered(1))


def _params(n_axes):
    return pltpu.CompilerParams(
        dimension_semantics=("arbitrary",) * n_axes, vmem_limit_bytes=VMEM_LIMIT)


def _mlstm_kernel(x0_ref, xnext_ref, g_ref, wt_ref, gb_ref, hn_ref, wup_ref, wdown_ref,
                  out_ref, wup_bf_ref, wdown_bf_ref, xn_ref, xnn_ref, ct_ref, m_ref,
                  q0_ref, k0_ref, vt0_ref, ot0_ref, stat0_ref, ccol0_ref, *, tiles_per_seq):
    tm = xnext_ref.shape[0]
    L = M_CHUNK
    H = M_HEADS
    dk, dv = M_QK_DIM, M_V_DIM
    nq = H * dk
    nv = H * dv
    pairs = H // 2
    G = MLSTM_GROUP
    cpg = G // L
    i = pl.program_id(0)

    @pl.when(i % tiles_per_seq == 0)
    def _():
        ct_ref[...] = jnp.zeros_like(ct_ref)
        m_ref[...] = jnp.zeros_like(m_ref)

    xn, prepare_next = _next_input_norm(i, x0_ref, xnext_ref, g_ref, xn_ref, xnn_ref)

    upper = lax.broadcasted_iota(jnp.int32, (L, L), 0) <= lax.broadcasted_iota(jnp.int32, (L, L), 1)
    triu = jnp.where(upper, 1.0, 0.0).astype(bf16)
    lane_h = lax.broadcasted_iota(jnp.int32, (H, L), 1)
    lane = lax.broadcasted_iota(jnp.int32, (L, LANES), 1)
    low = lane < dk
    ones_rows = jnp.ones((BF16_ROWS, L), bf16)
    zeros_half = jnp.zeros((dk, L), bf16)

    def projection_steps(xg):
        res = {}
        piece = lambda r0, r1: _dot_nt(wt_ref[r0:r1, :], xg)
        o0 = 2 * nq + nv

        def gates_and_o_hi():
            p = piece(o0 + nv // 2, o0 + nv + 2 * H)
            res["o_hi"] = _sigmoid(p[0:nv // 2, :]).astype(bf16)
            raw = p[nv // 2:, :] + _tile_lanes(gb_ref[...], G // LANES)
            capped = GATE_CAP * jnp.tanh(raw * (1.0 / GATE_CAP))
            res["grow"] = jnp.where(lax.broadcasted_iota(jnp.int32, raw.shape, 0) < H, capped, _log_sigmoid(capped))

        def o_lo():
            res["o_lo"] = _sigmoid(piece(o0, o0 + nv // 2)).astype(bf16)

        def k_and_stats():
            res["k"] = piece(nq, 2 * nq).T.astype(bf16)
            grow = res["grow"]
            stacked = jnp.concatenate([grow[:, c * L:(c + 1) * L] for c in range(cpg)], axis=0)
            sums = _dot(jnp.concatenate(_split3(stacked), axis=0), triu)
            n = cpg * 2 * H
            cum = sums[0:n] + sums[n:2 * n] + sums[2 * n:]
            stats = []
            for c in range(cpg):
                b_r = cum[c * 2 * H + H:(c + 1) * 2 * H, :]
                c_r = grow[0:H, c * L:(c + 1) * L] - b_r
                cmax = c_r
                shift = 1
                while shift < L:
                    cmax = jnp.where(lane_h >= shift, jnp.maximum(cmax, pltpu.roll(cmax, shift, axis=1)), cmax)
                    shift *= 2
                b_last = jnp.broadcast_to(b_r[:, L - 1:L], (H, L))
                tail = c_r + b_last
                tail_max = jnp.broadcast_to(jnp.max(tail, axis=-1, keepdims=True), (H, L))
                c_cols = jnp.concatenate([c_r * LOG2E, jnp.zeros((LANES - H, L), f32)], axis=0).T
                stats.append((b_r, cmax, tail, tail_max, b_last, c_cols))
            res["stats"] = stats

        def v_lo():
            res["v_lo"] = piece(2 * nq, 2 * nq + nv // 2).astype(bf16)

        def v_hi():
            res["v_hi"] = piece(2 * nq + nv // 2, 2 * nq + nv).astype(bf16)

        def q():
            res["q"] = (piece(0, nq) * (dk ** -0.5)).astype(bf16)

        return res, [gates_and_o_hi, o_lo, k_and_stats, v_lo, v_hi, q]

    def finalize(res):
        res["ot"] = jnp.concatenate([res.pop("o_lo"), res.pop("o_hi")], axis=0)
        res["vt"] = jnp.concatenate([res.pop("v_lo"), res.pop("v_hi")], axis=0)
        return res

    def kq(p, c, j):
        tok = slice(c * L, (c + 1) * L)
        kp = p["k"][tok, j * LANES:(j + 1) * LANES]
        qtp = p["q"][j * LANES:(j + 1) * LANES, tok]
        q_even = jnp.concatenate([qtp[0:dk, :], zeros_half], axis=0)
        q_odd = jnp.concatenate([zeros_half, qtp[dk:2 * dk, :]], axis=0)
        return kp, (q_even, q_odd), _dot(kp, jnp.concatenate([q_even, q_odd], axis=1))

    def scan_unit(p, g, c, j, m_prev, m_new, ct, kq_res):
        kp, q_eo, st = kq_res
        tok = slice(c * L, (c + 1) * L)
        out_tok = slice(g * G + c * L, g * G + (c + 1) * L)
        b_r, cmax, tail, _, b_last, c_cols = p["stats"][c]
        a_r = jnp.maximum(m_prev, cmax)
        a_r2 = a_r * LOG2E
        carry_w = jnp.exp(m_prev - a_r).astype(bf16)
        clamp_r = jnp.exp(-(a_r + b_r))
        ws_r = jnp.exp(tail - m_new).astype(bf16)
        decay = jnp.exp(b_last + m_prev - m_new)
        ct_pair_b = ct[j].astype(bf16)
        vws = []
        for e, q_e in enumerate(q_eo):
            h = 2 * j + e
            expo = jnp.where(upper, c_cols[:, h:h + 1] - a_r2[h:h + 1, :], NEG_BIG)
            wt = (st[:, e * L:(e + 1) * L] * jnp.exp2(expo)).astype(bf16)
            q_w = q_e * carry_w[h:h + 1, :]
            vext = jnp.concatenate([p["vt"][h * dv:(h + 1) * dv, tok], ones_rows], axis=0)
            nd = _dot(jnp.concatenate([vext, ct_pair_b], axis=1), jnp.concatenate([wt, q_w], axis=0))
            den = jnp.maximum(jnp.abs(nd[dv:dv + 1, :]), clamp_r[h:h + 1, :])
            inv = 1.0 / den
            num = nd[0:dv, :]
            scale = inv * lax.rsqrt(inv * inv * jnp.mean(num * num, axis=0, keepdims=True) + EPS)
            hn = num * scale * hn_ref[h * dv:(h + 1) * dv, :]
            gate_o = p["ot"][h * dv:(h + 1) * dv, tok].astype(f32)
            out_ref[h * dv:(h + 1) * dv, out_tok] = (gate_o * hn).astype(bf16)
            vws.append(vext * ws_r[h:h + 1, :])
        k_split = jnp.concatenate([jnp.where(low, kp, jnp.zeros_like(kp)),
                                   jnp.where(low, jnp.zeros_like(kp), kp)], axis=0)
        decay_pair = jnp.where(low[0:1, :], decay[2 * j:2 * j + 1, :], decay[2 * j + 1:2 * j + 2, :])
        ct[j] = decay_pair * ct[j] + _dot(jnp.concatenate(vws, axis=1), k_split)

    def store_first_group(res):
        q0_ref[...], k0_ref[...], vt0_ref[...], ot0_ref[...] = res["q"], res["k"], res["vt"], res["ot"]
        for c, stat in enumerate(res["stats"]):
            stat0_ref[c * 5 * H:(c + 1) * 5 * H, :] = jnp.concatenate(stat[0:5], axis=0)
            ccol0_ref[c * L:(c + 1) * L, :] = stat[5]

    def load_first_group():
        stats = [tuple(stat0_ref[(c * 5 + s) * H:(c * 5 + s + 1) * H, :] for s in range(5))
                 + (ccol0_ref[c * L:(c + 1) * L, :],) for c in range(cpg)]
        return {"q": q0_ref[...], "k": k0_ref[...], "vt": vt0_ref[...], "ot": ot0_ref[...], "stats": stats}

    @pl.when(i == 0)
    def _():
        res, first_steps = projection_steps(xn_ref[0:G, :])
        for step in first_steps:
            step()
        store_first_group(finalize(res))

    n_groups = tm // G
    cur = load_first_group()

    ct = [ct_ref[j] for j in range(pairs)]
    m = m_ref[...]
    units = [(g, c, j) for g in range(n_groups) for c in range(cpg) for j in range(pairs)]
    per_group = cpg * pairs
    projected = {0: cur}
    kq_queue = {}

    def queue_kq(t):
        g, c, j = units[t]
        kq_queue[t] = kq(projected[g], c, j)

    for t in range(MIXER_LOOKAHEAD):
        queue_kq(t)
    for t, (g, c, j) in enumerate(units):
        u = t % per_group
        if u == 0:
            nxt, steps = projection_steps(xn[(g + 1) * G:(g + 2) * G, :] if g + 1 < n_groups else xnn_ref[0:G, :])
            stats = projected[g]["stats"]
            m_prevs = []
            for cc in range(cpg):
                m_prevs.append(m)
                m = jnp.maximum(stats[cc][4] + m, stats[cc][3])
            m_prevs.append(m)
        if t == 0:
            prepare_next()
        if u < len(steps):
            steps[u]()
            if u + 1 == len(steps):
                if g + 1 < n_groups:
                    projected[g + 1] = finalize(nxt)
                else:
                    store_first_group(finalize(nxt))
        if t + MIXER_LOOKAHEAD < len(units):
            queue_kq(t + MIXER_LOOKAHEAD)
        scan_unit(projected[g], g, c, j, m_prevs[c], m_prevs[c + 1], ct, kq_queue.pop(t))
    m_ref[...] = m
    for j in range(pairs):
        ct_ref[j] = ct[j]
    wup_bf_ref[...] = wup_ref[...].astype(bf16)
    wdown_bf_ref[...] = wdown_ref[...].astype(bf16)


def _mlstm_mixer(x, gain, wt, gate_bias, hn_rep, w_up, w_down, seq):
    t, d = x.shape
    nq = M_HEADS * M_QK_DIM
    nv = M_HEADS * M_V_DIM
    tm = PROJ_TILE
    steps = t // tm
    slab = lambda w: pl.BlockSpec((w.shape[0], w.shape[1] // steps, w.shape[2]), lambda i: (0, i, 0))
    return pl.pallas_call(
        functools.partial(_mlstm_kernel, tiles_per_seq=seq // tm),
        grid=(steps,),
        in_specs=_proj_in_specs(tm, d, steps) + [
            _resident((1, d)), _resident(wt.shape), _resident(gate_bias.shape), _resident(hn_rep.shape),
            slab(w_up), slab(w_down)],
        out_specs=[pl.BlockSpec((nv, tm), lambda i: (0, i)), slab(w_up), slab(w_down)],
        out_shape=[jax.ShapeDtypeStruct((nv, t), bf16),
                   jax.ShapeDtypeStruct(w_up.shape, bf16), jax.ShapeDtypeStruct(w_down.shape, bf16)],
        scratch_shapes=[pltpu.VMEM((tm, d), bf16), pltpu.VMEM((tm, d), bf16),
                        pltpu.VMEM((M_HEADS // 2, M_V_DIM + BF16_ROWS, 2 * M_QK_DIM), f32),
                        pltpu.VMEM((M_HEADS, LANES), f32),
                        pltpu.VMEM((nq, MLSTM_GROUP), bf16), pltpu.VMEM((MLSTM_GROUP, nq), bf16),
                        pltpu.VMEM((nv, MLSTM_GROUP), bf16), pltpu.VMEM((nv, MLSTM_GROUP), bf16),
                        pltpu.VMEM((MLSTM_GROUP // M_CHUNK * 5 * M_HEADS, M_CHUNK), f32),
                        pltpu.VMEM((MLSTM_GROUP, LANES), f32)],
        compiler_params=_params(1),
        name="mlstm_mixer",
    )(x, x, gain, wt, gate_bias, hn_rep, w_up, w_down)


def _attn_kernel(x0_ref, xnext_ref, g_ref, wt_ref, b_ref, sink_ref, out_ref,
                 xn_ref, xnn_ref, kprev_ref, vtprev_ref, *, tiles_per_seq):
    tm = xnext_ref.shape[0]
    blk = A_BLOCK
    dh = A_HEAD_DIM
    nq = A_Q_HEADS * dh
    nkv = A_KV_HEADS * dh
    G = ATTN_GROUP
    bpg = G // blk
    i = pl.program_id(0)
    first = i % tiles_per_seq == 0

    @pl.when(i == 0)
    def _():
        kprev_ref[...] = jnp.zeros_like(kprev_ref)
        vtprev_ref[...] = jnp.zeros_like(vtprev_ref)

    xn, prepare_next = _next_input_norm(i, x0_ref, xnext_ref, g_ref, xn_ref, xnn_ref)

    ku = lax.broadcasted_iota(jnp.int32, (2 * blk, blk), 0)
    qi = lax.broadcasted_iota(jnp.int32, (2 * blk, blk), 1)
    diff = qi - (ku - blk)
    band = (diff >= 0) & (diff < WINDOW)
    bias = jnp.where(band | (ku == 0), 0.0, NEG_BIG)
    bias_first = jnp.where((band & (ku >= blk)) | (ku == 0), 0.0, NEG_BIG)
    krow = lax.broadcasted_iota(jnp.int32, (2 * blk, LANES), 0)
    klane = lax.broadcasted_iota(jnp.int32, (2 * blk, LANES), 1)
    k_aug = jnp.where((krow == 0) & (klane < 3), 1.0, 0.0).astype(bf16)
    vcol = lax.broadcasted_iota(jnp.int32, (dh, 2 * blk), 1)
    ones_rows = jnp.ones((BF16_ROWS, 2 * blk), bf16)
    hp_lanes = 2 * blk

    def projection_steps(xg):
        res = {"q": {}}
        piece = lambda r0, r1: _dot_nt(wt_ref[r0:r1, :], xg) + _tile_lanes(b_ref[r0:r1, :], G // LANES)

        def kv():
            p = piece(nq, nq + 2 * nkv)
            res["k"] = p[0:nkv, :].T.astype(bf16)
            res["vt"] = p[nkv:2 * nkv, :].astype(bf16)

        def q_rows(r0):
            def step():
                res["q"][r0] = (piece(r0, r0 + Q_ROWS_PER_DOT) * (dh ** -0.5 * LOG2E)).astype(bf16)
            return step

        return res, [kv] + [q_rows(r0) for r0 in range(0, nq, Q_ROWS_PER_DOT)]

    def finalize(res):
        res["qt"] = jnp.concatenate([res["q"][r0] for r0 in sorted(res["q"])], axis=0)
        return res

    projected = {}
    shared = {}

    def operands(g, bl, grp):
        if (g, bl, grp) not in shared:
            cols = slice(bl * blk, (bl + 1) * blk)
            cur = projected[g]
            if bl > 0:
                k_prev, vt_prev = cur["k"][(bl - 1) * blk:bl * blk, :], cur["vt"][:, (bl - 1) * blk:bl * blk]
            elif g > 0:
                k_prev, vt_prev = projected[g - 1]["k"][G - blk:G, :], projected[g - 1]["vt"][:, G - blk:G]
            else:
                k_prev, vt_prev = kprev_ref[...], vtprev_ref[...]
            b1 = jnp.where(first, bias_first, bias) if (g == 0 and bl == 0) else bias
            kcat = jnp.concatenate([k_prev, cur["k"][cols, :]], axis=0)
            vtcat = jnp.concatenate([vt_prev, cur["vt"][:, cols]], axis=1)
            in_group = (klane >= grp * dh) & (klane < (grp + 1) * dh) & (krow > 0)
            km = jnp.concatenate([jnp.where(in_group, kcat, jnp.zeros_like(kcat)), k_aug], axis=1)
            vt_g = jnp.where(vcol == 0, jnp.zeros((dh, 2 * blk), bf16), vtcat[grp * dh:(grp + 1) * dh, :])
            shared[(g, bl, grp)] = km, jnp.concatenate([vt_g, ones_rows], axis=0), _tile_lanes(b1, 2)
        return shared[(g, bl, grp)]

    def logits(g, bl, grp, hp):
        cols = slice(bl * blk, (bl + 1) * blk)
        km, _, b2 = operands(g, bl, grp)
        blocks = []
        for e in range(2):
            h = grp * A_GROUP + 2 * hp + e
            pair = projected[g]["qt"][(h // 2) * LANES:(h // 2 + 1) * LANES, cols]
            if h % 2 != grp:
                pair = jnp.concatenate([pair[dh:2 * dh, :], pair[0:dh, :]], axis=0)
            blocks.append(pair)
        sink_rows = sink_ref[grp][:, hp * hp_lanes:(hp + 1) * hp_lanes]
        rhs = jnp.concatenate([jnp.concatenate(blocks, axis=1), sink_rows], axis=0)
        return _dot(km, rhs) + b2

    def finish(g, bl, grp, hp, st):
        out_cols = slice(g * G + bl * blk, g * G + (bl + 1) * blk)
        _, vext, _ = operands(g, bl, grp)
        p = jnp.exp2(st - jnp.max(st, axis=0, keepdims=True)).astype(bf16)
        oext = _dot(vext, p)
        o = (oext[0:dh, :] * (1.0 / oext[dh:dh + 1, :])).astype(bf16)
        for e in range(2):
            h = grp * A_GROUP + 2 * hp + e
            out_ref[h * dh:(h + 1) * dh, out_cols] = o[:, e * blk:(e + 1) * blk]

    n_groups = tm // G
    res, steps = projection_steps(xn[0:G, :])
    for k_step, step in enumerate(steps):
        step()
        if k_step == 0:
            prepare_next()
    projected[0] = finalize(res)

    units = [(g, bl, grp, hp) for g in range(n_groups) for bl in range(bpg)
             for grp in range(A_KV_HEADS) for hp in range(A_GROUP // 2)]
    per_group = bpg * A_KV_HEADS * (A_GROUP // 2)
    queued = {}
    for t in range(SWA_LOOKAHEAD):
        queued[t] = logits(*units[t])
    for t, unit in enumerate(units):
        g = unit[0]
        u = t % per_group
        if u == 0:
            nxt, steps = projection_steps(xn[(g + 1) * G:(g + 2) * G, :]) if g + 1 < n_groups else (None, [])
        if u < len(steps):
            steps[u]()
            if u + 1 == len(steps):
                projected[g + 1] = finalize(nxt)
        if t + SWA_LOOKAHEAD < len(units):
            queued[t + SWA_LOOKAHEAD] = logits(*units[t + SWA_LOOKAHEAD])
        finish(*unit, queued.pop(t))
    last = projected[n_groups - 1]
    kprev_ref[...] = last["k"][G - blk:G, :]
    vtprev_ref[...] = last["vt"][:, G - blk:G]


def _attn_mixer(x, gain, wt, b_rep, sink_aug, seq):
    t, d = x.shape
    nq = A_Q_HEADS * A_HEAD_DIM
    nkv = A_KV_HEADS * A_HEAD_DIM
    tm = PROJ_TILE
    steps = t // tm
    return pl.pallas_call(
        functools.partial(_attn_kernel, tiles_per_seq=seq // tm),
        grid=(steps,),
        in_specs=_proj_in_specs(tm, d, steps) + [
            _resident((1, d)), _resident(wt.shape), _resident(b_rep.shape), _resident(sink_aug.shape)],
        out_specs=pl.BlockSpec((nq, tm), lambda i: (0, i)),
        out_shape=jax.ShapeDtypeStruct((nq, t), bf16),
        scratch_shapes=[pltpu.VMEM((tm, d), bf16), pltpu.VMEM((tm, d), bf16),
                        pltpu.VMEM((A_BLOCK, nkv), bf16), pltpu.VMEM((nkv, A_BLOCK), bf16)],
        compiler_params=_params(1),
        name="attn_mixer",
    )(x, x, gain, wt, b_rep, sink_aug)


def _sink_rows(sinks):
    parts = jnp.stack(_split3(sinks.astype(f32) * LOG2E), axis=0)
    per_lane = jnp.repeat(parts.reshape(3, A_KV_HEADS, A_GROUP), A_BLOCK, axis=2)
    return jnp.pad(per_lane.transpose(1, 0, 2), ((0, 0), (0, LANES - 3), (0, 0)))


def _out_ffn_kernel(h0_ref, at0_ref, hn_ref, atn_ref, wo_ref, wup_ref, wd_ref, vec_ref, conv_ref,
                    out_ref, act_ref, h1_ref, xn_ref, h1n_ref, xnn_ref, carry_ref, *, tiles_per_seq):
    tm = hn_ref.shape[0]
    dff = wd_ref.shape[0]
    i = pl.program_id(0)

    d = wo_ref.shape[1]
    out_cols = [slice(n0, n0 + OUT_CHUNK) for n0 in range(0, d, OUT_CHUNK)]

    def out_proj(at_ref, cols):
        return _dot_tn(at_ref[...], wo_ref[:, cols]) + vec_ref[0:1, cols]

    def front(h_ref, z_parts):
        for r0 in range(0, tm, NORM_ROWS):
            rows = slice(r0, r0 + NORM_ROWS)
            z = jnp.concatenate([p[rows, :] for p in z_parts], axis=1)
            h1 = h_ref[rows, :] + _rmsnorm(z, vec_ref[1:2, :])
            h1n_ref[rows, :] = h1
            xnn_ref[rows, :] = _rmsnorm(h1, vec_ref[2:3, :]).astype(bf16)

    @pl.when(i == 0)
    def _():
        front(h0_ref, [out_proj(at0_ref, cols) for cols in out_cols])

    @pl.when(i % tiles_per_seq == 0)
    def _():
        carry_ref[...] = jnp.zeros_like(carry_ref)

    xn_ref[...] = xnn_ref[...]
    h1_ref[...] = h1n_ref[...]
    top = lax.broadcasted_iota(jnp.int32, (8, FF_CHUNK), 0)
    z_parts = []
    for ci, c0 in enumerate(range(0, dff, FF_CHUNK)):
        if FRONT_AFTER_CHUNKS <= ci < FRONT_AFTER_CHUNKS + len(out_cols):
            z_parts.append(out_proj(atn_ref, out_cols[ci - FRONT_AFTER_CHUNKS]))
        if ci == FRONT_AFTER_CHUNKS + len(out_cols):
            front(hn_ref, z_parts)
        cs = slice(c0, c0 + FF_CHUNK)
        gate = _dot(xn_ref[...], wup_ref[:, cs])
        val = _dot(xn_ref[...], wup_ref[:, dff + c0:dff + c0 + FF_CHUNK])
        prev = carry_ref[0:8, cs]
        carry_ref[0:8, cs] = gate[tm - 8:tm, :]
        shifted = []
        for lag in (1, 2):
            rolled = pltpu.roll(gate, lag, axis=0)
            head = jnp.where(top < lag, pltpu.roll(prev, lag, axis=0), rolled[0:8, :])
            shifted.append(jnp.concatenate([head, rolled[8:, :]], axis=0))
        g1, g2 = shifted
        gc = conv_ref[3:4, cs] + conv_ref[0:1, cs] * g2 + conv_ref[1:2, cs] * g1 + conv_ref[2:3, cs] * gate
        act_ref[:, cs] = (gc * jax.nn.sigmoid(gc) * val).astype(bf16)
    half = tm // 2
    ys = [_dot(act_ref[r0:r0 + half, :], wd_ref[...]) for r0 in (0, half)]
    for r0 in range(0, tm, NORM_ROWS):
        rows = slice(r0, r0 + NORM_ROWS)
        y = ys[r0 // half][r0 % half:r0 % half + NORM_ROWS, :]
        out_ref[rows, :] = h1_ref[rows, :] + _rmsnorm(y, vec_ref[3:4, :])


def _out_ffn(h, at, wo, wup, wd, vecs, conv, layer, seq):
    t, d = h.shape
    tm = TOKEN_TILE
    n = t // tm
    dff = wd.shape[1]
    nxt = lambda i: jnp.minimum(i + 1, n - 1)
    return pl.pallas_call(
        functools.partial(_out_ffn_kernel, tiles_per_seq=seq // tm),
        grid=(n,),
        in_specs=[pl.BlockSpec((tm, d), lambda i: (0, 0), pipeline_mode=pl.Buffered(1)),
                  pl.BlockSpec((at.shape[0], tm), lambda i: (0, 0), pipeline_mode=pl.Buffered(1)),
                  pl.BlockSpec((tm, d), lambda i: (nxt(i), 0)),
                  pl.BlockSpec((at.shape[0], tm), lambda i: (0, nxt(i))),
                  _resident(wo.shape), _layer_resident(wup.shape, layer), _layer_resident(wd.shape, layer),
                  _resident(vecs.shape), _resident(conv.shape)],
        out_specs=pl.BlockSpec((tm, d), lambda i: (i, 0)),
        out_shape=jax.ShapeDtypeStruct((t, d), f32),
        scratch_shapes=[pltpu.VMEM((tm, dff), bf16),
                        pltpu.VMEM((tm, d), f32), pltpu.VMEM((tm, d), bf16),
                        pltpu.VMEM((tm, d), f32), pltpu.VMEM((tm, d), bf16), pltpu.VMEM((16, dff), f32)],
        compiler_params=_params(1),
        name="out_ffn",
    )(h, at, h, at, wo, wup, wd, vecs, conv)


def kernel(x, m_w_in, m_gate_bias, m_head_norm, m_w_out, a_w_in, a_b_in, a_sinks, a_w_out, a_b_out,
           norm_mix_pre, norm_mix_post, norm_ffn_pre, norm_ffn_post, f_w_up, f_conv_w, f_conv_b, f_w_down):
    batch, seq, d = x.shape
    depth = norm_mix_pre.shape[0]
    h = x.reshape(batch * seq, d)
    row = lambda vec: vec.reshape(1, -1).astype(f32)
    lane_rep = lambda vec: jnp.broadcast_to(vec.astype(f32)[:, None], (vec.shape[0], LANES))

    for i in range(depth):
        j = i // 2
        if i % 2 == 0:
            mixed_t, w_up, w_down = _mlstm_mixer(
                h, row(norm_mix_pre[i]), m_w_in[j].T.astype(bf16), lane_rep(m_gate_bias[j].reshape(-1)),
                lane_rep(m_head_norm[j]), f_w_up, f_w_down, seq)
            w_out = m_w_out[j].astype(bf16)
            b_out = jnp.zeros((d,), f32)
        else:
            mixed_t = _attn_mixer(h, row(norm_mix_pre[i]), a_w_in[j].T.astype(bf16), lane_rep(a_b_in[j]),
                                  _sink_rows(a_sinks[j]), seq)
            w_out = a_w_out[j].astype(bf16)
            b_out = a_b_out[j]
        vecs = jnp.pad(jnp.stack([b_out, norm_mix_post[i], norm_ffn_pre[i], norm_ffn_post[i]]).astype(f32),
                       ((0, 4), (0, 0)))
        conv = jnp.pad(jnp.concatenate([f_conv_w[i], f_conv_b[i][None]], axis=0).astype(f32), ((0, 12), (0, 0)))
        h = _out_ffn(h, mixed_t, w_out, w_up, w_down, vecs, conv, i, seq)
    return h.reshape(batch, seq, d)
```
